```python
import math
import jax, jax.numpy as jnp
from jax import lax
import numpy as np


D_MODEL = 1024
BATCH = 2
SEQ = 8192
DEPTH = 2

HEAD_DIM = 64
RET_DIM = D_MODEL // 4
RET_HEADS = RET_DIM // HEAD_DIM
RET_CHUNK = 128
SSD_DIM = D_MODEL // 2
SSD_HEADS = SSD_DIM // HEAD_DIM
SSD_GROUPS = 2
SSD_STATE = 64
SSD_CONV = 4
SSD_CHUNK = 128
SSD_CONV_DIM = SSD_DIM + 2 * SSD_GROUPS * SSD_STATE
SWA_DIM = D_MODEL // 4
SWA_HEADS = SWA_DIM // HEAD_DIM
SWA_KV_HEADS = 2
SWA_KV_DIM = SWA_KV_HEADS * HEAD_DIM
WINDOW = 128
REL_BUCKETS = 32
REL_MAX_DIST = WINDOW
MIX_DIM = RET_DIM + SSD_DIM + SWA_DIM
IN_SIZES = (RET_DIM, RET_DIM, RET_DIM, RET_DIM,
            SSD_DIM, SSD_CONV_DIM, SSD_HEADS,
            SWA_DIM, SWA_KV_DIM, SWA_KV_DIM)
IN_DIM = sum(IN_SIZES)
D_FF = 2816
N_EXPERTS = 8
TOP_K = 2
D_FF_EXPERT = 3584
N_DENSE = (DEPTH + 1) // 2
N_MOE = DEPTH // 2
DEEPNORM_ALPHA = (2 * DEPTH) ** 0.25
DEEPNORM_BETA = (8 * DEPTH) ** -0.25
LN_EPS = 1e-5

kernel_name = 'hybrid_ret_ssd_swa_moe_deepnorm_adaln'


def _layer_norm(x, g, b):
    xf = x.astype(jnp.float32)
    mu = jnp.mean(xf, -1, keepdims=True)
    var = jnp.mean(jnp.square(xf - mu), -1, keepdims=True)
    return ((xf - mu) * lax.rsqrt(var + LN_EPS) * g + b).astype(x.dtype)


def _rotary(t, pos):
    half = t.shape[-1] // 2
    inv = jnp.exp(-math.log(10000.0) * jnp.arange(half, dtype=jnp.float32) / half)
    ang = pos.astype(jnp.float32)[..., None] * inv
    cos = jnp.cos(ang)[:, :, None, :]
    sin = jnp.sin(ang)[:, :, None, :]
    t1, t2 = t[..., :half], t[..., half:]
    return jnp.concatenate([t1 * cos - t2 * sin, t1 * sin + t2 * cos], -1).astype(t.dtype)


def _retention(q, k, v, g, pos):
    Bn, L, _ = q.shape
    H, d, C = RET_HEADS, HEAD_DIM, RET_CHUNK
    N = L // C
    dt = q.dtype
    q = _rotary(q.reshape(Bn, L, H, d), pos)
    k = _rotary(k.reshape(Bn, L, H, d), pos) * (d ** -0.5)
    v = v.reshape(Bn, L, H, d)
    log_gamma = jnp.log(1.0 - 2.0 ** (-5.0 - jnp.arange(H, dtype=jnp.float32)))
    idx = jnp.arange(C, dtype=jnp.float32)
    diff = idx[:, None] - idx[None, :]
    decay_in = jnp.where(diff >= 0, jnp.exp(log_gamma[:, None, None] * jnp.maximum(diff, 0.0)), 0.0).astype(dt)
    decay_q = jnp.exp(log_gamma[:, None] * (idx + 1.0)).astype(dt)
    decay_k = jnp.exp(log_gamma[:, None] * (C - 1.0 - idx)).astype(dt)
    decay_chunk = jnp.exp(log_gamma * C).astype(dt)
    qc = q.reshape(Bn, N, C, H, d)
    kc = k.reshape(Bn, N, C, H, d)
    vc = v.reshape(Bn, N, C, H, d)
    scores = jnp.einsum('bnihd,bnjhd->bnhij', qc, kc) * decay_in
    inner = jnp.einsum('bnhij,bnjhe->bnihe', scores, vc)
    chunk_kv = jnp.einsum('bnjhd,hj,bnjhe->nbhde', kc, decay_k, vc)

    def step(state, kv):
        return decay_chunk[None, :, None, None] * state + kv, state

    _, prev = lax.scan(step, jnp.zeros(chunk_kv.shape[1:], chunk_kv.dtype), chunk_kv)
    cross = jnp.einsum('bnihd,hi,nbhde->bnihe', qc, decay_q, prev)
    o = (inner + cross).reshape(Bn, L, H, d).astype(jnp.float32)
    mu = jnp.mean(o, -1, keepdims=True)
    var = jnp.mean(jnp.square(o - mu), -1, keepdims=True)
    o = ((o - mu) * lax.rsqrt(var + LN_EPS)).astype(dt).reshape(Bn, L, RET_DIM)
    return jax.nn.silu(g) * o


def _ssd(z, xbc, dt_raw, conv_w, conv_b, dt_bias, a_log, d_skip, norm_w):
    Bn, L, _ = z.shape
    G, R, P, NS, C = SSD_GROUPS, SSD_HEADS // SSD_GROUPS, HEAD_DIM, SSD_STATE, SSD_CHUNK
    NC = L // C
    xbc = lax.conv_general_dilated(xbc, conv_w[:, None, :], window_strides=(1,),
                                   padding=[(SSD_CONV - 1, 0)],
                                   dimension_numbers=('NWC', 'WIO', 'NWC'),
                                   feature_group_count=SSD_CONV_DIM) + conv_b
    xbc = jax.nn.silu(xbc)
    xs, Bm, Cm = jnp.split(xbc, [SSD_DIM, SSD_DIM + G * NS], -1)
    dt = jax.nn.softplus(dt_raw.astype(jnp.float32) + dt_bias)
    a = -jnp.exp(a_log.astype(jnp.float32)) * dt
    x = xs.reshape(Bn, NC, C, G, R, P)
    xdt = x * dt.reshape(Bn, NC, C, G, R, 1).astype(x.dtype)
    Bm = Bm.reshape(Bn, NC, C, G, NS)
    Cm = Cm.reshape(Bn, NC, C, G, NS)
    a_cs = jnp.cumsum(a.reshape(Bn, NC, C, G, R), axis=2).transpose(0, 1, 3, 4, 2)
    causal = jnp.tril(jnp.ones((C, C), dtype=bool))
    seg = a_cs[..., :, None] - a_cs[..., None, :]
    lmat = jnp.exp(jnp.where(causal, seg, -jnp.inf))
    cb = jnp.einsum('bclgn,bcsgn->bcgls', Cm, Bm)
    y_diag = jnp.einsum('bcgls,bcgrls,bcsgrp->bclgrp', cb, lmat, xdt)
    decay_states = jnp.exp(a_cs[..., -1:] - a_cs)
    states = jnp.einsum('bclgn,bcgrl,bclgrp->cbgrpn', Bm, decay_states, xdt).astype(jnp.float32)
    chunk_decay = jnp.exp(a_cs[..., -1]).transpose(1, 0, 2, 3)

    def step(S, inp):
        dec, st = inp
        return dec[..., None, None] * S + st, S

    _, prev = lax.scan(step, jnp.zeros(states.shape[1:], jnp.float32), (chunk_decay, states))
    y_off = jnp.einsum('bclgn,cbgrpn,bcgrl->bclgrp', Cm, prev, jnp.exp(a_cs))
    y = y_diag + y_off + x * d_skip.reshape(G, R, 1)
    y = y.reshape(Bn, L, SSD_DIM).astype(z.dtype)
    h = (y * jax.nn.silu(z)).astype(jnp.float32).reshape(Bn, L, G, SSD_DIM // G)
    h = h * lax.rsqrt(jnp.mean(jnp.square(h), -1, keepdims=True) + LN_EPS)
    return (h.reshape(Bn, L, SSD_DIM) * norm_w).astype(z.dtype)


def _t5_bucket(dist):
    exact = REL_BUCKETS // 2
    df = jnp.maximum(dist, 1).astype(jnp.float32)
    large = exact + (jnp.log(df / exact) / math.log(REL_MAX_DIST / exact) * (REL_BUCKETS - exact)).astype(jnp.int32)
    large = jnp.minimum(large, REL_BUCKETS - 1)
    return jnp.where(dist < exact, dist, large)


def _swa(q, k, v, rel_bias, sinks):
    Bn, L, _ = q.shape
    W, HK, GQ, d = WINDOW, SWA_KV_HEADS, SWA_HEADS // SWA_KV_HEADS, HEAD_DIM
    NB = L // W
    qb = q.reshape(Bn, NB, W, HK, GQ, d)
    kb = k.reshape(Bn, NB, W, HK, d)
    vb = v.reshape(Bn, NB, W, HK, d)

    def band(t):
        prev = jnp.concatenate([jnp.zeros_like(t[:, :1]), t[:, :-1]], axis=1)
        return jnp.concatenate([prev, t], axis=2)

    kband, vband = band(kb), band(vb)
    qi = jnp.arange(W)[:, None]
    kj = jnp.arange(2 * W)[None, :]
    dist = qi + W - kj
    in_band = (dist >= 0) & (dist < W)
    kpos = jnp.arange(NB)[:, None, None] * W - W + kj[None]
    valid = in_band[None] & (kpos >= 0)
    bias = rel_bias[_t5_bucket(jnp.clip(dist, 0, W - 1))]
    bias = bias.transpose(2, 0, 1).reshape(HK, GQ, W, 2 * W).astype(jnp.float32)
    logits = jnp.einsum('bnikgd,bnjkd->bnkgij', qb, kband).astype(jnp.float32) * (d ** -0.5) + bias
    logits = jnp.where(valid[None, :, None, None], logits, -jnp.inf)
    sink = sinks.astype(jnp.float32).reshape(HK, GQ, 1)
    m = jnp.maximum(jnp.max(logits, -1), sink)
    p = jnp.exp(logits - m[..., None])
    denom = jnp.sum(p, -1) + jnp.exp(sink - m)
    o = jnp.einsum('bnkgij,bnjkd->bnikgd', (p / denom[..., None]).astype(v.dtype), vband)
    return o.reshape(Bn, L, SWA_DIM)


def _swiglu(h, wg, wu, wd):
    return (jax.nn.silu(h @ wg) * (h @ wu)) @ wd


def _moe(h, w_router, b_router, wg, wu, wd):
    Bn, L, D = h.shape
    t = h.reshape(-1, D)
    logits = (t @ w_router).astype(jnp.float32) + b_router
    top_v, top_i = lax.top_k(logits, TOP_K)
    top_w = jax.nn.softmax(top_v, -1)
    gates = jnp.sum(jax.nn.one_hot(top_i, N_EXPERTS, dtype=jnp.float32) * top_w[..., None], axis=1)
    out = jnp.zeros_like(t)
    for e in range(N_EXPERTS):
        out = out + gates[:, e:e + 1].astype(t.dtype) * _swiglu(t, wg[e], wu[e], wd[e])
    return out.reshape(Bn, L, D)


def setup_inputs(seed: int = 0) -> dict:
    key = jax.random.key(seed)
    ks = jax.random.split(key, 32)
    f32 = jnp.float32
    D = D_MODEL

    def nrm(k, shape, scale):
        return jax.random.normal(k, shape, f32) * scale

    x = nrm(ks[0], (BATCH, SEQ, D), 1.0)
    c = nrm(ks[1], (BATCH, D), 1.0)
    offsets = jax.random.randint(ks[2], (BATCH, 1), 0, 1024, dtype=jnp.int32)
    positions = offsets + jnp.arange(SEQ, dtype=jnp.int32)[None, :]
    rel_bias = nrm(ks[3], (REL_BUCKETS, SWA_HEADS), 0.5)
    w_ada = nrm(ks[4], (DEPTH, D, 6 * D), 0.1 * D ** -0.5)
    b_ada = nrm(ks[5], (DEPTH, 6 * D), 0.01)
    w_in = nrm(ks[6], (DEPTH, D, IN_DIM), D ** -0.5)
    w_out = nrm(ks[7], (DEPTH, MIX_DIM, D), DEEPNORM_BETA * MIX_DIM ** -0.5)
    conv_w = nrm(ks[8], (DEPTH, SSD_CONV, SSD_CONV_DIM), SSD_CONV ** -0.5)
    conv_b = nrm(ks[9], (DEPTH, SSD_CONV_DIM), 0.01)
    u = jax.random.uniform(ks[10], (DEPTH, SSD_HEADS), f32)
    dt0 = jnp.exp(u * (math.log(0.1) - math.log(0.001)) + math.log(0.001))
    dt_bias = dt0 + jnp.log(-jnp.expm1(-dt0))
    a_log = jnp.log(jax.random.uniform(ks[11], (DEPTH, SSD_HEADS), f32, 1.0, 16.0))
    d_skip = 1.0 + nrm(ks[12], (DEPTH, SSD_HEADS), 0.1)
    ssd_norm_w = 1.0 + nrm(ks[13], (DEPTH, SSD_DIM), 0.02)
    sinks = nrm(ks[14], (DEPTH, SWA_HEADS), 0.5)
    ln_g = 1.0 + nrm(ks[15], (DEPTH, 2, D), 0.02)
    ln_b = nrm(ks[16], (DEPTH, 2, D), 0.02)
    ffn_w_gate = nrm(ks[17], (N_DENSE, D, D_FF), D ** -0.5)
    ffn_w_up = nrm(ks[18], (N_DENSE, D, D_FF), D ** -0.5)
    ffn_w_down = nrm(ks[19], (N_DENSE, D_FF, D), DEEPNORM_BETA * D_FF ** -0.5)
    router_w = nrm(ks[20], (N_MOE, D, N_EXPERTS), D ** -0.5)
    router_b = nrm(ks[21], (N_MOE, N_EXPERTS), 0.01)
    expert_w_gate = nrm(ks[22], (N_MOE, N_EXPERTS, D, D_FF_EXPERT), D ** -0.5)
    expert_w_up = nrm(ks[23], (N_MOE, N_EXPERTS, D, D_FF_EXPERT), D ** -0.5)
    expert_w_down = nrm(ks[24], (N_MOE, N_EXPERTS, D_FF_EXPERT, D), DEEPNORM_BETA * D_FF_EXPERT ** -0.5)
    return {'x': x, 'c': c, 'positions': positions, 'rel_bias': rel_bias,
            'w_ada': w_ada, 'b_ada': b_ada, 'w_in': w_in, 'w_out': w_out,
            'conv_w': conv_w, 'conv_b': conv_b, 'dt_bias': dt_bias, 'a_log': a_log,
            'd_skip': d_skip, 'ssd_norm_w': ssd_norm_w, 'sinks': sinks,
            'ln_g': ln_g, 'ln_b': ln_b,
            'ffn_w_gate': ffn_w_gate, 'ffn_w_up': ffn_w_up, 'ffn_w_down': ffn_w_down,
            'router_w': router_w, 'router_b': router_b,
            'expert_w_gate': expert_w_gate, 'expert_w_up': expert_w_up, 'expert_w_down': expert_w_down}


def reference(x, c, positions, rel_bias, w_ada, b_ada, w_in, w_out, conv_w, conv_b,
              dt_bias, a_log, d_skip, ssd_norm_w, sinks, ln_g, ln_b,
              ffn_w_gate, ffn_w_up, ffn_w_down, router_w, router_b,
              expert_w_gate, expert_w_up, expert_w_down):
    split_idx = np.cumsum(IN_SIZES)[:-1].tolist()
    for layer in range(DEPTH):
        mod = c @ w_ada[layer] + b_ada[layer]
        sh_a, sc_a, g_a, sh_f, sc_f, g_f = [m[:, None, :] for m in jnp.split(mod, 6, -1)]
        h = x * (1.0 + sc_a) + sh_a
        u = h @ w_in[layer]
        rq, rk, rv, rg, sz, sxbc, sdt, aq, ak, av = jnp.split(u, split_idx, -1)
        y_ret = _retention(rq, rk, rv, rg, positions)
        y_ssd = _ssd(sz, sxbc, sdt, conv_w[layer], conv_b[layer], dt_bias[layer],
                     a_log[layer], d_skip[layer], ssd_norm_w[layer])
        y_swa = _swa(aq, ak, av, rel_bias, sinks[layer])
        mix = jnp.concatenate([y_ret, y_ssd, y_swa], -1) @ w_out[layer]
        x = _layer_norm(DEEPNORM_ALPHA * x + (1.0 + g_a) * mix, ln_g[layer, 0], ln_b[layer, 0])
        h = x * (1.0 + sc_f) + sh_f
        if layer % 2 == 0:
            i = layer // 2
            f = _swiglu(h, ffn_w_gate[i], ffn_w_up[i], ffn_w_down[i])
        else:
            i = layer // 2
            f = _moe(h, router_w[i], router_b[i], expert_w_gate[i], expert_w_up[i], expert_w_down[i])
        x = _layer_norm(DEEPNORM_ALPHA * x + (1.0 + g_f) * f, ln_g[layer, 1], ln_b[layer, 1])
    return x
```

```python
import functools
import math

import numpy as np
import jax
import jax.numpy as jnp
from jax import lax
from jax.experimental import pallas as pl
from jax.experimental.pallas import tpu as pltpu

F32 = jnp.float32
BF16 = jnp.bfloat16
I32 = jnp.int32

HEAD_DIM = 64
CHUNK = 128
RET_HEADS = 4
SSD_HEADS = 8
SSD_GROUPS = 2
SSD_STATE = 64
SSD_CONV = 4
SWA_HEADS = 4
SWA_KV_HEADS = 2
REL_BUCKETS = 32
N_EXPERTS = 8
LN_EPS = 1e-5
LANES = 128
MASK_VALUE = -1e30

VMEM_LIMIT = 56 * 1024 * 1024


def _cparams(*sem):
    return pltpu.CompilerParams(dimension_semantics=sem, vmem_limit_bytes=VMEM_LIMIT)


def _silu(v):
    return v * (1.0 / (1.0 + jnp.exp(-v)))


def _softplus(v):
    return jnp.maximum(v, 0.0) + jnp.log(1.0 + jnp.exp(-jnp.abs(v)))


def _dot(a, b):
    return jnp.dot(a, b, preferred_element_type=F32)


def _dot_nt(a, b):
    return lax.dot_general(a, b, (((1,), (1,)), ((), ())), preferred_element_type=F32)


def _split3(v):
    h1 = v.astype(BF16)
    r1 = v - h1.astype(F32)
    h2 = r1.astype(BF16)
    r2 = r1 - h2.astype(F32)
    return h1, h2, r2.astype(BF16)


def _dot3(v, m):
    h1, h2, h3 = _split3(v)
    return _dot(h1, m) + _dot(h2, m) + _dot(h3, m)


def _dot3_left(m, v):
    h1, h2, h3 = _split3(v)
    return _dot(m, h1) + _dot(m, h2) + _dot(m, h3)


def _dot2(v, m):
    h1 = v.astype(BF16)
    h2 = (v - h1.astype(F32)).astype(BF16)
    return _dot(h1, m) + _dot(h2, m)


def _layer_norm(r, g, b):
    mu = jnp.mean(r, -1, keepdims=True)
    d = r - mu
    var = jnp.mean(d * d, -1, keepdims=True)
    return d * lax.rsqrt(var + LN_EPS) * g + b


def _ada_kernel(c_ref, w_ref, b_ref, o_ref):
    o_ref[0] = jnp.dot(c_ref[...], w_ref[0], preferred_element_type=F32,
                       precision=lax.Precision.HIGHEST) + b_ref[0]


def _ada_mod(c, w_ada, b_ada):
    depth, d, d6 = w_ada.shape
    nb = c.shape[0]
    rows = 8
    c_pad = jnp.zeros((rows, d), F32).at[:nb].set(c)
    out = pl.pallas_call(
        _ada_kernel,
        out_shape=jax.ShapeDtypeStruct((depth, rows, d6), F32),
        grid=(depth, d6 // d),
        in_specs=[pl.BlockSpec((rows, d), lambda l, j: (0, 0)),
                  pl.BlockSpec((1, d, d), lambda l, j: (l, 0, j)),
                  pl.BlockSpec((1, 1, d), lambda l, j: (l, 0, j))],
        out_specs=pl.BlockSpec((1, rows, d), lambda l, j: (l, 0, j)),
        compiler_params=_cparams("arbitrary", "arbitrary"),
        name="ada_mod",
    )(c_pad, w_ada, b_ada.reshape(depth, 1, d6))
    return out[:, :nb].reshape(depth, nb, 6, d)


def _rotary_kernel(pos_ref, cos_ref, sin_ref):
    half = HEAD_DIM // 2
    lane = lax.broadcasted_iota(I32, (1, LANES), 1)
    jj = lane % HEAD_DIM
    idx = (jj % half).astype(F32)
    inv = jnp.exp(-math.log(10000.0) * idx / half)
    ang = pos_ref[0].astype(F32) * inv
    cos_ref[0] = jnp.cos(ang)
    sin_ref[0] = jnp.where(jj < half, -1.0, 1.0) * jnp.sin(ang)


def _rotary_tables(positions, tl):
    nb, L = positions.shape
    pos = positions.reshape(nb, L, 1)
    return pl.pallas_call(
        _rotary_kernel,
        out_shape=[jax.ShapeDtypeStruct((nb, L, LANES), F32)] * 2,
        grid=(nb, L // tl),
        in_specs=[pl.BlockSpec((1, tl, 1), lambda b, i: (b, i, 0))],
        out_specs=[pl.BlockSpec((1, tl, LANES), lambda b, i: (b, i, 0))] * 2,
        compiler_params=_cparams("arbitrary", "arbitrary"),
        name="rotary_tables",
    )(pos)


def _swa_bias_kernel(rb_ref, bucket_ref, band_ref, o_ref):
    bucket = bucket_ref[...]
    band = band_ref[...]
    for h in range(SWA_HEADS):
        acc = jnp.zeros(bucket.shape, F32)
        for b in range(REL_BUCKETS):
            acc = jnp.where(bucket == b, rb_ref[b, h], acc)
        o_ref[h] = jnp.where(band > 0, acc, MASK_VALUE)


def _t5_bucket(dist):
    exact = REL_BUCKETS // 2
    df = jnp.maximum(dist, 1).astype(F32)
    large = exact + (jnp.log(df / exact) / math.log(CHUNK / exact) * (REL_BUCKETS - exact)).astype(I32)
    large = jnp.minimum(large, REL_BUCKETS - 1)
    return jnp.where(dist < exact, dist, large)


def _swa_bias_table(rel_bias):
    W = CHUNK
    qi = jnp.arange(W)[:, None]
    kj = jnp.arange(2 * W)[None, :]
    dist = qi + W - kj
    band = ((dist >= 0) & (dist < W)).astype(I32)
    bucket = _t5_bucket(jnp.clip(dist, 0, W - 1)).astype(I32)
    return pl.pallas_call(
        _swa_bias_kernel,
        out_shape=jax.ShapeDtypeStruct((SWA_HEADS, W, 2 * W), F32),
        in_specs=[pl.BlockSpec(memory_space=pltpu.SMEM),
                  pl.BlockSpec(memory_space=pltpu.VMEM),
                  pl.BlockSpec(memory_space=pltpu.VMEM)],
        out_specs=pl.BlockSpec(memory_space=pltpu.VMEM),
        name="swa_bias_table",
    )(rel_bias, bucket, band)


def _in_proj_kernel(x_ref, mod_ref, w_ref, ret_ref, ssd_ref, swa_ref, *, widths):
    sh = mod_ref[0, 0, 0:1, :]
    sc = mod_ref[0, 0, 1:2, :]
    h = (x_ref[0] * (1.0 + sc) + sh).astype(BF16)
    off = 0
    for ref, width in zip((ret_ref, ssd_ref, swa_ref), widths):
        step = 512 if width % 512 == 0 else 128
        for j in range(0, width, step):
            ref[0, :, j:j + step] = _dot(h, w_ref[:, off + j:off + j + step])
        off += width


def _in_proj(x, mod_l, w_cat, widths, tm):
    nb, L, d = x.shape
    return pl.pallas_call(
        functools.partial(_in_proj_kernel, widths=widths),
        out_shape=[jax.ShapeDtypeStruct((nb, L, w), F32) for w in widths],
        grid=(nb, L // tm),
        in_specs=[pl.BlockSpec((1, tm, d), lambda b, i: (b, i, 0)),
                  pl.BlockSpec((1, 1, 6, d), lambda b, i: (0, b, 0, 0)),
                  pl.BlockSpec(w_cat.shape, lambda b, i: (0, 0))],
        out_specs=[pl.BlockSpec((1, tm, w), lambda b, i: (b, i, 0)) for w in widths],
        compiler_params=_cparams("arbitrary", "arbitrary"),
        name="in_proj",
    )(x, mod_l, w_cat)


def _head_lane_mask(width, head):
    lane = lax.broadcasted_iota(I32, (1, width), 1)
    return (lane // HEAD_DIM) == head


def _rotate_half(t):
    width = t.shape[-1]
    lane = lax.broadcasted_iota(I32, (1, width), 1)
    half = HEAD_DIM // 2
    fwd = pltpu.roll(t, width - half, axis=1)
    bwd = pltpu.roll(t, half, axis=1)
    return jnp.where((lane % HEAD_DIM) < half, fwd, bwd)


def _retention_kernel(u_ref, cos_ref, sin_ref, din_ref, dq_ref, dk_ref, dc_ref,
                      bmask_ref, avg_ref, o_ref, state_ref, *, n_chunks):
    rd = RET_HEADS * HEAD_DIM

    @pl.when(pl.program_id(1) == 0)
    def _():
        state_ref[...] = jnp.zeros_like(state_ref)

    def chunk(ci, carry):
        r0 = pl.multiple_of(ci * CHUNK, CHUNK)
        rows = pl.ds(r0, CHUNK)
        cos = cos_ref[0, rows, :]
        sin = sin_ref[0, rows, :]
        cos2 = jnp.concatenate([cos, cos], axis=1)
        sin2 = jnp.concatenate([sin, sin], axis=1)
        q = u_ref[0, rows, 0:rd]
        k = u_ref[0, rows, rd:2 * rd]
        v = u_ref[0, rows, 2 * rd:3 * rd]
        g = u_ref[0, rows, 3 * rd:4 * rd]
        qr = q * cos2 + _rotate_half(q) * sin2
        kr = (k * cos2 + _rotate_half(k) * sin2) * (HEAD_DIM ** -0.5)
        vb = v.astype(BF16)
        q4 = jnp.concatenate(
            [jnp.where(_head_lane_mask(rd, h), qr, 0.0) for h in range(RET_HEADS)],
            axis=0).astype(BF16)
        scores = _dot_nt(q4, kr.astype(BF16)) * din_ref[...]
        res = _dot(scores.astype(BF16), vb)
        inner = jnp.zeros((CHUNK, rd), F32)
        for h in range(RET_HEADS):
            inner = inner + jnp.where(_head_lane_mask(rd, h),
                                      res[h * CHUNK:(h + 1) * CHUNK], 0.0)
        state = state_ref[...]
        cross = _dot((qr * dq_ref[...]).astype(BF16), state.astype(BF16))
        o = inner + cross
        kd_t = (kr * dk_ref[...]).T.astype(BF16)
        kv = _dot(kd_t, vb)
        state_ref[...] = dc_ref[...] * state + bmask_ref[...] * kv
        mu = _dot2(o, avg_ref[...])
        dev = o - mu
        var = _dot2(dev * dev, avg_ref[...])
        on = dev * lax.rsqrt(var + LN_EPS)
        o_ref[0, rows, :] = (_silu(g) * on).astype(o_ref.dtype)
        return carry

    lax.fori_loop(0, n_chunks, chunk, 0)


def _retention_tables():
    H, d, C = RET_HEADS, HEAD_DIM, CHUNK
    log_gamma = jnp.log(1.0 - 2.0 ** (-5.0 - jnp.arange(H, dtype=F32)))
    idx = jnp.arange(C, dtype=F32)
    diff = idx[:, None] - idx[None, :]
    decay_in = jnp.where(diff >= 0, jnp.exp(log_gamma[:, None, None] * jnp.maximum(diff, 0.0)), 0.0)
    decay_q = jnp.exp(log_gamma[:, None] * (idx + 1.0))
    decay_k = jnp.exp(log_gamma[:, None] * (C - 1.0 - idx))
    decay_chunk = jnp.exp(log_gamma * C)
    din = decay_in.reshape(H * C, C)
    dq = jnp.repeat(decay_q.T, d, axis=1)
    dk = jnp.repeat(decay_k.T, d, axis=1)
    dc = jnp.repeat(decay_chunk, d)[None, :]
    head = jnp.arange(H * d) // d
    bmask = (head[:, None] == head[None, :]).astype(F32)
    avg = (bmask / d).astype(BF16)
    return din, dq, dk, dc, bmask, avg


def _retention(u_ret, cos_t, sin_t, tl):
    nb, L, _ = u_ret.shape
    rd = RET_HEADS * HEAD_DIM
    tables = _retention_tables()
    const = lambda a: pl.BlockSpec(a.shape, lambda b, i: (0,) * a.ndim)
    return pl.pallas_call(
        functools.partial(_retention_kernel, n_chunks=tl // CHUNK),
        out_shape=jax.ShapeDtypeStruct((nb, L, rd), BF16),
        grid=(nb, L // tl),
        in_specs=[pl.BlockSpec((1, tl, 4 * rd), lambda b, i: (b, i, 0)),
                  pl.BlockSpec((1, tl, LANES), lambda b, i: (b, i, 0)),
                  pl.BlockSpec((1, tl, LANES), lambda b, i: (b, i, 0))]
                 + [const(t) for t in tables],
        out_specs=pl.BlockSpec((1, tl, rd), lambda b, i: (b, i, 0)),
        scratch_shapes=[pltpu.VMEM((rd, rd), F32)],
        compiler_params=_cparams("arbitrary", "arbitrary"),
        name="retention",
    )(u_ret, cos_t, sin_t, *tables)


def _swa_kernel(u_ref, bias_ref, sink_ref, o_ref, kprev_ref, vprev_ref, *, n_chunks):
    W = CHUNK
    qd = SWA_HEADS * HEAD_DIM
    kvd = SWA_KV_HEADS * HEAD_DIM

    @pl.when(pl.program_id(1) == 0)
    def _():
        kprev_ref[...] = jnp.zeros_like(kprev_ref)
        vprev_ref[...] = jnp.zeros_like(vprev_ref)

    first_step = pl.program_id(1) == 0
    lane = lax.broadcasted_iota(I32, (1, LANES), 1)
    low = lane < HEAD_DIM
    col = lax.broadcasted_iota(I32, (1, 2 * W), 1)

    def chunk(ci, carry):
        r0 = pl.multiple_of(ci * W, W)
        rows = pl.ds(r0, W)
        qa = u_ref[0, rows, 0:LANES]
        qb = u_ref[0, rows, LANES:qd]
        k = u_ref[0, rows, qd:qd + kvd].astype(BF16)
        v = u_ref[0, rows, qd + kvd:qd + 2 * kvd].astype(BF16)
        q4 = jnp.concatenate([jnp.where(low, qa, 0.0), jnp.where(low, 0.0, qa),
                              jnp.where(low, qb, 0.0), jnp.where(low, 0.0, qb)],
                             axis=0).astype(BF16)
        kband = jnp.concatenate([kprev_ref[...], k], axis=0)
        vband = jnp.concatenate([vprev_ref[...], v], axis=0)
        logits = _dot_nt(q4, kband) * (HEAD_DIM ** -0.5) + bias_ref[...]
        no_prev = jnp.logical_and(first_step, ci == 0)
        logits = jnp.where(jnp.logical_and(no_prev, col < W), MASK_VALUE, logits)
        sink = sink_ref[...]
        m = jnp.maximum(jnp.max(logits, -1, keepdims=True), sink)
        p = jnp.exp(logits - m)
        denom = jnp.sum(p, -1, keepdims=True) + jnp.exp(sink - m)
        res = _dot(p.astype(BF16), vband) / denom
        out_a = jnp.where(low, res[0:W], res[W:2 * W])
        out_b = jnp.where(low, res[2 * W:3 * W], res[3 * W:4 * W])
        o_ref[0, rows, 0:LANES] = out_a.astype(o_ref.dtype)
        o_ref[0, rows, LANES:qd] = out_b.astype(o_ref.dtype)
        kprev_ref[...] = k
        vprev_ref[...] = v
        return carry

    lax.fori_loop(0, n_chunks, chunk, 0)


SWA_HEAD_ORDER = (0, 2, 1, 3)


def _swa(u_swa, bias_tab, sinks_l, tl):
    nb, L, _ = u_swa.shape
    W = CHUNK
    qd = SWA_HEADS * HEAD_DIM
    order = jnp.array(SWA_HEAD_ORDER)
    bias_stacked = bias_tab[order].reshape(SWA_HEADS * W, 2 * W)
    sink_col = jnp.repeat(sinks_l.astype(F32)[order], W)[:, None]
    return pl.pallas_call(
        functools.partial(_swa_kernel, n_chunks=tl // W),
        out_shape=jax.ShapeDtypeStruct((nb, L, qd), BF16),
        grid=(nb, L // tl),
        in_specs=[pl.BlockSpec((1, tl, u_swa.shape[-1]), lambda b, i: (b, i, 0)),
                  pl.BlockSpec(bias_stacked.shape, lambda b, i: (0, 0)),
                  pl.BlockSpec(sink_col.shape, lambda b, i: (0, 0))],
        out_specs=pl.BlockSpec((1, tl, qd), lambda b, i: (b, i, 0)),
        scratch_shapes=[pltpu.VMEM((W, LANES), BF16), pltpu.VMEM((W, LANES), BF16)],
        compiler_params=_cparams("arbitrary", "arbitrary"),
        name="swa",
    )(u_swa, bias_stacked, sink_col)


def _ssd_kernel(u_ref, cw_ref, cb_ref, dtb_c_ref, alog_c_ref, dtb_x_ref, alog_x_ref,
                dskip_ref, nw_ref, tril_ref, triu_ref, eexp_ref, bsel_ref, gmask_ref,
                o_ref, state_ref, ext_ref, *, n_chunks):
    C = CHUNK
    sd = SSD_HEADS * HEAD_DIM
    gn = SSD_GROUPS * SSD_STATE
    cd = sd + 2 * gn
    heads_per_group = SSD_HEADS // SSD_GROUPS
    heads_per_slab = 2 * LANES // HEAD_DIM

    @pl.when(pl.program_id(1) == 0)
    def _():
        state_ref[...] = jnp.zeros_like(state_ref)
        ext_ref[0:8, :] = jnp.zeros((8, cd), F32)

    lane = lax.broadcasted_iota(I32, (1, LANES), 1)
    low = lane < SSD_STATE
    row_i = lax.broadcasted_iota(I32, (C, C), 0)
    col_i = lax.broadcasted_iota(I32, (C, C), 1)
    causal = row_i >= col_i
    neg_a_c = -jnp.exp(alog_c_ref[...])
    neg_a_x = -jnp.exp(alog_x_ref[...])

    def chunk(ci, carry):
        r0 = pl.multiple_of(ci * C, C)
        rows = pl.ds(r0, C)
        z = u_ref[0, rows, 0:sd]
        xbc_raw = u_ref[0, rows, sd:sd + cd]
        dt_raw = u_ref[0, rows, sd + cd:sd + cd + LANES]

        ext_ref[8:8 + C, :] = xbc_raw
        conv = cb_ref[...] + cw_ref[SSD_CONV - 1:SSD_CONV, :] * xbc_raw
        for w in range(SSD_CONV - 1):
            shift = SSD_CONV - 1 - w
            conv = conv + cw_ref[w:w + 1, :] * ext_ref[8 - shift:8 - shift + C, :]
        ext_ref[0:8, :] = xbc_raw[C - 8:C, :]
        xbc = _silu(conv)
        xs = xbc[:, 0:sd]
        bm = xbc[:, sd:sd + gn]
        cm = xbc[:, sd + gn:cd]

        dt_c = _softplus(dt_raw + dtb_c_ref[...])
        a_c = neg_a_c * dt_c
        dt_x = _softplus(_dot3(dt_raw, eexp_ref[...]) + dtb_x_ref[...])
        a_x = neg_a_x * dt_x
        acs_x = _dot3_left(tril_ref[...], a_x)
        acs_c = _dot3_left(tril_ref[...], a_c)
        acs_t = _dot3(a_c.T, triu_ref[...])
        acs_b = _dot3(acs_c, bsel_ref[...])
        xdt = xs * dt_x

        c2 = jnp.concatenate([jnp.where(low, cm, 0.0), jnp.where(low, 0.0, cm)],
                             axis=0).astype(BF16)
        cb = _dot_nt(c2, bm.astype(BF16))
        xdt_b = xdt.astype(BF16)
        y_diag = []
        for slab in range(sd // (2 * LANES)):
            ms = []
            for hh in range(heads_per_slab):
                h = slab * heads_per_slab + hh
                g = h // heads_per_group
                seg = acs_b[:, h * LANES:(h + 1) * LANES] - acs_t[h:h + 1, :]
                lmat = jnp.exp(jnp.where(causal, seg, MASK_VALUE))
                ms.append((cb[g * C:(g + 1) * C] * lmat).astype(BF16))
            res = _dot(jnp.concatenate(ms, axis=0),
                       xdt_b[:, slab * 2 * LANES:(slab + 1) * 2 * LANES])
            acc = jnp.zeros((C, 2 * LANES), F32)
            for hh in range(heads_per_slab):
                acc = acc + jnp.where(_head_lane_mask(2 * LANES, hh),
                                      res[hh * C:(hh + 1) * C], 0.0)
            y_diag.append(acc)
        y_diag = jnp.concatenate(y_diag, axis=1)

        state = state_ref[...]
        y_off = _dot(cm.astype(BF16), state.astype(BF16)) * jnp.exp(acs_x)
        last = acs_x[C - 1:C, :]
        dec = jnp.exp(last - acs_x)
        new = _dot(bm.T.astype(BF16), (xdt * dec).astype(BF16))
        state_ref[...] = jnp.exp(last) * state + gmask_ref[...] * new

        y = y_diag + y_off + xs * dskip_ref[...]
        hgate = y * _silu(z)
        gw = sd // SSD_GROUPS
        for g in range(SSD_GROUPS):
            hg = hgate[:, g * gw:(g + 1) * gw]
            ms_ = jnp.mean(hg * hg, -1, keepdims=True)
            o_ref[0, rows, g * gw:(g + 1) * gw] = (
                hg * lax.rsqrt(ms_ + LN_EPS) * nw_ref[:, g * gw:(g + 1) * gw]).astype(o_ref.dtype)
        return carry

    lax.fori_loop(0, n_chunks, chunk, 0)


def _ssd_tables():
    C = CHUNK
    sd = SSD_HEADS * HEAD_DIM
    gn = SSD_GROUPS * SSD_STATE
    t = jnp.arange(C)
    tril = (t[:, None] >= t[None, :]).astype(BF16)
    triu = (t[:, None] <= t[None, :]).astype(BF16)
    r = jnp.arange(LANES)[:, None]
    eexp = ((r == (jnp.arange(sd)[None, :] // HEAD_DIM)) & (r < SSD_HEADS)).astype(BF16)
    bsel = ((r == (jnp.arange(SSD_HEADS * LANES)[None, :] // LANES)) & (r < SSD_HEADS)).astype(BF16)
    heads_per_group = SSD_HEADS // SSD_GROUPS
    row_g = jnp.arange(gn)[:, None] // SSD_STATE
    col_g = (jnp.arange(sd)[None, :] // HEAD_DIM) // heads_per_group
    gmask = (row_g == col_g).astype(F32)
    return tril, triu, eexp, bsel, gmask


def _ssd(u_ssd, conv_w, conv_b, dt_bias, a_log, d_skip, norm_w, tl):
    nb, L, width = u_ssd.shape
    sd = SSD_HEADS * HEAD_DIM
    cd = conv_w.shape[-1]
    pad = lambda v: jnp.zeros((1, LANES), F32).at[0, :SSD_HEADS].set(v)
    rep = lambda v: jnp.repeat(v, HEAD_DIM)[None, :]
    params = (conv_w, conv_b[None, :], pad(dt_bias), pad(a_log), rep(dt_bias), rep(a_log),
              rep(d_skip), norm_w[None, :])
    tables = _ssd_tables()
    const = lambda a: pl.BlockSpec(a.shape, lambda b, i: (0,) * a.ndim)
    return pl.pallas_call(
        functools.partial(_ssd_kernel, n_chunks=tl // CHUNK),
        out_shape=jax.ShapeDtypeStruct((nb, L, sd), BF16),
        grid=(nb, L // tl),
        in_specs=[pl.BlockSpec((1, tl, width), lambda b, i: (b, i, 0))]
                 + [const(t) for t in params + tables],
        out_specs=pl.BlockSpec((1, tl, sd), lambda b, i: (b, i, 0)),
        scratch_shapes=[pltpu.VMEM((SSD_GROUPS * SSD_STATE, sd), F32),
                        pltpu.VMEM((8 + CHUNK, cd), F32)],
        compiler_params=_cparams("arbitrary", "arbitrary"),
        name="ssd",
    )(u_ssd, *params, *tables)


def _out_proj_kernel(x_ref, yr_ref, ys_ref, ya_ref, mod_ref, wr_ref, ws_ref, wa_ref,
                     g_ref, b_ref, o_ref, *, alpha):
    mix = (_dot(yr_ref[0], wr_ref[...]) + _dot(ys_ref[0], ws_ref[...])
           + _dot(ya_ref[0], wa_ref[...]))
    gate = mod_ref[0, 0, 2:3, :]
    r = alpha * x_ref[0] + (1.0 + gate) * mix
    o_ref[0] = _layer_norm(r, g_ref[...], b_ref[...])


def _out_proj(x, y_ret, y_ssd, y_swa, mod_l, w_r, w_s, w_a, ln_g, ln_b, alpha, tm):
    nb, L, d = x.shape
    tok = lambda w: pl.BlockSpec((1, tm, w), lambda b, i: (b, i, 0))
    const = lambda a: pl.BlockSpec(a.shape, lambda b, i: (0,) * a.ndim)
    return pl.pallas_call(
        functools.partial(_out_proj_kernel, alpha=alpha),
        out_shape=jax.ShapeDtypeStruct((nb, L, d), F32),
        grid=(nb, L // tm),
        in_specs=[tok(d), tok(y_ret.shape[-1]), tok(y_ssd.shape[-1]), tok(y_swa.shape[-1]),
                  pl.BlockSpec((1, 1, 6, d), lambda b, i: (0, b, 0, 0)),
                  const(w_r), const(w_s), const(w_a), const(ln_g), const(ln_b)],
        out_specs=tok(d),
        compiler_params=_cparams("arbitrary", "arbitrary"),
        name="out_proj_ln",
    )(x, y_ret, y_ssd, y_swa, mod_l, w_r, w_s, w_a, ln_g, ln_b)


FF_CHUNK = 256


def _dense_ffn_kernel(x_ref, mod_ref, wg_ref, wu_ref, wd_ref, g_ref, b_ref, o_ref, *, alpha):
    x = x_ref[0]
    sh = mod_ref[0, 0, 3:4, :]
    sc = mod_ref[0, 0, 4:5, :]
    gate = mod_ref[0, 0, 5:6, :]
    h = (x * (1.0 + sc) + sh).astype(BF16)
    dff = wg_ref.shape[-1]
    acc = jnp.zeros(x.shape, F32)
    for j in range(0, dff, FF_CHUNK):
        gj = _dot(h, wg_ref[:, j:j + FF_CHUNK])
        uj = _dot(h, wu_ref[:, j:j + FF_CHUNK])
        acc = acc + _dot((_silu(gj) * uj).astype(BF16), wd_ref[j:j + FF_CHUNK, :])
    r = alpha * x + (1.0 + gate) * acc
    o_ref[0] = _layer_norm(r, g_ref[...], b_ref[...])


def _dense_ffn(x, mod_l, wg, wu, wd, ln_g, ln_b, alpha, tm):
    nb, L, d = x.shape
    const = lambda a: pl.BlockSpec(a.shape, lambda b, i: (0,) * a.ndim,
                                   pipeline_mode=pl.Buffered(1))
    return pl.pallas_call(
        functools.partial(_dense_ffn_kernel, alpha=alpha),
        out_shape=jax.ShapeDtypeStruct((nb, L, d), F32),
        grid=(nb, L // tm),
        in_specs=[pl.BlockSpec((1, tm, d), lambda b, i: (b, i, 0)),
                  pl.BlockSpec((1, 1, 6, d), lambda b, i: (0, b, 0, 0)),
                  const(wg), const(wu), const(wd), const(ln_g), const(ln_b)],
        out_specs=pl.BlockSpec((1, tm, d), lambda b, i: (b, i, 0)),
        compiler_params=_cparams("arbitrary", "arbitrary"),
        name="dense_ffn_ln",
    )(x, mod_l, wg, wu, wd, ln_g, ln_b)


def _router_kernel(x_ref, mod_ref, wr_ref, br_ref, trils_ref, h_ref, route_ref, cnt_ref,
                   base_ref):
    first = jnp.logical_and(pl.program_id(0) == 0, pl.program_id(1) == 0)

    @pl.when(first)
    def _():
        base_ref[...] = jnp.zeros_like(base_ref)

    sh = mod_ref[0, 0, 3:4, :]
    sc = mod_ref[0, 0, 4:5, :]
    h = x_ref[0] * (1.0 + sc) + sh
    h_ref[0] = h
    logits = jnp.dot(h, wr_ref[...], preferred_element_type=F32,
                     precision=lax.Precision.HIGHEST) + br_ref[...]
    lane = lax.broadcasted_iota(I32, logits.shape, 1).astype(F32)
    logits = jnp.where(lane < N_EXPERTS, logits, MASK_VALUE)
    v1 = jnp.max(logits, -1, keepdims=True)
    e1 = jnp.min(jnp.where(logits == v1, lane, float(LANES)), -1, keepdims=True)
    rest = jnp.where(lane == e1, MASK_VALUE, logits)
    v2 = jnp.max(rest, -1, keepdims=True)
    e2 = jnp.min(jnp.where(rest == v2, lane, float(LANES)), -1, keepdims=True)
    t = jnp.exp(v2 - v1)
    w1 = 1.0 / (1.0 + t)
    w2 = t / (1.0 + t)
    hot1 = (lane == e1).astype(F32)
    hot2 = (lane == e2).astype(F32)
    both = hot1 + hot2
    base = base_ref[0:1, :]
    before = _dot(trils_ref[...], both.astype(BF16)) + base
    rank1 = jnp.sum(hot1 * before, -1, keepdims=True)
    rank2 = jnp.sum(hot2 * before, -1, keepdims=True)
    total = base + jnp.sum(both, 0, keepdims=True)
    base_ref[0:1, :] = total
    cnt_ref[...] = jnp.broadcast_to(total, cnt_ref.shape)
    route = jnp.where(lane == 0, e1, 0.0)
    route = jnp.where(lane == 1, e2, route)
    route = jnp.where(lane == 2, rank1, route)
    route = jnp.where(lane == 3, rank2, route)
    route = jnp.where(lane == 4, w1, route)
    route = jnp.where(lane == 5, w2, route)
    route_ref[0] = route


def _router(x, mod_l, w_router, b_router, tm):
    nb, L, d = x.shape
    wr = jnp.zeros((d, LANES), F32).at[:, :N_EXPERTS].set(w_router)
    br = jnp.zeros((1, LANES), F32).at[0, :N_EXPERTS].set(b_router)
    t = jnp.arange(tm)
    tril_strict = (t[:, None] > t[None, :]).astype(BF16)
    const = lambda a: pl.BlockSpec(a.shape, lambda b, i: (0,) * a.ndim)
    return pl.pallas_call(
        _router_kernel,
        out_shape=[jax.ShapeDtypeStruct((nb, L, d), F32),
                   jax.ShapeDtypeStruct((nb, L, LANES), F32),
                   jax.ShapeDtypeStruct((8, LANES), F32)],
        grid=(nb, L // tm),
        in_specs=[pl.BlockSpec((1, tm, d), lambda b, i: (b, i, 0)),
                  pl.BlockSpec((1, 1, 6, d), lambda b, i: (0, b, 0, 0)),
                  const(wr), const(br), const(tril_strict)],
        out_specs=[pl.BlockSpec((1, tm, d), lambda b, i: (b, i, 0)),
                   pl.BlockSpec((1, tm, LANES), lambda b, i: (b, i, 0)),
                   pl.BlockSpec((8, LANES), lambda b, i: (0, 0))],
        scratch_shapes=[pltpu.VMEM((8, LANES), F32)],
        compiler_params=_cparams("arbitrary", "arbitrary"),
        name="moe_router",
    )(x, mod_l, wr, br, tril_strict)


def _dispatch_kernel(dest_ref, h_ref, xs_in_ref, xs_ref, sem, *, td):
    del xs_in_ref
    base = pl.program_id(0) * td

    def row_copy(t, k):
        return pltpu.make_async_copy(h_ref.at[pl.ds(t, 1)],
                                     xs_ref.at[pl.ds(dest_ref[2 * t + k], 1)], sem)

    def issue(j, carry):
        row_copy(base + j, 0).start()
        row_copy(base + j, 1).start()
        return carry

    def drain(j, carry):
        row_copy(base + j, 0).wait()
        row_copy(base + j, 1).wait()
        return carry

    lax.fori_loop(0, td, issue, 0)
    lax.fori_loop(0, td, drain, 0)


def _dispatch(h_flat, dest, n_rows, td):
    T, d = h_flat.shape
    xs0 = jnp.zeros((n_rows, d), h_flat.dtype)
    return pl.pallas_call(
        functools.partial(_dispatch_kernel, td=td),
        out_shape=jax.ShapeDtypeStruct((n_rows, d), h_flat.dtype),
        grid_spec=pltpu.PrefetchScalarGridSpec(
            num_scalar_prefetch=1,
            grid=(T // td,),
            in_specs=[pl.BlockSpec(memory_space=pl.ANY), pl.BlockSpec(memory_space=pl.ANY)],
            out_specs=pl.BlockSpec(memory_space=pl.ANY),
            scratch_shapes=[pltpu.SemaphoreType.DMA]),
        input_output_aliases={2: 0},
        compiler_params=_cparams("arbitrary"),
        name="moe_dispatch",
    )(dest, h_flat, xs0)


EXPERT_FF_CHUNK = 512


def _expert_kernel(te_ref, nu_ref, xs_ref, wg_ref, wu_ref, wd_ref, y_ref):
    @pl.when(pl.program_id(0) < nu_ref[0])
    def _():
        h = xs_ref[...].astype(BF16)
        dff = wg_ref.shape[-1]
        acc = jnp.zeros(y_ref.shape, F32)
        for j in range(0, dff, EXPERT_FF_CHUNK):
            gj = _dot(h, wg_ref[0, :, j:j + EXPERT_FF_CHUNK])
            uj = _dot(h, wu_ref[0, :, j:j + EXPERT_FF_CHUNK])
            acc = acc + _dot((_silu(gj) * uj).astype(BF16), wd_ref[0, j:j + EXPERT_FF_CHUNK, :])
        y_ref[...] = acc

    @pl.when(pl.program_id(0) >= nu_ref[0])
    def _():
        y_ref[...] = jnp.zeros_like(y_ref)


def _expert_ffn(xs, tile_expert, n_used, wg, wu, wd, tme):
    n_rows, d = xs.shape
    dff = wg.shape[-1]
    return pl.pallas_call(
        _expert_kernel,
        out_shape=jax.ShapeDtypeStruct((n_rows, d), F32),
        grid_spec=pltpu.PrefetchScalarGridSpec(
            num_scalar_prefetch=2,
            grid=(n_rows // tme,),
            in_specs=[pl.BlockSpec((tme, d), lambda i, te, nu: (i, 0)),
                      pl.BlockSpec((1, d, dff), lambda i, te, nu: (te[i], 0, 0)),
                      pl.BlockSpec((1, d, dff), lambda i, te, nu: (te[i], 0, 0)),
                      pl.BlockSpec((1, dff, d), lambda i, te, nu: (te[i], 0, 0))],
            out_specs=pl.BlockSpec((tme, d), lambda i, te, nu: (i, 0))),
        compiler_params=_cparams("arbitrary"),
        name="moe_experts",
    )(tile_expert, n_used, xs, wg, wu, wd)


def _combine_kernel(dest_ref, x_ref, route_ref, mod_ref, y_ref, g_ref, b_ref, o_ref,
                    buf1, buf2, sem, *, alpha, tm, tiles_per_seq):
    tok0 = (pl.program_id(0) * tiles_per_seq + pl.program_id(1)) * tm

    def row_copy(j, k):
        buf = buf1 if k == 0 else buf2
        return pltpu.make_async_copy(y_ref.at[pl.ds(dest_ref[2 * (tok0 + j) + k], 1)],
                                     buf.at[pl.ds(j, 1)], sem)

    def issue(j, carry):
        row_copy(j, 0).start()
        row_copy(j, 1).start()
        return carry

    def drain(j, carry):
        row_copy(j, 0).wait()
        row_copy(j, 1).wait()
        return carry

    lax.fori_loop(0, tm, issue, 0)
    lax.fori_loop(0, tm, drain, 0)
    route = route_ref[0]
    w1 = route[:, 4:5]
    w2 = route[:, 5:6]
    f = w1 * buf1[...] + w2 * buf2[...]
    gate = mod_ref[0, 0, 5:6, :]
    r = alpha * x_ref[0] + (1.0 + gate) * f
    o_ref[0] = _layer_norm(r, g_ref[...], b_ref[...])


def _combine(x, route, mod_l, y, dest, ln_g, ln_b, alpha, tm):
    nb, L, d = x.shape
    tiles_per_seq = L // tm
    return pl.pallas_call(
        functools.partial(_combine_kernel, alpha=alpha, tm=tm, tiles_per_seq=tiles_per_seq),
        out_shape=jax.ShapeDtypeStruct((nb, L, d), F32),
        grid_spec=pltpu.PrefetchScalarGridSpec(
            num_scalar_prefetch=1,
            grid=(nb, tiles_per_seq),
            in_specs=[pl.BlockSpec((1, tm, d), lambda b, i, ds: (b, i, 0)),
                      pl.BlockSpec((1, tm, LANES), lambda b, i, ds: (b, i, 0)),
                      pl.BlockSpec((1, 1, 6, d), lambda b, i, ds: (0, b, 0, 0)),
                      pl.BlockSpec(memory_space=pl.ANY),
                      pl.BlockSpec(ln_g.shape, lambda b, i, ds: (0, 0)),
                      pl.BlockSpec(ln_b.shape, lambda b, i, ds: (0, 0))],
            out_specs=pl.BlockSpec((1, tm, d), lambda b, i, ds: (b, i, 0)),
            scratch_shapes=[pltpu.VMEM((tm, d), F32), pltpu.VMEM((tm, d), F32),
                            pltpu.SemaphoreType.DMA]),
        compiler_params=_cparams("arbitrary", "arbitrary"),
        name="moe_combine_ln",
    )(dest, x, route, mod_l, y, ln_g, ln_b)


def _moe(x, mod_l, w_router, b_router, wg, wu, wd, ln_g, ln_b, alpha, tm_route, tme, td, tmc):
    nb, L, d = x.shape
    T = nb * L
    h, route, counts = _router(x, mod_l, w_router, b_router, tm_route)
    route_flat = route.reshape(T, LANES)
    e = route_flat[:, 0:2].astype(I32)
    rank = route_flat[:, 2:4].astype(I32)
    cnt = counts[0, :N_EXPERTS].astype(I32)
    tiles = (cnt + tme - 1) // tme
    tile_end = jnp.cumsum(tiles)
    group_start = (tile_end - tiles) * tme
    dest = (group_start[e] + rank).reshape(-1)
    n_tiles = (TOP_K * T) // tme + N_EXPERTS
    tile_expert = jnp.minimum(
        jnp.searchsorted(tile_end, jnp.arange(n_tiles), side="right"), N_EXPERTS - 1).astype(I32)
    n_used = tile_end[-1:].astype(I32)
    xs = _dispatch(h.reshape(T, d), dest, n_tiles * tme, td)
    y = _expert_ffn(xs, tile_expert, n_used, wg, wu, wd, tme)
    return _combine(x, route, mod_l, y, dest, ln_g, ln_b, alpha, tmc)


TOP_K = 2


def kernel(x, c, positions, rel_bias, w_ada, b_ada, w_in, w_out, conv_w, conv_b, dt_bias, a_log,
           d_skip, ssd_norm_w, sinks, ln_g, ln_b, ffn_w_gate, ffn_w_up, ffn_w_down, router_w,
           router_b, expert_w_gate, expert_w_up, expert_w_down):
    depth = w_ada.shape[0]
    nb, L, d = x.shape
    alpha = (2 * depth) ** 0.25
    rd = RET_HEADS * HEAD_DIM
    sd = SSD_HEADS * HEAD_DIM
    cd = conv_w.shape[-1]
    qd = SWA_HEADS * HEAD_DIM
    kvd = SWA_KV_HEADS * HEAD_DIM
    sizes = (rd, rd, rd, rd, sd, cd, SSD_HEADS, qd, kvd, kvd)
    offs = np.concatenate([[0], np.cumsum(sizes)])
    tl = min(512, L)

    mod = _ada_mod(c, w_ada, b_ada)
    cos_t, sin_t = _rotary_tables(positions, tl)
    bias_tab = _swa_bias_table(rel_bias)

    q_perm = np.concatenate([np.arange(h * HEAD_DIM, (h + 1) * HEAD_DIM) for h in SWA_HEAD_ORDER])
    widths = (4 * rd, sd + cd + LANES, qd + 2 * kvd)

    for layer in range(depth):
        wl = w_in[layer]
        seg = lambda i: wl[:, offs[i]:offs[i + 1]]
        dt_cols = jnp.zeros((d, LANES), F32).at[:, :SSD_HEADS].set(seg(6))
        w_cat = jnp.concatenate(
            [seg(0), seg(1), seg(2), seg(3), seg(4), seg(5), dt_cols,
             seg(7)[:, q_perm], seg(8), seg(9)], axis=1).astype(BF16)
        wo = w_out[layer]
        w_r = wo[0:rd].astype(BF16)
        w_s = wo[rd:rd + sd].astype(BF16)
        w_a = wo[rd + sd:][q_perm].astype(BF16)
        mod_l = mod[layer:layer + 1]

        u_ret, u_ssd, u_swa = _in_proj(x, mod_l, w_cat, widths, tl)
        y_ret = _retention(u_ret, cos_t, sin_t, tl)
        y_ssd = _ssd(u_ssd, conv_w[layer], conv_b[layer], dt_bias[layer], a_log[layer],
                     d_skip[layer], ssd_norm_w[layer], tl)
        y_swa = _swa(u_swa, bias_tab, sinks[layer], tl)
        x = _out_proj(x, y_ret, y_ssd, y_swa, mod_l, w_r, w_s, w_a,
                      ln_g[layer, 0][None, :], ln_b[layer, 0][None, :], alpha, tl)

        g2 = ln_g[layer, 1][None, :]
        b2 = ln_b[layer, 1][None, :]
        i = layer // 2
        if layer % 2 == 0:
            x = _dense_ffn(x, mod_l, ffn_w_gate[i].astype(BF16), ffn_w_up[i].astype(BF16),
                           ffn_w_down[i].astype(BF16), g2, b2, alpha, tl)
        else:
            x = _moe(x, mod_l, router_w[i], router_b[i], expert_w_gate[i].astype(BF16),
                     expert_w_up[i].astype(BF16), expert_w_down[i].astype(BF16), g2, b2, alpha,
                     tm_route=min(256, L), tme=min(512, L), td=min(512, L), tmc=min(256, L))
    return x
```

```python
import functools
import math

import numpy as np
import jax
import jax.numpy as jnp
from jax import lax
from jax.experimental import pallas as pl
from jax.experimental.pallas import tpu as pltpu

F32 = jnp.float32
BF16 = jnp.bfloat16
I32 = jnp.int32

HEAD_DIM = 64
CHUNK = 128
RET_HEADS = 4
SSD_HEADS = 8
SSD_GROUPS = 2
SSD_STATE = 64
SSD_CONV = 4
SWA_HEADS = 4
SWA_KV_HEADS = 2
REL_BUCKETS = 32
N_EXPERTS = 8
LN_EPS = 1e-5
LANES = 128
MASK_VALUE = -1e30

VMEM_LIMIT = 56 * 1024 * 1024


def _cparams(*sem):
    return pltpu.CompilerParams(dimension_semantics=sem, vmem_limit_bytes=VMEM_LIMIT)


def _silu(v):
    return v * (1.0 / (1.0 + jnp.exp(-v)))


def _softplus(v):
    return jnp.maximum(v, 0.0) + jnp.log(1.0 + jnp.exp(-jnp.abs(v)))


def _dot(a, b):
    return jnp.dot(a, b, preferred_element_type=F32)


def _dot_nt(a, b):
    return lax.dot_general(a, b, (((1,), (1,)), ((), ())), preferred_element_type=F32)


def _split3(v):
    h1 = v.astype(BF16)
    r1 = v - h1.astype(F32)
    h2 = r1.astype(BF16)
    r2 = r1 - h2.astype(F32)
    return h1, h2, r2.astype(BF16)


def _dot3(v, m):
    h1, h2, h3 = _split3(v)
    return _dot(h1, m) + _dot(h2, m) + _dot(h3, m)


def _dot3_left(m, v):
    h1, h2, h3 = _split3(v)
    return _dot(m, h1) + _dot(m, h2) + _dot(m, h3)


def _dot2(v, m):
    h1 = v.astype(BF16)
    h2 = (v - h1.astype(F32)).astype(BF16)
    return _dot(h1, m) + _dot(h2, m)


def _layer_norm(r, g, b):
    mu = jnp.mean(r, -1, keepdims=True)
    d = r - mu
    var = jnp.mean(d * d, -1, keepdims=True)
    return d * lax.rsqrt(var + LN_EPS) * g + b


def _ada_kernel(c_ref, w_ref, b_ref, o_ref):
    o_ref[0] = jnp.dot(c_ref[...], w_ref[0], preferred_element_type=F32,
                       precision=lax.Precision.HIGHEST) + b_ref[0]


def _ada_mod(c, w_ada, b_ada):
    depth, d, d6 = w_ada.shape
    nb = c.shape[0]
    rows = 8
    c_pad = jnp.zeros((rows, d), F32).at[:nb].set(c)
    out = pl.pallas_call(
        _ada_kernel,
        out_shape=jax.ShapeDtypeStruct((depth, rows, d6), F32),
        grid=(depth, d6 // d),
        in_specs=[pl.BlockSpec((rows, d), lambda l, j: (0, 0)),
                  pl.BlockSpec((1, d, d), lambda l, j: (l, 0, j)),
                  pl.BlockSpec((1, 1, d), lambda l, j: (l, 0, j))],
        out_specs=pl.BlockSpec((1, rows, d), lambda l, j: (l, 0, j)),
        compiler_params=_cparams("arbitrary", "arbitrary"),
        name="ada_mod",
    )(c_pad, w_ada, b_ada.reshape(depth, 1, d6))
    return out[:, :nb].reshape(depth, nb, 6, d)


def _rotary_kernel(pos_ref, cos_ref, sin_ref):
    half = HEAD_DIM // 2
    lane = lax.broadcasted_iota(I32, (1, LANES), 1)
    jj = lane % HEAD_DIM
    idx = (jj % half).astype(F32)
    inv = jnp.exp(-math.log(10000.0) * idx / half)
    ang = pos_ref[0].astype(F32) * inv
    cos_ref[0] = jnp.cos(ang)
    sin_ref[0] = jnp.where(jj < half, -1.0, 1.0) * jnp.sin(ang)


def _rotary_tables(positions, tl):
    nb, L = positions.shape
    pos = positions.reshape(nb, L, 1)
    return pl.pallas_call(
        _rotary_kernel,
        out_shape=[jax.ShapeDtypeStruct((nb, L, LANES), F32)] * 2,
        grid=(nb, L // tl),
        in_specs=[pl.BlockSpec((1, tl, 1), lambda b, i: (b, i, 0))],
        out_specs=[pl.BlockSpec((1, tl, LANES), lambda b, i: (b, i, 0))] * 2,
        compiler_params=_cparams("arbitrary", "arbitrary"),
        name="rotary_tables",
    )(pos)


def _swa_bias_kernel(rb_ref, bucket_ref, band_ref, o_ref):
    bucket = bucket_ref[...]
    band = band_ref[...]
    for h in range(SWA_HEADS):
        acc = jnp.zeros(bucket.shape, F32)
        for b in range(REL_BUCKETS):
            acc = jnp.where(bucket == b, rb_ref[b, h], acc)
        o_ref[h] = jnp.where(band > 0, acc, MASK_VALUE)


def _t5_bucket(dist):
    exact = REL_BUCKETS // 2
    df = jnp.maximum(dist, 1).astype(F32)
    large = exact + (jnp.log(df / exact) / math.log(CHUNK / exact) * (REL_BUCKETS - exact)).astype(I32)
    large = jnp.minimum(large, REL_BUCKETS - 1)
    return jnp.where(dist < exact, dist, large)


def _swa_bias_table(rel_bias):
    W = CHUNK
    qi = jnp.arange(W)[:, None]
    kj = jnp.arange(2 * W)[None, :]
    dist = qi + W - kj
    band = ((dist >= 0) & (dist < W)).astype(I32)
    bucket = _t5_bucket(jnp.clip(dist, 0, W - 1)).astype(I32)
    return pl.pallas_call(
        _swa_bias_kernel,
        out_shape=jax.ShapeDtypeStruct((SWA_HEADS, W, 2 * W), F32),
        in_specs=[pl.BlockSpec(memory_space=pltpu.SMEM),
                  pl.BlockSpec(memory_space=pltpu.VMEM),
                  pl.BlockSpec(memory_space=pltpu.VMEM)],
        out_specs=pl.BlockSpec(memory_space=pltpu.VMEM),
        name="swa_bias_table",
    )(rel_bias, bucket, band)


def _in_proj_kernel(x_ref, mod_ref, w_ref, ret_ref, ssd_ref, swa_ref, *, widths):
    sh = mod_ref[0, 0, 0:1, :]
    sc = mod_ref[0, 0, 1:2, :]
    h = (x_ref[0] * (1.0 + sc) + sh).astype(BF16)
    off = 0
    for ref, width in zip((ret_ref, ssd_ref, swa_ref), widths):
        step = 512 if width % 512 == 0 else 128
        for j in range(0, width, step):
            ref[0, :, j:j + step] = _dot(h, w_ref[:, off + j:off + j + step])
        off += width


def _in_proj(x, mod_l, w_cat, widths, tm):
    nb, L, d = x.shape
    return pl.pallas_call(
        functools.partial(_in_proj_kernel, widths=widths),
        out_shape=[jax.ShapeDtypeStruct((nb, L, w), F32) for w in widths],
        grid=(nb, L // tm),
        in_specs=[pl.BlockSpec((1, tm, d), lambda b, i: (b, i, 0)),
                  pl.BlockSpec((1, 1, 6, d), lambda b, i: (0, b, 0, 0)),
                  pl.BlockSpec(w_cat.shape, lambda b, i: (0, 0))],
        out_specs=[pl.BlockSpec((1, tm, w), lambda b, i: (b, i, 0)) for w in widths],
        compiler_params=_cparams("arbitrary", "arbitrary"),
        name="in_proj",
    )(x, mod_l, w_cat)


def _head_lane_mask(width, head):
    lane = lax.broadcasted_iota(I32, (1, width), 1)
    return (lane // HEAD_DIM) == head


def _rotate_half(t):
    width = t.shape[-1]
    lane = lax.broadcasted_iota(I32, (1, width), 1)
    half = HEAD_DIM // 2
    fwd = pltpu.roll(t, width - half, axis=1)
    bwd = pltpu.roll(t, half, axis=1)
    return jnp.where((lane % HEAD_DIM) < half, fwd, bwd)


def _retention_kernel(u_ref, cos_ref, sin_ref, din_ref, dq_ref, dk_ref, dc_ref,
                      bmask_ref, avg_ref, o_ref, state_ref, *, n_chunks):
    rd = RET_HEADS * HEAD_DIM

    @pl.when(pl.program_id(1) == 0)
    def _():
        state_ref[...] = jnp.zeros_like(state_ref)

    masks = [_head_lane_mask(rd, h) for h in range(RET_HEADS)]

    def stack_heads(t):
        return jnp.concatenate([jnp.where(m, t, 0.0) for m in masks], axis=0).astype(BF16)

    state = state_ref[...]
    for ci in range(n_chunks):
        rows = slice(ci * CHUNK, (ci + 1) * CHUNK)
        cos = cos_ref[0, rows, :]
        sin = sin_ref[0, rows, :]
        cos2 = jnp.concatenate([cos, cos], axis=1)
        sin2 = jnp.concatenate([sin, sin], axis=1)
        q = u_ref[0, rows, 0:rd]
        k = u_ref[0, rows, rd:2 * rd]
        v = u_ref[0, rows, 2 * rd:3 * rd]
        g = u_ref[0, rows, 3 * rd:4 * rd]
        qr = q * cos2 + _rotate_half(q) * sin2
        kr = (k * cos2 + _rotate_half(k) * sin2) * (HEAD_DIM ** -0.5)
        scores = _dot_nt(qr.astype(BF16), stack_heads(kr)) * din_ref[...]
        inner = _dot(scores.astype(BF16), stack_heads(v))
        cross = _dot((qr * dq_ref[...]).astype(BF16), state.astype(BF16))
        o = inner + cross
        kd_t = (kr * dk_ref[...]).T.astype(BF16)
        kv = _dot(kd_t, v.astype(BF16))
        state = dc_ref[...] * state + bmask_ref[...] * kv
        mu = _dot2(o, avg_ref[...])
        dev = o - mu
        var = _dot2(dev * dev, avg_ref[...])
        on = dev * lax.rsqrt(var + LN_EPS)
        o_ref[0, rows, :] = (_silu(g) * on).astype(o_ref.dtype)
    state_ref[...] = state


def _retention_tables():
    H, d, C = RET_HEADS, HEAD_DIM, CHUNK
    log_gamma = jnp.log(1.0 - 2.0 ** (-5.0 - jnp.arange(H, dtype=F32)))
    idx = jnp.arange(C, dtype=F32)
    diff = idx[:, None] - idx[None, :]
    decay_in = jnp.where(diff >= 0, jnp.exp(log_gamma[:, None, None] * jnp.maximum(diff, 0.0)), 0.0)
    decay_q = jnp.exp(log_gamma[:, None] * (idx + 1.0))
    decay_k = jnp.exp(log_gamma[:, None] * (C - 1.0 - idx))
    decay_chunk = jnp.exp(log_gamma * C)
    din = decay_in.transpose(1, 0, 2).reshape(C, H * C)
    dq = jnp.repeat(decay_q.T, d, axis=1)
    dk = jnp.repeat(decay_k.T, d, axis=1)
    dc = jnp.repeat(decay_chunk, d)[None, :]
    head = jnp.arange(H * d) // d
    bmask = (head[:, None] == head[None, :]).astype(F32)
    avg = (bmask / d).astype(BF16)
    return din, dq, dk, dc, bmask, avg


def _retention(u_ret, cos_t, sin_t, tl):
    nb, L, _ = u_ret.shape
    rd = RET_HEADS * HEAD_DIM
    tables = _retention_tables()
    const = lambda a: pl.BlockSpec(a.shape, lambda b, i: (0,) * a.ndim)
    return pl.pallas_call(
        functools.partial(_retention_kernel, n_chunks=tl // CHUNK),
        out_shape=jax.ShapeDtypeStruct((nb, L, rd), BF16),
        grid=(nb, L // tl),
        in_specs=[pl.BlockSpec((1, tl, 4 * rd), lambda b, i: (b, i, 0)),
                  pl.BlockSpec((1, tl, LANES), lambda b, i: (b, i, 0)),
                  pl.BlockSpec((1, tl, LANES), lambda b, i: (b, i, 0))]
                 + [const(t) for t in tables],
        out_specs=pl.BlockSpec((1, tl, rd), lambda b, i: (b, i, 0)),
        scratch_shapes=[pltpu.VMEM((rd, rd), F32)],
        compiler_params=_cparams("arbitrary", "arbitrary"),
        name="retention",
    )(u_ret, cos_t, sin_t, *tables)


def _swa_kernel(u_ref, bias_ref, sink_ref, o_ref, kprev_ref, vprev_ref, *, n_chunks):
    W = CHUNK
    qd = SWA_HEADS * HEAD_DIM
    kvd = SWA_KV_HEADS * HEAD_DIM

    @pl.when(pl.program_id(1) == 0)
    def _():
        kprev_ref[...] = jnp.zeros_like(kprev_ref)
        vprev_ref[...] = jnp.zeros_like(vprev_ref)

    first_step = pl.program_id(1) == 0
    lane = lax.broadcasted_iota(I32, (1, LANES), 1)
    low = lane < HEAD_DIM
    col = lax.broadcasted_iota(I32, (1, 2 * W), 1)

    kprev = kprev_ref[...]
    vprev = vprev_ref[...]
    sink = sink_ref[...]
    for ci in range(n_chunks):
        rows = slice(ci * W, (ci + 1) * W)
        qa = u_ref[0, rows, 0:LANES]
        qb = u_ref[0, rows, LANES:qd]
        k = u_ref[0, rows, qd:qd + kvd].astype(BF16)
        v = u_ref[0, rows, qd + kvd:qd + 2 * kvd].astype(BF16)
        q4 = jnp.concatenate([jnp.where(low, qa, 0.0), jnp.where(low, 0.0, qa),
                              jnp.where(low, qb, 0.0), jnp.where(low, 0.0, qb)],
                             axis=0).astype(BF16)
        kband = jnp.concatenate([kprev, k], axis=0)
        vband = jnp.concatenate([vprev, v], axis=0)
        logits = _dot_nt(q4, kband) * (HEAD_DIM ** -0.5) + bias_ref[...]
        if ci == 0:
            logits = jnp.where(jnp.logical_and(first_step, col < W), MASK_VALUE, logits)
        m = jnp.maximum(jnp.max(logits, -1, keepdims=True), sink)
        p = jnp.exp(logits - m)
        denom = jnp.sum(p, -1, keepdims=True) + jnp.exp(sink - m)
        res = _dot(p.astype(BF16), vband) / denom
        out_a = jnp.where(low, res[0:W], res[W:2 * W])
        out_b = jnp.where(low, res[2 * W:3 * W], res[3 * W:4 * W])
        o_ref[0, rows, 0:LANES] = out_a.astype(o_ref.dtype)
        o_ref[0, rows, LANES:qd] = out_b.astype(o_ref.dtype)
        kprev, vprev = k, v
    kprev_ref[...] = kprev
    vprev_ref[...] = vprev


SWA_HEAD_ORDER = (0, 2, 1, 3)


def _swa(u_swa, bias_tab, sinks_l, tl):
    nb, L, _ = u_swa.shape
    W = CHUNK
    qd = SWA_HEADS * HEAD_DIM
    order = jnp.array(SWA_HEAD_ORDER)
    bias_stacked = bias_tab[order].reshape(SWA_HEADS * W, 2 * W)
    sink_col = jnp.repeat(sinks_l.astype(F32)[order], W)[:, None]
    return pl.pallas_call(
        functools.partial(_swa_kernel, n_chunks=tl // W),
        out_shape=jax.ShapeDtypeStruct((nb, L, qd), BF16),
        grid=(nb, L // tl),
        in_specs=[pl.BlockSpec((1, tl, u_swa.shape[-1]), lambda b, i: (b, i, 0)),
                  pl.BlockSpec(bias_stacked.shape, lambda b, i: (0, 0)),
                  pl.BlockSpec(sink_col.shape, lambda b, i: (0, 0))],
        out_specs=pl.BlockSpec((1, tl, qd), lambda b, i: (b, i, 0)),
        scratch_shapes=[pltpu.VMEM((W, LANES), BF16), pltpu.VMEM((W, LANES), BF16)],
        compiler_params=_cparams("arbitrary", "arbitrary"),
        name="swa",
    )(u_swa, bias_stacked, sink_col)


def _ssd_kernel(u_ref, cw_ref, cb_ref, dtb_c_ref, alog_c_ref, dskip_ref, nw_ref,
                tril_ref, triu_ref, expand_ref, gmask_ref,
                o_ref, state_ref, ext_ref, *, n_chunks):
    C = CHUNK
    sd = SSD_HEADS * HEAD_DIM
    gn = SSD_GROUPS * SSD_STATE
    cd = sd + 2 * gn
    tl = n_chunks * C
    slab = 2 * LANES
    heads_per_group = SSD_HEADS // SSD_GROUPS
    heads_per_slab = slab // HEAD_DIM

    @pl.when(pl.program_id(1) == 0)
    def _():
        state_ref[...] = jnp.zeros_like(state_ref)
        ext_ref[0:8, :] = jnp.zeros((8, cd), F32)

    lane = lax.broadcasted_iota(I32, (1, LANES), 1)
    low = lane < SSD_STATE
    slab_masks = [_head_lane_mask(slab, hh) for hh in range(heads_per_slab)]
    row_i = lax.broadcasted_iota(I32, (C, C), 0)
    col_i = lax.broadcasted_iota(I32, (C, C), 1)
    causal = row_i >= col_i
    neg_a_c = -jnp.exp(alog_c_ref[...])

    ext_ref[8:8 + tl, :] = u_ref[0, :, sd:sd + cd]
    state = state_ref[...]
    for ci in range(n_chunks):
        rows = slice(ci * C, (ci + 1) * C)
        z = u_ref[0, rows, 0:sd]
        dt_raw = u_ref[0, rows, sd + cd:sd + cd + LANES]
        base = 8 + ci * C
        conv = cb_ref[...]
        for w in range(SSD_CONV):
            shift = SSD_CONV - 1 - w
            conv = conv + cw_ref[w:w + 1, :] * ext_ref[base - shift:base - shift + C, :]
        xbc = _silu(conv)
        xs = xbc[:, 0:sd]
        bm = xbc[:, sd:sd + gn]
        cm = xbc[:, sd + gn:cd]

        dt_c = _softplus(dt_raw + dtb_c_ref[...])
        a_c = neg_a_c * dt_c
        acs_c = _dot3_left(tril_ref[...], a_c)
        acs_t = _dot3(a_c.T, triu_ref[...])
        spread = _dot3(acs_c, expand_ref[...])
        acs_x = spread[:, 0:sd]
        dt_x = _dot3(dt_c, expand_ref[:, 0:sd])
        xdt = xs * dt_x

        bstack = jnp.concatenate([jnp.where(low, bm, 0.0), jnp.where(low, 0.0, bm)],
                                 axis=0).astype(BF16)
        cb = _dot_nt(cm.astype(BF16), bstack)
        y_diag = []
        for s in range(sd // slab):
            ms = []
            for hh in range(heads_per_slab):
                h = s * heads_per_slab + hh
                g = h // heads_per_group
                col_bcast = spread[:, sd + h * LANES:sd + (h + 1) * LANES]
                seg = col_bcast - acs_t[h:h + 1, :]
                lmat = jnp.exp(jnp.where(causal, seg, MASK_VALUE))
                ms.append((cb[:, g * C:(g + 1) * C] * lmat).astype(BF16))
            xslab = xdt[:, s * slab:(s + 1) * slab]
            xstack = jnp.concatenate([jnp.where(m, xslab, 0.0) for m in slab_masks],
                                     axis=0).astype(BF16)
            y_diag.append(_dot(jnp.concatenate(ms, axis=1), xstack))
        y_diag = jnp.concatenate(y_diag, axis=1)

        y_off = _dot(cm.astype(BF16), state.astype(BF16)) * jnp.exp(acs_x)
        last = acs_x[C - 1:C, :]
        dec = jnp.exp(last - acs_x)
        new = _dot(bm.T.astype(BF16), (xdt * dec).astype(BF16))
        state = jnp.exp(last) * state + gmask_ref[...] * new

        y = y_diag + y_off + xs * dskip_ref[...]
        hgate = y * _silu(z)
        gw = sd // SSD_GROUPS
        for g in range(SSD_GROUPS):
            hg = hgate[:, g * gw:(g + 1) * gw]
            ms_ = jnp.mean(hg * hg, -1, keepdims=True)
            o_ref[0, rows, g * gw:(g + 1) * gw] = (
                hg * lax.rsqrt(ms_ + LN_EPS) * nw_ref[:, g * gw:(g + 1) * gw]).astype(o_ref.dtype)
    state_ref[...] = state
    ext_ref[0:8, :] = ext_ref[tl:tl + 8, :]


def _ssd_tables():
    C = CHUNK
    sd = SSD_HEADS * HEAD_DIM
    gn = SSD_GROUPS * SSD_STATE
    t = jnp.arange(C)
    tril = (t[:, None] >= t[None, :]).astype(BF16)
    triu = (t[:, None] <= t[None, :]).astype(BF16)
    r = jnp.arange(LANES)[:, None]
    eexp = ((r == (jnp.arange(sd)[None, :] // HEAD_DIM)) & (r < SSD_HEADS)).astype(BF16)
    bsel = ((r == (jnp.arange(SSD_HEADS * LANES)[None, :] // LANES)) & (r < SSD_HEADS)).astype(BF16)
    heads_per_group = SSD_HEADS // SSD_GROUPS
    row_g = jnp.arange(gn)[:, None] // SSD_STATE
    col_g = (jnp.arange(sd)[None, :] // HEAD_DIM) // heads_per_group
    gmask = (row_g == col_g).astype(F32)
    expand = jnp.concatenate([eexp, bsel], axis=1)
    return tril, triu, expand, gmask


def _ssd(u_ssd, conv_w, conv_b, dt_bias, a_log, d_skip, norm_w, tl):
    nb, L, width = u_ssd.shape
    sd = SSD_HEADS * HEAD_DIM
    cd = conv_w.shape[-1]
    pad = lambda v: jnp.zeros((1, LANES), F32).at[0, :SSD_HEADS].set(v)
    rep = lambda v: jnp.repeat(v, HEAD_DIM)[None, :]
    params = (conv_w, conv_b[None, :], pad(dt_bias), pad(a_log), rep(d_skip), norm_w[None, :])
    tables = _ssd_tables()
    const = lambda a: pl.BlockSpec(a.shape, lambda b, i: (0,) * a.ndim)
    return pl.pallas_call(
        functools.partial(_ssd_kernel, n_chunks=tl // CHUNK),
        out_shape=jax.ShapeDtypeStruct((nb, L, sd), BF16),
        grid=(nb, L // tl),
        in_specs=[pl.BlockSpec((1, tl, width), lambda b, i: (b, i, 0))]
                 + [const(t) for t in params + tables],
        out_specs=pl.BlockSpec((1, tl, sd), lambda b, i: (b, i, 0)),
        scratch_shapes=[pltpu.VMEM((SSD_GROUPS * SSD_STATE, sd), F32),
                        pltpu.VMEM((8 + tl, cd), F32)],
        compiler_params=_cparams("arbitrary", "arbitrary"),
        name="ssd",
    )(u_ssd, *params, *tables)


def _out_proj_kernel(x_ref, yr_ref, ys_ref, ya_ref, mod_ref, wr_ref, ws_ref, wa_ref,
                     g_ref, b_ref, o_ref, *, alpha):
    mix = (_dot(yr_ref[0], wr_ref[...]) + _dot(ys_ref[0], ws_ref[...])
           + _dot(ya_ref[0], wa_ref[...]))
    gate = mod_ref[0, 0, 2:3, :]
    r = alpha * x_ref[0] + (1.0 + gate) * mix
    o_ref[0] = _layer_norm(r, g_ref[...], b_ref[...])


def _out_proj(x, y_ret, y_ssd, y_swa, mod_l, w_r, w_s, w_a, ln_g, ln_b, alpha, tm):
    nb, L, d = x.shape
    tok = lambda w: pl.BlockSpec((1, tm, w), lambda b, i: (b, i, 0))
    const = lambda a: pl.BlockSpec(a.shape, lambda b, i: (0,) * a.ndim)
    return pl.pallas_call(
        functools.partial(_out_proj_kernel, alpha=alpha),
        out_shape=jax.ShapeDtypeStruct((nb, L, d), F32),
        grid=(nb, L // tm),
        in_specs=[tok(d), tok(y_ret.shape[-1]), tok(y_ssd.shape[-1]), tok(y_swa.shape[-1]),
                  pl.BlockSpec((1, 1, 6, d), lambda b, i: (0, b, 0, 0)),
                  const(w_r), const(w_s), const(w_a), const(ln_g), const(ln_b)],
        out_specs=tok(d),
        compiler_params=_cparams("arbitrary", "arbitrary"),
        name="out_proj_ln",
    )(x, y_ret, y_ssd, y_swa, mod_l, w_r, w_s, w_a, ln_g, ln_b)


FF_CHUNK = 256


def _dense_ffn_kernel(x_ref, mod_ref, wg_ref, wu_ref, wd_ref, g_ref, b_ref, o_ref, *, alpha):
    x = x_ref[0]
    sh = mod_ref[0, 0, 3:4, :]
    sc = mod_ref[0, 0, 4:5, :]
    gate = mod_ref[0, 0, 5:6, :]
    h = (x * (1.0 + sc) + sh).astype(BF16)
    dff = wg_ref.shape[-1]
    acc = jnp.zeros(x.shape, F32)
    for j in range(0, dff, FF_CHUNK):
        gj = _dot(h, wg_ref[:, j:j + FF_CHUNK])
        uj = _dot(h, wu_ref[:, j:j + FF_CHUNK])
        acc = acc + _dot((_silu(gj) * uj).astype(BF16), wd_ref[j:j + FF_CHUNK, :])
    r = alpha * x + (1.0 + gate) * acc
    o_ref[0] = _layer_norm(r, g_ref[...], b_ref[...])


def _dense_ffn(x, mod_l, wg, wu, wd, ln_g, ln_b, alpha, tm):
    nb, L, d = x.shape
    const = lambda a: pl.BlockSpec(a.shape, lambda b, i: (0,) * a.ndim,
                                   pipeline_mode=pl.Buffered(1))
    return pl.pallas_call(
        functools.partial(_dense_ffn_kernel, alpha=alpha),
        out_shape=jax.ShapeDtypeStruct((nb, L, d), F32),
        grid=(nb, L // tm),
        in_specs=[pl.BlockSpec((1, tm, d), lambda b, i: (b, i, 0)),
                  pl.BlockSpec((1, 1, 6, d), lambda b, i: (0, b, 0, 0)),
                  const(wg), const(wu), const(wd), const(ln_g), const(ln_b)],
        out_specs=pl.BlockSpec((1, tm, d), lambda b, i: (b, i, 0)),
        compiler_params=_cparams("arbitrary", "arbitrary"),
        name="dense_ffn_ln",
    )(x, mod_l, wg, wu, wd, ln_g, ln_b)


def _router_kernel(x_ref, mod_ref, wr_ref, br_ref, trils_ref, h_ref, route_ref, cnt_ref,
                   base_ref):
    first = jnp.logical_and(pl.program_id(0) == 0, pl.program_id(1) == 0)

    @pl.when(first)
    def _():
        base_ref[...] = jnp.zeros_like(base_ref)

    sh = mod_ref[0, 0, 3:4, :]
    sc = mod_ref[0, 0, 4:5, :]
    h = x_ref[0] * (1.0 + sc) + sh
    h_ref[0] = h
    logits = jnp.dot(h, wr_ref[...], preferred_element_type=F32,
                     precision=lax.Precision.HIGHEST) + br_ref[...]
    lane = lax.broadcasted_iota(I32, logits.shape, 1).astype(F32)
    logits = jnp.where(lane < N_EXPERTS, logits, MASK_VALUE)
    v1 = jnp.max(logits, -1, keepdims=True)
    e1 = jnp.min(jnp.where(logits == v1, lane, float(LANES)), -1, keepdims=True)
    rest = jnp.where(lane == e1, MASK_VALUE, logits)
    v2 = jnp.max(rest, -1, keepdims=True)
    e2 = jnp.min(jnp.where(rest == v2, lane, float(LANES)), -1, keepdims=True)
    t = jnp.exp(v2 - v1)
    w1 = 1.0 / (1.0 + t)
    w2 = t / (1.0 + t)
    hot1 = (lane == e1).astype(F32)
    hot2 = (lane == e2).astype(F32)
    both = hot1 + hot2
    base = base_ref[0:1, :]
    before = _dot(trils_ref[...], both.astype(BF16)) + base
    rank1 = jnp.sum(hot1 * before, -1, keepdims=True)
    rank2 = jnp.sum(hot2 * before, -1, keepdims=True)
    total = base + jnp.sum(both, 0, keepdims=True)
    base_ref[0:1, :] = total
    cnt_ref[...] = jnp.broadcast_to(total, cnt_ref.shape)
    route = jnp.where(lane == 0, e1, 0.0)
    route = jnp.where(lane == 1, e2, route)
    route = jnp.where(lane == 2, rank1, route)
    route = jnp.where(lane == 3, rank2, route)
    route = jnp.where(lane == 4, w1, route)
    route = jnp.where(lane == 5, w2, route)
    route_ref[0] = route


def _router(x, mod_l, w_router, b_router, tm):
    nb, L, d = x.shape
    wr = jnp.zeros((d, LANES), F32).at[:, :N_EXPERTS].set(w_router)
    br = jnp.zeros((1, LANES), F32).at[0, :N_EXPERTS].set(b_router)
    t = jnp.arange(tm)
    tril_strict = (t[:, None] > t[None, :]).astype(BF16)
    const = lambda a: pl.BlockSpec(a.shape, lambda b, i: (0,) * a.ndim)
    return pl.pallas_call(
        _router_kernel,
        out_shape=[jax.ShapeDtypeStruct((nb, L, d), F32),
                   jax.ShapeDtypeStruct((nb, L, LANES), F32),
                   jax.ShapeDtypeStruct((8, LANES), F32)],
        grid=(nb, L // tm),
        in_specs=[pl.BlockSpec((1, tm, d), lambda b, i: (b, i, 0)),
                  pl.BlockSpec((1, 1, 6, d), lambda b, i: (0, b, 0, 0)),
                  const(wr), const(br), const(tril_strict)],
        out_specs=[pl.BlockSpec((1, tm, d), lambda b, i: (b, i, 0)),
                   pl.BlockSpec((1, tm, LANES), lambda b, i: (b, i, 0)),
                   pl.BlockSpec((8, LANES), lambda b, i: (0, 0))],
        scratch_shapes=[pltpu.VMEM((8, LANES), F32)],
        compiler_params=_cparams("arbitrary", "arbitrary"),
        name="moe_router",
    )(x, mod_l, wr, br, tril_strict)


TOP_K = 2
EXPERT_FF_SPLITS = 2


def _ff_chunks(width, step=512):
    return [(a, min(a + step, width)) for a in range(0, width, step)]


def _expert_kernel(te_ref, nu_ref, gsrc_ref, sdst_ref,
                   h_ref, wg_ref, wu_ref, wd_ref, yk_ref,
                   xbuf, ybuf, hbuf, gsem, ssem, *, tme):
    i = pl.program_id(0)
    j = pl.program_id(1)
    n_used = nu_ref[0]
    slot = i % 2
    other = 1 - slot
    used = i < n_used
    chunks = _ff_chunks(wg_ref.shape[-1])
    rows_per_step = tme // EXPERT_FF_SPLITS
    rows_per_chunk = rows_per_step // len(chunks)

    def gather(tile, slot_, r):
        return pltpu.make_async_copy(h_ref.at[pl.ds(gsrc_ref[tile * tme + r], 1)],
                                     xbuf.at[slot_, pl.ds(r, 1)], gsem.at[slot_])

    def wait_gather(slot_):
        pltpu.make_async_copy(h_ref.at[pl.ds(0, tme)], xbuf.at[slot_], gsem.at[slot_]).wait()

    def scatter(block, slot_, r):
        return pltpu.make_async_copy(ybuf.at[slot_, pl.ds(r, 1)],
                                     yk_ref.at[pl.ds(sdst_ref[block * tme + r], 1)],
                                     ssem.at[slot_])

    def wait_scatter(slot_):
        pltpu.make_async_copy(ybuf.at[slot_], yk_ref.at[pl.ds(0, tme)], ssem.at[slot_]).wait()

    @pl.when(jnp.logical_and(i == 0, j == 0))
    def _():
        ybuf[1] = jnp.zeros(ybuf.shape[1:], F32)

        def body(r, carry):
            gather(0, 0, r).start()
            return carry
        lax.fori_loop(0, tme, body, 0, unroll=8)

    @pl.when(used)
    def _():
        @pl.when(j == 0)
        def _():
            wait_gather(slot)
            hbuf[...] = xbuf[slot].astype(BF16)
            ybuf[slot] = jnp.zeros(ybuf.shape[1:], F32)

        h = hbuf[...]
        for c, (a, b) in enumerate(chunks):
            row0 = j * rows_per_step + c * rows_per_chunk
            for r in range(rows_per_chunk):
                gather(i + 1, other, row0 + r).start()
                scatter(i, other, row0 + r).start()
            gj = _dot(h, wg_ref[0, :, a:b])
            uj = _dot(h, wu_ref[0, :, a:b])
            part = _dot((_silu(gj) * uj).astype(BF16), wd_ref[0, a:b, :])
            ybuf[slot] = ybuf[slot] + part

        @pl.when(j == EXPERT_FF_SPLITS - 1)
        def _():
            wait_scatter(other)

            @pl.when(i == n_used - 1)
            def _():
                wait_gather(other)

                def body(r, carry):
                    scatter(i + 1, slot, r).start()
                    return carry
                lax.fori_loop(0, tme, body, 0, unroll=8)
                wait_scatter(slot)


def _expert_ffn(h_flat, tile_expert, n_used, gsrc, sdst, wg, wu, wd, tme):
    T, d = h_flat.shape
    dff = wg.shape[-1]
    dffh = dff // EXPERT_FF_SPLITS
    n_tiles = gsrc.shape[0] // tme
    yk_rows = TOP_K * T + tme

    def half(i, j):
        return jnp.where(i % 2 == 0, j, EXPERT_FF_SPLITS - 1 - j)

    return pl.pallas_call(
        functools.partial(_expert_kernel, tme=tme),
        out_shape=jax.ShapeDtypeStruct((yk_rows, d), F32),
        grid_spec=pltpu.PrefetchScalarGridSpec(
            num_scalar_prefetch=4,
            grid=(n_tiles, EXPERT_FF_SPLITS),
            in_specs=[pl.BlockSpec(memory_space=pl.ANY),
                      pl.BlockSpec((1, d, dffh), lambda i, j, te, *_: (te[i], 0, half(i, j))),
                      pl.BlockSpec((1, d, dffh), lambda i, j, te, *_: (te[i], 0, half(i, j))),
                      pl.BlockSpec((1, dffh, d), lambda i, j, te, *_: (te[i], half(i, j), 0))],
            out_specs=pl.BlockSpec(memory_space=pl.ANY),
            scratch_shapes=[pltpu.VMEM((2, tme, d), F32), pltpu.VMEM((2, tme, d), F32),
                            pltpu.VMEM((tme, d), BF16),
                            pltpu.SemaphoreType.DMA((2,)), pltpu.SemaphoreType.DMA((2,))]),
        compiler_params=_cparams("arbitrary", "arbitrary"),
        name="moe_experts",
    )(tile_expert, n_used, gsrc, sdst, h_flat, wg, wu, wd)


def _moe_finish_kernel(x_ref, route_ref, mod_ref, y0_ref, y1_ref, g_ref, b_ref, o_ref, *, alpha):
    route = route_ref[0]
    f = route[:, 4:5] * y0_ref[...] + route[:, 5:6] * y1_ref[...]
    gate = mod_ref[0, 0, 5:6, :]
    r = alpha * x_ref[0] + (1.0 + gate) * f
    o_ref[0] = _layer_norm(r, g_ref[...], b_ref[...])


def _moe_finish(x, route, mod_l, yk, ln_g, ln_b, alpha, tm):
    nb, L, d = x.shape
    tiles_per_seq = L // tm
    tiles = nb * tiles_per_seq
    return pl.pallas_call(
        functools.partial(_moe_finish_kernel, alpha=alpha),
        out_shape=jax.ShapeDtypeStruct((nb, L, d), F32),
        grid=(nb, tiles_per_seq),
        in_specs=[pl.BlockSpec((1, tm, d), lambda b, i: (b, i, 0)),
                  pl.BlockSpec((1, tm, LANES), lambda b, i: (b, i, 0)),
                  pl.BlockSpec((1, 1, 6, d), lambda b, i: (0, b, 0, 0)),
                  pl.BlockSpec((tm, d), lambda b, i: (b * tiles_per_seq + i, 0)),
                  pl.BlockSpec((tm, d), lambda b, i: (tiles + b * tiles_per_seq + i, 0)),
                  pl.BlockSpec(ln_g.shape, lambda b, i: (0, 0)),
                  pl.BlockSpec(ln_b.shape, lambda b, i: (0, 0))],
        out_specs=pl.BlockSpec((1, tm, d), lambda b, i: (b, i, 0)),
        compiler_params=_cparams("arbitrary", "arbitrary"),
        name="moe_finish_ln",
    )(x, route, mod_l, yk, yk, ln_g, ln_b)


def _moe(x, mod_l, w_router, b_router, wg, wu, wd, ln_g, ln_b, alpha, tm_route, tme, tmf):
    nb, L, d = x.shape
    T = nb * L
    h, route, counts = _router(x, mod_l, w_router, b_router, tm_route)
    route_flat = route.reshape(T, LANES)
    e = route_flat[:, 0:2].astype(I32)
    rank = route_flat[:, 2:4].astype(I32)
    cnt = counts[0, :N_EXPERTS].astype(I32)
    tiles = (cnt + tme - 1) // tme
    tile_end = jnp.cumsum(tiles)
    group_start = (tile_end - tiles) * tme
    dest = (group_start[e] + rank).reshape(-1)
    n_tiles = (TOP_K * T) // tme + N_EXPERTS
    n_rows = n_tiles * tme
    tile_expert = jnp.minimum(
        jnp.searchsorted(tile_end, jnp.arange(n_tiles), side="right"), N_EXPERTS - 1).astype(I32)
    n_used = tile_end[-1:].astype(I32)
    flat = jnp.full((n_rows,), -1, I32).at[dest].set(jnp.arange(TOP_K * T, dtype=I32))
    tok = flat // TOP_K
    gsrc = jnp.where(flat >= 0, tok, 0)
    dump = TOP_K * T + jnp.arange(n_rows, dtype=I32) % tme
    sdst = jnp.where(flat >= 0, (flat % TOP_K) * T + tok, dump)
    sdst = jnp.concatenate([dump[:tme], sdst])
    yk = _expert_ffn(h.reshape(T, d), tile_expert, n_used, gsrc, sdst, wg, wu, wd, tme)
    return _moe_finish(x, route, mod_l, yk, ln_g, ln_b, alpha, tmf)


def kernel(x, c, positions, rel_bias, w_ada, b_ada, w_in, w_out, conv_w, conv_b, dt_bias, a_log,
           d_skip, ssd_norm_w, sinks, ln_g, ln_b, ffn_w_gate, ffn_w_up, ffn_w_down, router_w,
           router_b, expert_w_gate, expert_w_up, expert_w_down):
    depth = w_ada.shape[0]
    nb, L, d = x.shape
    alpha = (2 * depth) ** 0.25
    rd = RET_HEADS * HEAD_DIM
    sd = SSD_HEADS * HEAD_DIM
    cd = conv_w.shape[-1]
    qd = SWA_HEADS * HEAD_DIM
    kvd = SWA_KV_HEADS * HEAD_DIM
    sizes = (rd, rd, rd, rd, sd, cd, SSD_HEADS, qd, kvd, kvd)
    offs = np.concatenate([[0], np.cumsum(sizes)])
    tl = min(512, L)

    mod = _ada_mod(c, w_ada, b_ada)
    cos_t, sin_t = _rotary_tables(positions, tl)
    bias_tab = _swa_bias_table(rel_bias)

    q_perm = np.concatenate([np.arange(h * HEAD_DIM, (h + 1) * HEAD_DIM) for h in SWA_HEAD_ORDER])
    widths = (4 * rd, sd + cd + LANES, qd + 2 * kvd)

    for layer in range(depth):
        wl = w_in[layer]
        seg = lambda i: wl[:, offs[i]:offs[i + 1]]
        dt_cols = jnp.zeros((d, LANES), F32).at[:, :SSD_HEADS].set(seg(6))
        w_cat = jnp.concatenate(
            [seg(0), seg(1), seg(2), seg(3), seg(4), seg(5), dt_cols,
             seg(7)[:, q_perm], seg(8), seg(9)], axis=1).astype(BF16)
        wo = w_out[layer]
        w_r = wo[0:rd].astype(BF16)
        w_s = wo[rd:rd + sd].astype(BF16)
        w_a = wo[rd + sd:][q_perm].astype(BF16)
        mod_l = mod[layer:layer + 1]

        u_ret, u_ssd, u_swa = _in_proj(x, mod_l, w_cat, widths, tl)
        y_ret = _retention(u_ret, cos_t, sin_t, tl)
        y_ssd = _ssd(u_ssd, conv_w[layer], conv_b[layer], dt_bias[layer], a_log[layer],
                     d_skip[layer], ssd_norm_w[layer], tl)
        y_swa = _swa(u_swa, bias_tab, sinks[layer], tl)
        x = _out_proj(x, y_ret, y_ssd, y_swa, mod_l, w_r, w_s, w_a,
                      ln_g[layer, 0][None, :], ln_b[layer, 0][None, :], alpha, tl)

        g2 = ln_g[layer, 1][None, :]
        b2 = ln_b[layer, 1][None, :]
        i = layer // 2
        if layer % 2 == 0:
            x = _dense_ffn(x, mod_l, ffn_w_gate[i].astype(BF16), ffn_w_up[i].astype(BF16),
                           ffn_w_down[i].astype(BF16), g2, b2, alpha, tl)
        else:
            x = _moe(x, mod_l, router_w[i], router_b[i], expert_w_gate[i].astype(BF16),
                     expert_w_up[i].astype(BF16), expert_w_down[i].astype(BF16), g2, b2, alpha,
                     tm_route=min(256, L), tme=min(512, L), tmf=min(512, L))
    return x
```

```python
import functools
import math

import numpy as np
import jax
import jax.numpy as jnp
from jax import lax
from jax.experimental import pallas as pl
from jax.experimental.pallas import tpu as pltpu

F32 = jnp.float32
BF16 = jnp.bfloat16
I32 = jnp.int32

HEAD_DIM = 64
CHUNK = 128
RET_HEADS = 4
SSD_HEADS = 8
SSD_GROUPS = 2
SSD_STATE = 64
SSD_CONV = 4
SWA_HEADS = 4
SWA_KV_HEADS = 2
REL_BUCKETS = 32
N_EXPERTS = 8
LN_EPS = 1e-5
LANES = 128
MASK_VALUE = -1e30

VMEM_LIMIT = 56 * 1024 * 1024


def _cparams(*sem):
    return pltpu.CompilerParams(dimension_semantics=sem, vmem_limit_bytes=VMEM_LIMIT)


def _silu(v):
    return v * (1.0 / (1.0 + jnp.exp(-v)))


def _softplus(v):
    return jnp.maximum(v, 0.0) + jnp.log(1.0 + jnp.exp(-jnp.abs(v)))


def _dot(a, b):
    return jnp.dot(a, b, preferred_element_type=F32)


def _dot_nt(a, b):
    return lax.dot_general(a, b, (((1,), (1,)), ((), ())), preferred_element_type=F32)


def _split3(v):
    h1 = v.astype(BF16)
    r1 = v - h1.astype(F32)
    h2 = r1.astype(BF16)
    r2 = r1 - h2.astype(F32)
    return h1, h2, r2.astype(BF16)


def _dot3(v, m3):
    return _dot(jnp.concatenate(_split3(v), axis=1), m3)


def _dot3_left(m3, v):
    return _dot(m3, jnp.concatenate(_split3(v), axis=0))


def _dot2(v, m2):
    h1 = v.astype(BF16)
    h2 = (v - h1.astype(F32)).astype(BF16)
    return _dot(jnp.concatenate([h1, h2], axis=1), m2)


def _layer_norm(r, g, b):
    mu = jnp.mean(r, -1, keepdims=True)
    d = r - mu
    var = jnp.mean(d * d, -1, keepdims=True)
    return d * lax.rsqrt(var + LN_EPS) * g + b


def _ada_kernel(c_ref, w_ref, b_ref, o_ref):
    o_ref[0] = jnp.dot(c_ref[...], w_ref[0], preferred_element_type=F32,
                       precision=lax.Precision.HIGHEST) + b_ref[0]


def _ada_mod(c, w_ada, b_ada):
    depth, d, d6 = w_ada.shape
    nb = c.shape[0]
    rows = 8
    c_pad = jnp.zeros((rows, d), F32).at[:nb].set(c)
    out = pl.pallas_call(
        _ada_kernel,
        out_shape=jax.ShapeDtypeStruct((depth, rows, d6), F32),
        grid=(depth, d6 // d),
        in_specs=[pl.BlockSpec((rows, d), lambda l, j: (0, 0)),
                  pl.BlockSpec((1, d, d), lambda l, j: (l, 0, j)),
                  pl.BlockSpec((1, 1, d), lambda l, j: (l, 0, j))],
        out_specs=pl.BlockSpec((1, rows, d), lambda l, j: (l, 0, j)),
        compiler_params=_cparams("arbitrary", "arbitrary"),
        name="ada_mod",
    )(c_pad, w_ada, b_ada.reshape(depth, 1, d6))
    return out[:, :nb].reshape(depth, nb, 6, d)


def _rotary_kernel(pos_ref, cos_ref, sin_ref):
    half = HEAD_DIM // 2
    lane = lax.broadcasted_iota(I32, (1, LANES), 1)
    jj = lane % HEAD_DIM
    idx = (jj % half).astype(F32)
    inv = jnp.exp(-math.log(10000.0) * idx / half)
    ang = pos_ref[0].astype(F32) * inv
    cos_ref[0] = jnp.cos(ang)
    sin_ref[0] = jnp.where(jj < half, -1.0, 1.0) * jnp.sin(ang)


def _rotary_tables(positions, tl):
    nb, L = positions.shape
    pos = positions.reshape(nb, L, 1)
    return pl.pallas_call(
        _rotary_kernel,
        out_shape=[jax.ShapeDtypeStruct((nb, L, LANES), F32)] * 2,
        grid=(nb, L // tl),
        in_specs=[pl.BlockSpec((1, tl, 1), lambda b, i: (b, i, 0))],
        out_specs=[pl.BlockSpec((1, tl, LANES), lambda b, i: (b, i, 0))] * 2,
        compiler_params=_cparams("arbitrary", "arbitrary"),
        name="rotary_tables",
    )(pos)


def _swa_bias_kernel(rb_ref, bucket_ref, band_ref, o_ref):
    bucket = bucket_ref[...]
    band = band_ref[...]
    for h in range(SWA_HEADS):
        acc = jnp.zeros(bucket.shape, F32)
        for b in range(REL_BUCKETS):
            acc = jnp.where(bucket == b, rb_ref[b, h], acc)
        o_ref[h] = jnp.where(band > 0, acc, MASK_VALUE)


def _t5_bucket(dist):
    exact = REL_BUCKETS // 2
    df = jnp.maximum(dist, 1).astype(F32)
    large = exact + (jnp.log(df / exact) / math.log(CHUNK / exact) * (REL_BUCKETS - exact)).astype(I32)
    large = jnp.minimum(large, REL_BUCKETS - 1)
    return jnp.where(dist < exact, dist, large)


def _swa_bias_table(rel_bias):
    W = CHUNK
    qi = jnp.arange(W)[:, None]
    kj = jnp.arange(2 * W)[None, :]
    dist = qi + W - kj
    band = ((dist >= 0) & (dist < W)).astype(I32)
    bucket = _t5_bucket(jnp.clip(dist, 0, W - 1)).astype(I32)
    return pl.pallas_call(
        _swa_bias_kernel,
        out_shape=jax.ShapeDtypeStruct((SWA_HEADS, W, 2 * W), F32),
        in_specs=[pl.BlockSpec(memory_space=pltpu.SMEM),
                  pl.BlockSpec(memory_space=pltpu.VMEM),
                  pl.BlockSpec(memory_space=pltpu.VMEM)],
        out_specs=pl.BlockSpec(memory_space=pltpu.VMEM),
        name="swa_bias_table",
    )(rel_bias, bucket, band)


def _in_proj_kernel(x_ref, mod_ref, w_ref, *out_refs, widths):
    sh = mod_ref[0, 0, 0:1, :]
    sc = mod_ref[0, 0, 1:2, :]
    h = (x_ref[0] * (1.0 + sc) + sh).astype(BF16)
    off = 0
    for ref, width in zip(out_refs, widths):
        step = 512 if width % 512 == 0 else 128
        for j in range(0, width, step):
            ref[0, :, j:j + step] = _dot(h, w_ref[:, off + j:off + j + step]).astype(ref.dtype)
        off += width


def _in_proj(x, mod_l, w_cat, widths, dtypes, tm):
    nb, L, d = x.shape
    return pl.pallas_call(
        functools.partial(_in_proj_kernel, widths=widths),
        out_shape=[jax.ShapeDtypeStruct((nb, L, w), t) for w, t in zip(widths, dtypes)],
        grid=(nb, L // tm),
        in_specs=[pl.BlockSpec((1, tm, d), lambda b, i: (b, i, 0)),
                  pl.BlockSpec((1, 1, 6, d), lambda b, i: (0, b, 0, 0)),
                  pl.BlockSpec(w_cat.shape, lambda b, i: (0, 0))],
        out_specs=[pl.BlockSpec((1, tm, w), lambda b, i: (b, i, 0)) for w in widths],
        compiler_params=_cparams("arbitrary", "arbitrary"),
        name="in_proj",
    )(x, mod_l, w_cat)


def _head_lane_mask(width, head):
    lane = lax.broadcasted_iota(I32, (1, width), 1)
    return (lane // HEAD_DIM) == head


def _rotate_half(t):
    width = t.shape[-1]
    lane = lax.broadcasted_iota(I32, (1, width), 1)
    half = HEAD_DIM // 2
    fwd = pltpu.roll(t, width - half, axis=1)
    bwd = pltpu.roll(t, half, axis=1)
    return jnp.where((lane % HEAD_DIM) < half, fwd, bwd)


def _retention_body(u_ref, cos_ref, sin_ref, din_ref, dq_ref, dk_ref, dc_ref,
                    bmask_ref, avg_ref, o_ref, state_ref, *, n_chunks):
    rd = RET_HEADS * HEAD_DIM
    masks = [_head_lane_mask(rd, h) for h in range(RET_HEADS)]

    def stack_heads(t):
        return jnp.concatenate([jnp.where(m, t, 0.0) for m in masks], axis=0).astype(BF16)

    state = state_ref[...]
    for ci in range(n_chunks):
        rows = slice(ci * CHUNK, (ci + 1) * CHUNK)
        cos = cos_ref[0, rows, :]
        sin = sin_ref[0, rows, :]
        cos2 = jnp.concatenate([cos, cos], axis=1)
        sin2 = jnp.concatenate([sin, sin], axis=1)
        q = u_ref[0, rows, 0:rd].astype(F32)
        k = u_ref[0, rows, rd:2 * rd].astype(F32)
        v = u_ref[0, rows, 2 * rd:3 * rd].astype(F32)
        g = u_ref[0, rows, 3 * rd:4 * rd].astype(F32)
        qr = q * cos2 + _rotate_half(q) * sin2
        kr = (k * cos2 + _rotate_half(k) * sin2) * (HEAD_DIM ** -0.5)
        scores = _dot_nt(qr.astype(BF16), stack_heads(kr)) * din_ref[...]
        inner = _dot(scores.astype(BF16), stack_heads(v))
        cross = _dot((qr * dq_ref[...]).astype(BF16), state.astype(BF16))
        o = inner + cross
        kd_t = (kr * dk_ref[...]).T.astype(BF16)
        kv = _dot(kd_t, v.astype(BF16))
        state = dc_ref[...] * state + bmask_ref[...] * kv
        mu = _dot2(o, avg_ref[...])
        dev = o - mu
        var = _dot2(dev * dev, avg_ref[...])
        on = dev * lax.rsqrt(var + LN_EPS)
        o_ref[0, rows, :] = (_silu(g) * on).astype(o_ref.dtype)
    state_ref[...] = state


def _retention_tables():
    H, d, C = RET_HEADS, HEAD_DIM, CHUNK
    log_gamma = jnp.log(1.0 - 2.0 ** (-5.0 - jnp.arange(H, dtype=F32)))
    idx = jnp.arange(C, dtype=F32)
    diff = idx[:, None] - idx[None, :]
    decay_in = jnp.where(diff >= 0, jnp.exp(log_gamma[:, None, None] * jnp.maximum(diff, 0.0)), 0.0)
    decay_q = jnp.exp(log_gamma[:, None] * (idx + 1.0))
    decay_k = jnp.exp(log_gamma[:, None] * (C - 1.0 - idx))
    decay_chunk = jnp.exp(log_gamma * C)
    din = decay_in.transpose(1, 0, 2).reshape(C, H * C)
    dq = jnp.repeat(decay_q.T, d, axis=1)
    dk = jnp.repeat(decay_k.T, d, axis=1)
    dc = jnp.repeat(decay_chunk, d)[None, :]
    head = jnp.arange(H * d) // d
    bmask = (head[:, None] == head[None, :]).astype(F32)
    avg = jnp.tile((bmask / d).astype(BF16), (2, 1))
    return din, dq, dk, dc, bmask, avg


def _swa_body(u_ref, bias_ref, sink_ref, o_ref, kprev_ref, vprev_ref, *, n_chunks):
    W = CHUNK
    qd = SWA_HEADS * HEAD_DIM
    kvd = SWA_KV_HEADS * HEAD_DIM
    first_step = pl.program_id(1) == 0
    lane = lax.broadcasted_iota(I32, (1, LANES), 1)
    low = lane < HEAD_DIM
    col = lax.broadcasted_iota(I32, (1, 2 * W), 1)

    kprev = kprev_ref[...]
    vprev = vprev_ref[...]
    sink = sink_ref[...]
    for ci in range(n_chunks):
        rows = slice(ci * W, (ci + 1) * W)
        qa = u_ref[0, rows, 0:LANES].astype(F32)
        qb = u_ref[0, rows, LANES:qd].astype(F32)
        k = u_ref[0, rows, qd:qd + kvd].astype(BF16)
        v = u_ref[0, rows, qd + kvd:qd + 2 * kvd].astype(BF16)
        q4 = jnp.concatenate([jnp.where(low, qa, 0.0), jnp.where(low, 0.0, qa),
                              jnp.where(low, qb, 0.0), jnp.where(low, 0.0, qb)],
                             axis=0).astype(BF16)
        kband = jnp.concatenate([kprev, k], axis=0)
        vband = jnp.concatenate([vprev, v], axis=0)
        logits = _dot_nt(q4, kband) * (HEAD_DIM ** -0.5) + bias_ref[...]
        if ci == 0:
            logits = jnp.where(jnp.logical_and(first_step, col < W), MASK_VALUE, logits)
        m = jnp.maximum(jnp.max(logits, -1, keepdims=True), sink)
        p = jnp.exp(logits - m)
        denom = jnp.sum(p, -1, keepdims=True) + jnp.exp(sink - m)
        res = _dot(p.astype(BF16), vband) / denom
        out_a = jnp.where(low, res[0:W], res[W:2 * W])
        out_b = jnp.where(low, res[2 * W:3 * W], res[3 * W:4 * W])
        o_ref[0, rows, 0:LANES] = out_a.astype(o_ref.dtype)
        o_ref[0, rows, LANES:qd] = out_b.astype(o_ref.dtype)
        kprev, vprev = k, v
    kprev_ref[...] = kprev
    vprev_ref[...] = vprev


SWA_HEAD_ORDER = (0, 2, 1, 3)


def _ssd_body(u_ref, dt_ref, cw_ref, cb_ref, dtb_c_ref, alog_c_ref, dskip_ref, nw_ref,
              tril_ref, triu_ref, expand_ref, gmask_ref,
              o_ref, state_ref, ext_ref, *, n_chunks):
    C = CHUNK
    sd = SSD_HEADS * HEAD_DIM
    gn = SSD_GROUPS * SSD_STATE
    cd = sd + 2 * gn
    tl = n_chunks * C
    slab = 2 * LANES
    heads_per_group = SSD_HEADS // SSD_GROUPS
    heads_per_slab = slab // HEAD_DIM
    lane = lax.broadcasted_iota(I32, (1, LANES), 1)
    low = lane < SSD_STATE
    slab_masks = [_head_lane_mask(slab, hh) for hh in range(heads_per_slab)]
    row_i = lax.broadcasted_iota(I32, (C, C), 0)
    col_i = lax.broadcasted_iota(I32, (C, C), 1)
    causal = row_i >= col_i
    neg_a_c = -jnp.exp(alog_c_ref[...])

    ext_ref[8:8 + tl, :] = u_ref[0, :, sd:sd + cd].astype(F32)
    state = state_ref[...]
    for ci in range(n_chunks):
        rows = slice(ci * C, (ci + 1) * C)
        z = u_ref[0, rows, 0:sd].astype(F32)
        dt_raw = dt_ref[0, rows, :]
        base = 8 + ci * C
        conv = cb_ref[...]
        for w in range(SSD_CONV):
            shift = SSD_CONV - 1 - w
            conv = conv + cw_ref[w:w + 1, :] * ext_ref[base - shift:base - shift + C, :]
        xbc = _silu(conv)
        xs = xbc[:, 0:sd]
        bm = xbc[:, sd:sd + gn]
        cm = xbc[:, sd + gn:cd]

        dt_c = _softplus(dt_raw + dtb_c_ref[...])
        a_c = neg_a_c * dt_c
        acs_c = _dot3_left(tril_ref[...], a_c)
        acs_t = _dot3(a_c.T, triu_ref[...])
        spread = _dot3(acs_c, expand_ref[...])
        acs_x = spread[:, 0:sd]
        dt_x = _dot3(dt_c, expand_ref[:, 0:sd])
        xdt = xs * dt_x

        bstack = jnp.concatenate([jnp.where(low, bm, 0.0), jnp.where(low, 0.0, bm)],
                                 axis=0).astype(BF16)
        cb = _dot_nt(cm.astype(BF16), bstack)
        y_diag = []
        for s in range(sd // slab):
            ms = []
            for hh in range(heads_per_slab):
                h = s * heads_per_slab + hh
                g = h // heads_per_group
                col_bcast = spread[:, sd + h * LANES:sd + (h + 1) * LANES]
                seg = col_bcast - acs_t[h:h + 1, :]
                lmat = jnp.exp(jnp.where(causal, seg, MASK_VALUE))
                ms.append((cb[:, g * C:(g + 1) * C] * lmat).astype(BF16))
            xslab = xdt[:, s * slab:(s + 1) * slab]
            xstack = jnp.concatenate([jnp.where(m, xslab, 0.0) for m in slab_masks],
                                     axis=0).astype(BF16)
            y_diag.append(_dot(jnp.concatenate(ms, axis=1), xstack))
        y_diag = jnp.concatenate(y_diag, axis=1)

        y_off = _dot(cm.astype(BF16), state.astype(BF16)) * jnp.exp(acs_x)
        last = acs_x[C - 1:C, :]
        dec = jnp.exp(last - acs_x)
        new = _dot(bm.T.astype(BF16), (xdt * dec).astype(BF16))
        state = jnp.exp(last) * state + gmask_ref[...] * new

        y = y_diag + y_off + xs * dskip_ref[...]
        hgate = y * _silu(z)
        gw = sd // SSD_GROUPS
        for g in range(SSD_GROUPS):
            hg = hgate[:, g * gw:(g + 1) * gw]
            ms_ = jnp.mean(hg * hg, -1, keepdims=True)
            o_ref[0, rows, g * gw:(g + 1) * gw] = (
                hg * lax.rsqrt(ms_ + LN_EPS) * nw_ref[:, g * gw:(g + 1) * gw]).astype(o_ref.dtype)
    state_ref[...] = state
    ext_ref[0:8, :] = ext_ref[tl:tl + 8, :]


def _ssd_tables():
    C = CHUNK
    sd = SSD_HEADS * HEAD_DIM
    gn = SSD_GROUPS * SSD_STATE
    t = jnp.arange(C)
    tril = (t[:, None] >= t[None, :]).astype(BF16)
    triu = (t[:, None] <= t[None, :]).astype(BF16)
    r = jnp.arange(LANES)[:, None]
    eexp = ((r == (jnp.arange(sd)[None, :] // HEAD_DIM)) & (r < SSD_HEADS)).astype(BF16)
    bsel = ((r == (jnp.arange(SSD_HEADS * LANES)[None, :] // LANES)) & (r < SSD_HEADS)).astype(BF16)
    heads_per_group = SSD_HEADS // SSD_GROUPS
    row_g = jnp.arange(gn)[:, None] // SSD_STATE
    col_g = (jnp.arange(sd)[None, :] // HEAD_DIM) // heads_per_group
    gmask = (row_g == col_g).astype(F32)
    expand = jnp.concatenate([eexp, bsel], axis=1)
    return jnp.tile(tril, (1, 3)), jnp.tile(triu, (3, 1)), jnp.tile(expand, (3, 1)), gmask


N_RET_TABLES = 6
N_SSD_CONSTS = 10


def _mixers_kernel(*refs, n_chunks):
    u_ret, cos, sin, u_ssd, u_dt, u_swa = refs[:6]
    pos = 6
    ret_tables = refs[pos:pos + N_RET_TABLES]
    pos += N_RET_TABLES
    ssd_consts = refs[pos:pos + N_SSD_CONSTS]
    pos += N_SSD_CONSTS
    bias, sink = refs[pos:pos + 2]
    pos += 2
    y_ret, y_ssd, y_swa = refs[pos:pos + 3]
    ret_state, ssd_state, ssd_ext, kprev, vprev = refs[pos + 3:]

    @pl.when(pl.program_id(1) == 0)
    def _():
        ret_state[...] = jnp.zeros_like(ret_state)
        ssd_state[...] = jnp.zeros_like(ssd_state)
        ssd_ext[0:8, :] = jnp.zeros((8, ssd_ext.shape[1]), F32)
        kprev[...] = jnp.zeros_like(kprev)
        vprev[...] = jnp.zeros_like(vprev)

    _ssd_body(u_ssd, u_dt, *ssd_consts, y_ssd, ssd_state, ssd_ext, n_chunks=n_chunks)
    _swa_body(u_swa, bias, sink, y_swa, kprev, vprev, n_chunks=n_chunks)
    _retention_body(u_ret, cos, sin, *ret_tables, y_ret, ret_state, n_chunks=n_chunks)


def _mixers(u_ret, u_ssd, u_dt, u_swa, cos_t, sin_t, bias_tab, sinks_l, conv_w, conv_b,
            dt_bias, a_log, d_skip, norm_w, tl):
    nb, L, _ = u_ret.shape
    W = CHUNK
    rd = RET_HEADS * HEAD_DIM
    sd = SSD_HEADS * HEAD_DIM
    qd = SWA_HEADS * HEAD_DIM
    cd = conv_w.shape[-1]
    ret_tables = _retention_tables()
    pad = lambda v: jnp.zeros((1, LANES), F32).at[0, :SSD_HEADS].set(v)
    rep = lambda v: jnp.repeat(v, HEAD_DIM)[None, :]
    ssd_consts = (conv_w, conv_b[None, :], pad(dt_bias), pad(a_log), rep(d_skip),
                  norm_w[None, :]) + _ssd_tables()
    order = jnp.array(SWA_HEAD_ORDER)
    bias_stacked = bias_tab[order].reshape(SWA_HEADS * W, 2 * W)
    sink_col = jnp.repeat(sinks_l.astype(F32)[order], W)[:, None]
    assert len(ret_tables) == N_RET_TABLES and len(ssd_consts) == N_SSD_CONSTS
    tok = lambda a: pl.BlockSpec((1, tl, a.shape[-1]), lambda b, i: (b, i, 0))
    const = lambda a: pl.BlockSpec(a.shape, lambda b, i: (0,) * a.ndim)
    tokens = (u_ret, cos_t, sin_t, u_ssd, u_dt, u_swa)
    consts = ret_tables + ssd_consts + (bias_stacked, sink_col)
    return pl.pallas_call(
        functools.partial(_mixers_kernel, n_chunks=tl // CHUNK),
        out_shape=[jax.ShapeDtypeStruct((nb, L, w), BF16) for w in (rd, sd, qd)],
        grid=(nb, L // tl),
        in_specs=[tok(a) for a in tokens] + [const(a) for a in consts],
        out_specs=[pl.BlockSpec((1, tl, w), lambda b, i: (b, i, 0)) for w in (rd, sd, qd)],
        scratch_shapes=[pltpu.VMEM((rd, rd), F32),
                        pltpu.VMEM((SSD_GROUPS * SSD_STATE, sd), F32),
                        pltpu.VMEM((8 + tl, cd), F32),
                        pltpu.VMEM((W, LANES), BF16), pltpu.VMEM((W, LANES), BF16)],
        compiler_params=_cparams("arbitrary", "arbitrary"),
        name="mixers",
    )(*tokens, *consts)


def _out_proj_kernel(x_ref, yr_ref, ys_ref, ya_ref, mod_ref, wr_ref, ws_ref, wa_ref,
                     g_ref, b_ref, o_ref, *, alpha):
    mix = (_dot(yr_ref[0], wr_ref[...]) + _dot(ys_ref[0], ws_ref[...])
           + _dot(ya_ref[0], wa_ref[...]))
    gate = mod_ref[0, 0, 2:3, :]
    r = alpha * x_ref[0] + (1.0 + gate) * mix
    o_ref[0] = _layer_norm(r, g_ref[...], b_ref[...])


def _out_proj(x, y_ret, y_ssd, y_swa, mod_l, w_r, w_s, w_a, ln_g, ln_b, alpha, tm):
    nb, L, d = x.shape
    tok = lambda w: pl.BlockSpec((1, tm, w), lambda b, i: (b, i, 0))
    const = lambda a: pl.BlockSpec(a.shape, lambda b, i: (0,) * a.ndim)
    return pl.pallas_call(
        functools.partial(_out_proj_kernel, alpha=alpha),
        out_shape=jax.ShapeDtypeStruct((nb, L, d), F32),
        grid=(nb, L // tm),
        in_specs=[tok(d), tok(y_ret.shape[-1]), tok(y_ssd.shape[-1]), tok(y_swa.shape[-1]),
                  pl.BlockSpec((1, 1, 6, d), lambda b, i: (0, b, 0, 0)),
                  const(w_r), const(w_s), const(w_a), const(ln_g), const(ln_b)],
        out_specs=tok(d),
        compiler_params=_cparams("arbitrary", "arbitrary"),
        name="out_proj_ln",
    )(x, y_ret, y_ssd, y_swa, mod_l, w_r, w_s, w_a, ln_g, ln_b)


FF_CHUNK = 256


def _dense_ffn_kernel(x_ref, mod_ref, wg_ref, wu_ref, wd_ref, g_ref, b_ref, o_ref, *, alpha):
    x = x_ref[0]
    sh = mod_ref[0, 0, 3:4, :]
    sc = mod_ref[0, 0, 4:5, :]
    gate = mod_ref[0, 0, 5:6, :]
    h = (x * (1.0 + sc) + sh).astype(BF16)
    dff = wg_ref.shape[-1]
    acc = jnp.zeros(x.shape, F32)
    for j in range(0, dff, FF_CHUNK):
        gj = _dot(h, wg_ref[:, j:j + FF_CHUNK])
        uj = _dot(h, wu_ref[:, j:j + FF_CHUNK])
        acc = acc + _dot((_silu(gj) * uj).astype(BF16), wd_ref[j:j + FF_CHUNK, :])
    r = alpha * x + (1.0 + gate) * acc
    o_ref[0] = _layer_norm(r, g_ref[...], b_ref[...])


def _dense_ffn(x, mod_l, wg, wu, wd, ln_g, ln_b, alpha, tm):
    nb, L, d = x.shape
    const = lambda a: pl.BlockSpec(a.shape, lambda b, i: (0,) * a.ndim,
                                   pipeline_mode=pl.Buffered(1))
    return pl.pallas_call(
        functools.partial(_dense_ffn_kernel, alpha=alpha),
        out_shape=jax.ShapeDtypeStruct((nb, L, d), F32),
        grid=(nb, L // tm),
        in_specs=[pl.BlockSpec((1, tm, d), lambda b, i: (b, i, 0)),
                  pl.BlockSpec((1, 1, 6, d), lambda b, i: (0, b, 0, 0)),
                  const(wg), const(wu), const(wd), const(ln_g), const(ln_b)],
        out_specs=pl.BlockSpec((1, tm, d), lambda b, i: (b, i, 0)),
        compiler_params=_cparams("arbitrary", "arbitrary"),
        name="dense_ffn_ln",
    )(x, mod_l, wg, wu, wd, ln_g, ln_b)


def _router_kernel(x_ref, mod_ref, wr_ref, br_ref, trils_ref, h_ref, route_ref, cnt_ref,
                   base_ref):
    first = jnp.logical_and(pl.program_id(0) == 0, pl.program_id(1) == 0)

    @pl.when(first)
    def _():
        base_ref[...] = jnp.zeros_like(base_ref)

    sh = mod_ref[0, 0, 3:4, :]
    sc = mod_ref[0, 0, 4:5, :]
    h = x_ref[0] * (1.0 + sc) + sh
    h_ref[0] = h
    logits = jnp.dot(h, wr_ref[...], preferred_element_type=F32,
                     precision=lax.Precision.HIGHEST) + br_ref[...]
    lane = lax.broadcasted_iota(I32, logits.shape, 1).astype(F32)
    logits = jnp.where(lane < N_EXPERTS, logits, MASK_VALUE)
    v1 = jnp.max(logits, -1, keepdims=True)
    e1 = jnp.min(jnp.where(logits == v1, lane, float(LANES)), -1, keepdims=True)
    rest = jnp.where(lane == e1, MASK_VALUE, logits)
    v2 = jnp.max(rest, -1, keepdims=True)
    e2 = jnp.min(jnp.where(rest == v2, lane, float(LANES)), -1, keepdims=True)
    t = jnp.exp(v2 - v1)
    w1 = 1.0 / (1.0 + t)
    w2 = t / (1.0 + t)
    hot1 = (lane == e1).astype(F32)
    hot2 = (lane == e2).astype(F32)
    both = hot1 + hot2
    base = base_ref[0:1, :]
    before = _dot(trils_ref[...], both.astype(BF16)) + base
    rank1 = jnp.sum(hot1 * before, -1, keepdims=True)
    rank2 = jnp.sum(hot2 * before, -1, keepdims=True)
    total = base + jnp.sum(both, 0, keepdims=True)
    base_ref[0:1, :] = total
    cnt_ref[...] = jnp.broadcast_to(total, cnt_ref.shape)
    route = jnp.where(lane == 0, e1, 0.0)
    route = jnp.where(lane == 1, e2, route)
    route = jnp.where(lane == 2, rank1, route)
    route = jnp.where(lane == 3, rank2, route)
    route = jnp.where(lane == 4, w1, route)
    route = jnp.where(lane == 5, w2, route)
    route_ref[0] = route


def _router(x, mod_l, w_router, b_router, tm):
    nb, L, d = x.shape
    wr = jnp.zeros((d, LANES), F32).at[:, :N_EXPERTS].set(w_router)
    br = jnp.zeros((1, LANES), F32).at[0, :N_EXPERTS].set(b_router)
    t = jnp.arange(tm)
    tril_strict = (t[:, None] > t[None, :]).astype(BF16)
    const = lambda a: pl.BlockSpec(a.shape, lambda b, i: (0,) * a.ndim)
    return pl.pallas_call(
        _router_kernel,
        out_shape=[jax.ShapeDtypeStruct((nb, L, d), F32),
                   jax.ShapeDtypeStruct((nb, L, LANES), F32),
                   jax.ShapeDtypeStruct((8, LANES), F32)],
        grid=(nb, L // tm),
        in_specs=[pl.BlockSpec((1, tm, d), lambda b, i: (b, i, 0)),
                  pl.BlockSpec((1, 1, 6, d), lambda b, i: (0, b, 0, 0)),
                  const(wr), const(br), const(tril_strict)],
        out_specs=[pl.BlockSpec((1, tm, d), lambda b, i: (b, i, 0)),
                   pl.BlockSpec((1, tm, LANES), lambda b, i: (b, i, 0)),
                   pl.BlockSpec((8, LANES), lambda b, i: (0, 0))],
        scratch_shapes=[pltpu.VMEM((8, LANES), F32)],
        compiler_params=_cparams("arbitrary", "arbitrary"),
        name="moe_router",
    )(x, mod_l, wr, br, tril_strict)


TOP_K = 2
EXPERT_FF_SPLITS = 2


def _ff_chunks(width, step=512):
    return [(a, min(a + step, width)) for a in range(0, width, step)]


def _expert_kernel(te_ref, nu_ref, gsrc_ref, sdst_ref,
                   h_ref, wg_ref, wu_ref, wd_ref, yk_ref,
                   xbuf, ybuf, hbuf, gsem, ssem, *, tme):
    i = pl.program_id(0)
    j = pl.program_id(1)
    n_used = nu_ref[0]
    slot = i % 2
    other = 1 - slot
    used = i < n_used
    chunks = _ff_chunks(wg_ref.shape[-1])
    rows_per_step = tme // EXPERT_FF_SPLITS
    width = wg_ref.shape[-1]
    bounds = [rows_per_step * b // width for _, b in chunks]
    row_ranges = list(zip([0] + bounds[:-1], bounds))

    def gather(tile, slot_, r):
        return pltpu.make_async_copy(h_ref.at[pl.ds(gsrc_ref[tile * tme + r], 1)],
                                     xbuf.at[slot_, pl.ds(r, 1)], gsem.at[slot_])

    def wait_gather(slot_):
        pltpu.make_async_copy(h_ref.at[pl.ds(0, tme)], xbuf.at[slot_], gsem.at[slot_]).wait()

    def scatter(block, slot_, r):
        return pltpu.make_async_copy(ybuf.at[slot_, pl.ds(r, 1)],
                                     yk_ref.at[pl.ds(sdst_ref[block * tme + r], 1)],
                                     ssem.at[slot_])

    def wait_scatter(slot_):
        pltpu.make_async_copy(ybuf.at[slot_], yk_ref.at[pl.ds(0, tme)], ssem.at[slot_]).wait()

    @pl.when(jnp.logical_and(i == 0, j == 0))
    def _():
        ybuf[1] = jnp.zeros(ybuf.shape[1:], F32)

        def body(r, carry):
            gather(0, 0, r).start()
            return carry
        lax.fori_loop(0, tme, body, 0, unroll=8)

    @pl.when(used)
    def _():
        @pl.when(j == 0)
        def _():
            wait_gather(slot)
            hbuf[...] = xbuf[slot].astype(BF16)
            ybuf[slot] = jnp.zeros(ybuf.shape[1:], F32)

        h = hbuf[...]
        for c, (a, b) in enumerate(chunks):
            for r in range(*row_ranges[c]):
                gather(i + 1, other, j * rows_per_step + r).start()
                scatter(i, other, j * rows_per_step + r).start()
            gj = _dot(h, wg_ref[0, :, a:b])
            uj = _dot(h, wu_ref[0, :, a:b])
            part = _dot((_silu(gj) * uj).astype(BF16), wd_ref[0, a:b, :])
            ybuf[slot] = ybuf[slot] + part

        @pl.when(j == EXPERT_FF_SPLITS - 1)
        def _():
            wait_scatter(other)

            @pl.when(i == n_used - 1)
            def _():
                wait_gather(other)

                def body(r, carry):
                    scatter(i + 1, slot, r).start()
                    return carry
                lax.fori_loop(0, tme, body, 0, unroll=8)
                wait_scatter(slot)


def _expert_ffn(h_flat, tile_expert, n_used, gsrc, sdst, wg, wu, wd, tme):
    T, d = h_flat.shape
    dff = wg.shape[-1]
    dffh = dff // EXPERT_FF_SPLITS
    n_tiles = gsrc.shape[0] // tme
    yk_rows = TOP_K * T + tme

    def half(i, j):
        return jnp.where(i % 2 == 0, j, EXPERT_FF_SPLITS - 1 - j)

    return pl.pallas_call(
        functools.partial(_expert_kernel, tme=tme),
        out_shape=jax.ShapeDtypeStruct((yk_rows, d), F32),
        grid_spec=pltpu.PrefetchScalarGridSpec(
            num_scalar_prefetch=4,
            grid=(n_tiles, EXPERT_FF_SPLITS),
            in_specs=[pl.BlockSpec(memory_space=pl.ANY),
                      pl.BlockSpec((1, d, dffh), lambda i, j, te, *_: (te[i], 0, half(i, j))),
                      pl.BlockSpec((1, d, dffh), lambda i, j, te, *_: (te[i], 0, half(i, j))),
                      pl.BlockSpec((1, dffh, d), lambda i, j, te, *_: (te[i], half(i, j), 0))],
            out_specs=pl.BlockSpec(memory_space=pl.ANY),
            scratch_shapes=[pltpu.VMEM((2, tme, d), F32), pltpu.VMEM((2, tme, d), F32),
                            pltpu.VMEM((tme, d), BF16),
                            pltpu.SemaphoreType.DMA((2,)), pltpu.SemaphoreType.DMA((2,))]),
        compiler_params=_cparams("arbitrary", "arbitrary"),
        name="moe_experts",
    )(tile_expert, n_used, gsrc, sdst, h_flat, wg, wu, wd)


def _moe_finish_kernel(x_ref, route_ref, mod_ref, y0_ref, y1_ref, g_ref, b_ref, o_ref, *, alpha):
    route = route_ref[0]
    f = route[:, 4:5] * y0_ref[...] + route[:, 5:6] * y1_ref[...]
    gate = mod_ref[0, 0, 5:6, :]
    r = alpha * x_ref[0] + (1.0 + gate) * f
    o_ref[0] = _layer_norm(r, g_ref[...], b_ref[...])


def _moe_finish(x, route, mod_l, yk, ln_g, ln_b, alpha, tm):
    nb, L, d = x.shape
    tiles_per_seq = L // tm
    tiles = nb * tiles_per_seq
    return pl.pallas_call(
        functools.partial(_moe_finish_kernel, alpha=alpha),
        out_shape=jax.ShapeDtypeStruct((nb, L, d), F32),
        grid=(nb, tiles_per_seq),
        in_specs=[pl.BlockSpec((1, tm, d), lambda b, i: (b, i, 0)),
                  pl.BlockSpec((1, tm, LANES), lambda b, i: (b, i, 0)),
                  pl.BlockSpec((1, 1, 6, d), lambda b, i: (0, b, 0, 0)),
                  pl.BlockSpec((tm, d), lambda b, i: (b * tiles_per_seq + i, 0)),
                  pl.BlockSpec((tm, d), lambda b, i: (tiles + b * tiles_per_seq + i, 0)),
                  pl.BlockSpec(ln_g.shape, lambda b, i: (0, 0)),
                  pl.BlockSpec(ln_b.shape, lambda b, i: (0, 0))],
        out_specs=pl.BlockSpec((1, tm, d), lambda b, i: (b, i, 0)),
        compiler_params=_cparams("arbitrary", "arbitrary"),
        name="moe_finish_ln",
    )(x, route, mod_l, yk, yk, ln_g, ln_b)


def _row_index_kernel(dest_ref, gsrc0_ref, sdst0_ref, gsrc_ref, sdst_ref, sem, *, n_pairs, T, tme):
    init_g = pltpu.make_async_copy(gsrc0_ref, gsrc_ref, sem.at[0])
    init_s = pltpu.make_async_copy(sdst0_ref, sdst_ref, sem.at[1])
    init_g.start()
    init_s.start()
    init_g.wait()
    init_s.wait()

    def body(f, carry):
        row = dest_ref[f]
        tok = f // TOP_K
        gsrc_ref[row] = tok
        sdst_ref[tme + row] = (f % TOP_K) * T + tok
        return carry
    lax.fori_loop(0, n_pairs, body, 0, unroll=8)


def _row_indices(dest, n_rows, T, tme):
    dump = TOP_K * T + jnp.arange(n_rows + tme, dtype=I32) % tme
    return pl.pallas_call(
        functools.partial(_row_index_kernel, n_pairs=dest.shape[0], T=T, tme=tme),
        out_shape=[jax.ShapeDtypeStruct((n_rows,), I32),
                   jax.ShapeDtypeStruct((n_rows + tme,), I32)],
        in_specs=[pl.BlockSpec(memory_space=pltpu.SMEM),
                  pl.BlockSpec(memory_space=pl.ANY),
                  pl.BlockSpec(memory_space=pl.ANY)],
        out_specs=[pl.BlockSpec(memory_space=pltpu.SMEM),
                   pl.BlockSpec(memory_space=pltpu.SMEM)],
        scratch_shapes=[pltpu.SemaphoreType.DMA((2,))],
        name="moe_row_indices",
    )(dest, jnp.zeros((n_rows,), I32), dump)


def _moe(x, mod_l, w_router, b_router, wg, wu, wd, ln_g, ln_b, alpha, tm_route, tme, tmf):
    nb, L, d = x.shape
    T = nb * L
    h, route, counts = _router(x, mod_l, w_router, b_router, tm_route)
    route_flat = route.reshape(T, LANES)
    e = route_flat[:, 0:2].astype(I32)
    rank = route_flat[:, 2:4].astype(I32)
    cnt = counts[0, :N_EXPERTS].astype(I32)
    tiles = (cnt + tme - 1) // tme
    tile_end = jnp.cumsum(tiles)
    group_start = (tile_end - tiles) * tme
    dest = (group_start[e] + rank).reshape(-1)
    n_tiles = (TOP_K * T) // tme + N_EXPERTS
    n_rows = n_tiles * tme
    tile_expert = jnp.minimum(
        jnp.searchsorted(tile_end, jnp.arange(n_tiles), side="right"), N_EXPERTS - 1).astype(I32)
    n_used = tile_end[-1:].astype(I32)
    gsrc, sdst = _row_indices(dest, n_rows, T, tme)
    yk = _expert_ffn(h.reshape(T, d), tile_expert, n_used, gsrc, sdst, wg, wu, wd, tme)
    return _moe_finish(x, route, mod_l, yk, ln_g, ln_b, alpha, tmf)


def kernel(x, c, positions, rel_bias, w_ada, b_ada, w_in, w_out, conv_w, conv_b, dt_bias, a_log,
           d_skip, ssd_norm_w, sinks, ln_g, ln_b, ffn_w_gate, ffn_w_up, ffn_w_down, router_w,
           router_b, expert_w_gate, expert_w_up, expert_w_down):
    depth = w_ada.shape[0]
    nb, L, d = x.shape
    alpha = (2 * depth) ** 0.25
    rd = RET_HEADS * HEAD_DIM
    sd = SSD_HEADS * HEAD_DIM
    cd = conv_w.shape[-1]
    qd = SWA_HEADS * HEAD_DIM
    kvd = SWA_KV_HEADS * HEAD_DIM
    sizes = (rd, rd, rd, rd, sd, cd, SSD_HEADS, qd, kvd, kvd)
    offs = np.concatenate([[0], np.cumsum(sizes)])
    tl = min(512, L)

    mod = _ada_mod(c, w_ada, b_ada)
    cos_t, sin_t = _rotary_tables(positions, tl)
    bias_tab = _swa_bias_table(rel_bias)

    q_perm = np.concatenate([np.arange(h * HEAD_DIM, (h + 1) * HEAD_DIM) for h in SWA_HEAD_ORDER])
    widths = (4 * rd, sd + cd, LANES, qd + 2 * kvd)
    dtypes = (BF16, BF16, F32, BF16)

    for layer in range(depth):
        wl = w_in[layer]
        seg = lambda i: wl[:, offs[i]:offs[i + 1]]
        dt_cols = jnp.zeros((d, LANES), F32).at[:, :SSD_HEADS].set(seg(6))
        w_cat = jnp.concatenate(
            [seg(0), seg(1), seg(2), seg(3), seg(4), seg(5), dt_cols,
             seg(7)[:, q_perm], seg(8), seg(9)], axis=1).astype(BF16)
        wo = w_out[layer]
        w_r = wo[0:rd].astype(BF16)
        w_s = wo[rd:rd + sd].astype(BF16)
        w_a = wo[rd + sd:][q_perm].astype(BF16)
        mod_l = mod[layer:layer + 1]

        u_ret, u_ssd, u_dt, u_swa = _in_proj(x, mod_l, w_cat, widths, dtypes, tl)
        y_ret, y_ssd, y_swa = _mixers(u_ret, u_ssd, u_dt, u_swa, cos_t, sin_t, bias_tab,
                                      sinks[layer], conv_w[layer], conv_b[layer], dt_bias[layer],
                                      a_log[layer], d_skip[layer], ssd_norm_w[layer], tl)
        x = _out_proj(x, y_ret, y_ssd, y_swa, mod_l, w_r, w_s, w_a,
                      ln_g[layer, 0][None, :], ln_b[layer, 0][None, :], alpha, tl)

        g2 = ln_g[layer, 1][None, :]
        b2 = ln_b[layer, 1][None, :]
        i = layer // 2
        if layer % 2 == 0:
            x = _dense_ffn(x, mod_l, ffn_w_gate[i].astype(BF16), ffn_w_up[i].astype(BF16),
                           ffn_w_down[i].astype(BF16), g2, b2, alpha, tl)
        else:
            x = _moe(x, mod_l, router_w[i], router_b[i], expert_w_gate[i].astype(BF16),
                     expert_w_up[i].astype(BF16), expert_w_down[i].astype(BF16), g2, b2, alpha,
                     tm_route=min(512, L), tme=min(512, L), tmf=min(512, L))
    return x
```

```python
import functools
import math

import numpy as np
import jax
import jax.numpy as jnp
from jax import lax
from jax.experimental import pallas as pl
from jax.experimental.pallas import tpu as pltpu

F32 = jnp.float32
BF16 = jnp.bfloat16
I32 = jnp.int32

HEAD_DIM = 64
CHUNK = 128
RET_HEADS = 4
SSD_HEADS = 8
SSD_GROUPS = 2
SSD_STATE = 64
SSD_CONV = 4
SWA_HEADS = 4
SWA_KV_HEADS = 2
REL_BUCKETS = 32
N_EXPERTS = 8
LN_EPS = 1e-5
LANES = 128
MASK_VALUE = -1e30

VMEM_LIMIT = 56 * 1024 * 1024


def _cparams(*sem):
    return pltpu.CompilerParams(dimension_semantics=sem, vmem_limit_bytes=VMEM_LIMIT)


def _silu(v):
    return v * (1.0 / (1.0 + jnp.exp(-v)))


def _softplus(v):
    return jnp.maximum(v, 0.0) + jnp.log(1.0 + jnp.exp(-jnp.abs(v)))


def _dot(a, b):
    return jnp.dot(a, b, preferred_element_type=F32)


def _dot_nt(a, b):
    return lax.dot_general(a, b, (((1,), (1,)), ((), ())), preferred_element_type=F32)


def _split3(v):
    h1 = v.astype(BF16)
    r1 = v - h1.astype(F32)
    h2 = r1.astype(BF16)
    r2 = r1 - h2.astype(F32)
    return h1, h2, r2.astype(BF16)


def _dot3(v, m3):
    return _dot(jnp.concatenate(_split3(v), axis=1), m3)


def _dot3_left(m3, v):
    return _dot(m3, jnp.concatenate(_split3(v), axis=0))


def _dot2(v, m2):
    h1 = v.astype(BF16)
    h2 = (v - h1.astype(F32)).astype(BF16)
    return _dot(jnp.concatenate([h1, h2], axis=1), m2)


def _layer_norm(r, g, b):
    mu = jnp.mean(r, -1, keepdims=True)
    d = r - mu
    var = jnp.mean(d * d, -1, keepdims=True)
    return d * lax.rsqrt(var + LN_EPS) * g + b


def _ada_kernel(c_ref, w_ref, b_ref, o_ref):
    o_ref[0] = jnp.dot(c_ref[...], w_ref[0], preferred_element_type=F32,
                       precision=lax.Precision.HIGHEST) + b_ref[0]


def _ada_mod(c, w_ada, b_ada):
    depth, d, d6 = w_ada.shape
    nb = c.shape[0]
    rows = 8
    c_pad = jnp.zeros((rows, d), F32).at[:nb].set(c)
    out = pl.pallas_call(
        _ada_kernel,
        out_shape=jax.ShapeDtypeStruct((depth, rows, d6), F32),
        grid=(depth, d6 // d),
        in_specs=[pl.BlockSpec((rows, d), lambda l, j: (0, 0)),
                  pl.BlockSpec((1, d, d), lambda l, j: (l, 0, j)),
                  pl.BlockSpec((1, 1, d), lambda l, j: (l, 0, j))],
        out_specs=pl.BlockSpec((1, rows, d), lambda l, j: (l, 0, j)),
        compiler_params=_cparams("arbitrary", "arbitrary"),
        name="ada_mod",
    )(c_pad, w_ada, b_ada.reshape(depth, 1, d6))
    return out[:, :nb].reshape(depth, nb, 6, d)


def _rotary_kernel(pos_ref, cos_ref, sin_ref):
    half = HEAD_DIM // 2
    lane = lax.broadcasted_iota(I32, (1, LANES), 1)
    jj = lane % HEAD_DIM
    idx = (jj % half).astype(F32)
    inv = jnp.exp(-math.log(10000.0) * idx / half)
    ang = pos_ref[0].astype(F32) * inv
    cos_ref[0] = jnp.cos(ang)
    sin_ref[0] = jnp.where(jj < half, -1.0, 1.0) * jnp.sin(ang)


def _rotary_tables(positions, tl):
    nb, L = positions.shape
    pos = positions.reshape(nb, L, 1)
    return pl.pallas_call(
        _rotary_kernel,
        out_shape=[jax.ShapeDtypeStruct((nb, L, LANES), F32)] * 2,
        grid=(nb, L // tl),
        in_specs=[pl.BlockSpec((1, tl, 1), lambda b, i: (b, i, 0))],
        out_specs=[pl.BlockSpec((1, tl, LANES), lambda b, i: (b, i, 0))] * 2,
        compiler_params=_cparams("arbitrary", "arbitrary"),
        name="rotary_tables",
    )(pos)


def _swa_bias_kernel(rb_ref, bucket_ref, band_ref, o_ref):
    bucket = bucket_ref[...]
    band = band_ref[...]
    for h in range(SWA_HEADS):
        acc = jnp.zeros(bucket.shape, F32)
        for b in range(REL_BUCKETS):
            acc = jnp.where(bucket == b, rb_ref[b, h], acc)
        o_ref[h] = jnp.where(band > 0, acc, MASK_VALUE)


def _t5_bucket(dist):
    exact = REL_BUCKETS // 2
    df = jnp.maximum(dist, 1).astype(F32)
    large = exact + (jnp.log(df / exact) / math.log(CHUNK / exact) * (REL_BUCKETS - exact)).astype(I32)
    large = jnp.minimum(large, REL_BUCKETS - 1)
    return jnp.where(dist < exact, dist, large)


def _swa_bias_table(rel_bias):
    W = CHUNK
    qi = jnp.arange(W)[:, None]
    kj = jnp.arange(2 * W)[None, :]
    dist = qi + W - kj
    band = ((dist >= 0) & (dist < W)).astype(I32)
    bucket = _t5_bucket(jnp.clip(dist, 0, W - 1)).astype(I32)
    return pl.pallas_call(
        _swa_bias_kernel,
        out_shape=jax.ShapeDtypeStruct((SWA_HEADS, W, 2 * W), F32),
        in_specs=[pl.BlockSpec(memory_space=pltpu.SMEM),
                  pl.BlockSpec(memory_space=pltpu.VMEM),
                  pl.BlockSpec(memory_space=pltpu.VMEM)],
        out_specs=pl.BlockSpec(memory_space=pltpu.VMEM),
        name="swa_bias_table",
    )(rel_bias, bucket, band)


def _in_proj_kernel(x_ref, mod_ref, w_ref, *out_refs, widths):
    sh = mod_ref[0, 0, 0:1, :]
    sc = mod_ref[0, 0, 1:2, :]
    h = (x_ref[0] * (1.0 + sc) + sh).astype(BF16)
    off = 0
    for ref, width in zip(out_refs, widths):
        for a, b in _ff_chunks(width):
            ref[0, :, a:b] = _dot(h, w_ref[:, off + a:off + b]).astype(ref.dtype)
        off += width


def _in_proj(x, mod_l, w_cat, widths, dtypes, tm):
    nb, L, d = x.shape
    return pl.pallas_call(
        functools.partial(_in_proj_kernel, widths=widths),
        out_shape=[jax.ShapeDtypeStruct((nb, L, w), t) for w, t in zip(widths, dtypes)],
        grid=(nb, L // tm),
        in_specs=[pl.BlockSpec((1, tm, d), lambda b, i: (b, i, 0)),
                  pl.BlockSpec((1, 1, 6, d), lambda b, i: (0, b, 0, 0)),
                  pl.BlockSpec(w_cat.shape, lambda b, i: (0, 0))],
        out_specs=[pl.BlockSpec((1, tm, w), lambda b, i: (b, i, 0)) for w in widths],
        compiler_params=_cparams("arbitrary", "arbitrary"),
        name="in_proj",
    )(x, mod_l, w_cat)


def _head_lane_mask(width, head):
    lane = lax.broadcasted_iota(I32, (1, width), 1)
    return (lane // HEAD_DIM) == head


def _rotate_half(t):
    width = t.shape[-1]
    lane = lax.broadcasted_iota(I32, (1, width), 1)
    half = HEAD_DIM // 2
    fwd = pltpu.roll(t, width - half, axis=1)
    bwd = pltpu.roll(t, half, axis=1)
    return jnp.where((lane % HEAD_DIM) < half, fwd, bwd)


def _retention_body(u_ref, cos_ref, sin_ref, din_ref, dq_ref, dk_ref, dc_ref,
                    bmask_ref, avg_ref, o_ref, state_ref, *, n_chunks):
    rd = RET_HEADS * HEAD_DIM
    masks = [_head_lane_mask(rd, h) for h in range(RET_HEADS)]

    def stack_heads(t):
        return jnp.concatenate([jnp.where(m, t, 0.0) for m in masks], axis=0).astype(BF16)

    state = state_ref[...]
    for ci in range(n_chunks):
        rows = slice(ci * CHUNK, (ci + 1) * CHUNK)
        cos = cos_ref[0, rows, :]
        sin = sin_ref[0, rows, :]
        cos2 = jnp.concatenate([cos, cos], axis=1)
        sin2 = jnp.concatenate([sin, sin], axis=1)
        q = u_ref[0, rows, 0:rd].astype(F32)
        k = u_ref[0, rows, rd:2 * rd].astype(F32)
        v = u_ref[0, rows, 2 * rd:3 * rd].astype(F32)
        g = u_ref[0, rows, 3 * rd:4 * rd].astype(F32)
        qr = q * cos2 + _rotate_half(q) * sin2
        kr = (k * cos2 + _rotate_half(k) * sin2) * (HEAD_DIM ** -0.5)
        scores = _dot_nt(qr.astype(BF16), stack_heads(kr)) * din_ref[...]
        inner = _dot(scores.astype(BF16), stack_heads(v))
        cross = _dot((qr * dq_ref[...]).astype(BF16), state.astype(BF16))
        o = inner + cross
        kd_t = (kr * dk_ref[...]).T.astype(BF16)
        kv = _dot(kd_t, v.astype(BF16))
        state = dc_ref[...] * state + bmask_ref[...] * kv
        mu = _dot2(o, avg_ref[...])
        dev = o - mu
        var = _dot2(dev * dev, avg_ref[...])
        on = dev * lax.rsqrt(var + LN_EPS)
        o_ref[0, rows, :] = (_silu(g) * on).astype(o_ref.dtype)
    state_ref[...] = state


def _retention_tables():
    H, d, C = RET_HEADS, HEAD_DIM, CHUNK
    log_gamma = jnp.log(1.0 - 2.0 ** (-5.0 - jnp.arange(H, dtype=F32)))
    idx = jnp.arange(C, dtype=F32)
    diff = idx[:, None] - idx[None, :]
    decay_in = jnp.where(diff >= 0, jnp.exp(log_gamma[:, None, None] * jnp.maximum(diff, 0.0)), 0.0)
    decay_q = jnp.exp(log_gamma[:, None] * (idx + 1.0))
    decay_k = jnp.exp(log_gamma[:, None] * (C - 1.0 - idx))
    decay_chunk = jnp.exp(log_gamma * C)
    din = decay_in.transpose(1, 0, 2).reshape(C, H * C)
    dq = jnp.repeat(decay_q.T, d, axis=1)
    dk = jnp.repeat(decay_k.T, d, axis=1)
    dc = jnp.repeat(decay_chunk, d)[None, :]
    head = jnp.arange(H * d) // d
    bmask = (head[:, None] == head[None, :]).astype(F32)
    avg = jnp.tile((bmask / d).astype(BF16), (2, 1))
    return din, dq, dk, dc, bmask, avg


def _swa_body(u_ref, bias_ref, sink_ref, o_ref, kprev_ref, vprev_ref, *, n_chunks):
    W = CHUNK
    qd = SWA_HEADS * HEAD_DIM
    kvd = SWA_KV_HEADS * HEAD_DIM
    first_step = pl.program_id(1) == 0
    lane = lax.broadcasted_iota(I32, (1, LANES), 1)
    low = lane < HEAD_DIM
    col = lax.broadcasted_iota(I32, (1, 2 * W), 1)

    kprev = kprev_ref[...]
    vprev = vprev_ref[...]
    sink = sink_ref[...]
    for ci in range(n_chunks):
        rows = slice(ci * W, (ci + 1) * W)
        qa = u_ref[0, rows, 0:LANES].astype(F32)
        qb = u_ref[0, rows, LANES:qd].astype(F32)
        k = u_ref[0, rows, qd:qd + kvd].astype(BF16)
        v = u_ref[0, rows, qd + kvd:qd + 2 * kvd].astype(BF16)
        q4 = jnp.concatenate([jnp.where(low, qa, 0.0), jnp.where(low, 0.0, qa),
                              jnp.where(low, qb, 0.0), jnp.where(low, 0.0, qb)],
                             axis=0).astype(BF16)
        kband = jnp.concatenate([kprev, k], axis=0)
        vband = jnp.concatenate([vprev, v], axis=0)
        logits = _dot_nt(q4, kband) * (HEAD_DIM ** -0.5) + bias_ref[...]
        if ci == 0:
            logits = jnp.where(jnp.logical_and(first_step, col < W), MASK_VALUE, logits)
        m = jnp.maximum(jnp.max(logits, -1, keepdims=True), sink)
        p = jnp.exp(logits - m)
        denom = jnp.sum(p, -1, keepdims=True) + jnp.exp(sink - m)
        res = _dot(p.astype(BF16), vband) / denom
        out_a = jnp.where(low, res[0:W], res[W:2 * W])
        out_b = jnp.where(low, res[2 * W:3 * W], res[3 * W:4 * W])
        o_ref[0, rows, 0:LANES] = out_a.astype(o_ref.dtype)
        o_ref[0, rows, LANES:qd] = out_b.astype(o_ref.dtype)
        kprev, vprev = k, v
    kprev_ref[...] = kprev
    vprev_ref[...] = vprev


SWA_HEAD_ORDER = (0, 2, 1, 3)


def _ssd_body(u_ref, dt_ref, cw_ref, cb_ref, dtb_c_ref, alog_c_ref, dskip_ref, nw_ref,
              tril_ref, triu_ref, expand_ref, gmask_ref,
              o_ref, state_ref, ext_ref, *, n_chunks):
    C = CHUNK
    sd = SSD_HEADS * HEAD_DIM
    gn = SSD_GROUPS * SSD_STATE
    cd = sd + 2 * gn
    tl = n_chunks * C
    slab = 2 * LANES
    heads_per_group = SSD_HEADS // SSD_GROUPS
    heads_per_slab = slab // HEAD_DIM
    lane = lax.broadcasted_iota(I32, (1, LANES), 1)
    low = lane < SSD_STATE
    slab_masks = [_head_lane_mask(slab, hh) for hh in range(heads_per_slab)]
    row_i = lax.broadcasted_iota(I32, (C, C), 0)
    col_i = lax.broadcasted_iota(I32, (C, C), 1)
    causal = row_i >= col_i
    neg_a_c = -jnp.exp(alog_c_ref[...])

    ext_ref[8:8 + tl, :] = u_ref[0, :, sd:sd + cd].astype(F32)
    state = state_ref[...]
    for ci in range(n_chunks):
        rows = slice(ci * C, (ci + 1) * C)
        z = u_ref[0, rows, 0:sd].astype(F32)
        dt_raw = dt_ref[0, rows, :]
        base = 8 + ci * C
        conv = cb_ref[...]
        for w in range(SSD_CONV):
            shift = SSD_CONV - 1 - w
            conv = conv + cw_ref[w:w + 1, :] * ext_ref[base - shift:base - shift + C, :]
        xbc = _silu(conv)
        xs = xbc[:, 0:sd]
        bm = xbc[:, sd:sd + gn]
        cm = xbc[:, sd + gn:cd]

        dt_c = _softplus(dt_raw + dtb_c_ref[...])
        a_c = neg_a_c * dt_c
        acs_c = _dot3_left(tril_ref[...], a_c)
        acs_t = _dot3(a_c.T, triu_ref[...])
        spread = _dot3(acs_c, expand_ref[...])
        acs_x = spread[:, 0:sd]
        dt_x = _dot3(dt_c, expand_ref[:, 0:sd])
        xdt = xs * dt_x

        bstack = jnp.concatenate([jnp.where(low, bm, 0.0), jnp.where(low, 0.0, bm)],
                                 axis=0).astype(BF16)
        cb = _dot_nt(cm.astype(BF16), bstack)
        y_diag = []
        for s in range(sd // slab):
            ms = []
            for hh in range(heads_per_slab):
                h = s * heads_per_slab + hh
                g = h // heads_per_group
                col_bcast = spread[:, sd + h * LANES:sd + (h + 1) * LANES]
                seg = col_bcast - acs_t[h:h + 1, :]
                lmat = jnp.exp(jnp.where(causal, seg, MASK_VALUE))
                ms.append((cb[:, g * C:(g + 1) * C] * lmat).astype(BF16))
            xslab = xdt[:, s * slab:(s + 1) * slab]
            xstack = jnp.concatenate([jnp.where(m, xslab, 0.0) for m in slab_masks],
                                     axis=0).astype(BF16)
            y_diag.append(_dot(jnp.concatenate(ms, axis=1), xstack))
        y_diag = jnp.concatenate(y_diag, axis=1)

        y_off = _dot(cm.astype(BF16), state.astype(BF16)) * jnp.exp(acs_x)
        last = acs_x[C - 1:C, :]
        dec = jnp.exp(last - acs_x)
        new = _dot(bm.T.astype(BF16), (xdt * dec).astype(BF16))
        state = jnp.exp(last) * state + gmask_ref[...] * new

        y = y_diag + y_off + xs * dskip_ref[...]
        hgate = y * _silu(z)
        gw = sd // SSD_GROUPS
        for g in range(SSD_GROUPS):
            hg = hgate[:, g * gw:(g + 1) * gw]
            ms_ = jnp.mean(hg * hg, -1, keepdims=True)
            o_ref[0, rows, g * gw:(g + 1) * gw] = (
                hg * lax.rsqrt(ms_ + LN_EPS) * nw_ref[:, g * gw:(g + 1) * gw]).astype(o_ref.dtype)
    state_ref[...] = state
    ext_ref[0:8, :] = ext_ref[tl:tl + 8, :]


def _ssd_tables():
    C = CHUNK
    sd = SSD_HEADS * HEAD_DIM
    gn = SSD_GROUPS * SSD_STATE
    t = jnp.arange(C)
    tril = (t[:, None] >= t[None, :]).astype(BF16)
    triu = (t[:, None] <= t[None, :]).astype(BF16)
    r = jnp.arange(LANES)[:, None]
    eexp = ((r == (jnp.arange(sd)[None, :] // HEAD_DIM)) & (r < SSD_HEADS)).astype(BF16)
    bsel = ((r == (jnp.arange(SSD_HEADS * LANES)[None, :] // LANES)) & (r < SSD_HEADS)).astype(BF16)
    heads_per_group = SSD_HEADS // SSD_GROUPS
    row_g = jnp.arange(gn)[:, None] // SSD_STATE
    col_g = (jnp.arange(sd)[None, :] // HEAD_DIM) // heads_per_group
    gmask = (row_g == col_g).astype(F32)
    expand = jnp.concatenate([eexp, bsel], axis=1)
    return jnp.tile(tril, (1, 3)), jnp.tile(triu, (3, 1)), jnp.tile(expand, (3, 1)), gmask


N_RET_TABLES = 6
N_SSD_CONSTS = 10


def _mixers_kernel(*refs, n_chunks):
    u_ret, cos, sin, u_ssd, u_dt, u_swa = refs[:6]
    pos = 6
    ret_tables = refs[pos:pos + N_RET_TABLES]
    pos += N_RET_TABLES
    ssd_consts = refs[pos:pos + N_SSD_CONSTS]
    pos += N_SSD_CONSTS
    bias, sink = refs[pos:pos + 2]
    pos += 2
    y_ret, y_ssd, y_swa = refs[pos:pos + 3]
    ret_state, ssd_state, ssd_ext, kprev, vprev = refs[pos + 3:]

    @pl.when(pl.program_id(1) == 0)
    def _():
        ret_state[...] = jnp.zeros_like(ret_state)
        ssd_state[...] = jnp.zeros_like(ssd_state)
        ssd_ext[0:8, :] = jnp.zeros((8, ssd_ext.shape[1]), F32)
        kprev[...] = jnp.zeros_like(kprev)
        vprev[...] = jnp.zeros_like(vprev)

    _ssd_body(u_ssd, u_dt, *ssd_consts, y_ssd, ssd_state, ssd_ext, n_chunks=n_chunks)
    _swa_body(u_swa, bias, sink, y_swa, kprev, vprev, n_chunks=n_chunks)
    _retention_body(u_ret, cos, sin, *ret_tables, y_ret, ret_state, n_chunks=n_chunks)


def _mixers(u_ret, u_ssd, u_dt, u_swa, cos_t, sin_t, bias_tab, sinks_l, conv_w, conv_b,
            dt_bias, a_log, d_skip, norm_w, tl):
    nb, L, _ = u_ret.shape
    W = CHUNK
    rd = RET_HEADS * HEAD_DIM
    sd = SSD_HEADS * HEAD_DIM
    qd = SWA_HEADS * HEAD_DIM
    cd = conv_w.shape[-1]
    ret_tables = _retention_tables()
    pad = lambda v: jnp.zeros((1, LANES), F32).at[0, :SSD_HEADS].set(v)
    rep = lambda v: jnp.repeat(v, HEAD_DIM)[None, :]
    ssd_consts = (conv_w, conv_b[None, :], pad(dt_bias), pad(a_log), rep(d_skip),
                  norm_w[None, :]) + _ssd_tables()
    order = jnp.array(SWA_HEAD_ORDER)
    bias_stacked = bias_tab[order].reshape(SWA_HEADS * W, 2 * W)
    sink_col = jnp.repeat(sinks_l.astype(F32)[order], W)[:, None]
    assert len(ret_tables) == N_RET_TABLES and len(ssd_consts) == N_SSD_CONSTS
    tok = lambda a: pl.BlockSpec((1, tl, a.shape[-1]), lambda b, i: (b, i, 0))
    const = lambda a: pl.BlockSpec(a.shape, lambda b, i: (0,) * a.ndim)
    tokens = (u_ret, cos_t, sin_t, u_ssd, u_dt, u_swa)
    consts = ret_tables + ssd_consts + (bias_stacked, sink_col)
    return pl.pallas_call(
        functools.partial(_mixers_kernel, n_chunks=tl // CHUNK),
        out_shape=[jax.ShapeDtypeStruct((nb, L, w), BF16) for w in (rd, sd, qd)],
        grid=(nb, L // tl),
        in_specs=[tok(a) for a in tokens] + [const(a) for a in consts],
        out_specs=[pl.BlockSpec((1, tl, w), lambda b, i: (b, i, 0)) for w in (rd, sd, qd)],
        scratch_shapes=[pltpu.VMEM((rd, rd), F32),
                        pltpu.VMEM((SSD_GROUPS * SSD_STATE, sd), F32),
                        pltpu.VMEM((8 + tl, cd), F32),
                        pltpu.VMEM((W, LANES), BF16), pltpu.VMEM((W, LANES), BF16)],
        compiler_params=_cparams("arbitrary", "arbitrary"),
        name="mixers",
    )(*tokens, *consts)


def _out_proj_kernel(x_ref, yr_ref, ys_ref, ya_ref, mod_ref, wr_ref, ws_ref, wa_ref,
                     g_ref, b_ref, o_ref, *, alpha):
    mix = (_dot(yr_ref[0], wr_ref[...]) + _dot(ys_ref[0], ws_ref[...])
           + _dot(ya_ref[0], wa_ref[...]))
    gate = mod_ref[0, 0, 2:3, :]
    r = alpha * x_ref[0] + (1.0 + gate) * mix
    o_ref[0] = _layer_norm(r, g_ref[...], b_ref[...])


def _out_proj(x, y_ret, y_ssd, y_swa, mod_l, w_r, w_s, w_a, ln_g, ln_b, alpha, tm):
    nb, L, d = x.shape
    tok = lambda w: pl.BlockSpec((1, tm, w), lambda b, i: (b, i, 0))
    const = lambda a: pl.BlockSpec(a.shape, lambda b, i: (0,) * a.ndim)
    return pl.pallas_call(
        functools.partial(_out_proj_kernel, alpha=alpha),
        out_shape=jax.ShapeDtypeStruct((nb, L, d), F32),
        grid=(nb, L // tm),
        in_specs=[tok(d), tok(y_ret.shape[-1]), tok(y_ssd.shape[-1]), tok(y_swa.shape[-1]),
                  pl.BlockSpec((1, 1, 6, d), lambda b, i: (0, b, 0, 0)),
                  const(w_r), const(w_s), const(w_a), const(ln_g), const(ln_b)],
        out_specs=tok(d),
        compiler_params=_cparams("arbitrary", "arbitrary"),
        name="out_proj_ln",
    )(x, y_ret, y_ssd, y_swa, mod_l, w_r, w_s, w_a, ln_g, ln_b)


FF_CHUNK = 256


def _dense_ffn_kernel(x_ref, mod_ref, wg_ref, wu_ref, wd_ref, g_ref, b_ref, o_ref, *, alpha):
    x = x_ref[0]
    sh = mod_ref[0, 0, 3:4, :]
    sc = mod_ref[0, 0, 4:5, :]
    gate = mod_ref[0, 0, 5:6, :]
    h = (x * (1.0 + sc) + sh).astype(BF16)
    dff = wg_ref.shape[-1]
    acc = jnp.zeros(x.shape, F32)
    for j in range(0, dff, FF_CHUNK):
        gj = _dot(h, wg_ref[:, j:j + FF_CHUNK])
        uj = _dot(h, wu_ref[:, j:j + FF_CHUNK])
        acc = acc + _dot((_silu(gj) * uj).astype(BF16), wd_ref[j:j + FF_CHUNK, :])
    r = alpha * x + (1.0 + gate) * acc
    o_ref[0] = _layer_norm(r, g_ref[...], b_ref[...])


def _dense_ffn(x, mod_l, wg, wu, wd, ln_g, ln_b, alpha, tm):
    nb, L, d = x.shape
    const = lambda a: pl.BlockSpec(a.shape, lambda b, i: (0,) * a.ndim,
                                   pipeline_mode=pl.Buffered(1))
    return pl.pallas_call(
        functools.partial(_dense_ffn_kernel, alpha=alpha),
        out_shape=jax.ShapeDtypeStruct((nb, L, d), F32),
        grid=(nb, L // tm),
        in_specs=[pl.BlockSpec((1, tm, d), lambda b, i: (b, i, 0)),
                  pl.BlockSpec((1, 1, 6, d), lambda b, i: (0, b, 0, 0)),
                  const(wg), const(wu), const(wd), const(ln_g), const(ln_b)],
        out_specs=pl.BlockSpec((1, tm, d), lambda b, i: (b, i, 0)),
        compiler_params=_cparams("arbitrary", "arbitrary"),
        name="dense_ffn_ln",
    )(x, mod_l, wg, wu, wd, ln_g, ln_b)


def _router_kernel(x_ref, mod_ref, wr_ref, br_ref, trils_ref, h_ref, route_ref, cnt_ref,
                   base_ref):
    first = jnp.logical_and(pl.program_id(0) == 0, pl.program_id(1) == 0)

    @pl.when(first)
    def _():
        base_ref[...] = jnp.zeros_like(base_ref)

    sh = mod_ref[0, 0, 3:4, :]
    sc = mod_ref[0, 0, 4:5, :]
    h = x_ref[0] * (1.0 + sc) + sh
    h_ref[0] = h
    logits = jnp.dot(h, wr_ref[...], preferred_element_type=F32,
                     precision=lax.Precision.HIGHEST) + br_ref[...]
    lane = lax.broadcasted_iota(I32, logits.shape, 1).astype(F32)
    logits = jnp.where(lane < N_EXPERTS, logits, MASK_VALUE)
    v1 = jnp.max(logits, -1, keepdims=True)
    e1 = jnp.min(jnp.where(logits == v1, lane, float(LANES)), -1, keepdims=True)
    rest = jnp.where(lane == e1, MASK_VALUE, logits)
    v2 = jnp.max(rest, -1, keepdims=True)
    e2 = jnp.min(jnp.where(rest == v2, lane, float(LANES)), -1, keepdims=True)
    t = jnp.exp(v2 - v1)
    w1 = 1.0 / (1.0 + t)
    w2 = t / (1.0 + t)
    hot1 = (lane == e1).astype(F32)
    hot2 = (lane == e2).astype(F32)
    both = hot1 + hot2
    base = base_ref[0:1, :]
    before = _dot(trils_ref[...], both.astype(BF16)) + base
    rank1 = jnp.sum(hot1 * before, -1, keepdims=True)
    rank2 = jnp.sum(hot2 * before, -1, keepdims=True)
    total = base + jnp.sum(both, 0, keepdims=True)
    base_ref[0:1, :] = total
    cnt_ref[...] = jnp.broadcast_to(total, cnt_ref.shape)
    route = jnp.where(lane == 0, e1, 0.0)
    route = jnp.where(lane == 1, e2, route)
    route = jnp.where(lane == 2, rank1, route)
    route = jnp.where(lane == 3, rank2, route)
    route = jnp.where(lane == 4, w1, route)
    route = jnp.where(lane == 5, w2, route)
    route_ref[0] = route


def _router(x, mod_l, w_router, b_router, tm):
    nb, L, d = x.shape
    wr = jnp.zeros((d, LANES), F32).at[:, :N_EXPERTS].set(w_router)
    br = jnp.zeros((1, LANES), F32).at[0, :N_EXPERTS].set(b_router)
    t = jnp.arange(tm)
    tril_strict = (t[:, None] > t[None, :]).astype(BF16)
    const = lambda a: pl.BlockSpec(a.shape, lambda b, i: (0,) * a.ndim)
    return pl.pallas_call(
        _router_kernel,
        out_shape=[jax.ShapeDtypeStruct((nb, L, d), F32),
                   jax.ShapeDtypeStruct((nb, L, LANES), F32),
                   jax.ShapeDtypeStruct((8, LANES), F32)],
        grid=(nb, L // tm),
        in_specs=[pl.BlockSpec((1, tm, d), lambda b, i: (b, i, 0)),
                  pl.BlockSpec((1, 1, 6, d), lambda b, i: (0, b, 0, 0)),
                  const(wr), const(br), const(tril_strict)],
        out_specs=[pl.BlockSpec((1, tm, d), lambda b, i: (b, i, 0)),
                   pl.BlockSpec((1, tm, LANES), lambda b, i: (b, i, 0)),
                   pl.BlockSpec((8, LANES), lambda b, i: (0, 0))],
        scratch_shapes=[pltpu.VMEM((8, LANES), F32)],
        compiler_params=_cparams("arbitrary", "arbitrary"),
        name="moe_router",
    )(x, mod_l, wr, br, tril_strict)


TOP_K = 2
EXPERT_FF_SPLITS = 2


def _ff_chunks(width, step=512):
    return [(a, min(a + step, width)) for a in range(0, width, step)]


def _expert_kernel(te_ref, nu_ref, gsrc_ref, sdst_ref,
                   h_ref, wg_ref, wu_ref, wd_ref, yk_ref,
                   xbuf, ybuf, hbuf, gsem, ssem, *, tme):
    i = pl.program_id(0)
    j = pl.program_id(1)
    n_used = nu_ref[0]
    slot = i % 2
    other = 1 - slot
    used = i < n_used
    chunks = _ff_chunks(wg_ref.shape[-1])
    rows_per_step = tme // EXPERT_FF_SPLITS
    n_front = len(chunks) // 2
    front_w = chunks[n_front - 1][1]
    back_w = wg_ref.shape[-1] - front_w
    gather_ranges, scatter_ranges = [], []
    for c, (a, b) in enumerate(chunks):
        if c < n_front:
            gather_ranges.append((rows_per_step * a // front_w, rows_per_step * b // front_w))
            scatter_ranges.append((0, 0))
        else:
            gather_ranges.append((0, 0))
            scatter_ranges.append((rows_per_step * (a - front_w) // back_w,
                                   rows_per_step * (b - front_w) // back_w))

    def gather(tile, slot_, r):
        return pltpu.make_async_copy(h_ref.at[pl.ds(gsrc_ref[tile * tme + r], 1)],
                                     xbuf.at[slot_, pl.ds(r, 1)], gsem.at[slot_])

    def wait_gather(slot_):
        pltpu.make_async_copy(h_ref.at[pl.ds(0, tme)], xbuf.at[slot_], gsem.at[slot_]).wait()

    def scatter(block, slot_, r):
        return pltpu.make_async_copy(ybuf.at[slot_, pl.ds(r, 1)],
                                     yk_ref.at[pl.ds(sdst_ref[block * tme + r], 1)],
                                     ssem.at[slot_])

    def wait_scatter(slot_):
        pltpu.make_async_copy(ybuf.at[slot_], yk_ref.at[pl.ds(0, tme)], ssem.at[slot_]).wait()

    @pl.when(jnp.logical_and(i == 0, j == 0))
    def _():
        ybuf[1] = jnp.zeros(ybuf.shape[1:], F32)

        def body(r, carry):
            gather(0, 0, r).start()
            return carry
        lax.fori_loop(0, tme, body, 0, unroll=8)

    @pl.when(used)
    def _():
        @pl.when(j == 0)
        def _():
            wait_gather(slot)
            hbuf[...] = xbuf[slot].astype(BF16)

            @pl.when(i > 0)
            def _():
                wait_scatter(slot)
            ybuf[slot] = jnp.zeros(ybuf.shape[1:], F32)

        h = hbuf[...]
        for c, (a, b) in enumerate(chunks):
            for r in range(*gather_ranges[c]):
                gather(i + 1, other, j * rows_per_step + r).start()
            for r in range(*scatter_ranges[c]):
                scatter(i, other, j * rows_per_step + r).start()
            gj = _dot(h, wg_ref[0, :, a:b])
            uj = _dot(h, wu_ref[0, :, a:b])
            part = _dot((_silu(gj) * uj).astype(BF16), wd_ref[0, a:b, :])
            ybuf[slot] = ybuf[slot] + part

        @pl.when(jnp.logical_and(j == EXPERT_FF_SPLITS - 1, i == n_used - 1))
        def _():
            wait_gather(other)
            wait_scatter(other)

            def body(r, carry):
                scatter(i + 1, slot, r).start()
                return carry
            lax.fori_loop(0, tme, body, 0, unroll=8)
            wait_scatter(slot)


def _expert_ffn(h_flat, tile_expert, n_used, gsrc, sdst, wg, wu, wd, tme):
    T, d = h_flat.shape
    dff = wg.shape[-1]
    dffh = dff // EXPERT_FF_SPLITS
    n_tiles = gsrc.shape[0] // tme
    yk_rows = TOP_K * T + tme

    def half(i, j):
        return jnp.where(i % 2 == 0, j, EXPERT_FF_SPLITS - 1 - j)

    return pl.pallas_call(
        functools.partial(_expert_kernel, tme=tme),
        out_shape=jax.ShapeDtypeStruct((yk_rows, d), F32),
        grid_spec=pltpu.PrefetchScalarGridSpec(
            num_scalar_prefetch=4,
            grid=(n_tiles, EXPERT_FF_SPLITS),
            in_specs=[pl.BlockSpec(memory_space=pl.ANY),
                      pl.BlockSpec((1, d, dffh), lambda i, j, te, *_: (te[i], 0, half(i, j))),
                      pl.BlockSpec((1, d, dffh), lambda i, j, te, *_: (te[i], 0, half(i, j))),
                      pl.BlockSpec((1, dffh, d), lambda i, j, te, *_: (te[i], half(i, j), 0))],
            out_specs=pl.BlockSpec(memory_space=pl.ANY),
            scratch_shapes=[pltpu.VMEM((2, tme, d), F32), pltpu.VMEM((2, tme, d), F32),
                            pltpu.VMEM((tme, d), BF16),
                            pltpu.SemaphoreType.DMA((2,)), pltpu.SemaphoreType.DMA((2,))]),
        compiler_params=_cparams("arbitrary", "arbitrary"),
        name="moe_experts",
    )(tile_expert, n_used, gsrc, sdst, h_flat, wg, wu, wd)


def _moe_finish_kernel(x_ref, route_ref, mod_ref, y0_ref, y1_ref, g_ref, b_ref, o_ref, *, alpha):
    route = route_ref[0]
    f = route[:, 4:5] * y0_ref[...] + route[:, 5:6] * y1_ref[...]
    gate = mod_ref[0, 0, 5:6, :]
    r = alpha * x_ref[0] + (1.0 + gate) * f
    o_ref[0] = _layer_norm(r, g_ref[...], b_ref[...])


def _moe_finish(x, route, mod_l, yk, ln_g, ln_b, alpha, tm):
    nb, L, d = x.shape
    tiles_per_seq = L // tm
    tiles = nb * tiles_per_seq
    return pl.pallas_call(
        functools.partial(_moe_finish_kernel, alpha=alpha),
        out_shape=jax.ShapeDtypeStruct((nb, L, d), F32),
        grid=(nb, tiles_per_seq),
        in_specs=[pl.BlockSpec((1, tm, d), lambda b, i: (b, i, 0)),
                  pl.BlockSpec((1, tm, LANES), lambda b, i: (b, i, 0)),
                  pl.BlockSpec((1, 1, 6, d), lambda b, i: (0, b, 0, 0)),
                  pl.BlockSpec((tm, d), lambda b, i: (b * tiles_per_seq + i, 0)),
                  pl.BlockSpec((tm, d), lambda b, i: (tiles + b * tiles_per_seq + i, 0)),
                  pl.BlockSpec(ln_g.shape, lambda b, i: (0, 0)),
                  pl.BlockSpec(ln_b.shape, lambda b, i: (0, 0))],
        out_specs=pl.BlockSpec((1, tm, d), lambda b, i: (b, i, 0)),
        compiler_params=_cparams("arbitrary", "arbitrary"),
        name="moe_finish_ln",
    )(x, route, mod_l, yk, yk, ln_g, ln_b)


def _row_index_kernel(dest_ref, gsrc0_ref, sdst0_ref, gsrc_ref, sdst_ref, sem, *, n_pairs, T, tme):
    init_g = pltpu.make_async_copy(gsrc0_ref, gsrc_ref, sem.at[0])
    init_s = pltpu.make_async_copy(sdst0_ref, sdst_ref, sem.at[1])
    init_g.start()
    init_s.start()
    init_g.wait()
    init_s.wait()

    assert TOP_K == 2
    group = 8

    def body(g, carry):
        f0 = g * group
        rows = [dest_ref[f0 + k] for k in range(group)]
        for k in range(group):
            tok = g * (group // TOP_K) + k // TOP_K
            gsrc_ref[rows[k]] = tok
            sdst_ref[tme + rows[k]] = (k % TOP_K) * T + tok
        return carry
    lax.fori_loop(0, n_pairs // group, body, 0, unroll=2)


def _row_indices(dest, n_rows, T, tme):
    dump = TOP_K * T + jnp.arange(n_rows + tme, dtype=I32) % tme
    return pl.pallas_call(
        functools.partial(_row_index_kernel, n_pairs=dest.shape[0], T=T, tme=tme),
        out_shape=[jax.ShapeDtypeStruct((n_rows,), I32),
                   jax.ShapeDtypeStruct((n_rows + tme,), I32)],
        in_specs=[pl.BlockSpec(memory_space=pltpu.SMEM),
                  pl.BlockSpec(memory_space=pl.ANY),
                  pl.BlockSpec(memory_space=pl.ANY)],
        out_specs=[pl.BlockSpec(memory_space=pltpu.SMEM),
                   pl.BlockSpec(memory_space=pltpu.SMEM)],
        scratch_shapes=[pltpu.SemaphoreType.DMA((2,))],
        name="moe_row_indices",
    )(dest, jnp.zeros((n_rows,), I32), dump)


def _moe(x, mod_l, w_router, b_router, wg, wu, wd, ln_g, ln_b, alpha, tm_route, tme, tmf):
    nb, L, d = x.shape
    T = nb * L
    h, route, counts = _router(x, mod_l, w_router, b_router, tm_route)
    route_flat = route.reshape(T, LANES)
    e = route_flat[:, 0:2].astype(I32)
    rank = route_flat[:, 2:4].astype(I32)
    cnt = counts[0, :N_EXPERTS].astype(I32)
    tiles = (cnt + tme - 1) // tme
    tile_end = jnp.cumsum(tiles)
    group_start = (tile_end - tiles) * tme
    dest = (group_start[e] + rank).reshape(-1)
    n_tiles = (TOP_K * T) // tme + N_EXPERTS
    n_rows = n_tiles * tme
    tile_expert = jnp.minimum(
        jnp.searchsorted(tile_end, jnp.arange(n_tiles), side="right"), N_EXPERTS - 1).astype(I32)
    n_used = tile_end[-1:].astype(I32)
    gsrc, sdst = _row_indices(dest, n_rows, T, tme)
    yk = _expert_ffn(h.reshape(T, d), tile_expert, n_used, gsrc, sdst, wg, wu, wd, tme)
    return _moe_finish(x, route, mod_l, yk, ln_g, ln_b, alpha, tmf)


def kernel(x, c, positions, rel_bias, w_ada, b_ada, w_in, w_out, conv_w, conv_b, dt_bias, a_log,
           d_skip, ssd_norm_w, sinks, ln_g, ln_b, ffn_w_gate, ffn_w_up, ffn_w_down, router_w,
           router_b, expert_w_gate, expert_w_up, expert_w_down):
    depth = w_ada.shape[0]
    nb, L, d = x.shape
    alpha = (2 * depth) ** 0.25
    rd = RET_HEADS * HEAD_DIM
    sd = SSD_HEADS * HEAD_DIM
    cd = conv_w.shape[-1]
    qd = SWA_HEADS * HEAD_DIM
    kvd = SWA_KV_HEADS * HEAD_DIM
    sizes = (rd, rd, rd, rd, sd, cd, SSD_HEADS, qd, kvd, kvd)
    offs = np.concatenate([[0], np.cumsum(sizes)])
    tl = min(512, L)

    mod = _ada_mod(c, w_ada, b_ada)
    cos_t, sin_t = _rotary_tables(positions, tl)
    bias_tab = _swa_bias_table(rel_bias)

    q_perm = np.concatenate([np.arange(h * HEAD_DIM, (h + 1) * HEAD_DIM) for h in SWA_HEAD_ORDER])
    widths = (4 * rd, sd + cd, LANES, qd + 2 * kvd)
    dtypes = (BF16, BF16, F32, BF16)

    for layer in range(depth):
        wl = w_in[layer]
        seg = lambda i: wl[:, offs[i]:offs[i + 1]]
        dt_cols = jnp.zeros((d, LANES), F32).at[:, :SSD_HEADS].set(seg(6))
        w_cat = jnp.concatenate(
            [seg(0), seg(1), seg(2), seg(3), seg(4), seg(5), dt_cols,
             seg(7)[:, q_perm], seg(8), seg(9)], axis=1).astype(BF16)
        wo = w_out[layer]
        w_r = wo[0:rd].astype(BF16)
        w_s = wo[rd:rd + sd].astype(BF16)
        w_a = wo[rd + sd:][q_perm].astype(BF16)
        mod_l = mod[layer:layer + 1]

        u_ret, u_ssd, u_dt, u_swa = _in_proj(x, mod_l, w_cat, widths, dtypes, tl)
        y_ret, y_ssd, y_swa = _mixers(u_ret, u_ssd, u_dt, u_swa, cos_t, sin_t, bias_tab,
                                      sinks[layer], conv_w[layer], conv_b[layer], dt_bias[layer],
                                      a_log[layer], d_skip[layer], ssd_norm_w[layer], tl)
        x = _out_proj(x, y_ret, y_ssd, y_swa, mod_l, w_r, w_s, w_a,
                      ln_g[layer, 0][None, :], ln_b[layer, 0][None, :], alpha, tl)

        g2 = ln_g[layer, 1][None, :]
        b2 = ln_b[layer, 1][None, :]
        i = layer // 2
        if layer % 2 == 0:
            x = _dense_ffn(x, mod_l, ffn_w_gate[i].astype(BF16), ffn_w_up[i].astype(BF16),
                           ffn_w_down[i].astype(BF16), g2, b2, alpha, tl)
        else:
            x = _moe(x, mod_l, router_w[i], router_b[i], expert_w_gate[i].astype(BF16),
                     expert_w_up[i].astype(BF16), expert_w_down[i].astype(BF16), g2, b2, alpha,
                     tm_route=min(512, L), tme=min(512, L), tmf=min(512, L))
    return x
```

```python
import functools
import math

import numpy as np
import jax
import jax.numpy as jnp
from jax import lax
from jax.experimental import pallas as pl
from jax.experimental.pallas import tpu as pltpu

F32 = jnp.float32
BF16 = jnp.bfloat16
I32 = jnp.int32

HEAD_DIM = 64
CHUNK = 128
RET_HEADS = 4
SSD_HEADS = 8
SSD_GROUPS = 2
SSD_STATE = 64
SSD_CONV = 4
SWA_HEADS = 4
SWA_KV_HEADS = 2
REL_BUCKETS = 32
N_EXPERTS = 8
LN_EPS = 1e-5
LANES = 128
MASK_VALUE = -1e30

VMEM_LIMIT = 56 * 1024 * 1024


def _cparams(*sem):
    return pltpu.CompilerParams(dimension_semantics=sem, vmem_limit_bytes=VMEM_LIMIT)


def _silu(v):
    return v * (1.0 / (1.0 + jnp.exp(-v)))


def _softplus(v):
    return jnp.maximum(v, 0.0) + jnp.log(1.0 + jnp.exp(-jnp.abs(v)))


def _dot(a, b):
    return jnp.dot(a, b, preferred_element_type=F32)


def _dot_nt(a, b):
    return lax.dot_general(a, b, (((1,), (1,)), ((), ())), preferred_element_type=F32)


def _split3(v):
    h1 = v.astype(BF16)
    r1 = v - h1.astype(F32)
    h2 = r1.astype(BF16)
    r2 = r1 - h2.astype(F32)
    return h1, h2, r2.astype(BF16)


def _dot3(v, m3):
    return _dot(jnp.concatenate(_split3(v), axis=1), m3)


def _dot3_left(m3, v):
    return _dot(m3, jnp.concatenate(_split3(v), axis=0))


def _dot2(v, m2):
    h1 = v.astype(BF16)
    h2 = (v - h1.astype(F32)).astype(BF16)
    return _dot(jnp.concatenate([h1, h2], axis=1), m2)


def _layer_norm(r, g, b):
    mu = jnp.mean(r, -1, keepdims=True)
    d = r - mu
    var = jnp.mean(d * d, -1, keepdims=True)
    return d * lax.rsqrt(var + LN_EPS) * g + b


def _ada_kernel(c_ref, w_ref, b_ref, o_ref):
    o_ref[0] = jnp.dot(c_ref[...], w_ref[0], preferred_element_type=F32,
                       precision=lax.Precision.HIGHEST) + b_ref[0]


def _ada_mod(c, w_ada, b_ada):
    depth, d, d6 = w_ada.shape
    nb = c.shape[0]
    rows = 8
    c_pad = jnp.zeros((rows, d), F32).at[:nb].set(c)
    out = pl.pallas_call(
        _ada_kernel,
        out_shape=jax.ShapeDtypeStruct((depth, rows, d6), F32),
        grid=(depth, d6 // d),
        in_specs=[pl.BlockSpec((rows, d), lambda l, j: (0, 0)),
                  pl.BlockSpec((1, d, d), lambda l, j: (l, 0, j)),
                  pl.BlockSpec((1, 1, d), lambda l, j: (l, 0, j))],
        out_specs=pl.BlockSpec((1, rows, d), lambda l, j: (l, 0, j)),
        compiler_params=_cparams("arbitrary", "arbitrary"),
        name="ada_mod",
    )(c_pad, w_ada, b_ada.reshape(depth, 1, d6))
    return out[:, :nb].reshape(depth, nb, 6, d)


def _rotary_kernel(pos_ref, cos_ref, sin_ref):
    half = HEAD_DIM // 2
    lane = lax.broadcasted_iota(I32, (1, LANES), 1)
    jj = lane % HEAD_DIM
    idx = (jj % half).astype(F32)
    inv = jnp.exp(-math.log(10000.0) * idx / half)
    ang = pos_ref[0].astype(F32) * inv
    cos_ref[0] = jnp.cos(ang)
    sin_ref[0] = jnp.where(jj < half, -1.0, 1.0) * jnp.sin(ang)


def _rotary_tables(positions, tl):
    nb, L = positions.shape
    pos = positions.reshape(nb, L, 1)
    return pl.pallas_call(
        _rotary_kernel,
        out_shape=[jax.ShapeDtypeStruct((nb, L, LANES), F32)] * 2,
        grid=(nb, L // tl),
        in_specs=[pl.BlockSpec((1, tl, 1), lambda b, i: (b, i, 0))],
        out_specs=[pl.BlockSpec((1, tl, LANES), lambda b, i: (b, i, 0))] * 2,
        compiler_params=_cparams("arbitrary", "arbitrary"),
        name="rotary_tables",
    )(pos)


def _swa_bias_kernel(rb_ref, bucket_ref, band_ref, o_ref):
    bucket = bucket_ref[...]
    band = band_ref[...]
    for h in range(SWA_HEADS):
        acc = jnp.zeros(bucket.shape, F32)
        for b in range(REL_BUCKETS):
            acc = jnp.where(bucket == b, rb_ref[b, h], acc)
        o_ref[h] = jnp.where(band > 0, acc, MASK_VALUE)


def _t5_bucket(dist):
    exact = REL_BUCKETS // 2
    df = jnp.maximum(dist, 1).astype(F32)
    large = exact + (jnp.log(df / exact) / math.log(CHUNK / exact) * (REL_BUCKETS - exact)).astype(I32)
    large = jnp.minimum(large, REL_BUCKETS - 1)
    return jnp.where(dist < exact, dist, large)


def _swa_bias_table(rel_bias):
    W = CHUNK
    qi = jnp.arange(W)[:, None]
    kj = jnp.arange(2 * W)[None, :]
    dist = qi + W - kj
    band = ((dist >= 0) & (dist < W)).astype(I32)
    bucket = _t5_bucket(jnp.clip(dist, 0, W - 1)).astype(I32)
    return pl.pallas_call(
        _swa_bias_kernel,
        out_shape=jax.ShapeDtypeStruct((SWA_HEADS, W, 2 * W), F32),
        in_specs=[pl.BlockSpec(memory_space=pltpu.SMEM),
                  pl.BlockSpec(memory_space=pltpu.VMEM),
                  pl.BlockSpec(memory_space=pltpu.VMEM)],
        out_specs=pl.BlockSpec(memory_space=pltpu.VMEM),
        name="swa_bias_table",
    )(rel_bias, bucket, band)


def _in_proj_kernel(x_ref, mod_ref, w_ref, *out_refs, widths):
    sh = mod_ref[0, 0, 0:1, :]
    sc = mod_ref[0, 0, 1:2, :]
    h = (x_ref[0] * (1.0 + sc) + sh).astype(BF16)
    off = 0
    for ref, width in zip(out_refs, widths):
        for a, b in _ff_chunks(width):
            ref[0, :, a:b] = _dot(h, w_ref[:, off + a:off + b]).astype(ref.dtype)
        off += width


def _in_proj(x, mod_l, w_cat, widths, dtypes, tm):
    nb, L, d = x.shape
    return pl.pallas_call(
        functools.partial(_in_proj_kernel, widths=widths),
        out_shape=[jax.ShapeDtypeStruct((nb, L, w), t) for w, t in zip(widths, dtypes)],
        grid=(nb, L // tm),
        in_specs=[pl.BlockSpec((1, tm, d), lambda b, i: (b, i, 0)),
                  pl.BlockSpec((1, 1, 6, d), lambda b, i: (0, b, 0, 0)),
                  pl.BlockSpec(w_cat.shape, lambda b, i: (0, 0))],
        out_specs=[pl.BlockSpec((1, tm, w), lambda b, i: (b, i, 0)) for w in widths],
        compiler_params=_cparams("arbitrary", "arbitrary"),
        name="in_proj",
    )(x, mod_l, w_cat)


def _head_lane_mask(width, head):
    lane = lax.broadcasted_iota(I32, (1, width), 1)
    return (lane // HEAD_DIM) == head


def _rotate_half(t):
    width = t.shape[-1]
    lane = lax.broadcasted_iota(I32, (1, width), 1)
    half = HEAD_DIM // 2
    fwd = pltpu.roll(t, width - half, axis=1)
    bwd = pltpu.roll(t, half, axis=1)
    return jnp.where((lane % HEAD_DIM) < half, fwd, bwd)


def _retention_body(u_ref, cos_ref, sin_ref, din_ref, dq_ref, dk_ref, dc_ref,
                    bmask_ref, avg_ref, o_ref, state_ref, *, n_chunks):
    rd = RET_HEADS * HEAD_DIM
    masks = [_head_lane_mask(rd, h) for h in range(RET_HEADS)]

    def stack_heads(t):
        return jnp.concatenate([jnp.where(m, t, 0.0) for m in masks], axis=0).astype(BF16)

    state = state_ref[...]
    for ci in range(n_chunks):
        rows = slice(ci * CHUNK, (ci + 1) * CHUNK)
        cos = cos_ref[0, rows, :]
        sin = sin_ref[0, rows, :]
        cos2 = jnp.concatenate([cos, cos], axis=1)
        sin2 = jnp.concatenate([sin, sin], axis=1)
        q = u_ref[0, rows, 0:rd].astype(F32)
        k = u_ref[0, rows, rd:2 * rd].astype(F32)
        v = u_ref[0, rows, 2 * rd:3 * rd].astype(F32)
        g = u_ref[0, rows, 3 * rd:4 * rd].astype(F32)
        qr = q * cos2 + _rotate_half(q) * sin2
        kr = (k * cos2 + _rotate_half(k) * sin2) * (HEAD_DIM ** -0.5)
        scores = _dot_nt(qr.astype(BF16), stack_heads(kr)) * din_ref[...]
        inner = _dot(scores.astype(BF16), stack_heads(v))
        cross = _dot((qr * dq_ref[...]).astype(BF16), state.astype(BF16))
        o = inner + cross
        kd_t = (kr * dk_ref[...]).T.astype(BF16)
        kv = _dot(kd_t, v.astype(BF16))
        state = dc_ref[...] * state + bmask_ref[...] * kv
        mu = _dot2(o, avg_ref[...])
        dev = o - mu
        var = _dot2(dev * dev, avg_ref[...])
        on = dev * lax.rsqrt(var + LN_EPS)
        o_ref[0, rows, :] = (_silu(g) * on).astype(o_ref.dtype)
    state_ref[...] = state


def _retention_tables():
    H, d, C = RET_HEADS, HEAD_DIM, CHUNK
    log_gamma = jnp.log(1.0 - 2.0 ** (-5.0 - jnp.arange(H, dtype=F32)))
    idx = jnp.arange(C, dtype=F32)
    diff = idx[:, None] - idx[None, :]
    decay_in = jnp.where(diff >= 0, jnp.exp(log_gamma[:, None, None] * jnp.maximum(diff, 0.0)), 0.0)
    decay_q = jnp.exp(log_gamma[:, None] * (idx + 1.0))
    decay_k = jnp.exp(log_gamma[:, None] * (C - 1.0 - idx))
    decay_chunk = jnp.exp(log_gamma * C)
    din = decay_in.transpose(1, 0, 2).reshape(C, H * C)
    dq = jnp.repeat(decay_q.T, d, axis=1)
    dk = jnp.repeat(decay_k.T, d, axis=1)
    dc = jnp.repeat(decay_chunk, d)[None, :]
    head = jnp.arange(H * d) // d
    bmask = (head[:, None] == head[None, :]).astype(F32)
    avg = jnp.tile((bmask / d).astype(BF16), (2, 1))
    return din, dq, dk, dc, bmask, avg


def _swa_body(u_ref, bias_ref, sink_ref, o_ref, kprev_ref, vprev_ref, *, n_chunks):
    W = CHUNK
    qd = SWA_HEADS * HEAD_DIM
    kvd = SWA_KV_HEADS * HEAD_DIM
    first_step = pl.program_id(1) == 0
    lane = lax.broadcasted_iota(I32, (1, LANES), 1)
    low = lane < HEAD_DIM
    col = lax.broadcasted_iota(I32, (1, 2 * W), 1)

    kprev = kprev_ref[...]
    vprev = vprev_ref[...]
    sink = sink_ref[...]
    for ci in range(n_chunks):
        rows = slice(ci * W, (ci + 1) * W)
        qa = u_ref[0, rows, 0:LANES].astype(F32)
        qb = u_ref[0, rows, LANES:qd].astype(F32)
        k = u_ref[0, rows, qd:qd + kvd].astype(BF16)
        v = u_ref[0, rows, qd + kvd:qd + 2 * kvd].astype(BF16)
        q4 = jnp.concatenate([jnp.where(low, qa, 0.0), jnp.where(low, 0.0, qa),
                              jnp.where(low, qb, 0.0), jnp.where(low, 0.0, qb)],
                             axis=0).astype(BF16)
        kband = jnp.concatenate([kprev, k], axis=0)
        vband = jnp.concatenate([vprev, v], axis=0)
        logits = _dot_nt(q4, kband) * (HEAD_DIM ** -0.5) + bias_ref[...]
        if ci == 0:
            logits = jnp.where(jnp.logical_and(first_step, col < W), MASK_VALUE, logits)
        m = jnp.maximum(jnp.max(logits, -1, keepdims=True), sink)
        p = jnp.exp(logits - m)
        denom = jnp.sum(p, -1, keepdims=True) + jnp.exp(sink - m)
        res = _dot(p.astype(BF16), vband) / denom
        out_a = jnp.where(low, res[0:W], res[W:2 * W])
        out_b = jnp.where(low, res[2 * W:3 * W], res[3 * W:4 * W])
        o_ref[0, rows, 0:LANES] = out_a.astype(o_ref.dtype)
        o_ref[0, rows, LANES:qd] = out_b.astype(o_ref.dtype)
        kprev, vprev = k, v
    kprev_ref[...] = kprev
    vprev_ref[...] = vprev


SWA_HEAD_ORDER = (0, 2, 1, 3)


def _ssd_body(u_ref, dt_ref, cw_ref, cb_ref, dtb_c_ref, alog_c_ref, dskip_ref, nw_ref,
              tril_ref, triu_ref, expand_ref, gmask_ref,
              o_ref, state_ref, ext_ref, *, n_chunks):
    C = CHUNK
    sd = SSD_HEADS * HEAD_DIM
    gn = SSD_GROUPS * SSD_STATE
    cd = sd + 2 * gn
    tl = n_chunks * C
    slab = 2 * LANES
    heads_per_group = SSD_HEADS // SSD_GROUPS
    heads_per_slab = slab // HEAD_DIM
    lane = lax.broadcasted_iota(I32, (1, LANES), 1)
    low = lane < SSD_STATE
    slab_masks = [_head_lane_mask(slab, hh) for hh in range(heads_per_slab)]
    row_i = lax.broadcasted_iota(I32, (C, C), 0)
    col_i = lax.broadcasted_iota(I32, (C, C), 1)
    causal = row_i >= col_i
    neg_a_c = -jnp.exp(alog_c_ref[...])

    ext_ref[8:8 + tl, :] = u_ref[0, :, sd:sd + cd].astype(F32)
    state = state_ref[...]
    for ci in range(n_chunks):
        rows = slice(ci * C, (ci + 1) * C)
        z = u_ref[0, rows, 0:sd].astype(F32)
        dt_raw = dt_ref[0, rows, :]
        base = 8 + ci * C
        conv = cb_ref[...]
        for w in range(SSD_CONV):
            shift = SSD_CONV - 1 - w
            conv = conv + cw_ref[w:w + 1, :] * ext_ref[base - shift:base - shift + C, :]
        xbc = _silu(conv)
        xs = xbc[:, 0:sd]
        bm = xbc[:, sd:sd + gn]
        cm = xbc[:, sd + gn:cd]

        dt_c = _softplus(dt_raw + dtb_c_ref[...])
        a_c = neg_a_c * dt_c
        acs_c = _dot3_left(tril_ref[...], a_c)
        acs_t = _dot3(a_c.T, triu_ref[...])
        spread = _dot3(acs_c, expand_ref[...])
        acs_x = spread[:, 0:sd]
        dt_x = _dot3(dt_c, expand_ref[:, 0:sd])
        xdt = xs * dt_x

        bstack = jnp.concatenate([jnp.where(low, bm, 0.0), jnp.where(low, 0.0, bm)],
                                 axis=0).astype(BF16)
        cb = _dot_nt(cm.astype(BF16), bstack)
        y_diag = []
        for s in range(sd // slab):
            ms = []
            for hh in range(heads_per_slab):
                h = s * heads_per_slab + hh
                g = h // heads_per_group
                col_bcast = spread[:, sd + h * LANES:sd + (h + 1) * LANES]
                seg = col_bcast - acs_t[h:h + 1, :]
                lmat = jnp.exp(jnp.where(causal, seg, MASK_VALUE))
                ms.append((cb[:, g * C:(g + 1) * C] * lmat).astype(BF16))
            xslab = xdt[:, s * slab:(s + 1) * slab]
            xstack = jnp.concatenate([jnp.where(m, xslab, 0.0) for m in slab_masks],
                                     axis=0).astype(BF16)
            y_diag.append(_dot(jnp.concatenate(ms, axis=1), xstack))
        y_diag = jnp.concatenate(y_diag, axis=1)

        y_off = _dot(cm.astype(BF16), state.astype(BF16)) * jnp.exp(acs_x)
        last = acs_x[C - 1:C, :]
        dec = jnp.exp(last - acs_x)
        new = _dot(bm.T.astype(BF16), (xdt * dec).astype(BF16))
        state = jnp.exp(last) * state + gmask_ref[...] * new

        y = y_diag + y_off + xs * dskip_ref[...]
        hgate = y * _silu(z)
        gw = sd // SSD_GROUPS
        for g in range(SSD_GROUPS):
            hg = hgate[:, g * gw:(g + 1) * gw]
            ms_ = jnp.mean(hg * hg, -1, keepdims=True)
            o_ref[0, rows, g * gw:(g + 1) * gw] = (
                hg * lax.rsqrt(ms_ + LN_EPS) * nw_ref[:, g * gw:(g + 1) * gw]).astype(o_ref.dtype)
    state_ref[...] = state
    ext_ref[0:8, :] = ext_ref[tl:tl + 8, :]


def _ssd_tables():
    C = CHUNK
    sd = SSD_HEADS * HEAD_DIM
    gn = SSD_GROUPS * SSD_STATE
    t = jnp.arange(C)
    tril = (t[:, None] >= t[None, :]).astype(BF16)
    triu = (t[:, None] <= t[None, :]).astype(BF16)
    r = jnp.arange(LANES)[:, None]
    eexp = ((r == (jnp.arange(sd)[None, :] // HEAD_DIM)) & (r < SSD_HEADS)).astype(BF16)
    bsel = ((r == (jnp.arange(SSD_HEADS * LANES)[None, :] // LANES)) & (r < SSD_HEADS)).astype(BF16)
    heads_per_group = SSD_HEADS // SSD_GROUPS
    row_g = jnp.arange(gn)[:, None] // SSD_STATE
    col_g = (jnp.arange(sd)[None, :] // HEAD_DIM) // heads_per_group
    gmask = (row_g == col_g).astype(F32)
    expand = jnp.concatenate([eexp, bsel], axis=1)
    return jnp.tile(tril, (1, 3)), jnp.tile(triu, (3, 1)), jnp.tile(expand, (3, 1)), gmask


N_RET_TABLES = 6
N_SSD_CONSTS = 10


def _mixers_kernel(*refs, n_chunks):
    u_ret, cos, sin, u_ssd, u_dt, u_swa = refs[:6]
    pos = 6
    ret_tables = refs[pos:pos + N_RET_TABLES]
    pos += N_RET_TABLES
    ssd_consts = refs[pos:pos + N_SSD_CONSTS]
    pos += N_SSD_CONSTS
    bias, sink = refs[pos:pos + 2]
    pos += 2
    y_ret, y_ssd, y_swa = refs[pos:pos + 3]
    ret_state, ssd_state, ssd_ext, kprev, vprev = refs[pos + 3:]

    @pl.when(pl.program_id(1) == 0)
    def _():
        ret_state[...] = jnp.zeros_like(ret_state)
        ssd_state[...] = jnp.zeros_like(ssd_state)
        ssd_ext[0:8, :] = jnp.zeros((8, ssd_ext.shape[1]), F32)
        kprev[...] = jnp.zeros_like(kprev)
        vprev[...] = jnp.zeros_like(vprev)

    _ssd_body(u_ssd, u_dt, *ssd_consts, y_ssd, ssd_state, ssd_ext, n_chunks=n_chunks)
    _swa_body(u_swa, bias, sink, y_swa, kprev, vprev, n_chunks=n_chunks)
    _retention_body(u_ret, cos, sin, *ret_tables, y_ret, ret_state, n_chunks=n_chunks)


def _mixers(u_ret, u_ssd, u_dt, u_swa, cos_t, sin_t, bias_tab, sinks_l, conv_w, conv_b,
            dt_bias, a_log, d_skip, norm_w, tl):
    nb, L, _ = u_ret.shape
    W = CHUNK
    rd = RET_HEADS * HEAD_DIM
    sd = SSD_HEADS * HEAD_DIM
    qd = SWA_HEADS * HEAD_DIM
    cd = conv_w.shape[-1]
    ret_tables = _retention_tables()
    pad = lambda v: jnp.zeros((1, LANES), F32).at[0, :SSD_HEADS].set(v)
    rep = lambda v: jnp.repeat(v, HEAD_DIM)[None, :]
    ssd_consts = (conv_w, conv_b[None, :], pad(dt_bias), pad(a_log), rep(d_skip),
                  norm_w[None, :]) + _ssd_tables()
    order = jnp.array(SWA_HEAD_ORDER)
    bias_stacked = bias_tab[order].reshape(SWA_HEADS * W, 2 * W)
    sink_col = jnp.repeat(sinks_l.astype(F32)[order], W)[:, None]
    assert len(ret_tables) == N_RET_TABLES and len(ssd_consts) == N_SSD_CONSTS
    tok = lambda a: pl.BlockSpec((1, tl, a.shape[-1]), lambda b, i: (b, i, 0))
    const = lambda a: pl.BlockSpec(a.shape, lambda b, i: (0,) * a.ndim)
    tokens = (u_ret, cos_t, sin_t, u_ssd, u_dt, u_swa)
    consts = ret_tables + ssd_consts + (bias_stacked, sink_col)
    return pl.pallas_call(
        functools.partial(_mixers_kernel, n_chunks=tl // CHUNK),
        out_shape=[jax.ShapeDtypeStruct((nb, L, w), BF16) for w in (rd, sd, qd)],
        grid=(nb, L // tl),
        in_specs=[tok(a) for a in tokens] + [const(a) for a in consts],
        out_specs=[pl.BlockSpec((1, tl, w), lambda b, i: (b, i, 0)) for w in (rd, sd, qd)],
        scratch_shapes=[pltpu.VMEM((rd, rd), F32),
                        pltpu.VMEM((SSD_GROUPS * SSD_STATE, sd), F32),
                        pltpu.VMEM((8 + tl, cd), F32),
                        pltpu.VMEM((W, LANES), BF16), pltpu.VMEM((W, LANES), BF16)],
        compiler_params=_cparams("arbitrary", "arbitrary"),
        name="mixers",
    )(*tokens, *consts)


def _out_proj_kernel(x_ref, yr_ref, ys_ref, ya_ref, mod_ref, wr_ref, ws_ref, wa_ref,
                     g_ref, b_ref, o_ref, *, alpha):
    mix = (_dot(yr_ref[0], wr_ref[...]) + _dot(ys_ref[0], ws_ref[...])
           + _dot(ya_ref[0], wa_ref[...]))
    gate = mod_ref[0, 0, 2:3, :]
    r = alpha * x_ref[0] + (1.0 + gate) * mix
    o_ref[0] = _layer_norm(r, g_ref[...], b_ref[...])


def _out_proj(x, y_ret, y_ssd, y_swa, mod_l, w_r, w_s, w_a, ln_g, ln_b, alpha, tm):
    nb, L, d = x.shape
    tok = lambda w: pl.BlockSpec((1, tm, w), lambda b, i: (b, i, 0))
    const = lambda a: pl.BlockSpec(a.shape, lambda b, i: (0,) * a.ndim)
    return pl.pallas_call(
        functools.partial(_out_proj_kernel, alpha=alpha),
        out_shape=jax.ShapeDtypeStruct((nb, L, d), F32),
        grid=(nb, L // tm),
        in_specs=[tok(d), tok(y_ret.shape[-1]), tok(y_ssd.shape[-1]), tok(y_swa.shape[-1]),
                  pl.BlockSpec((1, 1, 6, d), lambda b, i: (0, b, 0, 0)),
                  const(w_r), const(w_s), const(w_a), const(ln_g), const(ln_b)],
        out_specs=tok(d),
        compiler_params=_cparams("arbitrary", "arbitrary"),
        name="out_proj_ln",
    )(x, y_ret, y_ssd, y_swa, mod_l, w_r, w_s, w_a, ln_g, ln_b)


FF_CHUNK = 256


def _dense_ffn_kernel(x_ref, mod_ref, wg_ref, wu_ref, wd_ref, g_ref, b_ref, o_ref, *, alpha):
    x = x_ref[0]
    sh = mod_ref[0, 0, 3:4, :]
    sc = mod_ref[0, 0, 4:5, :]
    gate = mod_ref[0, 0, 5:6, :]
    h = (x * (1.0 + sc) + sh).astype(BF16)
    dff = wg_ref.shape[-1]
    acc = jnp.zeros(x.shape, F32)
    for j in range(0, dff, FF_CHUNK):
        gj = _dot(h, wg_ref[:, j:j + FF_CHUNK])
        uj = _dot(h, wu_ref[:, j:j + FF_CHUNK])
        acc = acc + _dot((_silu(gj) * uj).astype(BF16), wd_ref[j:j + FF_CHUNK, :])
    r = alpha * x + (1.0 + gate) * acc
    o_ref[0] = _layer_norm(r, g_ref[...], b_ref[...])


def _dense_ffn(x, mod_l, wg, wu, wd, ln_g, ln_b, alpha, tm):
    nb, L, d = x.shape
    const = lambda a: pl.BlockSpec(a.shape, lambda b, i: (0,) * a.ndim,
                                   pipeline_mode=pl.Buffered(1))
    return pl.pallas_call(
        functools.partial(_dense_ffn_kernel, alpha=alpha),
        out_shape=jax.ShapeDtypeStruct((nb, L, d), F32),
        grid=(nb, L // tm),
        in_specs=[pl.BlockSpec((1, tm, d), lambda b, i: (b, i, 0)),
                  pl.BlockSpec((1, 1, 6, d), lambda b, i: (0, b, 0, 0)),
                  const(wg), const(wu), const(wd), const(ln_g), const(ln_b)],
        out_specs=pl.BlockSpec((1, tm, d), lambda b, i: (b, i, 0)),
        compiler_params=_cparams("arbitrary", "arbitrary"),
        name="dense_ffn_ln",
    )(x, mod_l, wg, wu, wd, ln_g, ln_b)


def _store_row_tiles(ref, value):
    rows, d = value.shape
    nt = d // LANES
    for c in range(nt):
        ref[pl.ds(c, rows, stride=nt), :] = value[:, c * LANES:(c + 1) * LANES].astype(ref.dtype)


def _load_row_tiles(ref):
    nt = ROW_TILE
    rows = ref.shape[0] // nt
    return jnp.concatenate([ref[pl.ds(c, rows, stride=nt), :] for c in range(nt)], axis=1)


def _router_kernel(x_ref, mod_ref, wr_ref, br_ref, trils_ref, h_ref, route_ref, cnt_ref,
                   base_ref):
    first = jnp.logical_and(pl.program_id(0) == 0, pl.program_id(1) == 0)

    @pl.when(first)
    def _():
        base_ref[...] = jnp.zeros_like(base_ref)

    sh = mod_ref[0, 0, 3:4, :]
    sc = mod_ref[0, 0, 4:5, :]
    h = x_ref[0] * (1.0 + sc) + sh
    _store_row_tiles(h_ref.at[0], h)
    logits = jnp.dot(h, wr_ref[...], preferred_element_type=F32,
                     precision=lax.Precision.HIGHEST) + br_ref[...]
    lane = lax.broadcasted_iota(I32, logits.shape, 1).astype(F32)
    logits = jnp.where(lane < N_EXPERTS, logits, MASK_VALUE)
    v1 = jnp.max(logits, -1, keepdims=True)
    e1 = jnp.min(jnp.where(logits == v1, lane, float(LANES)), -1, keepdims=True)
    rest = jnp.where(lane == e1, MASK_VALUE, logits)
    v2 = jnp.max(rest, -1, keepdims=True)
    e2 = jnp.min(jnp.where(rest == v2, lane, float(LANES)), -1, keepdims=True)
    t = jnp.exp(v2 - v1)
    w1 = 1.0 / (1.0 + t)
    w2 = t / (1.0 + t)
    hot1 = (lane == e1).astype(F32)
    hot2 = (lane == e2).astype(F32)
    both = hot1 + hot2
    base = base_ref[0:1, :]
    before = _dot(trils_ref[...], both.astype(BF16)) + base
    rank1 = jnp.sum(hot1 * before, -1, keepdims=True)
    rank2 = jnp.sum(hot2 * before, -1, keepdims=True)
    total = base + jnp.sum(both, 0, keepdims=True)
    base_ref[0:1, :] = total
    cnt_ref[...] = jnp.broadcast_to(total, cnt_ref.shape)
    route = jnp.where(lane == 0, e1, 0.0)
    route = jnp.where(lane == 1, e2, route)
    route = jnp.where(lane == 2, rank1, route)
    route = jnp.where(lane == 3, rank2, route)
    route = jnp.where(lane == 4, w1, route)
    route = jnp.where(lane == 5, w2, route)
    route_ref[0] = route


def _router(x, mod_l, w_router, b_router, tm):
    nb, L, d = x.shape
    wr = jnp.zeros((d, LANES), F32).at[:, :N_EXPERTS].set(w_router)
    br = jnp.zeros((1, LANES), F32).at[0, :N_EXPERTS].set(b_router)
    t = jnp.arange(tm)
    tril_strict = (t[:, None] > t[None, :]).astype(BF16)
    const = lambda a: pl.BlockSpec(a.shape, lambda b, i: (0,) * a.ndim)
    return pl.pallas_call(
        _router_kernel,
        out_shape=[jax.ShapeDtypeStruct((nb, L * (d // LANES), LANES), F32),
                   jax.ShapeDtypeStruct((nb, L, LANES), F32),
                   jax.ShapeDtypeStruct((8, LANES), F32)],
        grid=(nb, L // tm),
        in_specs=[pl.BlockSpec((1, tm, d), lambda b, i: (b, i, 0)),
                  pl.BlockSpec((1, 1, 6, d), lambda b, i: (0, b, 0, 0)),
                  const(wr), const(br), const(tril_strict)],
        out_specs=[pl.BlockSpec((1, tm * (d // LANES), LANES), lambda b, i: (b, i, 0)),
                   pl.BlockSpec((1, tm, LANES), lambda b, i: (b, i, 0)),
                   pl.BlockSpec((8, LANES), lambda b, i: (0, 0))],
        scratch_shapes=[pltpu.VMEM((8, LANES), F32)],
        compiler_params=_cparams("arbitrary", "arbitrary"),
        name="moe_router",
    )(x, mod_l, wr, br, tril_strict)


TOP_K = 2
ROW_TILE = 8
EXPERT_FF_SPLITS = 2


def _ff_chunks(width, step=512):
    return [(a, min(a + step, width)) for a in range(0, width, step)]


def _expert_kernel(te_ref, nu_ref, gsrc_ref, sdst_ref,
                   h_ref, wg_ref, wu_ref, wd_ref, yk_ref,
                   xbuf, ybuf, hbuf, acc, gsem, ssem, *, tme):
    i = pl.program_id(0)
    j = pl.program_id(1)
    n_used = nu_ref[0]
    slot = i % 2
    other = 1 - slot
    used = i < n_used
    chunks = _ff_chunks(wg_ref.shape[-1])
    rows_per_step = tme // EXPERT_FF_SPLITS
    n_front = len(chunks) // 2
    front_w = chunks[n_front - 1][1]
    back_w = wg_ref.shape[-1] - front_w
    gather_ranges, scatter_ranges = [], []
    for c, (a, b) in enumerate(chunks):
        if c < n_front:
            gather_ranges.append((rows_per_step * a // front_w, rows_per_step * b // front_w))
            scatter_ranges.append((0, 0))
        else:
            gather_ranges.append((0, 0))
            scatter_ranges.append((rows_per_step * (a - front_w) // back_w,
                                   rows_per_step * (b - front_w) // back_w))

    nt = ROW_TILE

    def gather(tile, slot_, r):
        src = pl.multiple_of(gsrc_ref[tile * tme + r], nt)
        return pltpu.make_async_copy(h_ref.at[pl.ds(src, nt)],
                                     xbuf.at[slot_, pl.ds(pl.multiple_of(r * nt, nt), nt)],
                                     gsem.at[slot_])

    def wait_gather(slot_):
        pltpu.make_async_copy(h_ref.at[pl.ds(0, tme * nt)], xbuf.at[slot_],
                              gsem.at[slot_]).wait()

    def scatter(block, slot_, r):
        dst = pl.multiple_of(sdst_ref[block * tme + r], nt)
        return pltpu.make_async_copy(ybuf.at[slot_, pl.ds(pl.multiple_of(r * nt, nt), nt)],
                                     yk_ref.at[pl.ds(dst, nt)], ssem.at[slot_])

    def wait_scatter(slot_):
        pltpu.make_async_copy(ybuf.at[slot_], yk_ref.at[pl.ds(0, tme * nt)],
                              ssem.at[slot_]).wait()

    @pl.when(jnp.logical_and(i == 0, j == 0))
    def _():
        ybuf[1] = jnp.zeros(ybuf.shape[1:], F32)

        def body(r, carry):
            gather(0, 0, r).start()
            return carry
        lax.fori_loop(0, tme, body, 0, unroll=8)

    @pl.when(used)
    def _():
        @pl.when(j == 0)
        def _():
            wait_gather(slot)
            hbuf[...] = _load_row_tiles(xbuf.at[slot]).astype(BF16)

            @pl.when(i > 0)
            def _():
                wait_scatter(slot)
            acc[...] = jnp.zeros(acc.shape, F32)

        h = hbuf[...]
        for c, (a, b) in enumerate(chunks):
            for r in range(*gather_ranges[c]):
                gather(i + 1, other, j * rows_per_step + r).start()
            for r in range(*scatter_ranges[c]):
                scatter(i, other, j * rows_per_step + r).start()
            gj = _dot(h, wg_ref[0, :, a:b])
            uj = _dot(h, wu_ref[0, :, a:b])
            part = _dot((_silu(gj) * uj).astype(BF16), wd_ref[0, a:b, :])
            acc[...] = acc[...] + part
            ybuf[slot, 0:ROW_TILE, :] = part[0:ROW_TILE, 0:LANES]

        @pl.when(j == EXPERT_FF_SPLITS - 1)
        def _():
            _store_row_tiles(ybuf.at[slot], acc[...])

        @pl.when(jnp.logical_and(j == EXPERT_FF_SPLITS - 1, i == n_used - 1))
        def _():
            wait_gather(other)
            wait_scatter(other)

            def body(r, carry):
                scatter(i + 1, slot, r).start()
                return carry
            lax.fori_loop(0, tme, body, 0, unroll=8)
            wait_scatter(slot)


def _expert_ffn(h_rows, tile_expert, n_used, gsrc, sdst, wg, wu, wd, tme):
    nt = ROW_TILE
    T = h_rows.shape[0] // nt
    d = nt * LANES
    dff = wg.shape[-1]
    dffh = dff // EXPERT_FF_SPLITS
    n_tiles = gsrc.shape[0] // tme
    yk_rows = TOP_K * T + tme

    def half(i, j):
        return jnp.where(i % 2 == 0, j, EXPERT_FF_SPLITS - 1 - j)

    return pl.pallas_call(
        functools.partial(_expert_kernel, tme=tme),
        out_shape=jax.ShapeDtypeStruct((yk_rows * nt, LANES), F32),
        grid_spec=pltpu.PrefetchScalarGridSpec(
            num_scalar_prefetch=4,
            grid=(n_tiles, EXPERT_FF_SPLITS),
            in_specs=[pl.BlockSpec(memory_space=pl.ANY),
                      pl.BlockSpec((1, d, dffh), lambda i, j, te, *_: (te[i], 0, half(i, j))),
                      pl.BlockSpec((1, d, dffh), lambda i, j, te, *_: (te[i], 0, half(i, j))),
                      pl.BlockSpec((1, dffh, d), lambda i, j, te, *_: (te[i], half(i, j), 0))],
            out_specs=pl.BlockSpec(memory_space=pl.ANY),
            scratch_shapes=[pltpu.VMEM((2, tme * nt, LANES), F32),
                            pltpu.VMEM((2, tme * nt, LANES), F32),
                            pltpu.VMEM((tme, d), BF16), pltpu.VMEM((tme, d), F32),
                            pltpu.SemaphoreType.DMA((2,)), pltpu.SemaphoreType.DMA((2,))]),
        compiler_params=_cparams("arbitrary", "arbitrary"),
        name="moe_experts",
    )(tile_expert, n_used, gsrc, sdst, h_rows, wg, wu, wd)


def _moe_finish_kernel(x_ref, route_ref, mod_ref, y0_ref, y1_ref, g_ref, b_ref, o_ref, *, alpha):
    route = route_ref[0]
    f = route[:, 4:5] * _load_row_tiles(y0_ref) + route[:, 5:6] * _load_row_tiles(y1_ref)
    gate = mod_ref[0, 0, 5:6, :]
    r = alpha * x_ref[0] + (1.0 + gate) * f
    o_ref[0] = _layer_norm(r, g_ref[...], b_ref[...])


def _moe_finish(x, route, mod_l, yk, ln_g, ln_b, alpha, tm):
    nb, L, d = x.shape
    tiles_per_seq = L // tm
    tiles = nb * tiles_per_seq
    return pl.pallas_call(
        functools.partial(_moe_finish_kernel, alpha=alpha),
        out_shape=jax.ShapeDtypeStruct((nb, L, d), F32),
        grid=(nb, tiles_per_seq),
        in_specs=[pl.BlockSpec((1, tm, d), lambda b, i: (b, i, 0)),
                  pl.BlockSpec((1, tm, LANES), lambda b, i: (b, i, 0)),
                  pl.BlockSpec((1, 1, 6, d), lambda b, i: (0, b, 0, 0)),
                  pl.BlockSpec((tm * ROW_TILE, LANES), lambda b, i: (b * tiles_per_seq + i, 0)),
                  pl.BlockSpec((tm * ROW_TILE, LANES),
                               lambda b, i: (tiles + b * tiles_per_seq + i, 0)),
                  pl.BlockSpec(ln_g.shape, lambda b, i: (0, 0)),
                  pl.BlockSpec(ln_b.shape, lambda b, i: (0, 0))],
        out_specs=pl.BlockSpec((1, tm, d), lambda b, i: (b, i, 0)),
        compiler_params=_cparams("arbitrary", "arbitrary"),
        name="moe_finish_ln",
    )(x, route, mod_l, yk, yk, ln_g, ln_b)


def _row_index_kernel(dest_ref, gsrc0_ref, sdst0_ref, gsrc_ref, sdst_ref, sem, *, n_pairs, T, tme):
    init_g = pltpu.make_async_copy(gsrc0_ref, gsrc_ref, sem.at[0])
    init_s = pltpu.make_async_copy(sdst0_ref, sdst_ref, sem.at[1])
    init_g.start()
    init_s.start()
    init_g.wait()
    init_s.wait()

    assert TOP_K == 2
    group = 8

    def body(g, carry):
        f0 = g * group
        rows = [dest_ref[f0 + k] for k in range(group)]
        for k in range(group):
            tok = g * (group // TOP_K) + k // TOP_K
            gsrc_ref[rows[k]] = tok * ROW_TILE
            sdst_ref[tme + rows[k]] = ((k % TOP_K) * T + tok) * ROW_TILE
        return carry
    lax.fori_loop(0, n_pairs // group, body, 0, unroll=2)


def _row_indices(dest, n_rows, T, tme):
    dump = (TOP_K * T + jnp.arange(n_rows + tme, dtype=I32) % tme) * ROW_TILE
    return pl.pallas_call(
        functools.partial(_row_index_kernel, n_pairs=dest.shape[0], T=T, tme=tme),
        out_shape=[jax.ShapeDtypeStruct((n_rows,), I32),
                   jax.ShapeDtypeStruct((n_rows + tme,), I32)],
        in_specs=[pl.BlockSpec(memory_space=pltpu.SMEM),
                  pl.BlockSpec(memory_space=pl.ANY),
                  pl.BlockSpec(memory_space=pl.ANY)],
        out_specs=[pl.BlockSpec(memory_space=pltpu.SMEM),
                   pl.BlockSpec(memory_space=pltpu.SMEM)],
        scratch_shapes=[pltpu.SemaphoreType.DMA((2,))],
        name="moe_row_indices",
    )(dest, jnp.zeros((n_rows,), I32), dump)


def _moe(x, mod_l, w_router, b_router, wg, wu, wd, ln_g, ln_b, alpha, tm_route, tme, tmf):
    nb, L, d = x.shape
    T = nb * L
    h, route, counts = _router(x, mod_l, w_router, b_router, tm_route)
    route_flat = route.reshape(T, LANES)
    e = route_flat[:, 0:2].astype(I32)
    rank = route_flat[:, 2:4].astype(I32)
    cnt = counts[0, :N_EXPERTS].astype(I32)
    tiles = (cnt + tme - 1) // tme
    tile_end = jnp.cumsum(tiles)
    group_start = (tile_end - tiles) * tme
    dest = (group_start[e] + rank).reshape(-1)
    n_tiles = (TOP_K * T) // tme + N_EXPERTS
    n_rows = n_tiles * tme
    tile_expert = jnp.minimum(
        jnp.searchsorted(tile_end, jnp.arange(n_tiles), side="right"), N_EXPERTS - 1).astype(I32)
    n_used = tile_end[-1:].astype(I32)
    gsrc, sdst = _row_indices(dest, n_rows, T, tme)
    yk = _expert_ffn(h.reshape(T * ROW_TILE, LANES), tile_expert, n_used, gsrc, sdst,
                     wg, wu, wd, tme)
    return _moe_finish(x, route, mod_l, yk, ln_g, ln_b, alpha, tmf)


def kernel(x, c, positions, rel_bias, w_ada, b_ada, w_in, w_out, conv_w, conv_b, dt_bias, a_log,
           d_skip, ssd_norm_w, sinks, ln_g, ln_b, ffn_w_gate, ffn_w_up, ffn_w_down, router_w,
           router_b, expert_w_gate, expert_w_up, expert_w_down):
    depth = w_ada.shape[0]
    nb, L, d = x.shape
    alpha = (2 * depth) ** 0.25
    rd = RET_HEADS * HEAD_DIM
    sd = SSD_HEADS * HEAD_DIM
    cd = conv_w.shape[-1]
    qd = SWA_HEADS * HEAD_DIM
    kvd = SWA_KV_HEADS * HEAD_DIM
    sizes = (rd, rd, rd, rd, sd, cd, SSD_HEADS, qd, kvd, kvd)
    offs = np.concatenate([[0], np.cumsum(sizes)])
    tl = min(512, L)

    mod = _ada_mod(c, w_ada, b_ada)
    cos_t, sin_t = _rotary_tables(positions, tl)
    bias_tab = _swa_bias_table(rel_bias)

    q_perm = np.concatenate([np.arange(h * HEAD_DIM, (h + 1) * HEAD_DIM) for h in SWA_HEAD_ORDER])
    widths = (4 * rd, sd + cd, LANES, qd + 2 * kvd)
    dtypes = (BF16, BF16, F32, BF16)

    for layer in range(depth):
        wl = w_in[layer]
        seg = lambda i: wl[:, offs[i]:offs[i + 1]]
        dt_cols = jnp.zeros((d, LANES), F32).at[:, :SSD_HEADS].set(seg(6))
        w_cat = jnp.concatenate(
            [seg(0), seg(1), seg(2), seg(3), seg(4), seg(5), dt_cols,
             seg(7)[:, q_perm], seg(8), seg(9)], axis=1).astype(BF16)
        wo = w_out[layer]
        w_r = wo[0:rd].astype(BF16)
        w_s = wo[rd:rd + sd].astype(BF16)
        w_a = wo[rd + sd:][q_perm].astype(BF16)
        mod_l = mod[layer:layer + 1]

        u_ret, u_ssd, u_dt, u_swa = _in_proj(x, mod_l, w_cat, widths, dtypes, tl)
        y_ret, y_ssd, y_swa = _mixers(u_ret, u_ssd, u_dt, u_swa, cos_t, sin_t, bias_tab,
                                      sinks[layer], conv_w[layer], conv_b[layer], dt_bias[layer],
                                      a_log[layer], d_skip[layer], ssd_norm_w[layer], tl)
        x = _out_proj(x, y_ret, y_ssd, y_swa, mod_l, w_r, w_s, w_a,
                      ln_g[layer, 0][None, :], ln_b[layer, 0][None, :], alpha, tl)

        g2 = ln_g[layer, 1][None, :]
        b2 = ln_b[layer, 1][None, :]
        i = layer // 2
        if layer % 2 == 0:
            x = _dense_ffn(x, mod_l, ffn_w_gate[i].astype(BF16), ffn_w_up[i].astype(BF16),
                           ffn_w_down[i].astype(BF16), g2, b2, alpha, tl)
        else:
            x = _moe(x, mod_l, router_w[i], router_b[i], expert_w_gate[i].astype(BF16),
                     expert_w_up[i].astype(BF16), expert_w_down[i].astype(BF16), g2, b2, alpha,
                     tm_route=min(512, L), tme=min(512, L), tmf=min(512, L))
    return x
```

```python
import functools
import math

import numpy as np
import jax
import jax.numpy as jnp
from jax import lax
from jax.experimental import pallas as pl
from jax.experimental.pallas import tpu as pltpu

F32 = jnp.float32
BF16 = jnp.bfloat16
I32 = jnp.int32

HEAD_DIM = 64
CHUNK = 128
RET_HEADS = 4
SSD_HEADS = 8
SSD_GROUPS = 2
SSD_STATE = 64
SSD_CONV = 4
SWA_HEADS = 4
SWA_KV_HEADS = 2
REL_BUCKETS = 32
N_EXPERTS = 8
LN_EPS = 1e-5
LANES = 128
MASK_VALUE = -1e30

VMEM_LIMIT = 56 * 1024 * 1024


def _cparams(*sem):
    return pltpu.CompilerParams(dimension_semantics=sem, vmem_limit_bytes=VMEM_LIMIT)


def _silu(v):
    return v * (1.0 / (1.0 + jnp.exp(-v)))


def _softplus(v):
    return jnp.maximum(v, 0.0) + jnp.log(1.0 + jnp.exp(-jnp.abs(v)))


def _dot(a, b):
    return jnp.dot(a, b, preferred_element_type=F32)


def _dot_nt(a, b):
    return lax.dot_general(a, b, (((1,), (1,)), ((), ())), preferred_element_type=F32)


def _split3(v):
    h1 = v.astype(BF16)
    r1 = v - h1.astype(F32)
    h2 = r1.astype(BF16)
    r2 = r1 - h2.astype(F32)
    return h1, h2, r2.astype(BF16)


def _dot3(v, m3):
    return _dot(jnp.concatenate(_split3(v), axis=1), m3)


def _dot3_left(m3, v):
    return _dot(m3, jnp.concatenate(_split3(v), axis=0))


def _dot2(v, m2):
    h1 = v.astype(BF16)
    h2 = (v - h1.astype(F32)).astype(BF16)
    return _dot(jnp.concatenate([h1, h2], axis=1), m2)


def _layer_norm(r, g, b):
    mu = jnp.mean(r, -1, keepdims=True)
    d = r - mu
    var = jnp.mean(d * d, -1, keepdims=True)
    return d * lax.rsqrt(var + LN_EPS) * g + b


def _ada_kernel(c_ref, w_ref, b_ref, o_ref):
    o_ref[0] = jnp.dot(c_ref[...], w_ref[0], preferred_element_type=F32,
                       precision=lax.Precision.HIGHEST) + b_ref[0]


def _ada_mod(c, w_ada, b_ada):
    depth, d, d6 = w_ada.shape
    nb = c.shape[0]
    rows = 8
    c_pad = jnp.zeros((rows, d), F32).at[:nb].set(c)
    out = pl.pallas_call(
        _ada_kernel,
        out_shape=jax.ShapeDtypeStruct((depth, rows, d6), F32),
        grid=(depth, d6 // d),
        in_specs=[pl.BlockSpec((rows, d), lambda l, j: (0, 0)),
                  pl.BlockSpec((1, d, d), lambda l, j: (l, 0, j)),
                  pl.BlockSpec((1, 1, d), lambda l, j: (l, 0, j))],
        out_specs=pl.BlockSpec((1, rows, d), lambda l, j: (l, 0, j)),
        compiler_params=_cparams("arbitrary", "arbitrary"),
        name="ada_mod",
    )(c_pad, w_ada, b_ada.reshape(depth, 1, d6))
    return out[:, :nb].reshape(depth, nb, 6, d)


def _rotary_kernel(pos_ref, cos_ref, sin_ref):
    half = HEAD_DIM // 2
    lane = lax.broadcasted_iota(I32, (1, LANES), 1)
    jj = lane % HEAD_DIM
    idx = (jj % half).astype(F32)
    inv = jnp.exp(-math.log(10000.0) * idx / half)
    ang = pos_ref[0].astype(F32) * inv
    cos_ref[0] = jnp.cos(ang)
    sin_ref[0] = jnp.where(jj < half, -1.0, 1.0) * jnp.sin(ang)


def _rotary_tables(positions, tl):
    nb, L = positions.shape
    pos = positions.reshape(nb, L, 1)
    return pl.pallas_call(
        _rotary_kernel,
        out_shape=[jax.ShapeDtypeStruct((nb, L, LANES), F32)] * 2,
        grid=(nb, L // tl),
        in_specs=[pl.BlockSpec((1, tl, 1), lambda b, i: (b, i, 0))],
        out_specs=[pl.BlockSpec((1, tl, LANES), lambda b, i: (b, i, 0))] * 2,
        compiler_params=_cparams("arbitrary", "arbitrary"),
        name="rotary_tables",
    )(pos)


def _swa_bias_kernel(rb_ref, bucket_ref, band_ref, o_ref):
    bucket = bucket_ref[...]
    band = band_ref[...]
    for h in range(SWA_HEADS):
        acc = jnp.zeros(bucket.shape, F32)
        for b in range(REL_BUCKETS):
            acc = jnp.where(bucket == b, rb_ref[b, h], acc)
        o_ref[h] = jnp.where(band > 0, acc, MASK_VALUE)


def _t5_bucket(dist):
    exact = REL_BUCKETS // 2
    df = jnp.maximum(dist, 1).astype(F32)
    large = exact + (jnp.log(df / exact) / math.log(CHUNK / exact) * (REL_BUCKETS - exact)).astype(I32)
    large = jnp.minimum(large, REL_BUCKETS - 1)
    return jnp.where(dist < exact, dist, large)


def _swa_bias_table(rel_bias):
    W = CHUNK
    qi = jnp.arange(W)[:, None]
    kj = jnp.arange(2 * W)[None, :]
    dist = qi + W - kj
    band = ((dist >= 0) & (dist < W)).astype(I32)
    bucket = _t5_bucket(jnp.clip(dist, 0, W - 1)).astype(I32)
    return pl.pallas_call(
        _swa_bias_kernel,
        out_shape=jax.ShapeDtypeStruct((SWA_HEADS, W, 2 * W), F32),
        in_specs=[pl.BlockSpec(memory_space=pltpu.SMEM),
                  pl.BlockSpec(memory_space=pltpu.VMEM),
                  pl.BlockSpec(memory_space=pltpu.VMEM)],
        out_specs=pl.BlockSpec(memory_space=pltpu.VMEM),
        name="swa_bias_table",
    )(rel_bias, bucket, band)


def _in_proj_kernel(x_ref, mod_ref, w_ref, *out_refs, widths):
    sh = mod_ref[0, 0, 0:1, :]
    sc = mod_ref[0, 0, 1:2, :]
    h = (x_ref[0] * (1.0 + sc) + sh).astype(BF16)
    off = 0
    for ref, width in zip(out_refs, widths):
        for a, b in _ff_chunks(width):
            ref[0, :, a:b] = _dot(h, w_ref[:, off + a:off + b]).astype(ref.dtype)
        off += width


def _in_proj(x, mod_l, w_cat, widths, dtypes, tm):
    nb, L, d = x.shape
    return pl.pallas_call(
        functools.partial(_in_proj_kernel, widths=widths),
        out_shape=[jax.ShapeDtypeStruct((nb, L, w), t) for w, t in zip(widths, dtypes)],
        grid=(nb, L // tm),
        in_specs=[pl.BlockSpec((1, tm, d), lambda b, i: (b, i, 0)),
                  pl.BlockSpec((1, 1, 6, d), lambda b, i: (0, b, 0, 0)),
                  pl.BlockSpec(w_cat.shape, lambda b, i: (0, 0))],
        out_specs=[pl.BlockSpec((1, tm, w), lambda b, i: (b, i, 0)) for w in widths],
        compiler_params=_cparams("arbitrary", "arbitrary"),
        name="in_proj",
    )(x, mod_l, w_cat)


def _head_lane_mask(width, head):
    lane = lax.broadcasted_iota(I32, (1, width), 1)
    return (lane // HEAD_DIM) == head


def _rotate_half(t):
    width = t.shape[-1]
    lane = lax.broadcasted_iota(I32, (1, width), 1)
    half = HEAD_DIM // 2
    fwd = pltpu.roll(t, width - half, axis=1)
    bwd = pltpu.roll(t, half, axis=1)
    return jnp.where((lane % HEAD_DIM) < half, fwd, bwd)


def _retention_body(u_ref, cos_ref, sin_ref, din_ref, dq_ref, dk_ref, dc_ref,
                    bmask_ref, avg_ref, o_ref, state_ref, *, n_chunks):
    rd = RET_HEADS * HEAD_DIM
    masks = [_head_lane_mask(rd, h) for h in range(RET_HEADS)]

    def stack_heads(t):
        return jnp.concatenate([jnp.where(m, t, 0.0) for m in masks], axis=0).astype(BF16)

    state = state_ref[...]
    for ci in range(n_chunks):
        rows = slice(ci * CHUNK, (ci + 1) * CHUNK)
        cos = cos_ref[0, rows, :]
        sin = sin_ref[0, rows, :]
        cos2 = jnp.concatenate([cos, cos], axis=1)
        sin2 = jnp.concatenate([sin, sin], axis=1)
        q = u_ref[0, rows, 0:rd].astype(F32)
        k = u_ref[0, rows, rd:2 * rd].astype(F32)
        v = u_ref[0, rows, 2 * rd:3 * rd].astype(F32)
        g = u_ref[0, rows, 3 * rd:4 * rd].astype(F32)
        qr = q * cos2 + _rotate_half(q) * sin2
        kr = (k * cos2 + _rotate_half(k) * sin2) * (HEAD_DIM ** -0.5)
        scores = _dot_nt(qr.astype(BF16), stack_heads(kr)) * din_ref[...]
        inner = _dot(scores.astype(BF16), stack_heads(v))
        cross = _dot((qr * dq_ref[...]).astype(BF16), state.astype(BF16))
        o = inner + cross
        kd_t = (kr * dk_ref[...]).T.astype(BF16)
        kv = _dot(kd_t, v.astype(BF16))
        state = dc_ref[...] * state + bmask_ref[...] * kv
        mu = _dot2(o, avg_ref[...])
        dev = o - mu
        var = _dot2(dev * dev, avg_ref[...])
        on = dev * lax.rsqrt(var + LN_EPS)
        o_ref[0, rows, :] = (_silu(g) * on).astype(o_ref.dtype)
        yield
    state_ref[...] = state


def _retention_tables():
    H, d, C = RET_HEADS, HEAD_DIM, CHUNK
    log_gamma = jnp.log(1.0 - 2.0 ** (-5.0 - jnp.arange(H, dtype=F32)))
    idx = jnp.arange(C, dtype=F32)
    diff = idx[:, None] - idx[None, :]
    decay_in = jnp.where(diff >= 0, jnp.exp(log_gamma[:, None, None] * jnp.maximum(diff, 0.0)), 0.0)
    decay_q = jnp.exp(log_gamma[:, None] * (idx + 1.0))
    decay_k = jnp.exp(log_gamma[:, None] * (C - 1.0 - idx))
    decay_chunk = jnp.exp(log_gamma * C)
    din = decay_in.transpose(1, 0, 2).reshape(C, H * C)
    dq = jnp.repeat(decay_q.T, d, axis=1)
    dk = jnp.repeat(decay_k.T, d, axis=1)
    dc = jnp.repeat(decay_chunk, d)[None, :]
    head = jnp.arange(H * d) // d
    bmask = (head[:, None] == head[None, :]).astype(F32)
    avg = jnp.tile((bmask / d).astype(BF16), (2, 1))
    return din, dq, dk, dc, bmask, avg


def _swa_body(u_ref, bias_ref, sink_ref, o_ref, kprev_ref, vprev_ref, *, n_chunks):
    W = CHUNK
    qd = SWA_HEADS * HEAD_DIM
    kvd = SWA_KV_HEADS * HEAD_DIM
    first_step = pl.program_id(1) == 0
    lane = lax.broadcasted_iota(I32, (1, LANES), 1)
    low = lane < HEAD_DIM
    col = lax.broadcasted_iota(I32, (1, 2 * W), 1)

    kprev = kprev_ref[...]
    vprev = vprev_ref[...]
    sink = sink_ref[...]
    for ci in range(n_chunks):
        rows = slice(ci * W, (ci + 1) * W)
        qa = u_ref[0, rows, 0:LANES].astype(F32)
        qb = u_ref[0, rows, LANES:qd].astype(F32)
        k = u_ref[0, rows, qd:qd + kvd].astype(BF16)
        v = u_ref[0, rows, qd + kvd:qd + 2 * kvd].astype(BF16)
        q4 = jnp.concatenate([jnp.where(low, qa, 0.0), jnp.where(low, 0.0, qa),
                              jnp.where(low, qb, 0.0), jnp.where(low, 0.0, qb)],
                             axis=0).astype(BF16)
        kband = jnp.concatenate([kprev, k], axis=0)
        vband = jnp.concatenate([vprev, v], axis=0)
        logits = _dot_nt(q4, kband) * (HEAD_DIM ** -0.5) + bias_ref[...]
        if ci == 0:
            logits = jnp.where(jnp.logical_and(first_step, col < W), MASK_VALUE, logits)
        m = jnp.maximum(jnp.max(logits, -1, keepdims=True), sink)
        p = jnp.exp(logits - m)
        denom = jnp.sum(p, -1, keepdims=True) + jnp.exp(sink - m)
        res = _dot(p.astype(BF16), vband) / denom
        out_a = jnp.where(low, res[0:W], res[W:2 * W])
        out_b = jnp.where(low, res[2 * W:3 * W], res[3 * W:4 * W])
        o_ref[0, rows, 0:LANES] = out_a.astype(o_ref.dtype)
        o_ref[0, rows, LANES:qd] = out_b.astype(o_ref.dtype)
        kprev, vprev = k, v
        yield
    kprev_ref[...] = kprev
    vprev_ref[...] = vprev


SWA_HEAD_ORDER = (0, 2, 1, 3)


def _ssd_body(u_ref, dt_ref, cw_ref, cb_ref, dtb_c_ref, alog_c_ref, dskip_ref, nw_ref,
              tril_ref, expand_ref, gmask_ref,
              o_ref, state_ref, ext_ref, *, n_chunks):
    C = CHUNK
    sd = SSD_HEADS * HEAD_DIM
    gn = SSD_GROUPS * SSD_STATE
    cd = sd + 2 * gn
    tl = n_chunks * C
    slab = 2 * LANES
    heads_per_group = SSD_HEADS // SSD_GROUPS
    heads_per_slab = slab // HEAD_DIM
    lane = lax.broadcasted_iota(I32, (1, LANES), 1)
    low = lane < SSD_STATE
    slab_masks = [_head_lane_mask(slab, hh) for hh in range(heads_per_slab)]
    row_i = lax.broadcasted_iota(I32, (C, C), 0)
    col_i = lax.broadcasted_iota(I32, (C, C), 1)
    causal = row_i >= col_i
    neg_a_c = -jnp.exp(alog_c_ref[...])

    ext_ref[8:8 + tl, :] = u_ref[0, :, sd:sd + cd].astype(F32)
    state = state_ref[...]
    for ci in range(n_chunks):
        rows = slice(ci * C, (ci + 1) * C)
        z = u_ref[0, rows, 0:sd].astype(F32)
        dt_raw = dt_ref[0, rows, :]
        base = 8 + ci * C
        conv = cb_ref[...]
        for w in range(SSD_CONV):
            shift = SSD_CONV - 1 - w
            conv = conv + cw_ref[w:w + 1, :] * ext_ref[base - shift:base - shift + C, :]
        xbc = _silu(conv)
        xs = xbc[:, 0:sd]
        bm = xbc[:, sd:sd + gn]
        cm = xbc[:, sd + gn:cd]

        dt_c = _softplus(dt_raw + dtb_c_ref[...])
        a_c = neg_a_c * dt_c
        acs_c = _dot3_left(tril_ref[...], a_c)
        acs_t = acs_c.T
        spread = _dot3(acs_c, expand_ref[...])
        acs_x = spread[:, 0:sd]
        dt_x = _dot3(dt_c, expand_ref[:, 0:sd])
        xdt = xs * dt_x

        bstack = jnp.concatenate([jnp.where(low, bm, 0.0), jnp.where(low, 0.0, bm)],
                                 axis=0).astype(BF16)
        cb = _dot_nt(cm.astype(BF16), bstack)
        y_diag = []
        for s in range(sd // slab):
            ms = []
            for hh in range(heads_per_slab):
                h = s * heads_per_slab + hh
                g = h // heads_per_group
                col_bcast = spread[:, sd + h * LANES:sd + (h + 1) * LANES]
                seg = col_bcast - acs_t[h:h + 1, :]
                lmat = jnp.exp(jnp.where(causal, seg, MASK_VALUE))
                ms.append((cb[:, g * C:(g + 1) * C] * lmat).astype(BF16))
            xslab = xdt[:, s * slab:(s + 1) * slab]
            xstack = jnp.concatenate([jnp.where(m, xslab, 0.0) for m in slab_masks],
                                     axis=0).astype(BF16)
            y_diag.append(_dot(jnp.concatenate(ms, axis=1), xstack))
        y_diag = jnp.concatenate(y_diag, axis=1)

        y_off = _dot(cm.astype(BF16), state.astype(BF16)) * jnp.exp(acs_x)
        last = acs_x[C - 1:C, :]
        dec = jnp.exp(last - acs_x)
        new = _dot(bm.T.astype(BF16), (xdt * dec).astype(BF16))
        state = jnp.exp(last) * state + gmask_ref[...] * new

        y = y_diag + y_off + xs * dskip_ref[...]
        hgate = y * _silu(z)
        gw = sd // SSD_GROUPS
        for g in range(SSD_GROUPS):
            hg = hgate[:, g * gw:(g + 1) * gw]
            ms_ = jnp.mean(hg * hg, -1, keepdims=True)
            o_ref[0, rows, g * gw:(g + 1) * gw] = (
                hg * lax.rsqrt(ms_ + LN_EPS) * nw_ref[:, g * gw:(g + 1) * gw]).astype(o_ref.dtype)
        yield
    state_ref[...] = state
    ext_ref[0:8, :] = ext_ref[tl:tl + 8, :]


def _ssd_tables():
    C = CHUNK
    sd = SSD_HEADS * HEAD_DIM
    gn = SSD_GROUPS * SSD_STATE
    t = jnp.arange(C)
    tril = (t[:, None] >= t[None, :]).astype(BF16)
    r = jnp.arange(LANES)[:, None]
    eexp = ((r == (jnp.arange(sd)[None, :] // HEAD_DIM)) & (r < SSD_HEADS)).astype(BF16)
    bsel = ((r == (jnp.arange(SSD_HEADS * LANES)[None, :] // LANES)) & (r < SSD_HEADS)).astype(BF16)
    heads_per_group = SSD_HEADS // SSD_GROUPS
    row_g = jnp.arange(gn)[:, None] // SSD_STATE
    col_g = (jnp.arange(sd)[None, :] // HEAD_DIM) // heads_per_group
    gmask = (row_g == col_g).astype(F32)
    expand = jnp.concatenate([eexp, bsel], axis=1)
    return jnp.tile(tril, (1, 3)), jnp.tile(expand, (3, 1)), gmask


N_RET_TABLES = 6
N_SSD_CONSTS = 9


def _mixers_kernel(*refs, n_chunks, n_cast):
    u_ret, cos, sin, u_ssd, u_dt, u_swa = refs[:6]
    pos = 6
    ret_tables = refs[pos:pos + N_RET_TABLES]
    pos += N_RET_TABLES
    ssd_consts = refs[pos:pos + N_SSD_CONSTS]
    pos += N_SSD_CONSTS
    bias, sink = refs[pos:pos + 2]
    pos += 2
    cast_in = refs[pos:pos + n_cast]
    pos += n_cast
    y_ret, y_ssd, y_swa = refs[pos:pos + 3]
    pos += 3
    cast_out = refs[pos:pos + n_cast]
    ret_state, ssd_state, ssd_ext, kprev, vprev = refs[pos + n_cast:]
    _side_cast(cast_in, cast_out)

    @pl.when(pl.program_id(1) == 0)
    def _():
        ret_state[...] = jnp.zeros_like(ret_state)
        ssd_state[...] = jnp.zeros_like(ssd_state)
        ssd_ext[0:8, :] = jnp.zeros((8, ssd_ext.shape[1]), F32)
        kprev[...] = jnp.zeros_like(kprev)
        vprev[...] = jnp.zeros_like(vprev)

    bodies = [
        _ssd_body(u_ssd, u_dt, *ssd_consts, y_ssd, ssd_state, ssd_ext, n_chunks=n_chunks),
        _swa_body(u_swa, bias, sink, y_swa, kprev, vprev, n_chunks=n_chunks),
        _retention_body(u_ret, cos, sin, *ret_tables, y_ret, ret_state, n_chunks=n_chunks),
    ]
    for _ in range(n_chunks + 1):
        for body in bodies:
            next(body, None)


def _mixers(u_ret, u_ssd, u_dt, u_swa, cos_t, sin_t, bias_tab, sinks_l, conv_w, conv_b,
            dt_bias, a_log, d_skip, norm_w, tl, side_cast=()):
    nb, L, _ = u_ret.shape
    W = CHUNK
    rd = RET_HEADS * HEAD_DIM
    sd = SSD_HEADS * HEAD_DIM
    qd = SWA_HEADS * HEAD_DIM
    cd = conv_w.shape[-1]
    ret_tables = _retention_tables()
    pad = lambda v: jnp.zeros((1, LANES), F32).at[0, :SSD_HEADS].set(v)
    rep = lambda v: jnp.repeat(v, HEAD_DIM)[None, :]
    ssd_consts = (conv_w, conv_b[None, :], pad(dt_bias), pad(a_log), rep(d_skip),
                  norm_w[None, :]) + _ssd_tables()
    order = jnp.array(SWA_HEAD_ORDER)
    bias_stacked = bias_tab[order].reshape(SWA_HEADS * W, 2 * W)
    sink_col = jnp.repeat(sinks_l.astype(F32)[order], W)[:, None]
    assert len(ret_tables) == N_RET_TABLES and len(ssd_consts) == N_SSD_CONSTS
    tok = lambda a: pl.BlockSpec((1, tl, a.shape[-1]), lambda b, i: (b, i, 0))
    const = lambda a: pl.BlockSpec(a.shape, lambda b, i: (0,) * a.ndim)
    tokens = (u_ret, cos_t, sin_t, u_ssd, u_dt, u_swa)
    consts = ret_tables + ssd_consts + (bias_stacked, sink_col)
    c_in, c_out, c_shapes = _side_cast_specs(side_cast, nb * (L // tl), L // tl)
    return pl.pallas_call(
        functools.partial(_mixers_kernel, n_chunks=tl // CHUNK, n_cast=len(side_cast)),
        out_shape=[jax.ShapeDtypeStruct((nb, L, w), BF16) for w in (rd, sd, qd)] + c_shapes,
        grid=(nb, L // tl),
        in_specs=[tok(a) for a in tokens] + [const(a) for a in consts] + c_in,
        out_specs=[pl.BlockSpec((1, tl, w), lambda b, i: (b, i, 0))
                   for w in (rd, sd, qd)] + c_out,
        scratch_shapes=[pltpu.VMEM((rd, rd), F32),
                        pltpu.VMEM((SSD_GROUPS * SSD_STATE, sd), F32),
                        pltpu.VMEM((8 + tl, cd), F32),
                        pltpu.VMEM((W, LANES), BF16), pltpu.VMEM((W, LANES), BF16)],
        compiler_params=_cparams("arbitrary", "arbitrary"),
        name="mixers",
    )(*tokens, *consts, *side_cast)


def _out_proj_kernel(x_ref, yr_ref, ys_ref, ya_ref, mod_ref, wr_ref, ws_ref, wa_ref,
                     g_ref, b_ref, o_ref, *, alpha):
    mix = (_dot(yr_ref[0], wr_ref[...]) + _dot(ys_ref[0], ws_ref[...])
           + _dot(ya_ref[0], wa_ref[...]))
    gate = mod_ref[0, 0, 2:3, :]
    r = alpha * x_ref[0] + (1.0 + gate) * mix
    o_ref[0] = _layer_norm(r, g_ref[...], b_ref[...])


def _out_proj(x, y_ret, y_ssd, y_swa, mod_l, w_r, w_s, w_a, ln_g, ln_b, alpha, tm):
    nb, L, d = x.shape
    tok = lambda w: pl.BlockSpec((1, tm, w), lambda b, i: (b, i, 0))
    const = lambda a: pl.BlockSpec(a.shape, lambda b, i: (0,) * a.ndim)
    return pl.pallas_call(
        functools.partial(_out_proj_kernel, alpha=alpha),
        out_shape=jax.ShapeDtypeStruct((nb, L, d), F32),
        grid=(nb, L // tm),
        in_specs=[tok(d), tok(y_ret.shape[-1]), tok(y_ssd.shape[-1]), tok(y_swa.shape[-1]),
                  pl.BlockSpec((1, 1, 6, d), lambda b, i: (0, b, 0, 0)),
                  const(w_r), const(w_s), const(w_a), const(ln_g), const(ln_b)],
        out_specs=tok(d),
        compiler_params=_cparams("arbitrary", "arbitrary"),
        name="out_proj_ln",
    )(x, y_ret, y_ssd, y_swa, mod_l, w_r, w_s, w_a, ln_g, ln_b)


FF_CHUNK = 256


def _side_cast_specs(arrays, steps, steps_per_seq):
    in_specs, out_specs, out_shapes = [], [], []
    for a in arrays:
        e, r, c = a.shape
        per = steps // e
        spec = pl.BlockSpec((1, r // per, c),
                            lambda b, i, per=per: ((b * steps_per_seq + i) // per,
                                                   (b * steps_per_seq + i) % per, 0))
        in_specs.append(spec)
        out_specs.append(spec)
        out_shapes.append(jax.ShapeDtypeStruct(a.shape, BF16))
    return in_specs, out_specs, out_shapes


def _side_cast(in_refs, out_refs):
    for i_ref, o_ref in zip(in_refs, out_refs):
        o_ref[...] = i_ref[...].astype(o_ref.dtype)


def _dense_ffn_kernel(x_ref, mod_ref, wg_ref, wu_ref, wd_ref, g_ref, b_ref, *rest, alpha):
    n_cast = (len(rest) - 1) // 2
    o_ref = rest[n_cast]
    _side_cast(rest[:n_cast], rest[n_cast + 1:])
    x = x_ref[0]
    sh = mod_ref[0, 0, 3:4, :]
    sc = mod_ref[0, 0, 4:5, :]
    gate = mod_ref[0, 0, 5:6, :]
    h = (x * (1.0 + sc) + sh).astype(BF16)
    dff = wg_ref.shape[-1]
    acc = jnp.zeros(x.shape, F32)
    for j in range(0, dff, FF_CHUNK):
        gj = _dot(h, wg_ref[:, j:j + FF_CHUNK])
        uj = _dot(h, wu_ref[:, j:j + FF_CHUNK])
        acc = acc + _dot((_silu(gj) * uj).astype(BF16), wd_ref[j:j + FF_CHUNK, :])
    r = alpha * x + (1.0 + gate) * acc
    o_ref[0] = _layer_norm(r, g_ref[...], b_ref[...])


def _dense_ffn(x, mod_l, wg, wu, wd, ln_g, ln_b, alpha, tm, side_cast=()):
    nb, L, d = x.shape
    const = lambda a: pl.BlockSpec(a.shape, lambda b, i: (0,) * a.ndim,
                                   pipeline_mode=pl.Buffered(1))
    c_in, c_out, c_shapes = _side_cast_specs(side_cast, nb * (L // tm), L // tm)
    return pl.pallas_call(
        functools.partial(_dense_ffn_kernel, alpha=alpha),
        out_shape=[jax.ShapeDtypeStruct((nb, L, d), F32)] + c_shapes,
        grid=(nb, L // tm),
        in_specs=[pl.BlockSpec((1, tm, d), lambda b, i: (b, i, 0)),
                  pl.BlockSpec((1, 1, 6, d), lambda b, i: (0, b, 0, 0)),
                  const(wg), const(wu), const(wd), const(ln_g), const(ln_b)] + c_in,
        out_specs=[pl.BlockSpec((1, tm, d), lambda b, i: (b, i, 0))] + c_out,
        compiler_params=_cparams("arbitrary", "arbitrary"),
        name="dense_ffn_ln",
    )(x, mod_l, wg, wu, wd, ln_g, ln_b, *side_cast)


def _store_row_tiles(ref, value):
    rows, d = value.shape
    nt = d // LANES
    for c in range(nt):
        ref[pl.ds(c, rows, stride=nt), :] = value[:, c * LANES:(c + 1) * LANES].astype(ref.dtype)


def _load_row_tiles(ref):
    nt = ROW_TILE
    rows = ref.shape[0] // nt
    return jnp.concatenate([ref[pl.ds(c, rows, stride=nt), :] for c in range(nt)], axis=1)


def _router_kernel(x_ref, mod_ref, wr_ref, br_ref, trils_ref, h_ref, route_ref, cnt_ref,
                   base_ref):
    first = jnp.logical_and(pl.program_id(0) == 0, pl.program_id(1) == 0)

    @pl.when(first)
    def _():
        base_ref[...] = jnp.zeros_like(base_ref)

    sh = mod_ref[0, 0, 3:4, :]
    sc = mod_ref[0, 0, 4:5, :]
    h = x_ref[0] * (1.0 + sc) + sh
    _store_row_tiles(h_ref.at[0], h)
    logits = jnp.dot(h, wr_ref[...], preferred_element_type=F32,
                     precision=lax.Precision.HIGHEST) + br_ref[...]
    lane = lax.broadcasted_iota(I32, logits.shape, 1).astype(F32)
    logits = jnp.where(lane < N_EXPERTS, logits, MASK_VALUE)
    v1 = jnp.max(logits, -1, keepdims=True)
    e1 = jnp.min(jnp.where(logits == v1, lane, float(LANES)), -1, keepdims=True)
    rest = jnp.where(lane == e1, MASK_VALUE, logits)
    v2 = jnp.max(rest, -1, keepdims=True)
    e2 = jnp.min(jnp.where(rest == v2, lane, float(LANES)), -1, keepdims=True)
    t = jnp.exp(v2 - v1)
    w1 = 1.0 / (1.0 + t)
    w2 = t / (1.0 + t)
    hot1 = (lane == e1).astype(F32)
    hot2 = (lane == e2).astype(F32)
    both = hot1 + hot2
    base = base_ref[0:1, :]
    before = _dot(trils_ref[...], both.astype(BF16)) + base
    rank1 = jnp.sum(hot1 * before, -1, keepdims=True)
    rank2 = jnp.sum(hot2 * before, -1, keepdims=True)
    total = base + jnp.sum(both, 0, keepdims=True)
    base_ref[0:1, :] = total
    cnt_ref[...] = jnp.broadcast_to(total, cnt_ref.shape)
    route = jnp.where(lane == 0, e1, 0.0)
    route = jnp.where(lane == 1, e2, route)
    route = jnp.where(lane == 2, rank1, route)
    route = jnp.where(lane == 3, rank2, route)
    route = jnp.where(lane == 4, w1, route)
    route = jnp.where(lane == 5, w2, route)
    route_ref[0] = route


def _router(x, mod_l, w_router, b_router, tm):
    nb, L, d = x.shape
    wr = jnp.zeros((d, LANES), F32).at[:, :N_EXPERTS].set(w_router)
    br = jnp.zeros((1, LANES), F32).at[0, :N_EXPERTS].set(b_router)
    t = jnp.arange(tm)
    tril_strict = (t[:, None] > t[None, :]).astype(BF16)
    const = lambda a: pl.BlockSpec(a.shape, lambda b, i: (0,) * a.ndim)
    return pl.pallas_call(
        _router_kernel,
        out_shape=[jax.ShapeDtypeStruct((nb, L * (d // LANES), LANES), F32),
                   jax.ShapeDtypeStruct((nb, L, LANES), F32),
                   jax.ShapeDtypeStruct((8, LANES), F32)],
        grid=(nb, L // tm),
        in_specs=[pl.BlockSpec((1, tm, d), lambda b, i: (b, i, 0)),
                  pl.BlockSpec((1, 1, 6, d), lambda b, i: (0, b, 0, 0)),
                  const(wr), const(br), const(tril_strict)],
        out_specs=[pl.BlockSpec((1, tm * (d // LANES), LANES), lambda b, i: (b, i, 0)),
                   pl.BlockSpec((1, tm, LANES), lambda b, i: (b, i, 0)),
                   pl.BlockSpec((8, LANES), lambda b, i: (0, 0))],
        scratch_shapes=[pltpu.VMEM((8, LANES), F32)],
        compiler_params=_cparams("arbitrary", "arbitrary"),
        name="moe_router",
    )(x, mod_l, wr, br, tril_strict)


TOP_K = 2
ROW_TILE = 8
EXPERT_FF_SPLITS = 2


def _ff_chunks(width, step=512):
    return [(a, min(a + step, width)) for a in range(0, width, step)]


def _expert_kernel(te_ref, nu_ref, gsrc_ref, sdst_ref,
                   h_ref, wg_ref, wu_ref, wd_ref, yk_ref,
                   xbuf, ybuf, hbuf, acc, gsem, ssem, *, tme):
    i = pl.program_id(0)
    j = pl.program_id(1)
    n_used = nu_ref[0]
    slot = i % 2
    other = 1 - slot
    used = i < n_used
    chunks = _ff_chunks(wg_ref.shape[-1])
    rows_per_step = tme // EXPERT_FF_SPLITS
    n_front = len(chunks) // 2
    front_w = chunks[n_front - 1][1]
    back_w = wg_ref.shape[-1] - front_w
    gather_ranges, scatter_ranges = [], []
    for c, (a, b) in enumerate(chunks):
        if c < n_front:
            gather_ranges.append((rows_per_step * a // front_w, rows_per_step * b // front_w))
            scatter_ranges.append((0, 0))
        else:
            gather_ranges.append((0, 0))
            scatter_ranges.append((rows_per_step * (a - front_w) // back_w,
                                   rows_per_step * (b - front_w) // back_w))

    nt = ROW_TILE

    def gather(tile, slot_, r):
        src = pl.multiple_of(gsrc_ref[tile * tme + r], nt)
        return pltpu.make_async_copy(h_ref.at[pl.ds(src, nt)],
                                     xbuf.at[slot_, pl.ds(pl.multiple_of(r * nt, nt), nt)],
                                     gsem.at[slot_])

    def wait_gather(slot_):
        pltpu.make_async_copy(h_ref.at[pl.ds(0, tme * nt)], xbuf.at[slot_],
                              gsem.at[slot_]).wait()

    def scatter(block, slot_, r):
        dst = pl.multiple_of(sdst_ref[block * tme + r], nt)
        return pltpu.make_async_copy(ybuf.at[slot_, pl.ds(pl.multiple_of(r * nt, nt), nt)],
                                     yk_ref.at[pl.ds(dst, nt)], ssem.at[slot_])

    def wait_scatter(slot_):
        pltpu.make_async_copy(ybuf.at[slot_], yk_ref.at[pl.ds(0, tme * nt)],
                              ssem.at[slot_]).wait()

    @pl.when(jnp.logical_and(i == 0, j == 0))
    def _():
        ybuf[1] = jnp.zeros(ybuf.shape[1:], F32)

        def body(r, carry):
            gather(0, 0, r).start()
            return carry
        lax.fori_loop(0, tme, body, 0, unroll=8)

    @pl.when(used)
    def _():
        @pl.when(j == 0)
        def _():
            wait_gather(slot)
            hbuf[...] = _load_row_tiles(xbuf.at[slot]).astype(BF16)

            @pl.when(i > 0)
            def _():
                wait_scatter(slot)
            acc[...] = jnp.zeros(acc.shape, F32)

        h = hbuf[...]
        for c, (a, b) in enumerate(chunks):
            for r in range(*gather_ranges[c]):
                gather(i + 1, other, j * rows_per_step + r).start()
            for r in range(*scatter_ranges[c]):
                scatter(i, other, j * rows_per_step + r).start()
            gj = _dot(h, wg_ref[0, :, a:b])
            uj = _dot(h, wu_ref[0, :, a:b])
            part = _dot((_silu(gj) * uj).astype(BF16), wd_ref[0, a:b, :])
            acc[...] = acc[...] + part
            ybuf[slot, 0:ROW_TILE, :] = part[0:ROW_TILE, 0:LANES]

        @pl.when(j == EXPERT_FF_SPLITS - 1)
        def _():
            _store_row_tiles(ybuf.at[slot], acc[...])

        @pl.when(jnp.logical_and(j == EXPERT_FF_SPLITS - 1, i == n_used - 1))
        def _():
            wait_gather(other)
            wait_scatter(other)

            def body(r, carry):
                scatter(i + 1, slot, r).start()
                return carry
            lax.fori_loop(0, tme, body, 0, unroll=8)
            wait_scatter(slot)


def _expert_ffn(h_rows, tile_expert, n_used, gsrc, sdst, wg, wu, wd, tme):
    nt = ROW_TILE
    T = h_rows.shape[0] // nt
    d = nt * LANES
    dff = wg.shape[-1]
    dffh = dff // EXPERT_FF_SPLITS
    n_tiles = gsrc.shape[0] // tme
    yk_rows = TOP_K * T + tme

    def half(i, j):
        return jnp.where(i % 2 == 0, j, EXPERT_FF_SPLITS - 1 - j)

    return pl.pallas_call(
        functools.partial(_expert_kernel, tme=tme),
        out_shape=jax.ShapeDtypeStruct((yk_rows * nt, LANES), F32),
        grid_spec=pltpu.PrefetchScalarGridSpec(
            num_scalar_prefetch=4,
            grid=(n_tiles, EXPERT_FF_SPLITS),
            in_specs=[pl.BlockSpec(memory_space=pl.ANY),
                      pl.BlockSpec((1, d, dffh), lambda i, j, te, *_: (te[i], 0, half(i, j))),
                      pl.BlockSpec((1, d, dffh), lambda i, j, te, *_: (te[i], 0, half(i, j))),
                      pl.BlockSpec((1, dffh, d), lambda i, j, te, *_: (te[i], half(i, j), 0))],
            out_specs=pl.BlockSpec(memory_space=pl.ANY),
            scratch_shapes=[pltpu.VMEM((2, tme * nt, LANES), F32),
                            pltpu.VMEM((2, tme * nt, LANES), F32),
                            pltpu.VMEM((tme, d), BF16), pltpu.VMEM((tme, d), F32),
                            pltpu.SemaphoreType.DMA((2,)), pltpu.SemaphoreType.DMA((2,))]),
        compiler_params=_cparams("arbitrary", "arbitrary"),
        name="moe_experts",
    )(tile_expert, n_used, gsrc, sdst, h_rows, wg, wu, wd)


def _moe_finish_kernel(x_ref, route_ref, mod_ref, y0_ref, y1_ref, g_ref, b_ref, o_ref, *, alpha):
    route = route_ref[0]
    f = route[:, 4:5] * _load_row_tiles(y0_ref) + route[:, 5:6] * _load_row_tiles(y1_ref)
    gate = mod_ref[0, 0, 5:6, :]
    r = alpha * x_ref[0] + (1.0 + gate) * f
    o_ref[0] = _layer_norm(r, g_ref[...], b_ref[...])


def _moe_finish(x, route, mod_l, yk, ln_g, ln_b, alpha, tm):
    nb, L, d = x.shape
    tiles_per_seq = L // tm
    tiles = nb * tiles_per_seq
    return pl.pallas_call(
        functools.partial(_moe_finish_kernel, alpha=alpha),
        out_shape=jax.ShapeDtypeStruct((nb, L, d), F32),
        grid=(nb, tiles_per_seq),
        in_specs=[pl.BlockSpec((1, tm, d), lambda b, i: (b, i, 0)),
                  pl.BlockSpec((1, tm, LANES), lambda b, i: (b, i, 0)),
                  pl.BlockSpec((1, 1, 6, d), lambda b, i: (0, b, 0, 0)),
                  pl.BlockSpec((tm * ROW_TILE, LANES), lambda b, i: (b * tiles_per_seq + i, 0)),
                  pl.BlockSpec((tm * ROW_TILE, LANES),
                               lambda b, i: (tiles + b * tiles_per_seq + i, 0)),
                  pl.BlockSpec(ln_g.shape, lambda b, i: (0, 0)),
                  pl.BlockSpec(ln_b.shape, lambda b, i: (0, 0))],
        out_specs=pl.BlockSpec((1, tm, d), lambda b, i: (b, i, 0)),
        compiler_params=_cparams("arbitrary", "arbitrary"),
        name="moe_finish_ln",
    )(x, route, mod_l, yk, yk, ln_g, ln_b)


def _row_index_kernel(dest_ref, gsrc0_ref, sdst0_ref, gsrc_ref, sdst_ref, sem, *, n_pairs, T, tme):
    init_g = pltpu.make_async_copy(gsrc0_ref, gsrc_ref, sem.at[0])
    init_s = pltpu.make_async_copy(sdst0_ref, sdst_ref, sem.at[1])
    init_g.start()
    init_s.start()
    init_g.wait()
    init_s.wait()

    assert TOP_K == 2
    group = 8

    def body(g, carry):
        f0 = g * group
        rows = [dest_ref[f0 + k] for k in range(group)]
        for k in range(group):
            tok = g * (group // TOP_K) + k // TOP_K
            gsrc_ref[rows[k]] = tok * ROW_TILE
            sdst_ref[tme + rows[k]] = ((k % TOP_K) * T + tok) * ROW_TILE
        return carry
    lax.fori_loop(0, n_pairs // group, body, 0, unroll=2)


def _row_indices(dest, n_rows, T, tme):
    dump = (TOP_K * T + jnp.arange(n_rows + tme, dtype=I32) % tme) * ROW_TILE
    return pl.pallas_call(
        functools.partial(_row_index_kernel, n_pairs=dest.shape[0], T=T, tme=tme),
        out_shape=[jax.ShapeDtypeStruct((n_rows,), I32),
                   jax.ShapeDtypeStruct((n_rows + tme,), I32)],
        in_specs=[pl.BlockSpec(memory_space=pltpu.SMEM),
                  pl.BlockSpec(memory_space=pl.ANY),
                  pl.BlockSpec(memory_space=pl.ANY)],
        out_specs=[pl.BlockSpec(memory_space=pltpu.SMEM),
                   pl.BlockSpec(memory_space=pltpu.SMEM)],
        scratch_shapes=[pltpu.SemaphoreType.DMA((2,))],
        name="moe_row_indices",
    )(dest, jnp.zeros((n_rows,), I32), dump)


def _moe(x, mod_l, w_router, b_router, wg, wu, wd, ln_g, ln_b, alpha, tm_route, tme, tmf):
    nb, L, d = x.shape
    T = nb * L
    h, route, counts = _router(x, mod_l, w_router, b_router, tm_route)
    route_flat = route.reshape(T, LANES)
    e = route_flat[:, 0:2].astype(I32)
    rank = route_flat[:, 2:4].astype(I32)
    cnt = counts[0, :N_EXPERTS].astype(I32)
    tiles = (cnt + tme - 1) // tme
    tile_end = jnp.cumsum(tiles)
    group_start = (tile_end - tiles) * tme
    dest = (group_start[e] + rank).reshape(-1)
    n_tiles = (TOP_K * T) // tme + N_EXPERTS
    n_rows = n_tiles * tme
    tile_expert = jnp.minimum(
        jnp.searchsorted(tile_end, jnp.arange(n_tiles), side="right"), N_EXPERTS - 1).astype(I32)
    n_used = tile_end[-1:].astype(I32)
    gsrc, sdst = _row_indices(dest, n_rows, T, tme)
    yk = _expert_ffn(h.reshape(T * ROW_TILE, LANES), tile_expert, n_used, gsrc, sdst,
                     wg, wu, wd, tme)
    return _moe_finish(x, route, mod_l, yk, ln_g, ln_b, alpha, tmf)


def kernel(x, c, positions, rel_bias, w_ada, b_ada, w_in, w_out, conv_w, conv_b, dt_bias, a_log,
           d_skip, ssd_norm_w, sinks, ln_g, ln_b, ffn_w_gate, ffn_w_up, ffn_w_down, router_w,
           router_b, expert_w_gate, expert_w_up, expert_w_down):
    depth = w_ada.shape[0]
    nb, L, d = x.shape
    alpha = (2 * depth) ** 0.25
    rd = RET_HEADS * HEAD_DIM
    sd = SSD_HEADS * HEAD_DIM
    cd = conv_w.shape[-1]
    qd = SWA_HEADS * HEAD_DIM
    kvd = SWA_KV_HEADS * HEAD_DIM
    sizes = (rd, rd, rd, rd, sd, cd, SSD_HEADS, qd, kvd, kvd)
    offs = np.concatenate([[0], np.cumsum(sizes)])
    tl = min(512, L)

    mod = _ada_mod(c, w_ada, b_ada)
    cos_t, sin_t = _rotary_tables(positions, tl)
    bias_tab = _swa_bias_table(rel_bias)

    q_perm = np.concatenate([np.arange(h * HEAD_DIM, (h + 1) * HEAD_DIM) for h in SWA_HEAD_ORDER])
    widths = (4 * rd, sd + cd, LANES, qd + 2 * kvd)
    dtypes = (BF16, BF16, F32, BF16)

    steps = nb * (L // tl)
    n_exp, _, dff_e = expert_w_gate.shape[1:]
    per = steps // n_exp
    side_ok = (steps % n_exp == 0 and d % per == 0 and dff_e % per == 0
               and (d // per) % 8 == 0 and (dff_e // per) % 8 == 0)

    for layer in range(depth):
        wl = w_in[layer]
        seg = lambda i: wl[:, offs[i]:offs[i + 1]]
        dt_cols = jnp.zeros((d, LANES), F32).at[:, :SSD_HEADS].set(seg(6))
        w_cat = jnp.concatenate(
            [seg(0), seg(1), seg(2), seg(3), seg(4), seg(5), dt_cols,
             seg(7)[:, q_perm], seg(8), seg(9)], axis=1).astype(BF16)
        wo = w_out[layer]
        w_r = wo[0:rd].astype(BF16)
        w_s = wo[rd:rd + sd].astype(BF16)
        w_a = wo[rd + sd:][q_perm].astype(BF16)
        mod_l = mod[layer:layer + 1]

        u_ret, u_ssd, u_dt, u_swa = _in_proj(x, mod_l, w_cat, widths, dtypes, tl)
        i = layer // 2
        is_moe = layer % 2 == 1
        down_cast = (expert_w_down[i],) if is_moe and side_ok else ()
        y_ret, y_ssd, y_swa, *down_bf16 = _mixers(
            u_ret, u_ssd, u_dt, u_swa, cos_t, sin_t, bias_tab, sinks[layer], conv_w[layer],
            conv_b[layer], dt_bias[layer], a_log[layer], d_skip[layer], ssd_norm_w[layer], tl,
            side_cast=down_cast)
        x = _out_proj(x, y_ret, y_ssd, y_swa, mod_l, w_r, w_s, w_a,
                      ln_g[layer, 0][None, :], ln_b[layer, 0][None, :], alpha, tl)

        g2 = ln_g[layer, 1][None, :]
        b2 = ln_b[layer, 1][None, :]
        if not is_moe:
            nxt = (layer + 1) // 2
            ride = (expert_w_gate[nxt], expert_w_up[nxt]) if layer + 1 < depth and side_ok else ()
            x, *gate_up_bf16 = _dense_ffn(
                x, mod_l, ffn_w_gate[i].astype(BF16), ffn_w_up[i].astype(BF16),
                ffn_w_down[i].astype(BF16), g2, b2, alpha, tl, side_cast=ride)
        else:
            if side_ok:
                wg_e, wu_e = gate_up_bf16
                wd_e, = down_bf16
            else:
                wg_e, wu_e, wd_e = (expert_w_gate[i].astype(BF16), expert_w_up[i].astype(BF16),
                                    expert_w_down[i].astype(BF16))
            x = _moe(x, mod_l, router_w[i], router_b[i], wg_e, wu_e, wd_e, g2, b2, alpha,
                     tm_route=min(512, L), tme=min(512, L), tmf=min(512, L))
    return x
```

```python
import functools
import math

import numpy as np
import jax
import jax.numpy as jnp
from jax import lax
from jax.experimental import pallas as pl
from jax.experimental.pallas import tpu as pltpu

F32 = jnp.float32
BF16 = jnp.bfloat16
I32 = jnp.int32

HEAD_DIM = 64
CHUNK = 128
RET_HEADS = 4
SSD_HEADS = 8
SSD_GROUPS = 2
SSD_STATE = 64
SSD_CONV = 4
CONV_TAIL = 16
SWA_HEADS = 4
SWA_KV_HEADS = 2
REL_BUCKETS = 32
N_EXPERTS = 8
LN_EPS = 1e-5
LANES = 128
MASK_VALUE = -1e30

VMEM_LIMIT = 56 * 1024 * 1024


def _cparams(*sem):
    return pltpu.CompilerParams(dimension_semantics=sem, vmem_limit_bytes=VMEM_LIMIT)


def _silu(v):
    return v * (1.0 / (1.0 + jnp.exp(-v)))


def _softplus(v):
    return jnp.maximum(v, 0.0) + jnp.log(1.0 + jnp.exp(-jnp.abs(v)))


def _dot(a, b):
    return jnp.dot(a, b, preferred_element_type=F32)


def _dot_nt(a, b):
    return lax.dot_general(a, b, (((1,), (1,)), ((), ())), preferred_element_type=F32)


def _split3(v):
    h1 = v.astype(BF16)
    r1 = v - h1.astype(F32)
    h2 = r1.astype(BF16)
    r2 = r1 - h2.astype(F32)
    return h1, h2, r2.astype(BF16)


def _dot3(v, m3):
    return _dot(jnp.concatenate(_split3(v), axis=1), m3)


def _dot3_left(m3, v):
    return _dot(m3, jnp.concatenate(_split3(v), axis=0))


def _dot2(v, m2):
    h1 = v.astype(BF16)
    h2 = (v - h1.astype(F32)).astype(BF16)
    return _dot(jnp.concatenate([h1, h2], axis=1), m2)


def _layer_norm(r, g, b):
    mu = jnp.mean(r, -1, keepdims=True)
    d = r - mu
    var = jnp.mean(d * d, -1, keepdims=True)
    return d * lax.rsqrt(var + LN_EPS) * g + b


def _ada_kernel(c_ref, w_ref, b_ref, o_ref):
    o_ref[0] = jnp.dot(c_ref[...], w_ref[0], preferred_element_type=F32,
                       precision=lax.Precision.HIGHEST) + b_ref[0]


def _ada_mod(c, w_ada, b_ada):
    depth, d, d6 = w_ada.shape
    nb = c.shape[0]
    rows = 8
    c_pad = jnp.zeros((rows, d), F32).at[:nb].set(c)
    out = pl.pallas_call(
        _ada_kernel,
        out_shape=jax.ShapeDtypeStruct((depth, rows, d6), F32),
        grid=(depth, d6 // d),
        in_specs=[pl.BlockSpec((rows, d), lambda l, j: (0, 0)),
                  pl.BlockSpec((1, d, d), lambda l, j: (l, 0, j)),
                  pl.BlockSpec((1, 1, d), lambda l, j: (l, 0, j))],
        out_specs=pl.BlockSpec((1, rows, d), lambda l, j: (l, 0, j)),
        compiler_params=_cparams("arbitrary", "arbitrary"),
        name="ada_mod",
    )(c_pad, w_ada, b_ada.reshape(depth, 1, d6))
    return out[:, :nb].reshape(depth, nb, 6, d)


def _rotary_kernel(pos_ref, cos_ref, sin_ref):
    half = HEAD_DIM // 2
    lane = lax.broadcasted_iota(I32, (1, LANES), 1)
    jj = lane % HEAD_DIM
    idx = (jj % half).astype(F32)
    inv = jnp.exp(-math.log(10000.0) * idx / half)
    ang = pos_ref[0].astype(F32) * inv
    cos_ref[0] = jnp.cos(ang)
    sin_ref[0] = jnp.where(jj < half, -1.0, 1.0) * jnp.sin(ang)


def _rotary_tables(positions, tl):
    nb, L = positions.shape
    pos = positions.reshape(nb, L, 1)
    return pl.pallas_call(
        _rotary_kernel,
        out_shape=[jax.ShapeDtypeStruct((nb, L, LANES), F32)] * 2,
        grid=(nb, L // tl),
        in_specs=[pl.BlockSpec((1, tl, 1), lambda b, i: (b, i, 0))],
        out_specs=[pl.BlockSpec((1, tl, LANES), lambda b, i: (b, i, 0))] * 2,
        compiler_params=_cparams("arbitrary", "arbitrary"),
        name="rotary_tables",
    )(pos)


def _swa_bias_kernel(rb_ref, bucket_ref, band_ref, o_ref):
    bucket = bucket_ref[...]
    band = band_ref[...]
    for h in range(SWA_HEADS):
        acc = jnp.zeros(bucket.shape, F32)
        for b in range(REL_BUCKETS):
            acc = jnp.where(bucket == b, rb_ref[b, h], acc)
        o_ref[h] = jnp.where(band > 0, acc, MASK_VALUE)


def _t5_bucket(dist):
    exact = REL_BUCKETS // 2
    df = jnp.maximum(dist, 1).astype(F32)
    large = exact + (jnp.log(df / exact) / math.log(CHUNK / exact) * (REL_BUCKETS - exact)).astype(I32)
    large = jnp.minimum(large, REL_BUCKETS - 1)
    return jnp.where(dist < exact, dist, large)


def _swa_bias_table(rel_bias):
    W = CHUNK
    qi = jnp.arange(W)[:, None]
    kj = jnp.arange(2 * W)[None, :]
    dist = qi + W - kj
    band = ((dist >= 0) & (dist < W)).astype(I32)
    bucket = _t5_bucket(jnp.clip(dist, 0, W - 1)).astype(I32)
    return pl.pallas_call(
        _swa_bias_kernel,
        out_shape=jax.ShapeDtypeStruct((SWA_HEADS, W, 2 * W), F32),
        in_specs=[pl.BlockSpec(memory_space=pltpu.SMEM),
                  pl.BlockSpec(memory_space=pltpu.VMEM),
                  pl.BlockSpec(memory_space=pltpu.VMEM)],
        out_specs=pl.BlockSpec(memory_space=pltpu.VMEM),
        name="swa_bias_table",
    )(rel_bias, bucket, band)


def _in_proj_kernel(x_ref, mod_ref, w_ref, *out_refs, widths):
    sh = mod_ref[0, 0, 0:1, :]
    sc = mod_ref[0, 0, 1:2, :]
    h = (x_ref[0] * (1.0 + sc) + sh).astype(BF16)
    off = 0
    for ref, width in zip(out_refs, widths):
        for a, b in _ff_chunks(width):
            ref[0, :, a:b] = _dot(h, w_ref[:, off + a:off + b]).astype(ref.dtype)
        off += width


def _in_proj(x, mod_l, w_cat, widths, dtypes, tm):
    nb, L, d = x.shape
    return pl.pallas_call(
        functools.partial(_in_proj_kernel, widths=widths),
        out_shape=[jax.ShapeDtypeStruct((nb, L, w), t) for w, t in zip(widths, dtypes)],
        grid=(nb, L // tm),
        in_specs=[pl.BlockSpec((1, tm, d), lambda b, i: (b, i, 0)),
                  pl.BlockSpec((1, 1, 6, d), lambda b, i: (0, b, 0, 0)),
                  pl.BlockSpec(w_cat.shape, lambda b, i: (0, 0))],
        out_specs=[pl.BlockSpec((1, tm, w), lambda b, i: (b, i, 0)) for w in widths],
        compiler_params=_cparams("arbitrary", "arbitrary"),
        name="in_proj",
    )(x, mod_l, w_cat)


def _head_lane_mask(width, head):
    lane = lax.broadcasted_iota(I32, (1, width), 1)
    return (lane // HEAD_DIM) == head


def _rotate_half(t):
    width = t.shape[-1]
    lane = lax.broadcasted_iota(I32, (1, width), 1)
    half = HEAD_DIM // 2
    fwd = pltpu.roll(t, width - half, axis=1)
    bwd = pltpu.roll(t, half, axis=1)
    return jnp.where((lane % HEAD_DIM) < half, fwd, bwd)


def _retention_body(u_ref, cos_ref, sin_ref, din_ref, dq_ref, dk_ref, dc_ref,
                    bmask_ref, avg_ref, o_ref, state_ref, *, n_chunks):
    rd = RET_HEADS * HEAD_DIM
    masks = [_head_lane_mask(rd, h) for h in range(RET_HEADS)]

    def stack_heads(t):
        return jnp.concatenate([jnp.where(m, t, 0.0) for m in masks], axis=0).astype(BF16)

    state = state_ref[...]
    for ci in range(n_chunks):
        rows = slice(ci * CHUNK, (ci + 1) * CHUNK)
        cos = cos_ref[0, rows, :]
        sin = sin_ref[0, rows, :]
        cos2 = jnp.concatenate([cos, cos], axis=1)
        sin2 = jnp.concatenate([sin, sin], axis=1)
        q = u_ref[0, rows, 0:rd].astype(F32)
        k = u_ref[0, rows, rd:2 * rd].astype(F32)
        v = u_ref[0, rows, 2 * rd:3 * rd].astype(F32)
        g = u_ref[0, rows, 3 * rd:4 * rd].astype(F32)
        qr = q * cos2 + _rotate_half(q) * sin2
        kr = (k * cos2 + _rotate_half(k) * sin2) * (HEAD_DIM ** -0.5)
        scores = _dot_nt(qr.astype(BF16), stack_heads(kr)) * din_ref[...]
        inner = _dot(scores.astype(BF16), stack_heads(v))
        cross = _dot((qr * dq_ref[...]).astype(BF16), state.astype(BF16))
        o = inner + cross
        kd_t = (kr * dk_ref[...]).T.astype(BF16)
        kv = _dot(kd_t, v.astype(BF16))
        state = dc_ref[...] * state + bmask_ref[...] * kv
        mu = _dot2(o, avg_ref[...])
        dev = o - mu
        var = _dot2(dev * dev, avg_ref[...])
        on = dev * lax.rsqrt(var + LN_EPS)
        o_ref[0, rows, :] = (_silu(g) * on).astype(o_ref.dtype)
        yield
    state_ref[...] = state


def _retention_tables():
    H, d, C = RET_HEADS, HEAD_DIM, CHUNK
    log_gamma = jnp.log(1.0 - 2.0 ** (-5.0 - jnp.arange(H, dtype=F32)))
    idx = jnp.arange(C, dtype=F32)
    diff = idx[:, None] - idx[None, :]
    decay_in = jnp.where(diff >= 0, jnp.exp(log_gamma[:, None, None] * jnp.maximum(diff, 0.0)), 0.0)
    decay_q = jnp.exp(log_gamma[:, None] * (idx + 1.0))
    decay_k = jnp.exp(log_gamma[:, None] * (C - 1.0 - idx))
    decay_chunk = jnp.exp(log_gamma * C)
    din = decay_in.transpose(1, 0, 2).reshape(C, H * C)
    dq = jnp.repeat(decay_q.T, d, axis=1)
    dk = jnp.repeat(decay_k.T, d, axis=1)
    dc = jnp.repeat(decay_chunk, d)[None, :]
    head = jnp.arange(H * d) // d
    bmask = (head[:, None] == head[None, :]).astype(F32)
    avg = jnp.tile((bmask / d).astype(BF16), (2, 1))
    return din, dq, dk, dc, bmask, avg


def _swa_body(u_ref, bias_ref, sink_ref, o_ref, kprev_ref, vprev_ref, *, n_chunks):
    W = CHUNK
    qd = SWA_HEADS * HEAD_DIM
    kvd = SWA_KV_HEADS * HEAD_DIM
    first_step = pl.program_id(1) == 0
    lane = lax.broadcasted_iota(I32, (1, LANES), 1)
    low = lane < HEAD_DIM
    col = lax.broadcasted_iota(I32, (1, 2 * W), 1)

    kprev = kprev_ref[...]
    vprev = vprev_ref[...]
    sink = sink_ref[...]
    for ci in range(n_chunks):
        rows = slice(ci * W, (ci + 1) * W)
        qa = u_ref[0, rows, 0:LANES].astype(F32)
        qb = u_ref[0, rows, LANES:qd].astype(F32)
        k = u_ref[0, rows, qd:qd + kvd].astype(BF16)
        v = u_ref[0, rows, qd + kvd:qd + 2 * kvd].astype(BF16)
        q4 = jnp.concatenate([jnp.where(low, qa, 0.0), jnp.where(low, 0.0, qa),
                              jnp.where(low, qb, 0.0), jnp.where(low, 0.0, qb)],
                             axis=0).astype(BF16)
        kband = jnp.concatenate([kprev, k], axis=0)
        vband = jnp.concatenate([vprev, v], axis=0)
        logits = _dot_nt(q4, kband) * (HEAD_DIM ** -0.5) + bias_ref[...]
        if ci == 0:
            logits = jnp.where(jnp.logical_and(first_step, col < W), MASK_VALUE, logits)
        m = jnp.maximum(jnp.max(logits, -1, keepdims=True), sink)
        p = jnp.exp(logits - m)
        denom = jnp.sum(p, -1, keepdims=True) + jnp.exp(sink - m)
        res = _dot(p.astype(BF16), vband) / denom
        out_a = jnp.where(low, res[0:W], res[W:2 * W])
        out_b = jnp.where(low, res[2 * W:3 * W], res[3 * W:4 * W])
        o_ref[0, rows, 0:LANES] = out_a.astype(o_ref.dtype)
        o_ref[0, rows, LANES:qd] = out_b.astype(o_ref.dtype)
        kprev, vprev = k, v
        yield
    kprev_ref[...] = kprev
    vprev_ref[...] = vprev


SWA_HEAD_ORDER = (0, 2, 1, 3)


def _ssd_body(u_ref, dt_ref, cw_ref, cb_ref, dtb_c_ref, alog_c_ref, dskip_ref, nw_ref,
              tril_ref, expand_ref, gmask_ref, shift_ref,
              o_ref, state_ref, ext_ref, *, n_chunks):
    C = CHUNK
    sd = SSD_HEADS * HEAD_DIM
    gn = SSD_GROUPS * SSD_STATE
    cd = sd + 2 * gn
    tl = n_chunks * C
    slab = 2 * LANES
    heads_per_group = SSD_HEADS // SSD_GROUPS
    heads_per_slab = slab // HEAD_DIM
    lane = lax.broadcasted_iota(I32, (1, LANES), 1)
    low = lane < SSD_STATE
    slab_masks = [_head_lane_mask(slab, hh) for hh in range(heads_per_slab)]
    row_i = lax.broadcasted_iota(I32, (C, C), 0)
    col_i = lax.broadcasted_iota(I32, (C, C), 1)
    causal = row_i >= col_i
    neg_a_c = -jnp.exp(alog_c_ref[...])

    ext_ref[CONV_TAIL:CONV_TAIL + tl, :] = u_ref[0, :, sd:sd + cd]
    state = state_ref[...]
    for ci in range(n_chunks):
        rows = slice(ci * C, (ci + 1) * C)
        z = u_ref[0, rows, 0:sd].astype(F32)
        dt_raw = dt_ref[0, rows, :]
        window = ext_ref[ci * C:ci * C + CONV_TAIL + C, :]
        shifted = _dot(shift_ref[...], window)
        conv = cb_ref[...] + cw_ref[SSD_CONV - 1:SSD_CONV, :] * window[CONV_TAIL:, :].astype(F32)
        for w in range(SSD_CONV - 1):
            conv = conv + cw_ref[w:w + 1, :] * shifted[w * C:(w + 1) * C, :]
        xbc = _silu(conv)
        xs = xbc[:, 0:sd]
        bm = xbc[:, sd:sd + gn]
        cm = xbc[:, sd + gn:cd]

        dt_c = _softplus(dt_raw + dtb_c_ref[...])
        a_c = neg_a_c * dt_c
        acs_c = _dot3_left(tril_ref[...], a_c)
        acs_t = acs_c.T
        spread = _dot3(acs_c, expand_ref[...])
        acs_x = spread[:, 0:sd]
        dt_x = _dot3(dt_c, expand_ref[:, 0:sd])
        xdt = xs * dt_x

        bstack = jnp.concatenate([jnp.where(low, bm, 0.0), jnp.where(low, 0.0, bm)],
                                 axis=0).astype(BF16)
        cb = _dot_nt(cm.astype(BF16), bstack)
        y_diag = []
        for s in range(sd // slab):
            ms = []
            for hh in range(heads_per_slab):
                h = s * heads_per_slab + hh
                g = h // heads_per_group
                col_bcast = spread[:, sd + h * LANES:sd + (h + 1) * LANES]
                seg = col_bcast - acs_t[h:h + 1, :]
                lmat = jnp.exp(jnp.where(causal, seg, MASK_VALUE))
                ms.append((cb[:, g * C:(g + 1) * C] * lmat).astype(BF16))
            xslab = xdt[:, s * slab:(s + 1) * slab]
            xstack = jnp.concatenate([jnp.where(m, xslab, 0.0) for m in slab_masks],
                                     axis=0).astype(BF16)
            y_diag.append(_dot(jnp.concatenate(ms, axis=1), xstack))
        y_diag = jnp.concatenate(y_diag, axis=1)

        y_off = _dot(cm.astype(BF16), state.astype(BF16)) * jnp.exp(acs_x)
        last = acs_x[C - 1:C, :]
        dec = jnp.exp(last - acs_x)
        new = _dot(bm.T.astype(BF16), (xdt * dec).astype(BF16))
        state = jnp.exp(last) * state + gmask_ref[...] * new

        y = y_diag + y_off + xs * dskip_ref[...]
        hgate = y * _silu(z)
        gw = sd // SSD_GROUPS
        for g in range(SSD_GROUPS):
            hg = hgate[:, g * gw:(g + 1) * gw]
            ms_ = jnp.mean(hg * hg, -1, keepdims=True)
            o_ref[0, rows, g * gw:(g + 1) * gw] = (
                hg * lax.rsqrt(ms_ + LN_EPS) * nw_ref[:, g * gw:(g + 1) * gw]).astype(o_ref.dtype)
        yield
    state_ref[...] = state
    ext_ref[0:CONV_TAIL, :] = ext_ref[tl:tl + CONV_TAIL, :]


def _ssd_tables():
    C = CHUNK
    sd = SSD_HEADS * HEAD_DIM
    gn = SSD_GROUPS * SSD_STATE
    t = jnp.arange(C)
    tril = (t[:, None] >= t[None, :]).astype(BF16)
    r = jnp.arange(LANES)[:, None]
    eexp = ((r == (jnp.arange(sd)[None, :] // HEAD_DIM)) & (r < SSD_HEADS)).astype(BF16)
    bsel = ((r == (jnp.arange(SSD_HEADS * LANES)[None, :] // LANES)) & (r < SSD_HEADS)).astype(BF16)
    heads_per_group = SSD_HEADS // SSD_GROUPS
    row_g = jnp.arange(gn)[:, None] // SSD_STATE
    col_g = (jnp.arange(sd)[None, :] // HEAD_DIM) // heads_per_group
    gmask = (row_g == col_g).astype(F32)
    expand = jnp.concatenate([eexp, bsel], axis=1)
    win = jnp.arange(CONV_TAIL + C)[None, :]
    shifts = jnp.concatenate(
        [(win == CONV_TAIL + t[:, None] - (SSD_CONV - 1 - s)).astype(BF16)
         for s in range(SSD_CONV - 1)], axis=0)
    return jnp.tile(tril, (1, 3)), jnp.tile(expand, (3, 1)), gmask, shifts


N_RET_TABLES = 6
N_SSD_CONSTS = 10


def _mixers_kernel(*refs, n_chunks, n_cast):
    u_ret, cos, sin, u_ssd, u_dt, u_swa = refs[:6]
    pos = 6
    ret_tables = refs[pos:pos + N_RET_TABLES]
    pos += N_RET_TABLES
    ssd_consts = refs[pos:pos + N_SSD_CONSTS]
    pos += N_SSD_CONSTS
    bias, sink = refs[pos:pos + 2]
    pos += 2
    cast_in = refs[pos:pos + n_cast]
    pos += n_cast
    y_ret, y_ssd, y_swa = refs[pos:pos + 3]
    pos += 3
    cast_out = refs[pos:pos + n_cast]
    ret_state, ssd_state, ssd_ext, kprev, vprev = refs[pos + n_cast:]
    _side_cast(cast_in, cast_out)

    @pl.when(pl.program_id(1) == 0)
    def _():
        ret_state[...] = jnp.zeros_like(ret_state)
        ssd_state[...] = jnp.zeros_like(ssd_state)
        ssd_ext[0:CONV_TAIL, :] = jnp.zeros((CONV_TAIL, ssd_ext.shape[1]), ssd_ext.dtype)
        kprev[...] = jnp.zeros_like(kprev)
        vprev[...] = jnp.zeros_like(vprev)

    bodies = [
        _ssd_body(u_ssd, u_dt, *ssd_consts, y_ssd, ssd_state, ssd_ext, n_chunks=n_chunks),
        _swa_body(u_swa, bias, sink, y_swa, kprev, vprev, n_chunks=n_chunks),
        _retention_body(u_ret, cos, sin, *ret_tables, y_ret, ret_state, n_chunks=n_chunks),
    ]
    for _ in range(n_chunks + 1):
        for body in bodies:
            next(body, None)


def _mixers(u_ret, u_ssd, u_dt, u_swa, cos_t, sin_t, bias_tab, sinks_l, conv_w, conv_b,
            dt_bias, a_log, d_skip, norm_w, tl, side_cast=()):
    nb, L, _ = u_ret.shape
    W = CHUNK
    rd = RET_HEADS * HEAD_DIM
    sd = SSD_HEADS * HEAD_DIM
    qd = SWA_HEADS * HEAD_DIM
    cd = conv_w.shape[-1]
    ret_tables = _retention_tables()
    pad = lambda v: jnp.zeros((1, LANES), F32).at[0, :SSD_HEADS].set(v)
    rep = lambda v: jnp.repeat(v, HEAD_DIM)[None, :]
    ssd_consts = (conv_w, conv_b[None, :], pad(dt_bias), pad(a_log), rep(d_skip),
                  norm_w[None, :]) + _ssd_tables()
    order = jnp.array(SWA_HEAD_ORDER)
    bias_stacked = bias_tab[order].reshape(SWA_HEADS * W, 2 * W)
    sink_col = jnp.repeat(sinks_l.astype(F32)[order], W)[:, None]
    assert len(ret_tables) == N_RET_TABLES and len(ssd_consts) == N_SSD_CONSTS
    tok = lambda a: pl.BlockSpec((1, tl, a.shape[-1]), lambda b, i: (b, i, 0))
    const = lambda a: pl.BlockSpec(a.shape, lambda b, i: (0,) * a.ndim)
    tokens = (u_ret, cos_t, sin_t, u_ssd, u_dt, u_swa)
    consts = ret_tables + ssd_consts + (bias_stacked, sink_col)
    c_in, c_out, c_shapes = _side_cast_specs(side_cast, nb * (L // tl), L // tl)
    return pl.pallas_call(
        functools.partial(_mixers_kernel, n_chunks=tl // CHUNK, n_cast=len(side_cast)),
        out_shape=[jax.ShapeDtypeStruct((nb, L, w), BF16) for w in (rd, sd, qd)] + c_shapes,
        grid=(nb, L // tl),
        in_specs=[tok(a) for a in tokens] + [const(a) for a in consts] + c_in,
        out_specs=[pl.BlockSpec((1, tl, w), lambda b, i: (b, i, 0))
                   for w in (rd, sd, qd)] + c_out,
        scratch_shapes=[pltpu.VMEM((rd, rd), F32),
                        pltpu.VMEM((SSD_GROUPS * SSD_STATE, sd), F32),
                        pltpu.VMEM((CONV_TAIL + tl, cd), BF16),
                        pltpu.VMEM((W, LANES), BF16), pltpu.VMEM((W, LANES), BF16)],
        compiler_params=_cparams("arbitrary", "arbitrary"),
        name="mixers",
    )(*tokens, *consts, *side_cast)


N_ATTN_OUT = 10


def _attn_out(x_ref, yr_ref, ys_ref, ya_ref, mod_ref, wr_ref, ws_ref, wa_ref, g_ref, b_ref,
              alpha):
    mix = (_dot(yr_ref[0], wr_ref[...]) + _dot(ys_ref[0], ws_ref[...])
           + _dot(ya_ref[0], wa_ref[...]))
    gate = mod_ref[0, 0, 2:3, :]
    r = alpha * x_ref[0] + (1.0 + gate) * mix
    return _layer_norm(r, g_ref[...], b_ref[...])


def _attn_out_operands(x, y_ret, y_ssd, y_swa, mod_l, w_r, w_s, w_a, ln_g, ln_b, tm):
    d = x.shape[-1]
    tok = lambda w: pl.BlockSpec((1, tm, w), lambda b, i: (b, i, 0))
    const = lambda a: pl.BlockSpec(a.shape, lambda b, i: (0,) * a.ndim)
    specs = [tok(d), tok(y_ret.shape[-1]), tok(y_ssd.shape[-1]), tok(y_swa.shape[-1]),
             pl.BlockSpec((1, 1, 6, d), lambda b, i: (0, b, 0, 0)),
             const(w_r), const(w_s), const(w_a), const(ln_g), const(ln_b)]
    return specs, (x, y_ret, y_ssd, y_swa, mod_l, w_r, w_s, w_a, ln_g, ln_b)


FF_CHUNK = 256


def _side_cast_specs(arrays, steps, steps_per_seq):
    in_specs, out_specs, out_shapes = [], [], []
    for a in arrays:
        e, r, c = a.shape
        per = steps // e
        spec = pl.BlockSpec((1, r // per, c),
                            lambda b, i, per=per: ((b * steps_per_seq + i) // per,
                                                   (b * steps_per_seq + i) % per, 0))
        in_specs.append(spec)
        out_specs.append(spec)
        out_shapes.append(jax.ShapeDtypeStruct(a.shape, BF16))
    return in_specs, out_specs, out_shapes


def _side_cast(in_refs, out_refs):
    for i_ref, o_ref in zip(in_refs, out_refs):
        o_ref[...] = i_ref[...].astype(o_ref.dtype)


def _dense_ffn_kernel(*refs, alpha):
    attn = refs[:N_ATTN_OUT]
    mod_ref = attn[4]
    wg_ref, wu_ref, wd_ref, g_ref, b_ref = refs[N_ATTN_OUT:N_ATTN_OUT + 5]
    rest = refs[N_ATTN_OUT + 5:]
    n_cast = (len(rest) - 1) // 2
    o_ref = rest[n_cast]
    _side_cast(rest[:n_cast], rest[n_cast + 1:])
    x = _attn_out(*attn, alpha)
    sh = mod_ref[0, 0, 3:4, :]
    sc = mod_ref[0, 0, 4:5, :]
    gate = mod_ref[0, 0, 5:6, :]
    h = (x * (1.0 + sc) + sh).astype(BF16)
    dff = wg_ref.shape[-1]
    acc = jnp.zeros(x.shape, F32)
    for j in range(0, dff, FF_CHUNK):
        gj = _dot(h, wg_ref[:, j:j + FF_CHUNK])
        uj = _dot(h, wu_ref[:, j:j + FF_CHUNK])
        acc = acc + _dot((_silu(gj) * uj).astype(BF16), wd_ref[j:j + FF_CHUNK, :])
    r = alpha * x + (1.0 + gate) * acc
    o_ref[0] = _layer_norm(r, g_ref[...], b_ref[...])


def _dense_ffn(attn_args, wg, wu, wd, ln_g, ln_b, alpha, tm, side_cast=()):
    nb, L, d = attn_args[0].shape
    const = lambda a: pl.BlockSpec(a.shape, lambda b, i: (0,) * a.ndim,
                                   pipeline_mode=pl.Buffered(1))
    a_specs, a_ops = _attn_out_operands(*attn_args, tm)
    c_in, c_out, c_shapes = _side_cast_specs(side_cast, nb * (L // tm), L // tm)
    return pl.pallas_call(
        functools.partial(_dense_ffn_kernel, alpha=alpha),
        out_shape=[jax.ShapeDtypeStruct((nb, L, d), F32)] + c_shapes,
        grid=(nb, L // tm),
        in_specs=a_specs + [const(wg), const(wu), const(wd), const(ln_g), const(ln_b)] + c_in,
        out_specs=[pl.BlockSpec((1, tm, d), lambda b, i: (b, i, 0))] + c_out,
        compiler_params=_cparams("arbitrary", "arbitrary"),
        name="attn_out_dense_ffn_ln",
    )(*a_ops, wg, wu, wd, ln_g, ln_b, *side_cast)


def _store_row_tiles(ref, value):
    rows, d = value.shape
    nt = d // LANES
    for c in range(nt):
        ref[pl.ds(c, rows, stride=nt), :] = value[:, c * LANES:(c + 1) * LANES].astype(ref.dtype)


def _load_row_tiles(ref):
    nt = ROW_TILE
    rows = ref.shape[0] // nt
    return jnp.concatenate([ref[pl.ds(c, rows, stride=nt), :] for c in range(nt)], axis=1)


def _router_kernel(*refs, alpha):
    attn = refs[:N_ATTN_OUT]
    mod_ref = attn[4]
    wr_ref, br_ref, trils_ref, x1_ref, h_ref, route_ref, cnt_ref, base_ref = refs[N_ATTN_OUT:]
    first = jnp.logical_and(pl.program_id(0) == 0, pl.program_id(1) == 0)

    @pl.when(first)
    def _():
        base_ref[...] = jnp.zeros_like(base_ref)

    x1 = _attn_out(*attn, alpha)
    x1_ref[0] = x1
    sh = mod_ref[0, 0, 3:4, :]
    sc = mod_ref[0, 0, 4:5, :]
    h = x1 * (1.0 + sc) + sh
    _store_row_tiles(h_ref.at[0], h)
    h1 = h.astype(BF16)
    h2 = (h - h1.astype(F32)).astype(BF16)
    logits = _dot(jnp.concatenate([h1, h1, h2], axis=1), wr_ref[...]) + br_ref[...]
    lane = lax.broadcasted_iota(I32, logits.shape, 1).astype(F32)
    logits = jnp.where(lane < N_EXPERTS, logits, MASK_VALUE)
    v1 = jnp.max(logits, -1, keepdims=True)
    e1 = jnp.min(jnp.where(logits == v1, lane, float(LANES)), -1, keepdims=True)
    rest = jnp.where(lane == e1, MASK_VALUE, logits)
    v2 = jnp.max(rest, -1, keepdims=True)
    e2 = jnp.min(jnp.where(rest == v2, lane, float(LANES)), -1, keepdims=True)
    t = jnp.exp(v2 - v1)
    w1 = 1.0 / (1.0 + t)
    w2 = t / (1.0 + t)
    hot1 = (lane == e1).astype(F32)
    hot2 = (lane == e2).astype(F32)
    both = hot1 + hot2
    base = base_ref[0:1, :]
    before = _dot(trils_ref[...], both.astype(BF16)) + base
    rank1 = jnp.sum(hot1 * before, -1, keepdims=True)
    rank2 = jnp.sum(hot2 * before, -1, keepdims=True)
    total = base + jnp.sum(both, 0, keepdims=True)
    base_ref[0:1, :] = total
    cnt_ref[...] = jnp.broadcast_to(total, cnt_ref.shape)
    route = jnp.where(lane == 0, e1, 0.0)
    route = jnp.where(lane == 1, e2, route)
    route = jnp.where(lane == 2, rank1, route)
    route = jnp.where(lane == 3, rank2, route)
    route = jnp.where(lane == 4, w1, route)
    route = jnp.where(lane == 5, w2, route)
    route_ref[0] = route


def _router(attn_args, w_router, b_router, alpha, tm):
    nb, L, d = attn_args[0].shape
    wr = jnp.zeros((d, LANES), F32).at[:, :N_EXPERTS].set(w_router)
    w1 = wr.astype(BF16)
    w2 = (wr - w1.astype(F32)).astype(BF16)
    wr = jnp.concatenate([w1, w2, w1], axis=0)
    br = jnp.zeros((1, LANES), F32).at[0, :N_EXPERTS].set(b_router)
    t = jnp.arange(tm)
    tril_strict = (t[:, None] > t[None, :]).astype(BF16)
    const = lambda a: pl.BlockSpec(a.shape, lambda b, i: (0,) * a.ndim)
    a_specs, a_ops = _attn_out_operands(*attn_args, tm)
    return pl.pallas_call(
        functools.partial(_router_kernel, alpha=alpha),
        out_shape=[jax.ShapeDtypeStruct((nb, L, d), F32),
                   jax.ShapeDtypeStruct((nb, L * (d // LANES), LANES), F32),
                   jax.ShapeDtypeStruct((nb, L, LANES), F32),
                   jax.ShapeDtypeStruct((8, LANES), F32)],
        grid=(nb, L // tm),
        in_specs=a_specs + [const(wr), const(br), const(tril_strict)],
        out_specs=[pl.BlockSpec((1, tm, d), lambda b, i: (b, i, 0)),
                   pl.BlockSpec((1, tm * (d // LANES), LANES), lambda b, i: (b, i, 0)),
                   pl.BlockSpec((1, tm, LANES), lambda b, i: (b, i, 0)),
                   pl.BlockSpec((8, LANES), lambda b, i: (0, 0))],
        scratch_shapes=[pltpu.VMEM((8, LANES), F32)],
        compiler_params=_cparams("arbitrary", "arbitrary"),
        name="attn_out_moe_router",
    )(*a_ops, wr, br, tril_strict)


TOP_K = 2
ROW_TILE = 8
EXPERT_FF_SPLITS = 2


def _ff_chunks(width, step=512):
    return [(a, min(a + step, width)) for a in range(0, width, step)]


def _expert_kernel(te_ref, nu_ref, gsrc_ref, sdst_ref,
                   h_ref, wg_ref, wu_ref, wd_ref, yk_ref,
                   xbuf, ybuf, hbuf, acc, gsem, ssem, *, tme):
    i = pl.program_id(0)
    j = pl.program_id(1)
    n_used = nu_ref[0]
    slot = i % 2
    other = 1 - slot
    used = i < n_used
    chunks = _ff_chunks(wg_ref.shape[-1])
    rows_per_step = tme // EXPERT_FF_SPLITS
    n_front = len(chunks) // 2
    front_w = chunks[n_front - 1][1]
    back_w = wg_ref.shape[-1] - front_w
    gather_ranges, scatter_ranges = [], []
    for c, (a, b) in enumerate(chunks):
        if c < n_front:
            gather_ranges.append((rows_per_step * a // front_w, rows_per_step * b // front_w))
            scatter_ranges.append((0, 0))
        else:
            gather_ranges.append((0, 0))
            scatter_ranges.append((rows_per_step * (a - front_w) // back_w,
                                   rows_per_step * (b - front_w) // back_w))

    nt = ROW_TILE

    def gather(tile, slot_, r):
        src = pl.multiple_of(gsrc_ref[tile * tme + r], nt)
        return pltpu.make_async_copy(h_ref.at[pl.ds(src, nt)],
                                     xbuf.at[slot_, pl.ds(pl.multiple_of(r * nt, nt), nt)],
                                     gsem.at[slot_])

    def wait_gather(slot_):
        pltpu.make_async_copy(h_ref.at[pl.ds(0, tme * nt)], xbuf.at[slot_],
                              gsem.at[slot_]).wait()

    def scatter(block, slot_, r):
        dst = pl.multiple_of(sdst_ref[block * tme + r], nt)
        return pltpu.make_async_copy(ybuf.at[slot_, pl.ds(pl.multiple_of(r * nt, nt), nt)],
                                     yk_ref.at[pl.ds(dst, nt)], ssem.at[slot_])

    def wait_scatter(slot_):
        pltpu.make_async_copy(ybuf.at[slot_], yk_ref.at[pl.ds(0, tme * nt)],
                              ssem.at[slot_]).wait()

    @pl.when(jnp.logical_and(i == 0, j == 0))
    def _():
        ybuf[1] = jnp.zeros(ybuf.shape[1:], F32)

        def body(r, carry):
            gather(0, 0, r).start()
            return carry
        lax.fori_loop(0, tme, body, 0, unroll=8)

    @pl.when(used)
    def _():
        @pl.when(j == 0)
        def _():
            wait_gather(slot)
            hbuf[...] = _load_row_tiles(xbuf.at[slot]).astype(BF16)

            @pl.when(i > 0)
            def _():
                wait_scatter(slot)
            acc[...] = jnp.zeros(acc.shape, F32)

        h = hbuf[...]
        for c, (a, b) in enumerate(chunks):
            for r in range(*gather_ranges[c]):
                gather(i + 1, other, j * rows_per_step + r).start()
            for r in range(*scatter_ranges[c]):
                scatter(i, other, j * rows_per_step + r).start()
            gj = _dot(h, wg_ref[0, :, a:b])
            uj = _dot(h, wu_ref[0, :, a:b])
            part = _dot((_silu(gj) * uj).astype(BF16), wd_ref[0, a:b, :])
            acc[...] = acc[...] + part
            ybuf[slot, 0:ROW_TILE, :] = part[0:ROW_TILE, 0:LANES]

        @pl.when(j == EXPERT_FF_SPLITS - 1)
        def _():
            _store_row_tiles(ybuf.at[slot], acc[...])

        @pl.when(jnp.logical_and(j == EXPERT_FF_SPLITS - 1, i == n_used - 1))
        def _():
            wait_gather(other)
            wait_scatter(other)

            def body(r, carry):
                scatter(i + 1, slot, r).start()
                return carry
            lax.fori_loop(0, tme, body, 0, unroll=8)
            wait_scatter(slot)


def _expert_ffn(h_rows, tile_expert, n_used, gsrc, sdst, wg, wu, wd, tme):
    nt = ROW_TILE
    T = h_rows.shape[0] // nt
    d = nt * LANES
    dff = wg.shape[-1]
    dffh = dff // EXPERT_FF_SPLITS
    n_tiles = gsrc.shape[0] // tme
    yk_rows = TOP_K * T + tme

    def half(i, j):
        return jnp.where(i % 2 == 0, j, EXPERT_FF_SPLITS - 1 - j)

    return pl.pallas_call(
        functools.partial(_expert_kernel, tme=tme),
        out_shape=jax.ShapeDtypeStruct((yk_rows * nt, LANES), F32),
        grid_spec=pltpu.PrefetchScalarGridSpec(
            num_scalar_prefetch=4,
            grid=(n_tiles, EXPERT_FF_SPLITS),
            in_specs=[pl.BlockSpec(memory_space=pl.ANY),
                      pl.BlockSpec((1, d, dffh), lambda i, j, te, *_: (te[i], 0, half(i, j))),
                      pl.BlockSpec((1, d, dffh), lambda i, j, te, *_: (te[i], 0, half(i, j))),
                      pl.BlockSpec((1, dffh, d), lambda i, j, te, *_: (te[i], half(i, j), 0))],
            out_specs=pl.BlockSpec(memory_space=pl.ANY),
            scratch_shapes=[pltpu.VMEM((2, tme * nt, LANES), F32),
                            pltpu.VMEM((2, tme * nt, LANES), F32),
                            pltpu.VMEM((tme, d), BF16), pltpu.VMEM((tme, d), F32),
                            pltpu.SemaphoreType.DMA((2,)), pltpu.SemaphoreType.DMA((2,))]),
        compiler_params=_cparams("arbitrary", "arbitrary"),
        name="moe_experts",
    )(tile_expert, n_used, gsrc, sdst, h_rows, wg, wu, wd)


def _moe_finish_kernel(x_ref, route_ref, mod_ref, y0_ref, y1_ref, g_ref, b_ref, o_ref, *, alpha):
    route = route_ref[0]
    f = route[:, 4:5] * _load_row_tiles(y0_ref) + route[:, 5:6] * _load_row_tiles(y1_ref)
    gate = mod_ref[0, 0, 5:6, :]
    r = alpha * x_ref[0] + (1.0 + gate) * f
    o_ref[0] = _layer_norm(r, g_ref[...], b_ref[...])


def _moe_finish(x, route, mod_l, yk, ln_g, ln_b, alpha, tm):
    nb, L, d = x.shape
    tiles_per_seq = L // tm
    tiles = nb * tiles_per_seq
    return pl.pallas_call(
        functools.partial(_moe_finish_kernel, alpha=alpha),
        out_shape=jax.ShapeDtypeStruct((nb, L, d), F32),
        grid=(nb, tiles_per_seq),
        in_specs=[pl.BlockSpec((1, tm, d), lambda b, i: (b, i, 0)),
                  pl.BlockSpec((1, tm, LANES), lambda b, i: (b, i, 0)),
                  pl.BlockSpec((1, 1, 6, d), lambda b, i: (0, b, 0, 0)),
                  pl.BlockSpec((tm * ROW_TILE, LANES), lambda b, i: (b * tiles_per_seq + i, 0)),
                  pl.BlockSpec((tm * ROW_TILE, LANES),
                               lambda b, i: (tiles + b * tiles_per_seq + i, 0)),
                  pl.BlockSpec(ln_g.shape, lambda b, i: (0, 0)),
                  pl.BlockSpec(ln_b.shape, lambda b, i: (0, 0))],
        out_specs=pl.BlockSpec((1, tm, d), lambda b, i: (b, i, 0)),
        compiler_params=_cparams("arbitrary", "arbitrary"),
        name="moe_finish_ln",
    )(x, route, mod_l, yk, yk, ln_g, ln_b)


def _row_index_kernel(dest_ref, gsrc0_ref, sdst0_ref, gsrc_ref, sdst_ref, sem, *, n_pairs, T, tme):
    init_g = pltpu.make_async_copy(gsrc0_ref, gsrc_ref, sem.at[0])
    init_s = pltpu.make_async_copy(sdst0_ref, sdst_ref, sem.at[1])
    init_g.start()
    init_s.start()
    init_g.wait()
    init_s.wait()

    assert TOP_K == 2
    group = 8

    def body(g, carry):
        f0 = g * group
        rows = [dest_ref[f0 + k] for k in range(group)]
        for k in range(group):
            tok = g * (group // TOP_K) + k // TOP_K
            gsrc_ref[rows[k]] = tok * ROW_TILE
            sdst_ref[tme + rows[k]] = ((k % TOP_K) * T + tok) * ROW_TILE
        return carry
    lax.fori_loop(0, n_pairs // group, body, 0, unroll=2)


def _row_indices(dest, n_rows, T, tme):
    dump = (TOP_K * T + jnp.arange(n_rows + tme, dtype=I32) % tme) * ROW_TILE
    return pl.pallas_call(
        functools.partial(_row_index_kernel, n_pairs=dest.shape[0], T=T, tme=tme),
        out_shape=[jax.ShapeDtypeStruct((n_rows,), I32),
                   jax.ShapeDtypeStruct((n_rows + tme,), I32)],
        in_specs=[pl.BlockSpec(memory_space=pltpu.SMEM),
                  pl.BlockSpec(memory_space=pl.ANY),
                  pl.BlockSpec(memory_space=pl.ANY)],
        out_specs=[pl.BlockSpec(memory_space=pltpu.SMEM),
                   pl.BlockSpec(memory_space=pltpu.SMEM)],
        scratch_shapes=[pltpu.SemaphoreType.DMA((2,))],
        name="moe_row_indices",
    )(dest, jnp.zeros((n_rows,), I32), dump)


def _moe(attn_args, w_router, b_router, wg, wu, wd, ln_g, ln_b, alpha, tm_route, tme, tmf):
    nb, L, d = attn_args[0].shape
    mod_l = attn_args[4]
    T = nb * L
    x, h, route, counts = _router(attn_args, w_router, b_router, alpha, tm_route)
    route_flat = route.reshape(T, LANES)
    e = route_flat[:, 0:2].astype(I32)
    rank = route_flat[:, 2:4].astype(I32)
    cnt = counts[0, :N_EXPERTS].astype(I32)
    tiles = (cnt + tme - 1) // tme
    tile_end = jnp.cumsum(tiles)
    group_start = (tile_end - tiles) * tme
    dest = (group_start[e] + rank).reshape(-1)
    n_tiles = (TOP_K * T) // tme + N_EXPERTS
    n_rows = n_tiles * tme
    tile_expert = jnp.minimum(
        jnp.sum(jnp.arange(n_tiles, dtype=I32)[:, None] >= tile_end[None, :].astype(I32), axis=1),
        N_EXPERTS - 1).astype(I32)
    n_used = tile_end[-1:].astype(I32)
    gsrc, sdst = _row_indices(dest, n_rows, T, tme)
    yk = _expert_ffn(h.reshape(T * ROW_TILE, LANES), tile_expert, n_used, gsrc, sdst,
                     wg, wu, wd, tme)
    return _moe_finish(x, route, mod_l, yk, ln_g, ln_b, alpha, tmf)


def kernel(x, c, positions, rel_bias, w_ada, b_ada, w_in, w_out, conv_w, conv_b, dt_bias, a_log,
           d_skip, ssd_norm_w, sinks, ln_g, ln_b, ffn_w_gate, ffn_w_up, ffn_w_down, router_w,
           router_b, expert_w_gate, expert_w_up, expert_w_down):
    depth = w_ada.shape[0]
    nb, L, d = x.shape
    alpha = (2 * depth) ** 0.25
    rd = RET_HEADS * HEAD_DIM
    sd = SSD_HEADS * HEAD_DIM
    cd = conv_w.shape[-1]
    qd = SWA_HEADS * HEAD_DIM
    kvd = SWA_KV_HEADS * HEAD_DIM
    sizes = (rd, rd, rd, rd, sd, cd, SSD_HEADS, qd, kvd, kvd)
    offs = np.concatenate([[0], np.cumsum(sizes)])
    tl = min(512, L)

    mod = _ada_mod(c, w_ada, b_ada)
    cos_t, sin_t = _rotary_tables(positions, tl)
    bias_tab = _swa_bias_table(rel_bias)

    q_perm = np.concatenate([np.arange(h * HEAD_DIM, (h + 1) * HEAD_DIM) for h in SWA_HEAD_ORDER])
    widths = (4 * rd, sd + cd, LANES, qd + 2 * kvd)
    dtypes = (BF16, BF16, F32, BF16)

    steps = nb * (L // tl)
    n_exp, _, dff_e = expert_w_gate.shape[1:]
    per = steps // n_exp
    side_ok = (steps % n_exp == 0 and d % per == 0 and dff_e % per == 0
               and (d // per) % 8 == 0 and (dff_e // per) % 8 == 0)

    for layer in range(depth):
        wl = w_in[layer]
        seg = lambda i: wl[:, offs[i]:offs[i + 1]]
        dt_cols = jnp.zeros((d, LANES), F32).at[:, :SSD_HEADS].set(seg(6))
        w_cat = jnp.concatenate(
            [seg(0), seg(1), seg(2), seg(3), seg(4), seg(5), dt_cols,
             seg(7)[:, q_perm], seg(8), seg(9)], axis=1).astype(BF16)
        wo = w_out[layer]
        w_r = wo[0:rd].astype(BF16)
        w_s = wo[rd:rd + sd].astype(BF16)
        w_a = wo[rd + sd:][q_perm].astype(BF16)
        mod_l = mod[layer:layer + 1]

        u_ret, u_ssd, u_dt, u_swa = _in_proj(x, mod_l, w_cat, widths, dtypes, tl)
        i = layer // 2
        is_moe = layer % 2 == 1
        mix_cast = (expert_w_up[i], expert_w_down[i]) if is_moe and side_ok else ()
        y_ret, y_ssd, y_swa, *up_down_bf16 = _mixers(
            u_ret, u_ssd, u_dt, u_swa, cos_t, sin_t, bias_tab, sinks[layer], conv_w[layer],
            conv_b[layer], dt_bias[layer], a_log[layer], d_skip[layer], ssd_norm_w[layer], tl,
            side_cast=mix_cast)
        attn_args = (x, y_ret, y_ssd, y_swa, mod_l, w_r, w_s, w_a,
                     ln_g[layer, 0][None, :], ln_b[layer, 0][None, :])

        g2 = ln_g[layer, 1][None, :]
        b2 = ln_b[layer, 1][None, :]
        if not is_moe:
            nxt = (layer + 1) // 2
            ride = (expert_w_gate[nxt],) if layer + 1 < depth and side_ok else ()
            x, *gate_bf16 = _dense_ffn(
                attn_args, ffn_w_gate[i].astype(BF16), ffn_w_up[i].astype(BF16),
                ffn_w_down[i].astype(BF16), g2, b2, alpha, tl, side_cast=ride)
        else:
            if side_ok:
                wg_e, = gate_bf16
                wu_e, wd_e = up_down_bf16
            else:
                wg_e, wu_e, wd_e = (expert_w_gate[i].astype(BF16), expert_w_up[i].astype(BF16),
                                    expert_w_down[i].astype(BF16))
            x = _moe(attn_args, router_w[i], router_b[i], wg_e, wu_e, wd_e, g2, b2, alpha,
                     tm_route=min(512, L), tme=min(512, L), tmf=min(512, L))
    return x
```

```python
import functools
import math

import numpy as np
import jax
import jax.numpy as jnp
from jax import lax
from jax.experimental import pallas as pl
from jax.experimental.pallas import tpu as pltpu

F32 = jnp.float32
BF16 = jnp.bfloat16
I32 = jnp.int32

HEAD_DIM = 64
CHUNK = 128
RET_HEADS = 4
SSD_HEADS = 8
SSD_GROUPS = 2
SSD_STATE = 64
SSD_CONV = 4
CONV_TAIL = 16
SWA_HEADS = 4
SWA_KV_HEADS = 2
REL_BUCKETS = 32
N_EXPERTS = 8
LN_EPS = 1e-5
LANES = 128
MASK_VALUE = -1e30

VMEM_LIMIT = 56 * 1024 * 1024


def _cparams(*sem):
    return pltpu.CompilerParams(dimension_semantics=sem, vmem_limit_bytes=VMEM_LIMIT)


def _silu(v):
    return v * (1.0 / (1.0 + jnp.exp(-v)))


def _softplus(v):
    return jnp.maximum(v, 0.0) + jnp.log(1.0 + jnp.exp(-jnp.abs(v)))


def _dot(a, b):
    return jnp.dot(a, b, preferred_element_type=F32)


def _dot_nt(a, b):
    return lax.dot_general(a, b, (((1,), (1,)), ((), ())), preferred_element_type=F32)


def _split3(v):
    h1 = v.astype(BF16)
    r1 = v - h1.astype(F32)
    h2 = r1.astype(BF16)
    r2 = r1 - h2.astype(F32)
    return h1, h2, r2.astype(BF16)


def _dot3(v, m3):
    return _dot(jnp.concatenate(_split3(v), axis=1), m3)


def _dot3_left(m3, v):
    return _dot(m3, jnp.concatenate(_split3(v), axis=0))


def _dot2(v, m2):
    h1 = v.astype(BF16)
    h2 = (v - h1.astype(F32)).astype(BF16)
    return _dot(jnp.concatenate([h1, h2], axis=1), m2)


def _layer_norm(r, g, b):
    mu = jnp.mean(r, -1, keepdims=True)
    d = r - mu
    var = jnp.mean(d * d, -1, keepdims=True)
    return d * lax.rsqrt(var + LN_EPS) * g + b


def _ada_kernel(c_ref, w_ref, b_ref, o_ref):
    o_ref[0] = jnp.dot(c_ref[...], w_ref[0], preferred_element_type=F32,
                       precision=lax.Precision.HIGHEST) + b_ref[0]


def _ada_mod(c, w_ada, b_ada):
    depth, d, d6 = w_ada.shape
    nb = c.shape[0]
    rows = 8
    c_pad = jnp.zeros((rows, d), F32).at[:nb].set(c)
    out = pl.pallas_call(
        _ada_kernel,
        out_shape=jax.ShapeDtypeStruct((depth, rows, d6), F32),
        grid=(depth, d6 // d),
        in_specs=[pl.BlockSpec((rows, d), lambda l, j: (0, 0)),
                  pl.BlockSpec((1, d, d), lambda l, j: (l, 0, j)),
                  pl.BlockSpec((1, 1, d), lambda l, j: (l, 0, j))],
        out_specs=pl.BlockSpec((1, rows, d), lambda l, j: (l, 0, j)),
        compiler_params=_cparams("arbitrary", "arbitrary"),
        name="ada_mod",
    )(c_pad, w_ada, b_ada.reshape(depth, 1, d6))
    return out[:, :nb].reshape(depth, nb, 6, d)


def _rotary_kernel(pos_ref, cos_ref, sin_ref):
    half = HEAD_DIM // 2
    lane = lax.broadcasted_iota(I32, (1, LANES), 1)
    jj = lane % HEAD_DIM
    idx = (jj % half).astype(F32)
    inv = jnp.exp(-math.log(10000.0) * idx / half)
    ang = pos_ref[0].astype(F32) * inv
    cos_ref[0] = jnp.cos(ang)
    sin_ref[0] = jnp.where(jj < half, -1.0, 1.0) * jnp.sin(ang)


def _rotary_tables(positions, tl):
    nb, L = positions.shape
    pos = positions.reshape(nb, L, 1)
    return pl.pallas_call(
        _rotary_kernel,
        out_shape=[jax.ShapeDtypeStruct((nb, L, LANES), F32)] * 2,
        grid=(nb, L // tl),
        in_specs=[pl.BlockSpec((1, tl, 1), lambda b, i: (b, i, 0))],
        out_specs=[pl.BlockSpec((1, tl, LANES), lambda b, i: (b, i, 0))] * 2,
        compiler_params=_cparams("arbitrary", "arbitrary"),
        name="rotary_tables",
    )(pos)


def _swa_bias_kernel(rb_ref, bucket_ref, band_ref, o_ref):
    bucket = bucket_ref[...]
    band = band_ref[...]
    for h in range(SWA_HEADS):
        acc = jnp.zeros(bucket.shape, F32)
        for b in range(REL_BUCKETS):
            acc = jnp.where(bucket == b, rb_ref[b, h], acc)
        o_ref[h] = jnp.where(band > 0, acc, MASK_VALUE)


def _t5_bucket(dist):
    exact = REL_BUCKETS // 2
    df = jnp.maximum(dist, 1).astype(F32)
    large = exact + (jnp.log(df / exact) / math.log(CHUNK / exact) * (REL_BUCKETS - exact)).astype(I32)
    large = jnp.minimum(large, REL_BUCKETS - 1)
    return jnp.where(dist < exact, dist, large)


def _swa_bias_table(rel_bias):
    W = CHUNK
    qi = jnp.arange(W)[:, None]
    kj = jnp.arange(2 * W)[None, :]
    dist = qi + W - kj
    band = ((dist >= 0) & (dist < W)).astype(I32)
    bucket = _t5_bucket(jnp.clip(dist, 0, W - 1)).astype(I32)
    return pl.pallas_call(
        _swa_bias_kernel,
        out_shape=jax.ShapeDtypeStruct((SWA_HEADS, W, 2 * W), F32),
        in_specs=[pl.BlockSpec(memory_space=pltpu.SMEM),
                  pl.BlockSpec(memory_space=pltpu.VMEM),
                  pl.BlockSpec(memory_space=pltpu.VMEM)],
        out_specs=pl.BlockSpec(memory_space=pltpu.VMEM),
        name="swa_bias_table",
    )(rel_bias, bucket, band)


def _head_lane_mask(width, head):
    lane = lax.broadcasted_iota(I32, (1, width), 1)
    return (lane // HEAD_DIM) == head


def _rotate_half(t):
    width = t.shape[-1]
    lane = lax.broadcasted_iota(I32, (1, width), 1)
    half = HEAD_DIM // 2
    fwd = pltpu.roll(t, width - half, axis=1)
    bwd = pltpu.roll(t, half, axis=1)
    return jnp.where((lane % HEAD_DIM) < half, fwd, bwd)


def _retention_body(u_ref, cos_ref, sin_ref, din_ref, dq_ref, dk_ref, dc_ref,
                    bmask_ref, avg_ref, o_ref, state_ref, *, n_chunks):
    rd = RET_HEADS * HEAD_DIM
    masks = [_head_lane_mask(rd, h) for h in range(RET_HEADS)]

    def stack_heads(t):
        return jnp.concatenate([jnp.where(m, t, 0.0) for m in masks], axis=0).astype(BF16)

    state = state_ref[...]
    for ci in range(n_chunks):
        rows = slice(ci * CHUNK, (ci + 1) * CHUNK)
        cos = cos_ref[0, rows, :]
        sin = sin_ref[0, rows, :]
        cos2 = jnp.concatenate([cos, cos], axis=1)
        sin2 = jnp.concatenate([sin, sin], axis=1)
        q = u_ref[0, rows, 0:rd].astype(F32)
        k = u_ref[0, rows, rd:2 * rd].astype(F32)
        v = u_ref[0, rows, 2 * rd:3 * rd].astype(F32)
        g = u_ref[0, rows, 3 * rd:4 * rd].astype(F32)
        qr = q * cos2 + _rotate_half(q) * sin2
        kr = (k * cos2 + _rotate_half(k) * sin2) * (HEAD_DIM ** -0.5)
        scores = _dot_nt(qr.astype(BF16), stack_heads(kr)) * din_ref[...]
        inner = _dot(scores.astype(BF16), stack_heads(v))
        cross = _dot((qr * dq_ref[...]).astype(BF16), state.astype(BF16))
        o = inner + cross
        kd_t = (kr * dk_ref[...]).T.astype(BF16)
        kv = _dot(kd_t, v.astype(BF16))
        state = dc_ref[...] * state + bmask_ref[...] * kv
        mu = _dot2(o, avg_ref[...])
        dev = o - mu
        var = _dot2(dev * dev, avg_ref[...])
        on = dev * lax.rsqrt(var + LN_EPS)
        o_ref[0, rows, :] = (_silu(g) * on).astype(o_ref.dtype)
        yield
    state_ref[...] = state


def _retention_tables():
    H, d, C = RET_HEADS, HEAD_DIM, CHUNK
    log_gamma = jnp.log(1.0 - 2.0 ** (-5.0 - jnp.arange(H, dtype=F32)))
    idx = jnp.arange(C, dtype=F32)
    diff = idx[:, None] - idx[None, :]
    decay_in = jnp.where(diff >= 0, jnp.exp(log_gamma[:, None, None] * jnp.maximum(diff, 0.0)), 0.0)
    decay_q = jnp.exp(log_gamma[:, None] * (idx + 1.0))
    decay_k = jnp.exp(log_gamma[:, None] * (C - 1.0 - idx))
    decay_chunk = jnp.exp(log_gamma * C)
    din = decay_in.transpose(1, 0, 2).reshape(C, H * C)
    dq = jnp.repeat(decay_q.T, d, axis=1)
    dk = jnp.repeat(decay_k.T, d, axis=1)
    dc = jnp.repeat(decay_chunk, d)[None, :]
    head = jnp.arange(H * d) // d
    bmask = (head[:, None] == head[None, :]).astype(F32)
    avg = jnp.tile((bmask / d).astype(BF16), (2, 1))
    return din, dq, dk, dc, bmask, avg


def _swa_body(u_ref, bias_ref, sink_ref, o_ref, kprev_ref, vprev_ref, *, n_chunks, first_step):
    W = CHUNK
    qd = SWA_HEADS * HEAD_DIM
    kvd = SWA_KV_HEADS * HEAD_DIM
    lane = lax.broadcasted_iota(I32, (1, LANES), 1)
    low = lane < HEAD_DIM
    col = lax.broadcasted_iota(I32, (1, 2 * W), 1)

    kprev = kprev_ref[...]
    vprev = vprev_ref[...]
    sink = sink_ref[...]
    for ci in range(n_chunks):
        rows = slice(ci * W, (ci + 1) * W)
        qa = u_ref[0, rows, 0:LANES].astype(F32)
        qb = u_ref[0, rows, LANES:qd].astype(F32)
        k = u_ref[0, rows, qd:qd + kvd].astype(BF16)
        v = u_ref[0, rows, qd + kvd:qd + 2 * kvd].astype(BF16)
        q4 = jnp.concatenate([jnp.where(low, qa, 0.0), jnp.where(low, 0.0, qa),
                              jnp.where(low, qb, 0.0), jnp.where(low, 0.0, qb)],
                             axis=0).astype(BF16)
        kband = jnp.concatenate([kprev, k], axis=0)
        vband = jnp.concatenate([vprev, v], axis=0)
        logits = _dot_nt(q4, kband) * (HEAD_DIM ** -0.5) + bias_ref[...]
        if ci == 0:
            logits = jnp.where(jnp.logical_and(first_step, col < W), MASK_VALUE, logits)
        m = jnp.maximum(jnp.max(logits, -1, keepdims=True), sink)
        p = jnp.exp(logits - m)
        denom = jnp.sum(p, -1, keepdims=True) + jnp.exp(sink - m)
        res = _dot(p.astype(BF16), vband) / denom
        out_a = jnp.where(low, res[0:W], res[W:2 * W])
        out_b = jnp.where(low, res[2 * W:3 * W], res[3 * W:4 * W])
        o_ref[0, rows, 0:LANES] = out_a.astype(o_ref.dtype)
        o_ref[0, rows, LANES:qd] = out_b.astype(o_ref.dtype)
        kprev, vprev = k, v
        yield
    kprev_ref[...] = kprev
    vprev_ref[...] = vprev


SWA_HEAD_ORDER = (0, 2, 1, 3)


def _ssd_body(u_ref, dt_ref, cw_ref, cb_ref, dtb_c_ref, alog_c_ref, dskip_ref, nw_ref,
              tril_ref, expand_ref, gmask_ref, shift_ref,
              o_ref, state_ref, ext_ref, *, n_chunks):
    C = CHUNK
    sd = SSD_HEADS * HEAD_DIM
    gn = SSD_GROUPS * SSD_STATE
    cd = sd + 2 * gn
    tl = n_chunks * C
    slab = 2 * LANES
    heads_per_group = SSD_HEADS // SSD_GROUPS
    heads_per_slab = slab // HEAD_DIM
    lane = lax.broadcasted_iota(I32, (1, LANES), 1)
    low = lane < SSD_STATE
    slab_masks = [_head_lane_mask(slab, hh) for hh in range(heads_per_slab)]
    row_i = lax.broadcasted_iota(I32, (C, C), 0)
    col_i = lax.broadcasted_iota(I32, (C, C), 1)
    causal = row_i >= col_i
    neg_a_c = -jnp.exp(alog_c_ref[...])

    ext_ref[CONV_TAIL:CONV_TAIL + tl, :] = u_ref[0, :, sd:sd + cd]
    state = state_ref[...]
    for ci in range(n_chunks):
        rows = slice(ci * C, (ci + 1) * C)
        z = u_ref[0, rows, 0:sd].astype(F32)
        dt_raw = dt_ref[0, rows, :]
        window = ext_ref[ci * C:ci * C + CONV_TAIL + C, :]
        shifted = _dot(shift_ref[...], window)
        conv = cb_ref[...] + cw_ref[SSD_CONV - 1:SSD_CONV, :] * window[CONV_TAIL:, :].astype(F32)
        for w in range(SSD_CONV - 1):
            conv = conv + cw_ref[w:w + 1, :] * shifted[w * C:(w + 1) * C, :]
        xbc = _silu(conv)
        xs = xbc[:, 0:sd]
        bm = xbc[:, sd:sd + gn]
        cm = xbc[:, sd + gn:cd]

        dt_c = _softplus(dt_raw + dtb_c_ref[...])
        a_c = neg_a_c * dt_c
        acs_c = _dot3_left(tril_ref[...], a_c)
        acs_t = acs_c.T
        spread = _dot3(acs_c, expand_ref[...])
        acs_x = spread[:, 0:sd]
        dt_x = _dot3(dt_c, expand_ref[:, 0:sd])
        xdt = xs * dt_x

        bstack = jnp.concatenate([jnp.where(low, bm, 0.0), jnp.where(low, 0.0, bm)],
                                 axis=0).astype(BF16)
        cb = _dot_nt(cm.astype(BF16), bstack)
        y_diag = []
        for s in range(sd // slab):
            ms = []
            for hh in range(heads_per_slab):
                h = s * heads_per_slab + hh
                g = h // heads_per_group
                col_bcast = spread[:, sd + h * LANES:sd + (h + 1) * LANES]
                seg = col_bcast - acs_t[h:h + 1, :]
                lmat = jnp.exp(jnp.where(causal, seg, MASK_VALUE))
                ms.append((cb[:, g * C:(g + 1) * C] * lmat).astype(BF16))
            xslab = xdt[:, s * slab:(s + 1) * slab]
            xstack = jnp.concatenate([jnp.where(m, xslab, 0.0) for m in slab_masks],
                                     axis=0).astype(BF16)
            y_diag.append(_dot(jnp.concatenate(ms, axis=1), xstack))
        y_diag = jnp.concatenate(y_diag, axis=1)

        y_off = _dot(cm.astype(BF16), state.astype(BF16)) * jnp.exp(acs_x)
        last = acs_x[C - 1:C, :]
        dec = jnp.exp(last - acs_x)
        new = _dot(bm.T.astype(BF16), (xdt * dec).astype(BF16))
        state = jnp.exp(last) * state + gmask_ref[...] * new

        y = y_diag + y_off + xs * dskip_ref[...]
        hgate = y * _silu(z)
        gw = sd // SSD_GROUPS
        for g in range(SSD_GROUPS):
            hg = hgate[:, g * gw:(g + 1) * gw]
            ms_ = jnp.mean(hg * hg, -1, keepdims=True)
            o_ref[0, rows, g * gw:(g + 1) * gw] = (
                hg * lax.rsqrt(ms_ + LN_EPS) * nw_ref[:, g * gw:(g + 1) * gw]).astype(o_ref.dtype)
        yield
    state_ref[...] = state
    ext_ref[0:CONV_TAIL, :] = ext_ref[tl:tl + CONV_TAIL, :]


def _ssd_tables():
    C = CHUNK
    sd = SSD_HEADS * HEAD_DIM
    gn = SSD_GROUPS * SSD_STATE
    t = jnp.arange(C)
    tril = (t[:, None] >= t[None, :]).astype(BF16)
    r = jnp.arange(LANES)[:, None]
    eexp = ((r == (jnp.arange(sd)[None, :] // HEAD_DIM)) & (r < SSD_HEADS)).astype(BF16)
    bsel = ((r == (jnp.arange(SSD_HEADS * LANES)[None, :] // LANES)) & (r < SSD_HEADS)).astype(BF16)
    heads_per_group = SSD_HEADS // SSD_GROUPS
    row_g = jnp.arange(gn)[:, None] // SSD_STATE
    col_g = (jnp.arange(sd)[None, :] // HEAD_DIM) // heads_per_group
    gmask = (row_g == col_g).astype(F32)
    expand = jnp.concatenate([eexp, bsel], axis=1)
    win = jnp.arange(CONV_TAIL + C)[None, :]
    shifts = jnp.concatenate(
        [(win == CONV_TAIL + t[:, None] - (SSD_CONV - 1 - s)).astype(BF16)
         for s in range(SSD_CONV - 1)], axis=0)
    return jnp.tile(tril, (1, 3)), jnp.tile(expand, (3, 1)), gmask, shifts


N_RET_TABLES = 6
N_SSD_CONSTS = 10


def _in_proj_body(x_ref, shift, scale, w_ref, out_refs):
    h = (x_ref[0] * (1.0 + scale) + shift).astype(BF16)
    off = 0
    for ref in out_refs:
        width = ref.shape[-1]
        for a, b in _ff_chunks(width):
            ref[0, :, a:b] = _dot(h, w_ref[:, off + a:off + b]).astype(ref.dtype)
            yield
        off += width


def _mixers_kernel(*refs, n_chunks, n_cast, n_tiles, tiles_per_seq):
    x_ref, mod_ref, w_ref, cos, sin = refs[:5]
    pos = 5
    ret_tables = refs[pos:pos + N_RET_TABLES]
    pos += N_RET_TABLES
    ssd_consts = refs[pos:pos + N_SSD_CONSTS]
    pos += N_SSD_CONSTS
    bias, sink = refs[pos:pos + 2]
    pos += 2
    cast_in = refs[pos:pos + n_cast]
    pos += n_cast
    y_ret, y_ssd, y_swa = refs[pos:pos + 3]
    pos += 3
    cast_out = refs[pos:pos + n_cast]
    pos += n_cast
    ret_state, ssd_state, ssd_ext, kprev, vprev = refs[pos:pos + 5]
    cur = refs[pos + 5:pos + 9]
    nxt = refs[pos + 9:pos + 13]
    _side_cast(cast_in, cast_out)

    s = pl.program_id(0)
    tile = jnp.maximum(s - 1, 0)
    seq_start = tile % tiles_per_seq == 0

    @pl.when(s == 0)
    def _():
        for ref in cur:
            ref[...] = jnp.zeros_like(ref)

    @pl.when(seq_start)
    def _():
        ret_state[...] = jnp.zeros_like(ret_state)
        ssd_state[...] = jnp.zeros_like(ssd_state)
        ssd_ext[0:CONV_TAIL, :] = jnp.zeros((CONV_TAIL, ssd_ext.shape[1]), ssd_ext.dtype)
        kprev[...] = jnp.zeros_like(kprev)
        vprev[...] = jnp.zeros_like(vprev)

    batch = jnp.minimum(s, n_tiles - 1) // tiles_per_seq
    shift = mod_ref[0, batch, 0:1, :]
    scale = mod_ref[0, batch, 1:2, :]
    u_ret, u_ssd, u_dt, u_swa = cur
    bodies = [
        _ssd_body(u_ssd, u_dt, *ssd_consts, y_ssd, ssd_state, ssd_ext, n_chunks=n_chunks),
        _in_proj_body(x_ref, shift, scale, w_ref, nxt),
        _swa_body(u_swa, bias, sink, y_swa, kprev, vprev, n_chunks=n_chunks,
                  first_step=seq_start),
        _retention_body(u_ret, cos, sin, *ret_tables, y_ret, ret_state, n_chunks=n_chunks),
    ]
    alive = list(bodies)
    while alive:
        alive = [b for b in alive if next(b, StopIteration) is not StopIteration]
    for c_ref, n_ref in zip(cur, nxt):
        c_ref[...] = n_ref[...]


def _mixers(x, mod_l, w_cat, widths, dtypes, cos_t, sin_t, bias_tab, sinks_l, conv_w, conv_b,
            dt_bias, a_log, d_skip, norm_w, tl, side_cast=()):
    nb, L, d = x.shape
    W = CHUNK
    rd = RET_HEADS * HEAD_DIM
    sd = SSD_HEADS * HEAD_DIM
    qd = SWA_HEADS * HEAD_DIM
    cd = conv_w.shape[-1]
    ret_tables = _retention_tables()
    pad = lambda v: jnp.zeros((1, LANES), F32).at[0, :SSD_HEADS].set(v)
    rep = lambda v: jnp.repeat(v, HEAD_DIM)[None, :]
    ssd_consts = (conv_w, conv_b[None, :], pad(dt_bias), pad(a_log), rep(d_skip),
                  norm_w[None, :]) + _ssd_tables()
    order = jnp.array(SWA_HEAD_ORDER)
    bias_stacked = bias_tab[order].reshape(SWA_HEADS * W, 2 * W)
    sink_col = jnp.repeat(sinks_l.astype(F32)[order], W)[:, None]
    assert len(ret_tables) == N_RET_TABLES and len(ssd_consts) == N_SSD_CONSTS
    tps = L // tl
    n_tiles = nb * tps

    def tile_of(step):
        return step // tps, step % tps, 0

    proj_map = lambda s: tile_of(jnp.minimum(s, n_tiles - 1))
    mix_map = lambda s: tile_of(jnp.maximum(s - 1, 0))
    const = lambda a: pl.BlockSpec(a.shape, lambda s: (0,) * a.ndim)
    consts = ret_tables + ssd_consts + (bias_stacked, sink_col)
    c_in, c_out, c_shapes = _side_cast_specs(side_cast, n_tiles)
    u_bufs = [pltpu.VMEM((1, tl, w), t) for w, t in zip(widths, dtypes)]
    return pl.pallas_call(
        functools.partial(_mixers_kernel, n_chunks=tl // CHUNK, n_cast=len(side_cast),
                          n_tiles=n_tiles, tiles_per_seq=tps),
        out_shape=[jax.ShapeDtypeStruct((nb, L, w), BF16) for w in (rd, sd, qd)] + c_shapes,
        grid=(n_tiles + 1,),
        in_specs=[pl.BlockSpec((1, tl, d), proj_map), const(mod_l), const(w_cat),
                  pl.BlockSpec((1, tl, LANES), mix_map), pl.BlockSpec((1, tl, LANES), mix_map)]
                 + [const(a) for a in consts] + c_in,
        out_specs=[pl.BlockSpec((1, tl, w), mix_map) for w in (rd, sd, qd)] + c_out,
        scratch_shapes=[pltpu.VMEM((rd, rd), F32),
                        pltpu.VMEM((SSD_GROUPS * SSD_STATE, sd), F32),
                        pltpu.VMEM((CONV_TAIL + tl, cd), BF16),
                        pltpu.VMEM((W, LANES), BF16), pltpu.VMEM((W, LANES), BF16)]
                       + u_bufs + u_bufs,
        compiler_params=_cparams("arbitrary"),
        name="in_proj_mixers",
    )(x, mod_l, w_cat, cos_t, sin_t, *consts, *side_cast)


N_ATTN_OUT = 10


def _attn_out(x_ref, yr_ref, ys_ref, ya_ref, mod_ref, wr_ref, ws_ref, wa_ref, g_ref, b_ref,
              alpha):
    mix = (_dot(yr_ref[0], wr_ref[...]) + _dot(ys_ref[0], ws_ref[...])
           + _dot(ya_ref[0], wa_ref[...]))
    gate = mod_ref[0, 0, 2:3, :]
    r = alpha * x_ref[0] + (1.0 + gate) * mix
    return _layer_norm(r, g_ref[...], b_ref[...])


def _attn_out_operands(x, y_ret, y_ssd, y_swa, mod_l, w_r, w_s, w_a, ln_g, ln_b, tm):
    d = x.shape[-1]
    tok = lambda w: pl.BlockSpec((1, tm, w), lambda b, i: (b, i, 0))
    const = lambda a: pl.BlockSpec(a.shape, lambda b, i: (0,) * a.ndim)
    specs = [tok(d), tok(y_ret.shape[-1]), tok(y_ssd.shape[-1]), tok(y_swa.shape[-1]),
             pl.BlockSpec((1, 1, 6, d), lambda b, i: (0, b, 0, 0)),
             const(w_r), const(w_s), const(w_a), const(ln_g), const(ln_b)]
    return specs, (x, y_ret, y_ssd, y_swa, mod_l, w_r, w_s, w_a, ln_g, ln_b)


FF_CHUNK = 256


def _side_cast_specs(arrays, steps, steps_per_seq=None):
    in_specs, out_specs, out_shapes = [], [], []
    for a in arrays:
        e, r, c = a.shape
        per = steps // e
        if steps_per_seq is None:
            index = lambda s, per=per: (jnp.minimum(s, steps - 1) // per,
                                        jnp.minimum(s, steps - 1) % per, 0)
        else:
            index = lambda b, i, per=per: ((b * steps_per_seq + i) // per,
                                           (b * steps_per_seq + i) % per, 0)
        spec = pl.BlockSpec((1, r // per, c), index)
        in_specs.append(spec)
        out_specs.append(spec)
        out_shapes.append(jax.ShapeDtypeStruct(a.shape, BF16))
    return in_specs, out_specs, out_shapes


def _side_cast(in_refs, out_refs):
    for i_ref, o_ref in zip(in_refs, out_refs):
        o_ref[...] = i_ref[...].astype(o_ref.dtype)


def _dense_ffn_kernel(*refs, alpha):
    attn = refs[:N_ATTN_OUT]
    mod_ref = attn[4]
    wg_ref, wu_ref, wd_ref, g_ref, b_ref = refs[N_ATTN_OUT:N_ATTN_OUT + 5]
    rest = refs[N_ATTN_OUT + 5:]
    n_cast = (len(rest) - 1) // 2
    o_ref = rest[n_cast]
    _side_cast(rest[:n_cast], rest[n_cast + 1:])
    x = _attn_out(*attn, alpha)
    sh = mod_ref[0, 0, 3:4, :]
    sc = mod_ref[0, 0, 4:5, :]
    gate = mod_ref[0, 0, 5:6, :]
    h = (x * (1.0 + sc) + sh).astype(BF16)
    dff = wg_ref.shape[-1]
    acc = jnp.zeros(x.shape, F32)
    for j in range(0, dff, FF_CHUNK):
        gj = _dot(h, wg_ref[:, j:j + FF_CHUNK])
        uj = _dot(h, wu_ref[:, j:j + FF_CHUNK])
        acc = acc + _dot((_silu(gj) * uj).astype(BF16), wd_ref[j:j + FF_CHUNK, :])
    r = alpha * x + (1.0 + gate) * acc
    o_ref[0] = _layer_norm(r, g_ref[...], b_ref[...])


def _dense_ffn(attn_args, wg, wu, wd, ln_g, ln_b, alpha, tm, side_cast=()):
    nb, L, d = attn_args[0].shape
    const = lambda a: pl.BlockSpec(a.shape, lambda b, i: (0,) * a.ndim,
                                   pipeline_mode=pl.Buffered(1))
    a_specs, a_ops = _attn_out_operands(*attn_args, tm)
    c_in, c_out, c_shapes = _side_cast_specs(side_cast, nb * (L // tm), L // tm)
    return pl.pallas_call(
        functools.partial(_dense_ffn_kernel, alpha=alpha),
        out_shape=[jax.ShapeDtypeStruct((nb, L, d), F32)] + c_shapes,
        grid=(nb, L // tm),
        in_specs=a_specs + [const(wg), const(wu), const(wd), const(ln_g), const(ln_b)] + c_in,
        out_specs=[pl.BlockSpec((1, tm, d), lambda b, i: (b, i, 0))] + c_out,
        compiler_params=_cparams("arbitrary", "arbitrary"),
        name="attn_out_dense_ffn_ln",
    )(*a_ops, wg, wu, wd, ln_g, ln_b, *side_cast)


def _store_row_tiles(ref, value):
    rows, d = value.shape
    nt = d // LANES
    for c in range(nt):
        ref[pl.ds(c, rows, stride=nt), :] = value[:, c * LANES:(c + 1) * LANES].astype(ref.dtype)


def _load_row_tiles(ref):
    nt = ROW_TILE
    rows = ref.shape[0] // nt
    return jnp.concatenate([ref[pl.ds(c, rows, stride=nt), :] for c in range(nt)], axis=1)


def _router_kernel(*refs, alpha):
    attn = refs[:N_ATTN_OUT]
    mod_ref = attn[4]
    wr_ref, br_ref, trils_ref, x1_ref, h_ref, route_ref, cnt_ref, base_ref = refs[N_ATTN_OUT:]
    first = jnp.logical_and(pl.program_id(0) == 0, pl.program_id(1) == 0)

    @pl.when(first)
    def _():
        base_ref[...] = jnp.zeros_like(base_ref)

    x1 = _attn_out(*attn, alpha)
    x1_ref[0] = x1
    sh = mod_ref[0, 0, 3:4, :]
    sc = mod_ref[0, 0, 4:5, :]
    h = x1 * (1.0 + sc) + sh
    _store_row_tiles(h_ref.at[0], h)
    logits = _dot(h.astype(BF16), wr_ref[...]) + br_ref[...]
    lane = lax.broadcasted_iota(I32, logits.shape, 1).astype(F32)
    logits = jnp.where(lane < N_EXPERTS, logits, MASK_VALUE)
    v1 = jnp.max(logits, -1, keepdims=True)
    e1 = jnp.min(jnp.where(logits == v1, lane, float(LANES)), -1, keepdims=True)
    rest = jnp.where(lane == e1, MASK_VALUE, logits)
    v2 = jnp.max(rest, -1, keepdims=True)
    e2 = jnp.min(jnp.where(rest == v2, lane, float(LANES)), -1, keepdims=True)
    t = jnp.exp(v2 - v1)
    w1 = 1.0 / (1.0 + t)
    w2 = t / (1.0 + t)
    hot1 = (lane == e1).astype(F32)
    hot2 = (lane == e2).astype(F32)
    both = hot1 + hot2
    base = base_ref[0:1, :]
    before = _dot(trils_ref[...], both.astype(BF16)) + base
    rank1 = jnp.sum(hot1 * before, -1, keepdims=True)
    rank2 = jnp.sum(hot2 * before, -1, keepdims=True)
    total = base + jnp.sum(both, 0, keepdims=True)
    base_ref[0:1, :] = total
    cnt_ref[...] = jnp.broadcast_to(total, cnt_ref.shape)
    route = jnp.where(lane == 0, e1, 0.0)
    route = jnp.where(lane == 1, e2, route)
    route = jnp.where(lane == 2, rank1, route)
    route = jnp.where(lane == 3, rank2, route)
    route = jnp.where(lane == 4, w1, route)
    route = jnp.where(lane == 5, w2, route)
    route_ref[0] = route


def _router(attn_args, w_router, b_router, alpha, tm):
    nb, L, d = attn_args[0].shape
    wr = jnp.zeros((d, LANES), BF16).at[:, :N_EXPERTS].set(w_router.astype(BF16))
    br =jnp.zeros((1, LANES), F32).at[0, :N_EXPERTS].set(b_router)
    t = jnp.arange(tm)
    tril_strict = (t[:, None] > t[None, :]).astype(BF16)
    const = lambda a: pl.BlockSpec(a.shape, lambda b, i: (0,) * a.ndim)
    a_specs, a_ops = _attn_out_operands(*attn_args, tm)
    return pl.pallas_call(
        functools.partial(_router_kernel, alpha=alpha),
        out_shape=[jax.ShapeDtypeStruct((nb, L, d), F32),
                   jax.ShapeDtypeStruct((nb, L * (d // LANES), LANES), F32),
                   jax.ShapeDtypeStruct((nb, L, LANES), F32),
                   jax.ShapeDtypeStruct((8, LANES), F32)],
        grid=(nb, L // tm),
        in_specs=a_specs + [const(wr), const(br), const(tril_strict)],
        out_specs=[pl.BlockSpec((1, tm, d), lambda b, i: (b, i, 0)),
                   pl.BlockSpec((1, tm * (d // LANES), LANES), lambda b, i: (b, i, 0)),
                   pl.BlockSpec((1, tm, LANES), lambda b, i: (b, i, 0)),
                   pl.BlockSpec((8, LANES), lambda b, i: (0, 0))],
        scratch_shapes=[pltpu.VMEM((8, LANES), F32)],
        compiler_params=_cparams("arbitrary", "arbitrary"),
        name="attn_out_moe_router",
    )(*a_ops, wr, br, tril_strict)


TOP_K = 2
ROW_TILE = 8
EXPERT_FF_SPLITS = 2


def _ff_chunks(width, step=512):
    return [(a, min(a + step, width)) for a in range(0, width, step)]


def _expert_kernel(te_ref, nu_ref, gsrc_ref, sdst_ref,
                   h_ref, wg_ref, wu_ref, wd_ref, yk_ref,
                   xbuf, ybuf, hbuf, acc, gsem, ssem, *, tme):
    i = pl.program_id(0)
    j = pl.program_id(1)
    n_used = nu_ref[0]
    slot = i % 2
    other = 1 - slot
    used = i < n_used
    chunks = _ff_chunks(wg_ref.shape[-1])
    rows_per_step = tme // EXPERT_FF_SPLITS
    n_front = len(chunks) // 2
    front_w = chunks[n_front - 1][1]
    back_w = wg_ref.shape[-1] - front_w
    gather_ranges, scatter_ranges = [], []
    for c, (a, b) in enumerate(chunks):
        if c < n_front:
            gather_ranges.append((rows_per_step * a // front_w, rows_per_step * b // front_w))
            scatter_ranges.append((0, 0))
        else:
            gather_ranges.append((0, 0))
            scatter_ranges.append((rows_per_step * (a - front_w) // back_w,
                                   rows_per_step * (b - front_w) // back_w))

    nt = ROW_TILE

    def gather(tile, slot_, r):
        src = pl.multiple_of(gsrc_ref[tile * tme + r], nt)
        return pltpu.make_async_copy(h_ref.at[pl.ds(src, nt)],
                                     xbuf.at[slot_, pl.ds(pl.multiple_of(r * nt, nt), nt)],
                                     gsem.at[slot_])

    def wait_gather(slot_):
        pltpu.make_async_copy(h_ref.at[pl.ds(0, tme * nt)], xbuf.at[slot_],
                              gsem.at[slot_]).wait()

    def scatter(block, slot_, r):
        dst = pl.multiple_of(sdst_ref[block * tme + r], nt)
        return pltpu.make_async_copy(ybuf.at[slot_, pl.ds(pl.multiple_of(r * nt, nt), nt)],
                                     yk_ref.at[pl.ds(dst, nt)], ssem.at[slot_])

    def wait_scatter(slot_):
        pltpu.make_async_copy(ybuf.at[slot_], yk_ref.at[pl.ds(0, tme * nt)],
                              ssem.at[slot_]).wait()

    @pl.when(jnp.logical_and(i == 0, j == 0))
    def _():
        ybuf[1] = jnp.zeros(ybuf.shape[1:], F32)

        def body(r, carry):
            gather(0, 0, r).start()
            return carry
        lax.fori_loop(0, tme, body, 0, unroll=8)

    @pl.when(used)
    def _():
        @pl.when(j == 0)
        def _():
            wait_gather(slot)
            hbuf[...] = _load_row_tiles(xbuf.at[slot]).astype(BF16)

            @pl.when(i > 0)
            def _():
                wait_scatter(slot)
            acc[...] = jnp.zeros(acc.shape, F32)

        h = hbuf[...]
        for c, (a, b) in enumerate(chunks):
            for r in range(*gather_ranges[c]):
                gather(i + 1, other, j * rows_per_step + r).start()
            for r in range(*scatter_ranges[c]):
                scatter(i, other, j * rows_per_step + r).start()
            gj = _dot(h, wg_ref[0, :, a:b])
            uj = _dot(h, wu_ref[0, :, a:b])
            part = _dot((_silu(gj) * uj).astype(BF16), wd_ref[0, a:b, :])
            acc[...] = acc[...] + part
            ybuf[slot, 0:ROW_TILE, :] = part[0:ROW_TILE, 0:LANES]

        @pl.when(j == EXPERT_FF_SPLITS - 1)
        def _():
            _store_row_tiles(ybuf.at[slot], acc[...])

        @pl.when(jnp.logical_and(j == EXPERT_FF_SPLITS - 1, i == n_used - 1))
        def _():
            wait_gather(other)
            wait_scatter(other)

            def body(r, carry):
                scatter(i + 1, slot, r).start()
                return carry
            lax.fori_loop(0, tme, body, 0, unroll=8)
            wait_scatter(slot)


def _expert_ffn(h_rows, tile_expert, n_used, gsrc, sdst, wg, wu, wd, tme):
    nt = ROW_TILE
    T = h_rows.shape[0] // nt
    d = nt * LANES
    dff = wg.shape[-1]
    dffh = dff // EXPERT_FF_SPLITS
    n_tiles = gsrc.shape[0] // tme
    yk_rows = TOP_K * T + tme

    def half(i, j):
        return jnp.where(i % 2 == 0, j, EXPERT_FF_SPLITS - 1 - j)

    return pl.pallas_call(
        functools.partial(_expert_kernel, tme=tme),
        out_shape=jax.ShapeDtypeStruct((yk_rows * nt, LANES), F32),
        grid_spec=pltpu.PrefetchScalarGridSpec(
            num_scalar_prefetch=4,
            grid=(n_tiles, EXPERT_FF_SPLITS),
            in_specs=[pl.BlockSpec(memory_space=pl.ANY),
                      pl.BlockSpec((1, d, dffh), lambda i, j, te, *_: (te[i], 0, half(i, j))),
                      pl.BlockSpec((1, d, dffh), lambda i, j, te, *_: (te[i], 0, half(i, j))),
                      pl.BlockSpec((1, dffh, d), lambda i, j, te, *_: (te[i], half(i, j), 0))],
            out_specs=pl.BlockSpec(memory_space=pl.ANY),
            scratch_shapes=[pltpu.VMEM((2, tme * nt, LANES), F32),
                            pltpu.VMEM((2, tme * nt, LANES), F32),
                            pltpu.VMEM((tme, d), BF16), pltpu.VMEM((tme, d), F32),
                            pltpu.SemaphoreType.DMA((2,)), pltpu.SemaphoreType.DMA((2,))]),
        compiler_params=_cparams("arbitrary", "arbitrary"),
        name="moe_experts",
    )(tile_expert, n_used, gsrc, sdst, h_rows, wg, wu, wd)


def _moe_finish_kernel(x_ref, route_ref, mod_ref, y0_ref, y1_ref, g_ref, b_ref, o_ref, *, alpha):
    route = route_ref[0]
    f = route[:, 4:5] * _load_row_tiles(y0_ref) + route[:, 5:6] * _load_row_tiles(y1_ref)
    gate = mod_ref[0, 0, 5:6, :]
    r = alpha * x_ref[0] + (1.0 + gate) * f
    o_ref[0] = _layer_norm(r, g_ref[...], b_ref[...])


def _moe_finish(x, route, mod_l, yk, ln_g, ln_b, alpha, tm):
    nb, L, d = x.shape
    tiles_per_seq = L // tm
    tiles = nb * tiles_per_seq
    return pl.pallas_call(
        functools.partial(_moe_finish_kernel, alpha=alpha),
        out_shape=jax.ShapeDtypeStruct((nb, L, d), F32),
        grid=(nb, tiles_per_seq),
        in_specs=[pl.BlockSpec((1, tm, d), lambda b, i: (b, i, 0)),
                  pl.BlockSpec((1, tm, LANES), lambda b, i: (b, i, 0)),
                  pl.BlockSpec((1, 1, 6, d), lambda b, i: (0, b, 0, 0)),
                  pl.BlockSpec((tm * ROW_TILE, LANES), lambda b, i: (b * tiles_per_seq + i, 0)),
                  pl.BlockSpec((tm * ROW_TILE, LANES),
                               lambda b, i: (tiles + b * tiles_per_seq + i, 0)),
                  pl.BlockSpec(ln_g.shape, lambda b, i: (0, 0)),
                  pl.BlockSpec(ln_b.shape, lambda b, i: (0, 0))],
        out_specs=pl.BlockSpec((1, tm, d), lambda b, i: (b, i, 0)),
        compiler_params=_cparams("arbitrary", "arbitrary"),
        name="moe_finish_ln",
    )(x, route, mod_l, yk, yk, ln_g, ln_b)


def _row_index_kernel(dest_ref, gsrc0_ref, sdst0_ref, gsrc_ref, sdst_ref, sem, *, n_pairs, T, tme):
    init_g = pltpu.make_async_copy(gsrc0_ref, gsrc_ref, sem.at[0])
    init_s = pltpu.make_async_copy(sdst0_ref, sdst_ref, sem.at[1])
    init_g.start()
    init_s.start()
    init_g.wait()
    init_s.wait()

    assert TOP_K == 2
    group = 8

    def body(g, carry):
        f0 = g * group
        rows = [dest_ref[f0 + k] for k in range(group)]
        for k in range(group):
            tok = g * (group // TOP_K) + k // TOP_K
            gsrc_ref[rows[k]] = tok * ROW_TILE
            sdst_ref[tme + rows[k]] = ((k % TOP_K) * T + tok) * ROW_TILE
        return carry
    lax.fori_loop(0, n_pairs // group, body, 0, unroll=2)


def _row_indices(dest, n_rows, T, tme):
    dump = (TOP_K * T + jnp.arange(n_rows + tme, dtype=I32) % tme) * ROW_TILE
    return pl.pallas_call(
        functools.partial(_row_index_kernel, n_pairs=dest.shape[0], T=T, tme=tme),
        out_shape=[jax.ShapeDtypeStruct((n_rows,), I32),
                   jax.ShapeDtypeStruct((n_rows + tme,), I32)],
        in_specs=[pl.BlockSpec(memory_space=pltpu.SMEM),
                  pl.BlockSpec(memory_space=pl.ANY),
                  pl.BlockSpec(memory_space=pl.ANY)],
        out_specs=[pl.BlockSpec(memory_space=pltpu.SMEM),
                   pl.BlockSpec(memory_space=pltpu.SMEM)],
        scratch_shapes=[pltpu.SemaphoreType.DMA((2,))],
        name="moe_row_indices",
    )(dest, jnp.zeros((n_rows,), I32), dump)


def _moe(attn_args, w_router, b_router, wg, wu, wd, ln_g, ln_b, alpha, tm_route, tme, tmf):
    nb, L, d = attn_args[0].shape
    mod_l = attn_args[4]
    T = nb * L
    x, h, route, counts = _router(attn_args, w_router, b_router, alpha, tm_route)
    route_flat = route.reshape(T, LANES)
    e = route_flat[:, 0:2].astype(I32)
    rank = route_flat[:, 2:4].astype(I32)
    cnt = counts[0, :N_EXPERTS].astype(I32)
    tiles = (cnt + tme - 1) // tme
    tile_end = jnp.cumsum(tiles)
    group_start = (tile_end - tiles) * tme
    dest = (group_start[e] + rank).reshape(-1)
    n_tiles = (TOP_K * T) // tme + N_EXPERTS
    n_rows = n_tiles * tme
    tile_expert = jnp.minimum(
        jnp.sum(jnp.arange(n_tiles, dtype=I32)[:, None] >= tile_end[None, :].astype(I32), axis=1),
        N_EXPERTS - 1).astype(I32)
    n_used = tile_end[-1:].astype(I32)
    gsrc, sdst = _row_indices(dest, n_rows, T, tme)
    yk = _expert_ffn(h.reshape(T * ROW_TILE, LANES), tile_expert, n_used, gsrc, sdst,
                     wg, wu, wd, tme)
    return _moe_finish(x, route, mod_l, yk, ln_g, ln_b, alpha, tmf)


def kernel(x, c, positions, rel_bias, w_ada, b_ada, w_in, w_out, conv_w, conv_b, dt_bias, a_log,
           d_skip, ssd_norm_w, sinks, ln_g, ln_b, ffn_w_gate, ffn_w_up, ffn_w_down, router_w,
           router_b, expert_w_gate, expert_w_up, expert_w_down):
    depth = w_ada.shape[0]
    nb, L, d = x.shape
    alpha = (2 * depth) ** 0.25
    rd = RET_HEADS * HEAD_DIM
    sd = SSD_HEADS * HEAD_DIM
    cd = conv_w.shape[-1]
    qd = SWA_HEADS * HEAD_DIM
    kvd = SWA_KV_HEADS * HEAD_DIM
    sizes = (rd, rd, rd, rd, sd, cd, SSD_HEADS, qd, kvd, kvd)
    offs = np.concatenate([[0], np.cumsum(sizes)])
    tl = min(512, L)

    mod = _ada_mod(c, w_ada, b_ada)
    cos_t, sin_t = _rotary_tables(positions, tl)
    bias_tab = _swa_bias_table(rel_bias)

    q_perm = np.concatenate([np.arange(h * HEAD_DIM, (h + 1) * HEAD_DIM) for h in SWA_HEAD_ORDER])
    widths = (4 * rd, sd + cd, LANES, qd + 2 * kvd)
    dtypes = (BF16, BF16, F32, BF16)

    steps = nb * (L // tl)
    n_exp, _, dff_e = expert_w_gate.shape[1:]
    per = steps // n_exp
    side_ok = (steps % n_exp == 0 and d % per == 0 and dff_e % per == 0
               and (d // per) % 8 == 0 and (dff_e // per) % 8 == 0)

    for layer in range(depth):
        wl = w_in[layer]
        seg = lambda i: wl[:, offs[i]:offs[i + 1]]
        dt_cols = jnp.zeros((d, LANES), F32).at[:, :SSD_HEADS].set(seg(6))
        w_cat = jnp.concatenate(
            [seg(0), seg(1), seg(2), seg(3), seg(4), seg(5), dt_cols,
             seg(7)[:, q_perm], seg(8), seg(9)], axis=1).astype(BF16)
        wo = w_out[layer]
        w_r = wo[0:rd].astype(BF16)
        w_s = wo[rd:rd + sd].astype(BF16)
        w_a = wo[rd + sd:][q_perm].astype(BF16)
        mod_l = mod[layer:layer + 1]

        i = layer // 2
        is_moe = layer % 2 == 1
        mix_cast = (expert_w_up[i], expert_w_down[i]) if is_moe and side_ok else ()
        y_ret, y_ssd, y_swa, *up_down_bf16 = _mixers(
            x, mod_l, w_cat, widths, dtypes, cos_t, sin_t, bias_tab, sinks[layer], conv_w[layer],
            conv_b[layer], dt_bias[layer], a_log[layer], d_skip[layer], ssd_norm_w[layer], tl,
            side_cast=mix_cast)
        attn_args = (x, y_ret, y_ssd, y_swa, mod_l, w_r, w_s, w_a,
                     ln_g[layer, 0][None, :], ln_b[layer, 0][None, :])

        g2 = ln_g[layer, 1][None, :]
        b2 = ln_b[layer, 1][None, :]
        if not is_moe:
            nxt = (layer + 1) // 2
            ride = (expert_w_gate[nxt],) if layer + 1 < depth and side_ok else ()
            x, *gate_bf16 = _dense_ffn(
                attn_args, ffn_w_gate[i].astype(BF16), ffn_w_up[i].astype(BF16),
                ffn_w_down[i].astype(BF16), g2, b2, alpha, tl, side_cast=ride)
        else:
            if side_ok:
                wg_e, = gate_bf16
                wu_e, wd_e = up_down_bf16
            else:
                wg_e, wu_e, wd_e = (expert_w_gate[i].astype(BF16), expert_w_up[i].astype(BF16),
                                    expert_w_down[i].astype(BF16))
            x = _moe(attn_args, router_w[i], router_b[i], wg_e, wu_e, wd_e, g2, b2, alpha,
                     tm_route=min(512, L), tme=min(512, L), tmf=min(512, L))
    return x
```

```python
import functools
import math

import numpy as np
import jax
import jax.numpy as jnp
from jax import lax
from jax.experimental import pallas as pl
from jax.experimental.pallas import tpu as pltpu

F32 = jnp.float32
BF16 = jnp.bfloat16
I32 = jnp.int32

HEAD_DIM = 64
CHUNK = 128
RET_HEADS = 4
SSD_HEADS = 8
SSD_GROUPS = 2
SSD_STATE = 64
SSD_CONV = 4
CONV_TAIL = 16
SWA_HEADS = 4
SWA_KV_HEADS = 2
REL_BUCKETS = 32
N_EXPERTS = 8
LN_EPS = 1e-5
LANES = 128
MASK_VALUE = -1e30

VMEM_LIMIT = 56 * 1024 * 1024


def _cparams(*sem):
    return pltpu.CompilerParams(dimension_semantics=sem, vmem_limit_bytes=VMEM_LIMIT)


def _silu(v):
    return v * (1.0 / (1.0 + jnp.exp(-v)))


def _softplus(v):
    return jnp.maximum(v, 0.0) + jnp.log(1.0 + jnp.exp(-jnp.abs(v)))


def _dot(a, b):
    return jnp.dot(a, b, preferred_element_type=F32)


def _dot_nt(a, b):
    return lax.dot_general(a, b, (((1,), (1,)), ((), ())), preferred_element_type=F32)


def _hi_lo(v):
    bits = lax.bitcast_convert_type(v, jnp.uint32) & jnp.uint32(0xFFFF0000)
    hi = lax.bitcast_convert_type(bits, F32)
    return hi.astype(BF16), (v - hi).astype(BF16)


def _split3(v):
    h1 = v.astype(BF16)
    r1 = v - h1.astype(F32)
    h2 = r1.astype(BF16)
    r2 = r1 - h2.astype(F32)
    return h1, h2, r2.astype(BF16)


def _dot3(v, m3):
    return _dot(jnp.concatenate(_split3(v), axis=1), m3)


def _dot3_left(m3, v):
    return _dot(m3, jnp.concatenate(_split3(v), axis=0))


def _dot2(v, m2):
    h1 = v.astype(BF16)
    h2 = (v - h1.astype(F32)).astype(BF16)
    return _dot(jnp.concatenate([h1, h2], axis=1), m2)


def _layer_norm(r, g, b):
    mu = jnp.mean(r, -1, keepdims=True)
    d = r - mu
    var = jnp.mean(d * d, -1, keepdims=True)
    return d * lax.rsqrt(var + LN_EPS) * g + b


def _ada_kernel(c_ref, w_ref, b_ref, o_ref):
    o_ref[0] = jnp.dot(c_ref[...], w_ref[0], preferred_element_type=F32,
                       precision=lax.Precision.HIGHEST) + b_ref[0]


def _ada_mod(c, w_ada, b_ada):
    depth, d, d6 = w_ada.shape
    nb = c.shape[0]
    rows = 8
    c_pad = jnp.zeros((rows, d), F32).at[:nb].set(c)
    out = pl.pallas_call(
        _ada_kernel,
        out_shape=jax.ShapeDtypeStruct((depth, rows, d6), F32),
        grid=(depth, d6 // d),
        in_specs=[pl.BlockSpec((rows, d), lambda l, j: (0, 0)),
                  pl.BlockSpec((1, d, d), lambda l, j: (l, 0, j)),
                  pl.BlockSpec((1, 1, d), lambda l, j: (l, 0, j))],
        out_specs=pl.BlockSpec((1, rows, d), lambda l, j: (l, 0, j)),
        compiler_params=_cparams("arbitrary", "arbitrary"),
        name="ada_mod",
    )(c_pad, w_ada, b_ada.reshape(depth, 1, d6))
    return out[:, :nb].reshape(depth, nb, 6, d)


def _rotary_kernel(pos_ref, cos_ref, sin_ref):
    half = HEAD_DIM // 2
    lane = lax.broadcasted_iota(I32, (1, LANES), 1)
    jj = lane % HEAD_DIM
    idx = (jj % half).astype(F32)
    inv = jnp.exp(-math.log(10000.0) * idx / half)
    ang = pos_ref[0].astype(F32) * inv
    cos_ref[0] = jnp.cos(ang)
    sin_ref[0] = jnp.where(jj < half, -1.0, 1.0) * jnp.sin(ang)


def _rotary_tables(positions, tl):
    nb, L = positions.shape
    pos = positions.reshape(nb, L, 1)
    return pl.pallas_call(
        _rotary_kernel,
        out_shape=[jax.ShapeDtypeStruct((nb, L, LANES), F32)] * 2,
        grid=(nb, L // tl),
        in_specs=[pl.BlockSpec((1, tl, 1), lambda b, i: (b, i, 0))],
        out_specs=[pl.BlockSpec((1, tl, LANES), lambda b, i: (b, i, 0))] * 2,
        compiler_params=_cparams("arbitrary", "arbitrary"),
        name="rotary_tables",
    )(pos)


def _swa_bias_kernel(rb_ref, bucket_ref, band_ref, o_ref):
    bucket = bucket_ref[...]
    band = band_ref[...]
    for h in range(SWA_HEADS):
        acc = jnp.zeros(bucket.shape, F32)
        for b in range(REL_BUCKETS):
            acc = jnp.where(bucket == b, rb_ref[b, h], acc)
        o_ref[h] = jnp.where(band > 0, acc, MASK_VALUE)


def _t5_bucket(dist):
    exact = REL_BUCKETS // 2
    df = jnp.maximum(dist, 1).astype(F32)
    large = exact + (jnp.log(df / exact) / math.log(CHUNK / exact) * (REL_BUCKETS - exact)).astype(I32)
    large = jnp.minimum(large, REL_BUCKETS - 1)
    return jnp.where(dist < exact, dist, large)


def _swa_bias_table(rel_bias):
    W = CHUNK
    qi = jnp.arange(W)[:, None]
    kj = jnp.arange(2 * W)[None, :]
    dist = qi + W - kj
    band = ((dist >= 0) & (dist < W)).astype(I32)
    bucket = _t5_bucket(jnp.clip(dist, 0, W - 1)).astype(I32)
    return pl.pallas_call(
        _swa_bias_kernel,
        out_shape=jax.ShapeDtypeStruct((SWA_HEADS, W, 2 * W), F32),
        in_specs=[pl.BlockSpec(memory_space=pltpu.SMEM),
                  pl.BlockSpec(memory_space=pltpu.VMEM),
                  pl.BlockSpec(memory_space=pltpu.VMEM)],
        out_specs=pl.BlockSpec(memory_space=pltpu.VMEM),
        name="swa_bias_table",
    )(rel_bias, bucket, band)


def _head_lane_mask(width, head):
    lane = lax.broadcasted_iota(I32, (1, width), 1)
    return (lane // HEAD_DIM) == head


def _rotate_half(t):
    width = t.shape[-1]
    lane = lax.broadcasted_iota(I32, (1, width), 1)
    half = HEAD_DIM // 2
    fwd = pltpu.roll(t, width - half, axis=1)
    bwd = pltpu.roll(t, half, axis=1)
    return jnp.where((lane % HEAD_DIM) < half, fwd, bwd)


def _retention_body(u_ref, cos_ref, sin_ref, din_ref, dq_ref, dk_ref, dc_ref,
                    bmask_ref, avg_ref, o_ref, state_ref, *, n_chunks):
    rd = RET_HEADS * HEAD_DIM
    masks = [_head_lane_mask(rd, h) for h in range(RET_HEADS)]

    def stack_heads(t):
        return jnp.concatenate([jnp.where(m, t, 0.0) for m in masks], axis=0).astype(BF16)

    state = state_ref[...]
    for ci in range(n_chunks):
        rows = slice(ci * CHUNK, (ci + 1) * CHUNK)
        cos = cos_ref[0, rows, :]
        sin = sin_ref[0, rows, :]
        cos2 = jnp.concatenate([cos, cos], axis=1)
        sin2 = jnp.concatenate([sin, sin], axis=1)
        q = u_ref[0, rows, 0:rd].astype(F32)
        k = u_ref[0, rows, rd:2 * rd].astype(F32)
        v = u_ref[0, rows, 2 * rd:3 * rd].astype(F32)
        g = u_ref[0, rows, 3 * rd:4 * rd].astype(F32)
        qr = q * cos2 + _rotate_half(q) * sin2
        kr = (k * cos2 + _rotate_half(k) * sin2) * (HEAD_DIM ** -0.5)
        scores = _dot_nt(qr.astype(BF16), stack_heads(kr)) * din_ref[...]
        inner = _dot(scores.astype(BF16), stack_heads(v))
        cross = _dot((qr * dq_ref[...]).astype(BF16), state.astype(BF16))
        o = inner + cross
        kd_t = (kr * dk_ref[...]).T.astype(BF16)
        kv = _dot(kd_t, v.astype(BF16))
        state = dc_ref[...] * state + bmask_ref[...] * kv
        mu = _dot2(o, avg_ref[...])
        dev = o - mu
        var = _dot2(dev * dev, avg_ref[...])
        on = dev * lax.rsqrt(var + LN_EPS)
        o_ref[0, rows, :] = (_silu(g) * on).astype(o_ref.dtype)
        yield
    state_ref[...] = state


def _retention_tables():
    H, d, C = RET_HEADS, HEAD_DIM, CHUNK
    log_gamma = jnp.log(1.0 - 2.0 ** (-5.0 - jnp.arange(H, dtype=F32)))
    idx = jnp.arange(C, dtype=F32)
    diff = idx[:, None] - idx[None, :]
    decay_in = jnp.where(diff >= 0, jnp.exp(log_gamma[:, None, None] * jnp.maximum(diff, 0.0)), 0.0)
    decay_q = jnp.exp(log_gamma[:, None] * (idx + 1.0))
    decay_k = jnp.exp(log_gamma[:, None] * (C - 1.0 - idx))
    decay_chunk = jnp.exp(log_gamma * C)
    din = decay_in.transpose(1, 0, 2).reshape(C, H * C)
    dq = jnp.repeat(decay_q.T, d, axis=1)
    dk = jnp.repeat(decay_k.T, d, axis=1)
    dc = jnp.repeat(decay_chunk, d)[None, :]
    head = jnp.arange(H * d) // d
    bmask = (head[:, None] == head[None, :]).astype(F32)
    avg = jnp.tile((bmask / d).astype(BF16), (2, 1))
    return din, dq, dk, dc, bmask, avg


def _swa_body(u_ref, bias_ref, sink_ref, o_ref, kprev_ref, vprev_ref, *, n_chunks, first_step):
    W = CHUNK
    qd = SWA_HEADS * HEAD_DIM
    kvd = SWA_KV_HEADS * HEAD_DIM
    lane = lax.broadcasted_iota(I32, (1, LANES), 1)
    low = lane < HEAD_DIM
    col = lax.broadcasted_iota(I32, (1, 2 * W), 1)

    kprev = kprev_ref[...]
    vprev = vprev_ref[...]
    sink = sink_ref[...]
    for ci in range(n_chunks):
        rows = slice(ci * W, (ci + 1) * W)
        qa = u_ref[0, rows, 0:LANES].astype(F32)
        qb = u_ref[0, rows, LANES:qd].astype(F32)
        k = u_ref[0, rows, qd:qd + kvd].astype(BF16)
        v = u_ref[0, rows, qd + kvd:qd + 2 * kvd].astype(BF16)
        q4 = jnp.concatenate([jnp.where(low, qa, 0.0), jnp.where(low, 0.0, qa),
                              jnp.where(low, qb, 0.0), jnp.where(low, 0.0, qb)],
                             axis=0).astype(BF16)
        kband = jnp.concatenate([kprev, k], axis=0)
        vband = jnp.concatenate([vprev, v], axis=0)
        logits = _dot_nt(q4, kband) * (HEAD_DIM ** -0.5) + bias_ref[...]
        if ci == 0:
            logits = jnp.where(jnp.logical_and(first_step, col < W), MASK_VALUE, logits)
        m = jnp.maximum(jnp.max(logits, -1, keepdims=True), sink)
        p = jnp.exp(logits - m)
        denom = jnp.sum(p, -1, keepdims=True) + jnp.exp(sink - m)
        res = _dot(p.astype(BF16), vband) / denom
        out_a = jnp.where(low, res[0:W], res[W:2 * W])
        out_b = jnp.where(low, res[2 * W:3 * W], res[3 * W:4 * W])
        o_ref[0, rows, 0:LANES] = out_a.astype(o_ref.dtype)
        o_ref[0, rows, LANES:qd] = out_b.astype(o_ref.dtype)
        kprev, vprev = k, v
        yield
    kprev_ref[...] = kprev
    vprev_ref[...] = vprev


SWA_HEAD_ORDER = (0, 2, 1, 3)


def _ssd_body(u_ref, dt_ref, cw_ref, cb_ref, dtb_c_ref, alog_c_ref, dskip_ref, nw_ref,
              tril_ref, expand_ref, gmask_ref, shift_ref,
              o_ref, state_ref, ext_ref, *, n_chunks):
    C = CHUNK
    sd = SSD_HEADS * HEAD_DIM
    gn = SSD_GROUPS * SSD_STATE
    cd = sd + 2 * gn
    tl = n_chunks * C
    slab = 2 * LANES
    heads_per_group = SSD_HEADS // SSD_GROUPS
    heads_per_slab = slab // HEAD_DIM
    lane = lax.broadcasted_iota(I32, (1, LANES), 1)
    low = lane < SSD_STATE
    slab_masks = [_head_lane_mask(slab, hh) for hh in range(heads_per_slab)]
    row_i = lax.broadcasted_iota(I32, (C, C), 0)
    col_i = lax.broadcasted_iota(I32, (C, C), 1)
    causal = row_i >= col_i
    neg_a_c = -jnp.exp(alog_c_ref[...])

    ext_ref[CONV_TAIL:CONV_TAIL + tl, :] = u_ref[0, :, sd:sd + cd]
    state = state_ref[...]
    for ci in range(n_chunks):
        rows = slice(ci * C, (ci + 1) * C)
        z = u_ref[0, rows, 0:sd].astype(F32)
        dt_raw = dt_ref[0, rows, :]
        window = ext_ref[ci * C:ci * C + CONV_TAIL + C, :]
        shifted = _dot(shift_ref[...], window)
        conv = cb_ref[...] + cw_ref[SSD_CONV - 1:SSD_CONV, :] * window[CONV_TAIL:, :].astype(F32)
        for w in range(SSD_CONV - 1):
            conv = conv + cw_ref[w:w + 1, :] * shifted[w * C:(w + 1) * C, :]
        xbc = _silu(conv)
        xs = xbc[:, 0:sd]
        bm = xbc[:, sd:sd + gn]
        cm = xbc[:, sd + gn:cd]

        dt_c = _softplus(dt_raw + dtb_c_ref[...])
        a_c = neg_a_c * dt_c
        acs_c = _dot3_left(tril_ref[...], a_c)
        acs_t = acs_c.T
        spread = _dot3(acs_c, expand_ref[...])
        acs_x = spread[:, 0:sd]
        dt_x = _dot3(dt_c, expand_ref[:, 0:sd])
        xdt = xs * dt_x
        yield

        bstack = jnp.concatenate([jnp.where(low, bm, 0.0), jnp.where(low, 0.0, bm)],
                                 axis=0).astype(BF16)
        cb = _dot_nt(cm.astype(BF16), bstack)
        y_diag = []
        for s in range(sd // slab):
            ms = []
            for hh in range(heads_per_slab):
                h = s * heads_per_slab + hh
                g = h // heads_per_group
                col_bcast = spread[:, sd + h * LANES:sd + (h + 1) * LANES]
                seg = col_bcast - acs_t[h:h + 1, :]
                lmat = jnp.exp(jnp.where(causal, seg, MASK_VALUE))
                ms.append((cb[:, g * C:(g + 1) * C] * lmat).astype(BF16))
            xslab = xdt[:, s * slab:(s + 1) * slab]
            xstack = jnp.concatenate([jnp.where(m, xslab, 0.0) for m in slab_masks],
                                     axis=0).astype(BF16)
            y_diag.append(_dot(jnp.concatenate(ms, axis=1), xstack))
        y_diag = jnp.concatenate(y_diag, axis=1)
        yield

        y_off = _dot(cm.astype(BF16), state.astype(BF16)) * jnp.exp(acs_x)
        last = acs_x[C - 1:C, :]
        dec = jnp.exp(last - acs_x)
        new = _dot(bm.T.astype(BF16), (xdt * dec).astype(BF16))
        state = jnp.exp(last) * state + gmask_ref[...] * new

        y = y_diag + y_off + xs * dskip_ref[...]
        hgate = y * _silu(z)
        gw = sd // SSD_GROUPS
        for g in range(SSD_GROUPS):
            hg = hgate[:, g * gw:(g + 1) * gw]
            ms_ = jnp.mean(hg * hg, -1, keepdims=True)
            o_ref[0, rows, g * gw:(g + 1) * gw] = (
                hg * lax.rsqrt(ms_ + LN_EPS) * nw_ref[:, g * gw:(g + 1) * gw]).astype(o_ref.dtype)
        yield
    state_ref[...] = state
    ext_ref[0:CONV_TAIL, :] = ext_ref[tl:tl + CONV_TAIL, :]


def _ssd_tables():
    C = CHUNK
    sd = SSD_HEADS * HEAD_DIM
    gn = SSD_GROUPS * SSD_STATE
    t = jnp.arange(C)
    tril = (t[:, None] >= t[None, :]).astype(BF16)
    r = jnp.arange(LANES)[:, None]
    eexp = ((r == (jnp.arange(sd)[None, :] // HEAD_DIM)) & (r < SSD_HEADS)).astype(BF16)
    bsel = ((r == (jnp.arange(SSD_HEADS * LANES)[None, :] // LANES)) & (r < SSD_HEADS)).astype(BF16)
    heads_per_group = SSD_HEADS // SSD_GROUPS
    row_g = jnp.arange(gn)[:, None] // SSD_STATE
    col_g = (jnp.arange(sd)[None, :] // HEAD_DIM) // heads_per_group
    gmask = (row_g == col_g).astype(F32)
    expand = jnp.concatenate([eexp, bsel], axis=1)
    win = jnp.arange(CONV_TAIL + C)[None, :]
    shifts = jnp.concatenate(
        [(win == CONV_TAIL + t[:, None] - (SSD_CONV - 1 - s)).astype(BF16)
         for s in range(SSD_CONV - 1)], axis=0)
    return jnp.tile(tril, (1, 3)), jnp.tile(expand, (3, 1)), gmask, shifts


N_RET_TABLES = 6
N_SSD_CONSTS = 10


def _in_proj_body(x_ref, shift, scale, w_ref, out_refs):
    h = (x_ref[0] * (1.0 + scale) + shift).astype(BF16)
    off = 0
    for ref in out_refs:
        width = ref.shape[-1]
        for a, b in _ff_chunks(width):
            ref[0, :, a:b] = _dot(h, w_ref[:, off + a:off + b]).astype(ref.dtype)
            yield
        off += width


def _mixers_kernel(*refs, n_chunks, n_cast, n_tiles, tiles_per_seq):
    x_ref, mod_ref, w_ref, cos, sin = refs[:5]
    pos = 5
    ret_tables = refs[pos:pos + N_RET_TABLES]
    pos += N_RET_TABLES
    ssd_consts = refs[pos:pos + N_SSD_CONSTS]
    pos += N_SSD_CONSTS
    bias, sink = refs[pos:pos + 2]
    pos += 2
    cast_in = refs[pos:pos + n_cast]
    pos += n_cast
    y_ret, y_ssd, y_swa = refs[pos:pos + 3]
    pos += 3
    cast_out = refs[pos:pos + n_cast]
    pos += n_cast
    ret_state, ssd_state, ssd_ext, kprev, vprev = refs[pos:pos + 5]
    cur = refs[pos + 5:pos + 9]
    nxt = refs[pos + 9:pos + 13]
    _side_cast(cast_in, cast_out)

    s = pl.program_id(0)
    tile = jnp.maximum(s - 1, 0)
    seq_start = tile % tiles_per_seq == 0

    @pl.when(s == 0)
    def _():
        for ref in cur:
            ref[...] = jnp.zeros_like(ref)

    @pl.when(seq_start)
    def _():
        ret_state[...] = jnp.zeros_like(ret_state)
        ssd_state[...] = jnp.zeros_like(ssd_state)
        ssd_ext[0:CONV_TAIL, :] = jnp.zeros((CONV_TAIL, ssd_ext.shape[1]), ssd_ext.dtype)
        kprev[...] = jnp.zeros_like(kprev)
        vprev[...] = jnp.zeros_like(vprev)

    batch = jnp.minimum(s, n_tiles - 1) // tiles_per_seq
    shift = mod_ref[0, batch, 0:1, :]
    scale = mod_ref[0, batch, 1:2, :]
    u_ret, u_ssd, u_dt, u_swa = cur
    bodies = [
        _ssd_body(u_ssd, u_dt, *ssd_consts, y_ssd, ssd_state, ssd_ext, n_chunks=n_chunks),
        _in_proj_body(x_ref, shift, scale, w_ref, nxt),
        _swa_body(u_swa, bias, sink, y_swa, kprev, vprev, n_chunks=n_chunks,
                  first_step=seq_start),
        _retention_body(u_ret, cos, sin, *ret_tables, y_ret, ret_state, n_chunks=n_chunks),
    ]
    alive = list(bodies)
    while alive:
        alive = [b for b in alive if next(b, StopIteration) is not StopIteration]
    for c_ref, n_ref in zip(cur, nxt):
        c_ref[...] = n_ref[...]


def _mixers(x, mod_l, w_cat, widths, dtypes, cos_t, sin_t, bias_tab, sinks_l, conv_w, conv_b,
            dt_bias, a_log, d_skip, norm_w, tl, side_cast=()):
    nb, L, d = x.shape
    W = CHUNK
    rd = RET_HEADS * HEAD_DIM
    sd = SSD_HEADS * HEAD_DIM
    qd = SWA_HEADS * HEAD_DIM
    cd = conv_w.shape[-1]
    ret_tables = _retention_tables()
    pad = lambda v: jnp.zeros((1, LANES), F32).at[0, :SSD_HEADS].set(v)
    rep = lambda v: jnp.repeat(v, HEAD_DIM)[None, :]
    ssd_consts = (conv_w, conv_b[None, :], pad(dt_bias), pad(a_log), rep(d_skip),
                  norm_w[None, :]) + _ssd_tables()
    order = jnp.array(SWA_HEAD_ORDER)
    bias_stacked = bias_tab[order].reshape(SWA_HEADS * W, 2 * W)
    sink_col = jnp.repeat(sinks_l.astype(F32)[order], W)[:, None]
    assert len(ret_tables) == N_RET_TABLES and len(ssd_consts) == N_SSD_CONSTS
    tps = L // tl
    n_tiles = nb * tps

    def tile_of(step):
        return step // tps, step % tps, 0

    proj_map = lambda s: tile_of(jnp.minimum(s, n_tiles - 1))
    mix_map = lambda s: tile_of(jnp.maximum(s - 1, 0))
    const = lambda a: pl.BlockSpec(a.shape, lambda s: (0,) * a.ndim)
    consts = ret_tables + ssd_consts + (bias_stacked, sink_col)
    c_in, c_out, c_shapes = _side_cast_specs(side_cast, n_tiles)
    u_bufs = [pltpu.VMEM((1, tl, w), t) for w, t in zip(widths, dtypes)]
    return pl.pallas_call(
        functools.partial(_mixers_kernel, n_chunks=tl // CHUNK, n_cast=len(side_cast),
                          n_tiles=n_tiles, tiles_per_seq=tps),
        out_shape=[jax.ShapeDtypeStruct((nb, L, w), BF16) for w in (rd, sd, qd)] + c_shapes,
        grid=(n_tiles + 1,),
        in_specs=[pl.BlockSpec((1, tl, d), proj_map), const(mod_l), const(w_cat),
                  pl.BlockSpec((1, tl, LANES), mix_map), pl.BlockSpec((1, tl, LANES), mix_map)]
                 + [const(a) for a in consts] + c_in,
        out_specs=[pl.BlockSpec((1, tl, w), mix_map) for w in (rd, sd, qd)] + c_out,
        scratch_shapes=[pltpu.VMEM((rd, rd), F32),
                        pltpu.VMEM((SSD_GROUPS * SSD_STATE, sd), F32),
                        pltpu.VMEM((CONV_TAIL + tl, cd), BF16),
                        pltpu.VMEM((W, LANES), BF16), pltpu.VMEM((W, LANES), BF16)]
                       + u_bufs + u_bufs,
        compiler_params=_cparams("arbitrary"),
        name="in_proj_mixers",
    )(x, mod_l, w_cat, cos_t, sin_t, *consts, *side_cast)


N_ATTN_OUT = 10


def _attn_out(x_ref, yr_ref, ys_ref, ya_ref, mod_ref, wr_ref, ws_ref, wa_ref, g_ref, b_ref,
              alpha):
    mix = (_dot(yr_ref[0], wr_ref[...]) + _dot(ys_ref[0], ws_ref[...])
           + _dot(ya_ref[0], wa_ref[...]))
    gate = mod_ref[0, 0, 2:3, :]
    r = alpha * x_ref[0] + (1.0 + gate) * mix
    return _layer_norm(r, g_ref[...], b_ref[...])


def _attn_out_operands(x, y_ret, y_ssd, y_swa, mod_l, w_r, w_s, w_a, ln_g, ln_b, tm):
    d = x.shape[-1]
    tok = lambda w: pl.BlockSpec((1, tm, w), lambda b, i: (b, i, 0))
    const = lambda a: pl.BlockSpec(a.shape, lambda b, i: (0,) * a.ndim)
    specs = [tok(d), tok(y_ret.shape[-1]), tok(y_ssd.shape[-1]), tok(y_swa.shape[-1]),
             pl.BlockSpec((1, 1, 6, d), lambda b, i: (0, b, 0, 0)),
             const(w_r), const(w_s), const(w_a), const(ln_g), const(ln_b)]
    return specs, (x, y_ret, y_ssd, y_swa, mod_l, w_r, w_s, w_a, ln_g, ln_b)


FF_CHUNK = 256


def _side_cast_specs(arrays, steps, steps_per_seq=None):
    in_specs, out_specs, out_shapes = [], [], []
    for a in arrays:
        e, r, c = a.shape
        per = steps // e
        if steps_per_seq is None:
            index = lambda s, per=per: (jnp.minimum(s, steps - 1) // per,
                                        jnp.minimum(s, steps - 1) % per, 0)
        else:
            index = lambda b, i, per=per: ((b * steps_per_seq + i) // per,
                                           (b * steps_per_seq + i) % per, 0)
        spec = pl.BlockSpec((1, r // per, c), index)
        in_specs.append(spec)
        out_specs.append(spec)
        out_shapes.append(jax.ShapeDtypeStruct(a.shape, BF16))
    return in_specs, out_specs, out_shapes


def _side_cast(in_refs, out_refs):
    for i_ref, o_ref in zip(in_refs, out_refs):
        o_ref[...] = i_ref[...].astype(o_ref.dtype)


def _dense_ffn_kernel(*refs, alpha):
    attn = refs[:N_ATTN_OUT]
    mod_ref = attn[4]
    wg_ref, wu_ref, wd_ref, g_ref, b_ref = refs[N_ATTN_OUT:N_ATTN_OUT + 5]
    rest = refs[N_ATTN_OUT + 5:]
    n_cast = (len(rest) - 1) // 2
    o_ref = rest[n_cast]
    _side_cast(rest[:n_cast], rest[n_cast + 1:])
    x = _attn_out(*attn, alpha)
    sh = mod_ref[0, 0, 3:4, :]
    sc = mod_ref[0, 0, 4:5, :]
    gate = mod_ref[0, 0, 5:6, :]
    h = (x * (1.0 + sc) + sh).astype(BF16)
    dff = wg_ref.shape[-1]
    acc = jnp.zeros(x.shape, F32)
    for j in range(0, dff, FF_CHUNK):
        gj = _dot(h, wg_ref[:, j:j + FF_CHUNK])
        uj = _dot(h, wu_ref[:, j:j + FF_CHUNK])
        acc = acc + _dot((_silu(gj) * uj).astype(BF16), wd_ref[j:j + FF_CHUNK, :])
    r = alpha * x + (1.0 + gate) * acc
    o_ref[0] = _layer_norm(r, g_ref[...], b_ref[...])


def _dense_ffn(attn_args, wg, wu, wd, ln_g, ln_b, alpha, tm, side_cast=()):
    nb, L, d = attn_args[0].shape
    const = lambda a: pl.BlockSpec(a.shape, lambda b, i: (0,) * a.ndim,
                                   pipeline_mode=pl.Buffered(1))
    a_specs, a_ops = _attn_out_operands(*attn_args, tm)
    c_in, c_out, c_shapes = _side_cast_specs(side_cast, nb * (L // tm), L // tm)
    return pl.pallas_call(
        functools.partial(_dense_ffn_kernel, alpha=alpha),
        out_shape=[jax.ShapeDtypeStruct((nb, L, d), F32)] + c_shapes,
        grid=(nb, L // tm),
        in_specs=a_specs + [const(wg), const(wu), const(wd), const(ln_g), const(ln_b)] + c_in,
        out_specs=[pl.BlockSpec((1, tm, d), lambda b, i: (b, i, 0))] + c_out,
        compiler_params=_cparams("arbitrary", "arbitrary"),
        name="attn_out_dense_ffn_ln",
    )(*a_ops, wg, wu, wd, ln_g, ln_b, *side_cast)


def _store_row_tiles(ref, value):
    rows, d = value.shape
    nt = d // LANES
    for c in range(nt):
        ref[pl.ds(c, rows, stride=nt), :] = value[:, c * LANES:(c + 1) * LANES].astype(ref.dtype)


def _load_row_tiles(ref):
    nt = ROW_TILE
    rows = ref.shape[0] // nt
    return jnp.concatenate([ref[pl.ds(c, rows, stride=nt), :] for c in range(nt)], axis=1)


def _router_kernel(*refs, alpha):
    attn = refs[:N_ATTN_OUT]
    mod_ref = attn[4]
    wr_ref, br_ref, trils_ref, x1_ref, h_ref, route_ref, cnt_ref, base_ref = refs[N_ATTN_OUT:]
    first = jnp.logical_and(pl.program_id(0) == 0, pl.program_id(1) == 0)

    @pl.when(first)
    def _():
        base_ref[...] = jnp.zeros_like(base_ref)

    x1 = _attn_out(*attn, alpha)
    x1_ref[0] = x1
    sh = mod_ref[0, 0, 3:4, :]
    sc = mod_ref[0, 0, 4:5, :]
    h = x1 * (1.0 + sc) + sh
    _store_row_tiles(h_ref.at[0], h)
    h_hi, h_lo = _hi_lo(h)
    logits = _dot(jnp.concatenate([h_hi, h_hi, h_lo, h_lo], axis=1), wr_ref[...]) + br_ref[...]
    lane = lax.broadcasted_iota(I32, logits.shape, 1).astype(F32)
    logits = jnp.where(lane < N_EXPERTS, logits, MASK_VALUE)
    v1 = jnp.max(logits, -1, keepdims=True)
    e1 = jnp.min(jnp.where(logits == v1, lane, float(LANES)), -1, keepdims=True)
    rest = jnp.where(lane == e1, MASK_VALUE, logits)
    v2 = jnp.max(rest, -1, keepdims=True)
    e2 = jnp.min(jnp.where(rest == v2, lane, float(LANES)), -1, keepdims=True)
    t = jnp.exp(v2 - v1)
    w1 = 1.0 / (1.0 + t)
    w2 = t / (1.0 + t)
    hot1 = (lane == e1).astype(F32)
    hot2 = (lane == e2).astype(F32)
    both = hot1 + hot2
    base = base_ref[0:1, :]
    before = _dot(trils_ref[...], both.astype(BF16)) + base
    rank1 = jnp.sum(hot1 * before, -1, keepdims=True)
    rank2 = jnp.sum(hot2 * before, -1, keepdims=True)
    total = base + jnp.sum(both, 0, keepdims=True)
    base_ref[0:1, :] = total
    cnt_ref[...] = jnp.broadcast_to(total, cnt_ref.shape)
    route = jnp.where(lane == 0, e1, 0.0)
    route = jnp.where(lane == 1, e2, route)
    route = jnp.where(lane == 2, rank1, route)
    route = jnp.where(lane == 3, rank2, route)
    route = jnp.where(lane == 4, w1, route)
    route = jnp.where(lane == 5, w2, route)
    route_ref[0] = route


def _router(attn_args, w_router, b_router, alpha, tm):
    nb, L, d = attn_args[0].shape
    w_hi, w_lo = _hi_lo(jnp.zeros((d, LANES), F32).at[:, :N_EXPERTS].set(w_router))
    wr = jnp.concatenate([w_hi, w_lo, w_hi, w_lo], axis=0)
    br = jnp.zeros((1, LANES), F32).at[0, :N_EXPERTS].set(b_router)
    t = jnp.arange(tm)
    tril_strict = (t[:, None] > t[None, :]).astype(BF16)
    const = lambda a: pl.BlockSpec(a.shape, lambda b, i: (0,) * a.ndim)
    a_specs, a_ops = _attn_out_operands(*attn_args, tm)
    return pl.pallas_call(
        functools.partial(_router_kernel, alpha=alpha),
        out_shape=[jax.ShapeDtypeStruct((nb, L, d), F32),
                   jax.ShapeDtypeStruct((nb, L * (d // LANES), LANES), F32),
                   jax.ShapeDtypeStruct((nb, L, LANES), F32),
                   jax.ShapeDtypeStruct((8, LANES), F32)],
        grid=(nb, L // tm),
        in_specs=a_specs + [const(wr), const(br), const(tril_strict)],
        out_specs=[pl.BlockSpec((1, tm, d), lambda b, i: (b, i, 0)),
                   pl.BlockSpec((1, tm * (d // LANES), LANES), lambda b, i: (b, i, 0)),
                   pl.BlockSpec((1, tm, LANES), lambda b, i: (b, i, 0)),
                   pl.BlockSpec((8, LANES), lambda b, i: (0, 0))],
        scratch_shapes=[pltpu.VMEM((8, LANES), F32)],
        compiler_params=_cparams("arbitrary", "arbitrary"),
        name="attn_out_moe_router",
    )(*a_ops, wr, br, tril_strict)


TOP_K = 2
ROW_TILE = 8
EXPERT_FF_SPLITS = 2


def _ff_chunks(width, step=512):
    return [(a, min(a + step, width)) for a in range(0, width, step)]


def _expert_kernel(te_ref, nu_ref, gsrc_ref, sdst_ref,
                   h_ref, wg_ref, wu_ref, wd_ref, yk_ref,
                   xbuf, ybuf, hbuf, acc, gsem, ssem, *, tme):
    i = pl.program_id(0)
    j = pl.program_id(1)
    n_used = nu_ref[0]
    slot = i % 2
    other = 1 - slot
    used = i < n_used
    chunks = _ff_chunks(wg_ref.shape[-1])
    rows_per_step = tme // EXPERT_FF_SPLITS
    n_front = len(chunks) // 2
    front_w = chunks[n_front - 1][1]
    back_w = wg_ref.shape[-1] - front_w
    gather_ranges, scatter_ranges = [], []
    for c, (a, b) in enumerate(chunks):
        if c < n_front:
            gather_ranges.append((rows_per_step * a // front_w, rows_per_step * b // front_w))
            scatter_ranges.append((0, 0))
        else:
            gather_ranges.append((0, 0))
            scatter_ranges.append((rows_per_step * (a - front_w) // back_w,
                                   rows_per_step * (b - front_w) // back_w))

    nt = ROW_TILE

    def gather(tile, slot_, r):
        src = pl.multiple_of(gsrc_ref[tile * tme + r], nt)
        return pltpu.make_async_copy(h_ref.at[pl.ds(src, nt)],
                                     xbuf.at[slot_, pl.ds(pl.multiple_of(r * nt, nt), nt)],
                                     gsem.at[slot_])

    def wait_gather(slot_):
        pltpu.make_async_copy(h_ref.at[pl.ds(0, tme * nt)], xbuf.at[slot_],
                              gsem.at[slot_]).wait()

    def scatter(block, slot_, r):
        dst = pl.multiple_of(sdst_ref[block * tme + r], nt)
        return pltpu.make_async_copy(ybuf.at[slot_, pl.ds(pl.multiple_of(r * nt, nt), nt)],
                                     yk_ref.at[pl.ds(dst, nt)], ssem.at[slot_])

    def wait_scatter(slot_):
        pltpu.make_async_copy(ybuf.at[slot_], yk_ref.at[pl.ds(0, tme * nt)],
                              ssem.at[slot_]).wait()

    @pl.when(jnp.logical_and(i == 0, j == 0))
    def _():
        ybuf[1] = jnp.zeros(ybuf.shape[1:], F32)

        def body(r, carry):
            gather(0, 0, r).start()
            return carry
        lax.fori_loop(0, tme, body, 0, unroll=8)

    @pl.when(used)
    def _():
        @pl.when(j == 0)
        def _():
            wait_gather(slot)
            hbuf[...] = _load_row_tiles(xbuf.at[slot]).astype(BF16)

            @pl.when(i > 0)
            def _():
                wait_scatter(slot)
            acc[...] = jnp.zeros(acc.shape, F32)

        h = hbuf[...]
        for c, (a, b) in enumerate(chunks):
            for r in range(*gather_ranges[c]):
                gather(i + 1, other, j * rows_per_step + r).start()
            for r in range(*scatter_ranges[c]):
                scatter(i, other, j * rows_per_step + r).start()
            gj = _dot(h, wg_ref[0, :, a:b])
            uj = _dot(h, wu_ref[0, :, a:b])
            part = _dot((_silu(gj) * uj).astype(BF16), wd_ref[0, a:b, :])
            acc[...] = acc[...] + part
            ybuf[slot, 0:ROW_TILE, :] = part[0:ROW_TILE, 0:LANES]

        @pl.when(j == EXPERT_FF_SPLITS - 1)
        def _():
            _store_row_tiles(ybuf.at[slot], acc[...])

        @pl.when(jnp.logical_and(j == EXPERT_FF_SPLITS - 1, i == n_used - 1))
        def _():
            wait_gather(other)
            wait_scatter(other)

            def body(r, carry):
                scatter(i + 1, slot, r).start()
                return carry
            lax.fori_loop(0, tme, body, 0, unroll=8)
            wait_scatter(slot)


def _expert_ffn(h_rows, tile_expert, n_used, gsrc, sdst, wg, wu, wd, tme):
    nt = ROW_TILE
    T = h_rows.shape[0] // nt
    d = nt * LANES
    dff = wg.shape[-1]
    dffh = dff // EXPERT_FF_SPLITS
    n_tiles = gsrc.shape[0] // tme
    yk_rows = TOP_K * T + tme

    def half(i, j):
        return jnp.where(i % 2 == 0, j, EXPERT_FF_SPLITS - 1 - j)

    return pl.pallas_call(
        functools.partial(_expert_kernel, tme=tme),
        out_shape=jax.ShapeDtypeStruct((yk_rows * nt, LANES), F32),
        grid_spec=pltpu.PrefetchScalarGridSpec(
            num_scalar_prefetch=4,
            grid=(n_tiles, EXPERT_FF_SPLITS),
            in_specs=[pl.BlockSpec(memory_space=pl.ANY),
                      pl.BlockSpec((1, d, dffh), lambda i, j, te, *_: (te[i], 0, half(i, j))),
                      pl.BlockSpec((1, d, dffh), lambda i, j, te, *_: (te[i], 0, half(i, j))),
                      pl.BlockSpec((1, dffh, d), lambda i, j, te, *_: (te[i], half(i, j), 0))],
            out_specs=pl.BlockSpec(memory_space=pl.ANY),
            scratch_shapes=[pltpu.VMEM((2, tme * nt, LANES), F32),
                            pltpu.VMEM((2, tme * nt, LANES), F32),
                            pltpu.VMEM((tme, d), BF16), pltpu.VMEM((tme, d), F32),
                            pltpu.SemaphoreType.DMA((2,)), pltpu.SemaphoreType.DMA((2,))]),
        compiler_params=_cparams("arbitrary", "arbitrary"),
        name="moe_experts",
    )(tile_expert, n_used, gsrc, sdst, h_rows, wg, wu, wd)


def _moe_finish_kernel(x_ref, route_ref, mod_ref, y0_ref, y1_ref, g_ref, b_ref, o_ref, *, alpha):
    route = route_ref[0]
    f = route[:, 4:5] * _load_row_tiles(y0_ref) + route[:, 5:6] * _load_row_tiles(y1_ref)
    gate = mod_ref[0, 0, 5:6, :]
    r = alpha * x_ref[0] + (1.0 + gate) * f
    o_ref[0] = _layer_norm(r, g_ref[...], b_ref[...])


def _moe_finish(x, route, mod_l, yk, ln_g, ln_b, alpha, tm):
    nb, L, d = x.shape
    tiles_per_seq = L // tm
    tiles = nb * tiles_per_seq
    return pl.pallas_call(
        functools.partial(_moe_finish_kernel, alpha=alpha),
        out_shape=jax.ShapeDtypeStruct((nb, L, d), F32),
        grid=(nb, tiles_per_seq),
        in_specs=[pl.BlockSpec((1, tm, d), lambda b, i: (b, i, 0)),
                  pl.BlockSpec((1, tm, LANES), lambda b, i: (b, i, 0)),
                  pl.BlockSpec((1, 1, 6, d), lambda b, i: (0, b, 0, 0)),
                  pl.BlockSpec((tm * ROW_TILE, LANES), lambda b, i: (b * tiles_per_seq + i, 0)),
                  pl.BlockSpec((tm * ROW_TILE, LANES),
                               lambda b, i: (tiles + b * tiles_per_seq + i, 0)),
                  pl.BlockSpec(ln_g.shape, lambda b, i: (0, 0)),
                  pl.BlockSpec(ln_b.shape, lambda b, i: (0, 0))],
        out_specs=pl.BlockSpec((1, tm, d), lambda b, i: (b, i, 0)),
        compiler_params=_cparams("arbitrary", "arbitrary"),
        name="moe_finish_ln",
    )(x, route, mod_l, yk, yk, ln_g, ln_b)


def _row_index_kernel(dest_ref, gsrc0_ref, sdst0_ref, gsrc_ref, sdst_ref, sem, *, n_pairs, T, tme):
    init_g = pltpu.make_async_copy(gsrc0_ref, gsrc_ref, sem.at[0])
    init_s = pltpu.make_async_copy(sdst0_ref, sdst_ref, sem.at[1])
    init_g.start()
    init_s.start()
    init_g.wait()
    init_s.wait()

    assert TOP_K == 2
    group = 8

    def body(g, carry):
        f0 = g * group
        rows = [dest_ref[f0 + k] for k in range(group)]
        for k in range(group):
            tok = g * (group // TOP_K) + k // TOP_K
            gsrc_ref[rows[k]] = tok * ROW_TILE
            sdst_ref[tme + rows[k]] = ((k % TOP_K) * T + tok) * ROW_TILE
        return carry
    lax.fori_loop(0, n_pairs // group, body, 0, unroll=2)


def _row_indices(dest, n_rows, T, tme):
    dump = (TOP_K * T + jnp.arange(n_rows + tme, dtype=I32) % tme) * ROW_TILE
    return pl.pallas_call(
        functools.partial(_row_index_kernel, n_pairs=dest.shape[0], T=T, tme=tme),
        out_shape=[jax.ShapeDtypeStruct((n_rows,), I32),
                   jax.ShapeDtypeStruct((n_rows + tme,), I32)],
        in_specs=[pl.BlockSpec(memory_space=pltpu.SMEM),
                  pl.BlockSpec(memory_space=pl.ANY),
                  pl.BlockSpec(memory_space=pl.ANY)],
        out_specs=[pl.BlockSpec(memory_space=pltpu.SMEM),
                   pl.BlockSpec(memory_space=pltpu.SMEM)],
        scratch_shapes=[pltpu.SemaphoreType.DMA((2,))],
        name="moe_row_indices",
    )(dest, jnp.zeros((n_rows,), I32), dump)


def _moe(attn_args, w_router, b_router, wg, wu, wd, ln_g, ln_b, alpha, tm_route, tme, tmf):
    nb, L, d = attn_args[0].shape
    mod_l = attn_args[4]
    T = nb * L
    x, h, route, counts = _router(attn_args, w_router, b_router, alpha, tm_route)
    route_flat = route.reshape(T, LANES)
    e = route_flat[:, 0:2].astype(I32)
    rank = route_flat[:, 2:4].astype(I32)
    cnt = counts[0, :N_EXPERTS].astype(I32)
    tiles = (cnt + tme - 1) // tme
    tile_end = jnp.cumsum(tiles)
    group_start = (tile_end - tiles) * tme
    dest = (group_start[e] + rank).reshape(-1)
    n_tiles = (TOP_K * T) // tme + N_EXPERTS
    n_rows = n_tiles * tme
    tile_expert = jnp.minimum(
        jnp.sum(jnp.arange(n_tiles, dtype=I32)[:, None] >= tile_end[None, :].astype(I32), axis=1),
        N_EXPERTS - 1).astype(I32)
    n_used = tile_end[-1:].astype(I32)
    gsrc, sdst = _row_indices(dest, n_rows, T, tme)
    yk = _expert_ffn(h.reshape(T * ROW_TILE, LANES), tile_expert, n_used, gsrc, sdst,
                     wg, wu, wd, tme)
    return _moe_finish(x, route, mod_l, yk, ln_g, ln_b, alpha, tmf)


def kernel(x, c, positions, rel_bias, w_ada, b_ada, w_in, w_out, conv_w, conv_b, dt_bias, a_log,
           d_skip, ssd_norm_w, sinks, ln_g, ln_b, ffn_w_gate, ffn_w_up, ffn_w_down, router_w,
           router_b, expert_w_gate, expert_w_up, expert_w_down):
    depth = w_ada.shape[0]
    nb, L, d = x.shape
    alpha = (2 * depth) ** 0.25
    rd = RET_HEADS * HEAD_DIM
    sd = SSD_HEADS * HEAD_DIM
    cd = conv_w.shape[-1]
    qd = SWA_HEADS * HEAD_DIM
    kvd = SWA_KV_HEADS * HEAD_DIM
    sizes = (rd, rd, rd, rd, sd, cd, SSD_HEADS, qd, kvd, kvd)
    offs = np.concatenate([[0], np.cumsum(sizes)])
    tl = min(512, L)

    mod = _ada_mod(c, w_ada, b_ada)
    cos_t, sin_t = _rotary_tables(positions, tl)
    bias_tab = _swa_bias_table(rel_bias)

    q_perm = np.concatenate([np.arange(h * HEAD_DIM, (h + 1) * HEAD_DIM) for h in SWA_HEAD_ORDER])
    widths = (4 * rd, sd + cd, LANES, qd + 2 * kvd)
    dtypes = (BF16, BF16, F32, BF16)

    steps = nb * (L // tl)
    n_exp, _, dff_e = expert_w_gate.shape[1:]
    per = steps // n_exp
    side_ok = (steps % n_exp == 0 and d % per == 0 and dff_e % per == 0
               and (d // per) % 8 == 0 and (dff_e // per) % 8 == 0)

    for layer in range(depth):
        wl = w_in[layer]
        seg = lambda i: wl[:, offs[i]:offs[i + 1]]
        dt_cols = jnp.zeros((d, LANES), F32).at[:, :SSD_HEADS].set(seg(6))
        w_cat = jnp.concatenate(
            [seg(0), seg(1), seg(2), seg(3), seg(4), seg(5), dt_cols,
             seg(7)[:, q_perm], seg(8), seg(9)], axis=1).astype(BF16)
        wo = w_out[layer]
        w_r = wo[0:rd].astype(BF16)
        w_s = wo[rd:rd + sd].astype(BF16)
        w_a = wo[rd + sd:][q_perm].astype(BF16)
        mod_l = mod[layer:layer + 1]

        i = layer // 2
        is_moe = layer % 2 == 1
        mix_cast = (expert_w_up[i], expert_w_down[i]) if is_moe and side_ok else ()
        y_ret, y_ssd, y_swa, *up_down_bf16 = _mixers(
            x, mod_l, w_cat, widths, dtypes, cos_t, sin_t, bias_tab, sinks[layer], conv_w[layer],
            conv_b[layer], dt_bias[layer], a_log[layer], d_skip[layer], ssd_norm_w[layer], tl,
            side_cast=mix_cast)
        attn_args = (x, y_ret, y_ssd, y_swa, mod_l, w_r, w_s, w_a,
                     ln_g[layer, 0][None, :], ln_b[layer, 0][None, :])

        g2 = ln_g[layer, 1][None, :]
        b2 = ln_b[layer, 1][None, :]
        if not is_moe:
            nxt = (layer + 1) // 2
            ride = (expert_w_gate[nxt],) if layer + 1 < depth and side_ok else ()
            x, *gate_bf16 = _dense_ffn(
                attn_args, ffn_w_gate[i].astype(BF16), ffn_w_up[i].astype(BF16),
                ffn_w_down[i].astype(BF16), g2, b2, alpha, tl, side_cast=ride)
        else:
            if side_ok:
                wg_e, = gate_bf16
                wu_e, wd_e = up_down_bf16
            else:
                wg_e, wu_e, wd_e = (expert_w_gate[i].astype(BF16), expert_w_up[i].astype(BF16),
                                    expert_w_down[i].astype(BF16))
            x = _moe(attn_args, router_w[i], router_b[i], wg_e, wu_e, wd_e, g2, b2, alpha,
                     tm_route=min(512, L), tme=min(512, L), tmf=min(512, L))
    return x
```

```python
import functools
import math

import numpy as np
import jax
import jax.numpy as jnp
from jax import lax
from jax.experimental import pallas as pl
from jax.experimental.pallas import tpu as pltpu

F32 = jnp.float32
BF16 = jnp.bfloat16
I32 = jnp.int32

HEAD_DIM = 64
CHUNK = 128
RET_HEADS = 4
SSD_HEADS = 8
SSD_GROUPS = 2
SSD_STATE = 64
SSD_CONV = 4
CONV_TAIL = 16
SWA_HEADS = 4
SWA_KV_HEADS = 2
REL_BUCKETS = 32
N_EXPERTS = 8
LN_EPS = 1e-5
LANES = 128
MASK_VALUE = -1e30

VMEM_LIMIT = 56 * 1024 * 1024


def _cparams(*sem):
    return pltpu.CompilerParams(dimension_semantics=sem, vmem_limit_bytes=VMEM_LIMIT)


def _silu(v):
    return v * (1.0 / (1.0 + jnp.exp(-v)))


def _softplus(v):
    return jnp.maximum(v, 0.0) + jnp.log(1.0 + jnp.exp(-jnp.abs(v)))


def _dot(a, b):
    return jnp.dot(a, b, preferred_element_type=F32)


def _dot_nt(a, b):
    return lax.dot_general(a, b, (((1,), (1,)), ((), ())), preferred_element_type=F32)


def _hi_lo(v):
    bits = lax.bitcast_convert_type(v, jnp.uint32) & jnp.uint32(0xFFFF0000)
    hi = lax.bitcast_convert_type(bits, F32)
    return hi.astype(BF16), (v - hi).astype(BF16)


def _split3(v):
    h1 = v.astype(BF16)
    r1 = v - h1.astype(F32)
    h2 = r1.astype(BF16)
    r2 = r1 - h2.astype(F32)
    return h1, h2, r2.astype(BF16)


def _dot3(v, m3):
    return _dot(jnp.concatenate(_split3(v), axis=1), m3)


def _dot3_left(m3, v):
    return _dot(m3, jnp.concatenate(_split3(v), axis=0))


def _dot2(v, m2):
    h1 = v.astype(BF16)
    h2 = (v - h1.astype(F32)).astype(BF16)
    return _dot(jnp.concatenate([h1, h2], axis=1), m2)


def _layer_norm(r, g, b):
    mu = jnp.mean(r, -1, keepdims=True)
    d = r - mu
    var = jnp.mean(d * d, -1, keepdims=True)
    return d * lax.rsqrt(var + LN_EPS) * g + b


def _ada_kernel(c_ref, w_ref, b_ref, o_ref):
    o_ref[0] = jnp.dot(c_ref[...], w_ref[0], preferred_element_type=F32,
                       precision=lax.Precision.HIGHEST) + b_ref[0]


def _ada_mod(c, w_ada, b_ada):
    depth, d, d6 = w_ada.shape
    nb = c.shape[0]
    rows = 8
    c_pad = jnp.zeros((rows, d), F32).at[:nb].set(c)
    out = pl.pallas_call(
        _ada_kernel,
        out_shape=jax.ShapeDtypeStruct((depth, rows, d6), F32),
        grid=(depth, d6 // d),
        in_specs=[pl.BlockSpec((rows, d), lambda l, j: (0, 0)),
                  pl.BlockSpec((1, d, d), lambda l, j: (l, 0, j)),
                  pl.BlockSpec((1, 1, d), lambda l, j: (l, 0, j))],
        out_specs=pl.BlockSpec((1, rows, d), lambda l, j: (l, 0, j)),
        compiler_params=_cparams("arbitrary", "arbitrary"),
        name="ada_mod",
    )(c_pad, w_ada, b_ada.reshape(depth, 1, d6))
    return out[:, :nb].reshape(depth, nb, 6, d)


def _rotary_kernel(pos_ref, cos_ref, sin_ref):
    half = HEAD_DIM // 2
    lane = lax.broadcasted_iota(I32, (1, LANES), 1)
    jj = lane % HEAD_DIM
    idx = (jj % half).astype(F32)
    inv = jnp.exp(-math.log(10000.0) * idx / half)
    ang = pos_ref[0].astype(F32) * inv
    cos_ref[0] = jnp.cos(ang)
    sin_ref[0] = jnp.where(jj < half, -1.0, 1.0) * jnp.sin(ang)


def _rotary_tables(positions, tl):
    nb, L = positions.shape
    pos = positions.reshape(nb, L, 1)
    return pl.pallas_call(
        _rotary_kernel,
        out_shape=[jax.ShapeDtypeStruct((nb, L, LANES), F32)] * 2,
        grid=(nb, L // tl),
        in_specs=[pl.BlockSpec((1, tl, 1), lambda b, i: (b, i, 0))],
        out_specs=[pl.BlockSpec((1, tl, LANES), lambda b, i: (b, i, 0))] * 2,
        compiler_params=_cparams("arbitrary", "arbitrary"),
        name="rotary_tables",
    )(pos)


def _swa_bias_kernel(rb_ref, bucket_ref, band_ref, o_ref):
    bucket = bucket_ref[...]
    band = band_ref[...]
    for h in range(SWA_HEADS):
        acc = jnp.zeros(bucket.shape, F32)
        for b in range(REL_BUCKETS):
            acc = jnp.where(bucket == b, rb_ref[b, h], acc)
        o_ref[h] = jnp.where(band > 0, acc, MASK_VALUE)


def _t5_bucket(dist):
    exact = REL_BUCKETS // 2
    df = jnp.maximum(dist, 1).astype(F32)
    large = exact + (jnp.log(df / exact) / math.log(CHUNK / exact) * (REL_BUCKETS - exact)).astype(I32)
    large = jnp.minimum(large, REL_BUCKETS - 1)
    return jnp.where(dist < exact, dist, large)


def _swa_bias_table(rel_bias):
    W = CHUNK
    qi = jnp.arange(W)[:, None]
    kj = jnp.arange(2 * W)[None, :]
    dist = qi + W - kj
    band = ((dist >= 0) & (dist < W)).astype(I32)
    bucket = _t5_bucket(jnp.clip(dist, 0, W - 1)).astype(I32)
    return pl.pallas_call(
        _swa_bias_kernel,
        out_shape=jax.ShapeDtypeStruct((SWA_HEADS, W, 2 * W), F32),
        in_specs=[pl.BlockSpec(memory_space=pltpu.SMEM),
                  pl.BlockSpec(memory_space=pltpu.VMEM),
                  pl.BlockSpec(memory_space=pltpu.VMEM)],
        out_specs=pl.BlockSpec(memory_space=pltpu.VMEM),
        name="swa_bias_table",
    )(rel_bias, bucket, band)


def _head_lane_mask(width, head):
    lane = lax.broadcasted_iota(I32, (1, width), 1)
    return (lane // HEAD_DIM) == head


def _rotate_half(t):
    width = t.shape[-1]
    lane = lax.broadcasted_iota(I32, (1, width), 1)
    half = HEAD_DIM // 2
    fwd = pltpu.roll(t, width - half, axis=1)
    bwd = pltpu.roll(t, half, axis=1)
    return jnp.where((lane % HEAD_DIM) < half, fwd, bwd)


def _retention_body(u_ref, cos_ref, sin_ref, din_ref, dq_ref, dk_ref, dc_ref,
                    bmask_ref, avg_ref, o_ref, state_ref, *, n_chunks):
    rd = RET_HEADS * HEAD_DIM
    masks = [_head_lane_mask(rd, h) for h in range(RET_HEADS)]

    def stack_heads(t):
        return jnp.concatenate([jnp.where(m, t, 0.0) for m in masks], axis=0).astype(BF16)

    state = state_ref[...]
    for ci in range(n_chunks):
        rows = slice(ci * CHUNK, (ci + 1) * CHUNK)
        cos = cos_ref[0, rows, :]
        sin = sin_ref[0, rows, :]
        cos2 = jnp.concatenate([cos, cos], axis=1)
        sin2 = jnp.concatenate([sin, sin], axis=1)
        q = u_ref[0, rows, 0:rd].astype(F32)
        k = u_ref[0, rows, rd:2 * rd].astype(F32)
        v = u_ref[0, rows, 2 * rd:3 * rd].astype(F32)
        g = u_ref[0, rows, 3 * rd:4 * rd].astype(F32)
        qr = q * cos2 + _rotate_half(q) * sin2
        kr = (k * cos2 + _rotate_half(k) * sin2) * (HEAD_DIM ** -0.5)
        scores = _dot_nt(qr.astype(BF16), stack_heads(kr)) * din_ref[...]
        inner = _dot(scores.astype(BF16), stack_heads(v))
        cross = _dot((qr * dq_ref[...]).astype(BF16), state.astype(BF16))
        o = inner + cross
        kd_t = (kr * dk_ref[...]).T.astype(BF16)
        kv = _dot(kd_t, v.astype(BF16))
        state = dc_ref[...] * state + bmask_ref[...] * kv
        mu = _dot2(o, avg_ref[...])
        dev = o - mu
        var = _dot2(dev * dev, avg_ref[...])
        on = dev * lax.rsqrt(var + LN_EPS)
        o_ref[0, rows, :] = (_silu(g) * on).astype(o_ref.dtype)
        yield
    state_ref[...] = state


def _retention_tables():
    H, d, C = RET_HEADS, HEAD_DIM, CHUNK
    log_gamma = jnp.log(1.0 - 2.0 ** (-5.0 - jnp.arange(H, dtype=F32)))
    idx = jnp.arange(C, dtype=F32)
    diff = idx[:, None] - idx[None, :]
    decay_in = jnp.where(diff >= 0, jnp.exp(log_gamma[:, None, None] * jnp.maximum(diff, 0.0)), 0.0)
    decay_q = jnp.exp(log_gamma[:, None] * (idx + 1.0))
    decay_k = jnp.exp(log_gamma[:, None] * (C - 1.0 - idx))
    decay_chunk = jnp.exp(log_gamma * C)
    din = decay_in.transpose(1, 0, 2).reshape(C, H * C)
    dq = jnp.repeat(decay_q.T, d, axis=1)
    dk = jnp.repeat(decay_k.T, d, axis=1)
    dc = jnp.repeat(decay_chunk, d)[None, :]
    head = jnp.arange(H * d) // d
    bmask = (head[:, None] == head[None, :]).astype(F32)
    avg = jnp.tile((bmask / d).astype(BF16), (2, 1))
    return din, dq, dk, dc, bmask, avg


def _swa_body(u_ref, bias_ref, sink_ref, o_ref, kprev_ref, vprev_ref, *, n_chunks, first_step):
    W = CHUNK
    qd = SWA_HEADS * HEAD_DIM
    kvd = SWA_KV_HEADS * HEAD_DIM
    lane = lax.broadcasted_iota(I32, (1, LANES), 1)
    low = lane < HEAD_DIM
    col = lax.broadcasted_iota(I32, (1, 2 * W), 1)

    kprev = kprev_ref[...]
    vprev = vprev_ref[...]
    sink = sink_ref[...]
    for ci in range(n_chunks):
        rows = slice(ci * W, (ci + 1) * W)
        qa = u_ref[0, rows, 0:LANES].astype(F32)
        qb = u_ref[0, rows, LANES:qd].astype(F32)
        k = u_ref[0, rows, qd:qd + kvd].astype(BF16)
        v = u_ref[0, rows, qd + kvd:qd + 2 * kvd].astype(BF16)
        q4 = jnp.concatenate([jnp.where(low, qa, 0.0), jnp.where(low, 0.0, qa),
                              jnp.where(low, qb, 0.0), jnp.where(low, 0.0, qb)],
                             axis=0).astype(BF16)
        kband = jnp.concatenate([kprev, k], axis=0)
        vband = jnp.concatenate([vprev, v], axis=0)
        logits = _dot_nt(q4, kband) * (HEAD_DIM ** -0.5) + bias_ref[...]
        if ci == 0:
            logits = jnp.where(jnp.logical_and(first_step, col < W), MASK_VALUE, logits)
        m = jnp.maximum(jnp.max(logits, -1, keepdims=True), sink)
        p = jnp.exp(logits - m)
        denom = jnp.sum(p, -1, keepdims=True) + jnp.exp(sink - m)
        res = _dot(p.astype(BF16), vband) / denom
        out_a = jnp.where(low, res[0:W], res[W:2 * W])
        out_b = jnp.where(low, res[2 * W:3 * W], res[3 * W:4 * W])
        o_ref[0, rows, 0:LANES] = out_a.astype(o_ref.dtype)
        o_ref[0, rows, LANES:qd] = out_b.astype(o_ref.dtype)
        kprev, vprev = k, v
        yield
    kprev_ref[...] = kprev
    vprev_ref[...] = vprev


SWA_HEAD_ORDER = (0, 2, 1, 3)


def _ssd_body(u_ref, dt_ref, cw_ref, cb_ref, dtb_c_ref, alog_c_ref, dskip_ref, nw_ref,
              tril_ref, expand_ref, gmask_ref, shift_ref,
              o_ref, state_ref, ext_ref, *, n_chunks):
    C = CHUNK
    sd = SSD_HEADS * HEAD_DIM
    gn = SSD_GROUPS * SSD_STATE
    cd = sd + 2 * gn
    tl = n_chunks * C
    slab = 2 * LANES
    heads_per_group = SSD_HEADS // SSD_GROUPS
    heads_per_slab = slab // HEAD_DIM
    lane = lax.broadcasted_iota(I32, (1, LANES), 1)
    low = lane < SSD_STATE
    slab_masks = [_head_lane_mask(slab, hh) for hh in range(heads_per_slab)]
    row_i = lax.broadcasted_iota(I32, (C, C), 0)
    col_i = lax.broadcasted_iota(I32, (C, C), 1)
    causal = row_i >= col_i
    neg_a_c = -jnp.exp(alog_c_ref[...])

    ext_ref[CONV_TAIL:CONV_TAIL + tl, :] = u_ref[0, :, sd:sd + cd]
    state = state_ref[...]
    for ci in range(n_chunks):
        rows = slice(ci * C, (ci + 1) * C)
        z = u_ref[0, rows, 0:sd].astype(F32)
        dt_raw = dt_ref[0, rows, :]
        window = ext_ref[ci * C:ci * C + CONV_TAIL + C, :]
        shifted = _dot(shift_ref[...], window)
        conv = cb_ref[...] + cw_ref[SSD_CONV - 1:SSD_CONV, :] * window[CONV_TAIL:, :].astype(F32)
        for w in range(SSD_CONV - 1):
            conv = conv + cw_ref[w:w + 1, :] * shifted[w * C:(w + 1) * C, :]
        xbc = _silu(conv)
        xs = xbc[:, 0:sd]
        bm = xbc[:, sd:sd + gn]
        cm = xbc[:, sd + gn:cd]

        dt_c = _softplus(dt_raw + dtb_c_ref[...])
        a_c = neg_a_c * dt_c
        acs_c = _dot3_left(tril_ref[...], a_c)
        acs_t = acs_c.T
        spread = _dot3(acs_c, expand_ref[...])
        acs_x = spread[:, 0:sd]
        dt_x = _dot3(dt_c, expand_ref[:, 0:sd])
        xdt = xs * dt_x
        yield

        bstack = jnp.concatenate([jnp.where(low, bm, 0.0), jnp.where(low, 0.0, bm)],
                                 axis=0).astype(BF16)
        cb = _dot_nt(cm.astype(BF16), bstack)
        y_diag = []
        for s in range(sd // slab):
            ms = []
            for hh in range(heads_per_slab):
                h = s * heads_per_slab + hh
                g = h // heads_per_group
                col_bcast = spread[:, sd + h * LANES:sd + (h + 1) * LANES]
                seg = col_bcast - acs_t[h:h + 1, :]
                lmat = jnp.exp(jnp.where(causal, seg, MASK_VALUE))
                ms.append((cb[:, g * C:(g + 1) * C] * lmat).astype(BF16))
            xslab = xdt[:, s * slab:(s + 1) * slab]
            xstack = jnp.concatenate([jnp.where(m, xslab, 0.0) for m in slab_masks],
                                     axis=0).astype(BF16)
            y_diag.append(_dot(jnp.concatenate(ms, axis=1), xstack))
        y_diag = jnp.concatenate(y_diag, axis=1)
        yield

        y_off = _dot(cm.astype(BF16), state.astype(BF16)) * jnp.exp(acs_x)
        last = acs_x[C - 1:C, :]
        dec = jnp.exp(last - acs_x)
        new = _dot(bm.T.astype(BF16), (xdt * dec).astype(BF16))
        state = jnp.exp(last) * state + gmask_ref[...] * new

        y = y_diag + y_off + xs * dskip_ref[...]
        hgate = y * _silu(z)
        gw = sd // SSD_GROUPS
        for g in range(SSD_GROUPS):
            hg = hgate[:, g * gw:(g + 1) * gw]
            ms_ = jnp.mean(hg * hg, -1, keepdims=True)
            o_ref[0, rows, g * gw:(g + 1) * gw] = (
                hg * lax.rsqrt(ms_ + LN_EPS) * nw_ref[:, g * gw:(g + 1) * gw]).astype(o_ref.dtype)
        yield
    state_ref[...] = state
    ext_ref[0:CONV_TAIL, :] = ext_ref[tl:tl + CONV_TAIL, :]


def _ssd_tables():
    C = CHUNK
    sd = SSD_HEADS * HEAD_DIM
    gn = SSD_GROUPS * SSD_STATE
    t = jnp.arange(C)
    tril = (t[:, None] >= t[None, :]).astype(BF16)
    r = jnp.arange(LANES)[:, None]
    eexp = ((r == (jnp.arange(sd)[None, :] // HEAD_DIM)) & (r < SSD_HEADS)).astype(BF16)
    bsel = ((r == (jnp.arange(SSD_HEADS * LANES)[None, :] // LANES)) & (r < SSD_HEADS)).astype(BF16)
    heads_per_group = SSD_HEADS // SSD_GROUPS
    row_g = jnp.arange(gn)[:, None] // SSD_STATE
    col_g = (jnp.arange(sd)[None, :] // HEAD_DIM) // heads_per_group
    gmask = (row_g == col_g).astype(F32)
    expand = jnp.concatenate([eexp, bsel], axis=1)
    win = jnp.arange(CONV_TAIL + C)[None, :]
    shifts = jnp.concatenate(
        [(win == CONV_TAIL + t[:, None] - (SSD_CONV - 1 - s)).astype(BF16)
         for s in range(SSD_CONV - 1)], axis=0)
    return jnp.tile(tril, (1, 3)), jnp.tile(expand, (3, 1)), gmask, shifts


N_RET_TABLES = 6
N_SSD_CONSTS = 10


def _in_proj_body(x_ref, shift, scale, w_ref, out_refs):
    h = (x_ref[0] * (1.0 + scale) + shift).astype(BF16)
    off = 0
    for ref in out_refs:
        width = ref.shape[-1]
        for a, b in _ff_chunks(width):
            ref[0, :, a:b] = _dot(h, w_ref[:, off + a:off + b]).astype(ref.dtype)
            yield
        off += width


def _mixers_kernel(*refs, n_chunks, n_cast, n_tiles, tiles_per_seq):
    x_ref, mod_ref, w_ref, cos, sin = refs[:5]
    pos = 5
    ret_tables = refs[pos:pos + N_RET_TABLES]
    pos += N_RET_TABLES
    ssd_consts = refs[pos:pos + N_SSD_CONSTS]
    pos += N_SSD_CONSTS
    bias, sink = refs[pos:pos + 2]
    pos += 2
    cast_in = refs[pos:pos + n_cast]
    pos += n_cast
    y_ret, y_ssd, y_swa = refs[pos:pos + 3]
    pos += 3
    cast_out = refs[pos:pos + n_cast]
    pos += n_cast
    ret_state, ssd_state, ssd_ext, kprev, vprev = refs[pos:pos + 5]
    cur = refs[pos + 5:pos + 9]
    nxt = refs[pos + 9:pos + 13]
    _side_cast(cast_in, cast_out)

    s = pl.program_id(0)
    tile = jnp.maximum(s - 1, 0)
    seq_start = tile % tiles_per_seq == 0

    @pl.when(s == 0)
    def _():
        for ref in cur:
            ref[...] = jnp.zeros_like(ref)

    @pl.when(seq_start)
    def _():
        ret_state[...] = jnp.zeros_like(ret_state)
        ssd_state[...] = jnp.zeros_like(ssd_state)
        ssd_ext[0:CONV_TAIL, :] = jnp.zeros((CONV_TAIL, ssd_ext.shape[1]), ssd_ext.dtype)
        kprev[...] = jnp.zeros_like(kprev)
        vprev[...] = jnp.zeros_like(vprev)

    batch = jnp.minimum(s, n_tiles - 1) // tiles_per_seq
    shift = mod_ref[0, batch, 0:1, :]
    scale = mod_ref[0, batch, 1:2, :]
    u_ret, u_ssd, u_dt, u_swa = cur
    bodies = [
        _ssd_body(u_ssd, u_dt, *ssd_consts, y_ssd, ssd_state, ssd_ext, n_chunks=n_chunks),
        _in_proj_body(x_ref, shift, scale, w_ref, nxt),
        _swa_body(u_swa, bias, sink, y_swa, kprev, vprev, n_chunks=n_chunks,
                  first_step=seq_start),
        _retention_body(u_ret, cos, sin, *ret_tables, y_ret, ret_state, n_chunks=n_chunks),
    ]
    alive = list(bodies)
    while alive:
        alive = [b for b in alive if next(b, StopIteration) is not StopIteration]
    for c_ref, n_ref in zip(cur, nxt):
        c_ref[...] = n_ref[...]


def _mixers(x, mod_l, w_cat, widths, dtypes, cos_t, sin_t, bias_tab, sinks_l, conv_w, conv_b,
            dt_bias, a_log, d_skip, norm_w, tl, side_cast=()):
    nb, L, d = x.shape
    W = CHUNK
    rd = RET_HEADS * HEAD_DIM
    sd = SSD_HEADS * HEAD_DIM
    qd = SWA_HEADS * HEAD_DIM
    cd = conv_w.shape[-1]
    ret_tables = _retention_tables()
    pad = lambda v: jnp.zeros((1, LANES), F32).at[0, :SSD_HEADS].set(v)
    rep = lambda v: jnp.repeat(v, HEAD_DIM)[None, :]
    ssd_consts = (conv_w, conv_b[None, :], pad(dt_bias), pad(a_log), rep(d_skip),
                  norm_w[None, :]) + _ssd_tables()
    order = jnp.array(SWA_HEAD_ORDER)
    bias_stacked = bias_tab[order].reshape(SWA_HEADS * W, 2 * W)
    sink_col = jnp.repeat(sinks_l.astype(F32)[order], W)[:, None]
    assert len(ret_tables) == N_RET_TABLES and len(ssd_consts) == N_SSD_CONSTS
    tps = L // tl
    n_tiles = nb * tps

    def tile_of(step):
        return step // tps, step % tps, 0

    proj_map = lambda s: tile_of(jnp.minimum(s, n_tiles - 1))
    mix_map = lambda s: tile_of(jnp.maximum(s - 1, 0))
    const = lambda a: pl.BlockSpec(a.shape, lambda s: (0,) * a.ndim)
    consts = ret_tables + ssd_consts + (bias_stacked, sink_col)
    c_in, c_out, c_shapes = _side_cast_specs(side_cast, n_tiles)
    u_bufs = [pltpu.VMEM((1, tl, w), t) for w, t in zip(widths, dtypes)]
    return pl.pallas_call(
        functools.partial(_mixers_kernel, n_chunks=tl // CHUNK, n_cast=len(side_cast),
                          n_tiles=n_tiles, tiles_per_seq=tps),
        out_shape=[jax.ShapeDtypeStruct((nb, L, w), BF16) for w in (rd, sd, qd)] + c_shapes,
        grid=(n_tiles + 1,),
        in_specs=[pl.BlockSpec((1, tl, d), proj_map), const(mod_l), const(w_cat),
                  pl.BlockSpec((1, tl, LANES), mix_map), pl.BlockSpec((1, tl, LANES), mix_map)]
                 + [const(a) for a in consts] + c_in,
        out_specs=[pl.BlockSpec((1, tl, w), mix_map) for w in (rd, sd, qd)] + c_out,
        scratch_shapes=[pltpu.VMEM((rd, rd), F32),
                        pltpu.VMEM((SSD_GROUPS * SSD_STATE, sd), F32),
                        pltpu.VMEM((CONV_TAIL + tl, cd), BF16),
                        pltpu.VMEM((W, LANES), BF16), pltpu.VMEM((W, LANES), BF16)]
                       + u_bufs + u_bufs,
        compiler_params=_cparams("arbitrary"),
        name="in_proj_mixers",
    )(x, mod_l, w_cat, cos_t, sin_t, *consts, *side_cast)


N_ATTN_OUT = 10


def _attn_out(x_ref, yr_ref, ys_ref, ya_ref, mod_ref, wr_ref, ws_ref, wa_ref, g_ref, b_ref,
              alpha):
    mix = (_dot(yr_ref[0], wr_ref[...]) + _dot(ys_ref[0], ws_ref[...])
           + _dot(ya_ref[0], wa_ref[...]))
    gate = mod_ref[0, 0, 2:3, :]
    r = alpha * x_ref[0] + (1.0 + gate) * mix
    return _layer_norm(r, g_ref[...], b_ref[...])


def _attn_out_operands(x, y_ret, y_ssd, y_swa, mod_l, w_r, w_s, w_a, ln_g, ln_b, tm):
    d = x.shape[-1]
    tok = lambda w: pl.BlockSpec((1, tm, w), lambda b, i: (b, i, 0))
    const = lambda a: pl.BlockSpec(a.shape, lambda b, i: (0,) * a.ndim)
    specs = [tok(d), tok(y_ret.shape[-1]), tok(y_ssd.shape[-1]), tok(y_swa.shape[-1]),
             pl.BlockSpec((1, 1, 6, d), lambda b, i: (0, b, 0, 0)),
             const(w_r), const(w_s), const(w_a), const(ln_g), const(ln_b)]
    return specs, (x, y_ret, y_ssd, y_swa, mod_l, w_r, w_s, w_a, ln_g, ln_b)


FF_CHUNK = 256


def _side_cast_specs(arrays, steps, steps_per_seq=None):
    in_specs, out_specs, out_shapes = [], [], []
    for a in arrays:
        e, r, c = a.shape
        per = steps // e
        if steps_per_seq is None:
            index = lambda s, per=per: (jnp.minimum(s, steps - 1) // per,
                                        jnp.minimum(s, steps - 1) % per, 0)
        else:
            index = lambda b, i, per=per: ((b * steps_per_seq + i) // per,
                                           (b * steps_per_seq + i) % per, 0)
        spec = pl.BlockSpec((1, r // per, c), index)
        in_specs.append(spec)
        out_specs.append(spec)
        out_shapes.append(jax.ShapeDtypeStruct(a.shape, BF16))
    return in_specs, out_specs, out_shapes


def _side_cast(in_refs, out_refs):
    for i_ref, o_ref in zip(in_refs, out_refs):
        o_ref[...] = i_ref[...].astype(o_ref.dtype)


def _dense_ffn_kernel(*refs, alpha):
    attn = refs[:N_ATTN_OUT]
    mod_ref = attn[4]
    wg_ref, wu_ref, wd_ref, g_ref, b_ref = refs[N_ATTN_OUT:N_ATTN_OUT + 5]
    rest = refs[N_ATTN_OUT + 5:]
    n_cast = (len(rest) - 1) // 2
    o_ref = rest[n_cast]
    _side_cast(rest[:n_cast], rest[n_cast + 1:])
    x = _attn_out(*attn, alpha)
    sh = mod_ref[0, 0, 3:4, :]
    sc = mod_ref[0, 0, 4:5, :]
    gate = mod_ref[0, 0, 5:6, :]
    h = (x * (1.0 + sc) + sh).astype(BF16)
    dff = wg_ref.shape[-1]
    acc = jnp.zeros(x.shape, F32)
    for j in range(0, dff, FF_CHUNK):
        gj = _dot(h, wg_ref[:, j:j + FF_CHUNK])
        uj = _dot(h, wu_ref[:, j:j + FF_CHUNK])
        acc = acc + _dot((_silu(gj) * uj).astype(BF16), wd_ref[j:j + FF_CHUNK, :])
    r = alpha * x + (1.0 + gate) * acc
    o_ref[0] = _layer_norm(r, g_ref[...], b_ref[...])


def _dense_ffn(attn_args, wg, wu, wd, ln_g, ln_b, alpha, tm, side_cast=()):
    nb, L, d = attn_args[0].shape
    const = lambda a: pl.BlockSpec(a.shape, lambda b, i: (0,) * a.ndim,
                                   pipeline_mode=pl.Buffered(1))
    a_specs, a_ops = _attn_out_operands(*attn_args, tm)
    c_in, c_out, c_shapes = _side_cast_specs(side_cast, nb * (L // tm), L // tm)
    return pl.pallas_call(
        functools.partial(_dense_ffn_kernel, alpha=alpha),
        out_shape=[jax.ShapeDtypeStruct((nb, L, d), F32)] + c_shapes,
        grid=(nb, L // tm),
        in_specs=a_specs + [const(wg), const(wu), const(wd), const(ln_g), const(ln_b)] + c_in,
        out_specs=[pl.BlockSpec((1, tm, d), lambda b, i: (b, i, 0))] + c_out,
        compiler_params=_cparams("arbitrary", "arbitrary"),
        name="attn_out_dense_ffn_ln",
    )(*a_ops, wg, wu, wd, ln_g, ln_b, *side_cast)


def _store_row_tiles(ref, value):
    rows, d = value.shape
    nt = d // LANES
    for c in range(nt):
        ref[pl.ds(c, rows, stride=nt), :] = value[:, c * LANES:(c + 1) * LANES].astype(ref.dtype)


def _load_row_tiles(ref):
    nt = ROW_TILE
    rows = ref.shape[0] // nt
    return jnp.concatenate([ref[pl.ds(c, rows, stride=nt), :] for c in range(nt)], axis=1)


def _router_kernel(*refs, alpha):
    attn = refs[:N_ATTN_OUT]
    mod_ref = attn[4]
    wr_ref, br_ref, trils_ref, x1_ref, h_ref, route_ref, cnt_ref, base_ref = refs[N_ATTN_OUT:]
    first = jnp.logical_and(pl.program_id(0) == 0, pl.program_id(1) == 0)

    @pl.when(first)
    def _():
        base_ref[...] = jnp.zeros_like(base_ref)

    x1 = _attn_out(*attn, alpha)
    x1_ref[0] = x1
    sh = mod_ref[0, 0, 3:4, :]
    sc = mod_ref[0, 0, 4:5, :]
    h = x1 * (1.0 + sc) + sh
    _store_row_tiles(h_ref.at[0], h)
    h_hi, h_lo = _hi_lo(h)
    wide = _dot(h_hi, wr_ref[...]) + _dot(h_lo, wr_ref[...])
    logits = wide + pltpu.roll(wide, LANES - N_EXPERTS, axis=1) + br_ref[...]
    lane = lax.broadcasted_iota(I32, logits.shape, 1).astype(F32)
    logits = jnp.where(lane < N_EXPERTS, logits, MASK_VALUE)
    v1 = jnp.max(logits, -1, keepdims=True)
    e1 = jnp.min(jnp.where(logits == v1, lane, float(LANES)), -1, keepdims=True)
    rest = jnp.where(lane == e1, MASK_VALUE, logits)
    v2 = jnp.max(rest, -1, keepdims=True)
    e2 = jnp.min(jnp.where(rest == v2, lane, float(LANES)), -1, keepdims=True)
    t = jnp.exp(v2 - v1)
    w1 = 1.0 / (1.0 + t)
    w2 = t / (1.0 + t)
    hot1 = (lane == e1).astype(F32)
    hot2 = (lane == e2).astype(F32)
    both = hot1 + hot2
    base = base_ref[0:1, :]
    before = _dot(trils_ref[...], both.astype(BF16)) + base
    rank1 = jnp.sum(hot1 * before, -1, keepdims=True)
    rank2 = jnp.sum(hot2 * before, -1, keepdims=True)
    total = base + jnp.sum(both, 0, keepdims=True)
    base_ref[0:1, :] = total
    cnt_ref[...] = jnp.broadcast_to(total, cnt_ref.shape)
    route = jnp.where(lane == 0, e1, 0.0)
    route = jnp.where(lane == 1, e2, route)
    route = jnp.where(lane == 2, rank1, route)
    route = jnp.where(lane == 3, rank2, route)
    route = jnp.where(lane == 4, w1, route)
    route = jnp.where(lane == 5, w2, route)
    route_ref[0] = route


def _router(attn_args, w_router, b_router, alpha, tm):
    nb, L, d = attn_args[0].shape
    w_hi, w_lo = _hi_lo(w_router)
    wr = (jnp.zeros((d, LANES), BF16).at[:, :N_EXPERTS].set(w_hi)
          .at[:, N_EXPERTS:2 * N_EXPERTS].set(w_lo))
    br = jnp.zeros((1, LANES), F32).at[0, :N_EXPERTS].set(b_router)
    t = jnp.arange(tm)
    tril_strict = (t[:, None] > t[None, :]).astype(BF16)
    const = lambda a: pl.BlockSpec(a.shape, lambda b, i: (0,) * a.ndim)
    a_specs, a_ops = _attn_out_operands(*attn_args, tm)
    return pl.pallas_call(
        functools.partial(_router_kernel, alpha=alpha),
        out_shape=[jax.ShapeDtypeStruct((nb, L, d), F32),
                   jax.ShapeDtypeStruct((nb, L * (d // LANES), LANES), F32),
                   jax.ShapeDtypeStruct((nb, L, LANES), F32),
                   jax.ShapeDtypeStruct((8, LANES), F32)],
        grid=(nb, L // tm),
        in_specs=a_specs + [const(wr), const(br), const(tril_strict)],
        out_specs=[pl.BlockSpec((1, tm, d), lambda b, i: (b, i, 0)),
                   pl.BlockSpec((1, tm * (d // LANES), LANES), lambda b, i: (b, i, 0)),
                   pl.BlockSpec((1, tm, LANES), lambda b, i: (b, i, 0)),
                   pl.BlockSpec((8, LANES), lambda b, i: (0, 0))],
        scratch_shapes=[pltpu.VMEM((8, LANES), F32)],
        compiler_params=_cparams("arbitrary", "arbitrary"),
        name="attn_out_moe_router",
    )(*a_ops, wr, br, tril_strict)


TOP_K = 2
ROW_TILE = 8
EXPERT_FF_SPLITS = 2


def _ff_chunks(width, step=512):
    return [(a, min(a + step, width)) for a in range(0, width, step)]


def _expert_kernel(te_ref, nu_ref, gsrc_ref, sdst_ref,
                   h_ref, wg_ref, wu_ref, wd_ref, yk_ref,
                   xbuf, ybuf, hbuf, acc, gsem, ssem, *, tme):
    i = pl.program_id(0)
    j = pl.program_id(1)
    n_used = nu_ref[0]
    slot = i % 2
    other = 1 - slot
    used = i < n_used
    chunks = _ff_chunks(wg_ref.shape[-1])
    rows_per_step = tme // EXPERT_FF_SPLITS
    n_front = len(chunks) // 2
    front_w = chunks[n_front - 1][1]
    back_w = wg_ref.shape[-1] - front_w
    gather_ranges, scatter_ranges = [], []
    for c, (a, b) in enumerate(chunks):
        if c < n_front:
            gather_ranges.append((rows_per_step * a // front_w, rows_per_step * b // front_w))
            scatter_ranges.append((0, 0))
        else:
            gather_ranges.append((0, 0))
            scatter_ranges.append((rows_per_step * (a - front_w) // back_w,
                                   rows_per_step * (b - front_w) // back_w))

    nt = ROW_TILE

    def gather(tile, slot_, r):
        src = pl.multiple_of(gsrc_ref[tile * tme + r], nt)
        return pltpu.make_async_copy(h_ref.at[pl.ds(src, nt)],
                                     xbuf.at[slot_, pl.ds(pl.multiple_of(r * nt, nt), nt)],
                                     gsem.at[slot_])

    def wait_gather(slot_):
        pltpu.make_async_copy(h_ref.at[pl.ds(0, tme * nt)], xbuf.at[slot_],
                              gsem.at[slot_]).wait()

    def scatter(block, slot_, r):
        dst = pl.multiple_of(sdst_ref[block * tme + r], nt)
        return pltpu.make_async_copy(ybuf.at[slot_, pl.ds(pl.multiple_of(r * nt, nt), nt)],
                                     yk_ref.at[pl.ds(dst, nt)], ssem.at[slot_])

    def wait_scatter(slot_):
        pltpu.make_async_copy(ybuf.at[slot_], yk_ref.at[pl.ds(0, tme * nt)],
                              ssem.at[slot_]).wait()

    @pl.when(jnp.logical_and(i == 0, j == 0))
    def _():
        ybuf[1] = jnp.zeros(ybuf.shape[1:], F32)

        def body(r, carry):
            gather(0, 0, r).start()
            return carry
        lax.fori_loop(0, tme, body, 0, unroll=8)

    @pl.when(used)
    def _():
        @pl.when(j == 0)
        def _():
            wait_gather(slot)
            hbuf[...] = _load_row_tiles(xbuf.at[slot]).astype(BF16)

            @pl.when(i > 0)
            def _():
                wait_scatter(slot)
            acc[...] = jnp.zeros(acc.shape, F32)

        h = hbuf[...]
        for c, (a, b) in enumerate(chunks):
            for r in range(*gather_ranges[c]):
                gather(i + 1, other, j * rows_per_step + r).start()
            for r in range(*scatter_ranges[c]):
                scatter(i, other, j * rows_per_step + r).start()
            gj = _dot(h, wg_ref[0, :, a:b])
            uj = _dot(h, wu_ref[0, :, a:b])
            part = _dot((_silu(gj) * uj).astype(BF16), wd_ref[0, a:b, :])
            acc[...] = acc[...] + part
            ybuf[slot, 0:ROW_TILE, :] = part[0:ROW_TILE, 0:LANES]

        @pl.when(j == EXPERT_FF_SPLITS - 1)
        def _():
            _store_row_tiles(ybuf.at[slot], acc[...])

        @pl.when(jnp.logical_and(j == EXPERT_FF_SPLITS - 1, i == n_used - 1))
        def _():
            wait_gather(other)
            wait_scatter(other)

            def body(r, carry):
                scatter(i + 1, slot, r).start()
                return carry
            lax.fori_loop(0, tme, body, 0, unroll=8)
            wait_scatter(slot)


def _expert_ffn(h_rows, tile_expert, n_used, gsrc, sdst, wg, wu, wd, tme):
    nt = ROW_TILE
    T = h_rows.shape[0] // nt
    d = nt * LANES
    dff = wg.shape[-1]
    dffh = dff // EXPERT_FF_SPLITS
    n_tiles = gsrc.shape[0] // tme
    yk_rows = TOP_K * T + tme

    def half(i, j):
        return jnp.where(i % 2 == 0, j, EXPERT_FF_SPLITS - 1 - j)

    return pl.pallas_call(
        functools.partial(_expert_kernel, tme=tme),
        out_shape=jax.ShapeDtypeStruct((yk_rows * nt, LANES), F32),
        grid_spec=pltpu.PrefetchScalarGridSpec(
            num_scalar_prefetch=4,
            grid=(n_tiles, EXPERT_FF_SPLITS),
            in_specs=[pl.BlockSpec(memory_space=pl.ANY),
                      pl.BlockSpec((1, d, dffh), lambda i, j, te, *_: (te[i], 0, half(i, j))),
                      pl.BlockSpec((1, d, dffh), lambda i, j, te, *_: (te[i], 0, half(i, j))),
                      pl.BlockSpec((1, dffh, d), lambda i, j, te, *_: (te[i], half(i, j), 0))],
            out_specs=pl.BlockSpec(memory_space=pl.ANY),
            scratch_shapes=[pltpu.VMEM((2, tme * nt, LANES), F32),
                            pltpu.VMEM((2, tme * nt, LANES), F32),
                            pltpu.VMEM((tme, d), BF16), pltpu.VMEM((tme, d), F32),
                            pltpu.SemaphoreType.DMA((2,)), pltpu.SemaphoreType.DMA((2,))]),
        compiler_params=_cparams("arbitrary", "arbitrary"),
        name="moe_experts",
    )(tile_expert, n_used, gsrc, sdst, h_rows, wg, wu, wd)


def _moe_finish_kernel(x_ref, route_ref, mod_ref, y0_ref, y1_ref, g_ref, b_ref, o_ref, *, alpha):
    route = route_ref[0]
    f = route[:, 4:5] * _load_row_tiles(y0_ref) + route[:, 5:6] * _load_row_tiles(y1_ref)
    gate = mod_ref[0, 0, 5:6, :]
    r = alpha * x_ref[0] + (1.0 + gate) * f
    o_ref[0] = _layer_norm(r, g_ref[...], b_ref[...])


def _moe_finish(x, route, mod_l, yk, ln_g, ln_b, alpha, tm):
    nb, L, d = x.shape
    tiles_per_seq = L // tm
    tiles = nb * tiles_per_seq
    return pl.pallas_call(
        functools.partial(_moe_finish_kernel, alpha=alpha),
        out_shape=jax.ShapeDtypeStruct((nb, L, d), F32),
        grid=(nb, tiles_per_seq),
        in_specs=[pl.BlockSpec((1, tm, d), lambda b, i: (b, i, 0)),
                  pl.BlockSpec((1, tm, LANES), lambda b, i: (b, i, 0)),
                  pl.BlockSpec((1, 1, 6, d), lambda b, i: (0, b, 0, 0)),
                  pl.BlockSpec((tm * ROW_TILE, LANES), lambda b, i: (b * tiles_per_seq + i, 0)),
                  pl.BlockSpec((tm * ROW_TILE, LANES),
                               lambda b, i: (tiles + b * tiles_per_seq + i, 0)),
                  pl.BlockSpec(ln_g.shape, lambda b, i: (0, 0)),
                  pl.BlockSpec(ln_b.shape, lambda b, i: (0, 0))],
        out_specs=pl.BlockSpec((1, tm, d), lambda b, i: (b, i, 0)),
        compiler_params=_cparams("arbitrary", "arbitrary"),
        name="moe_finish_ln",
    )(x, route, mod_l, yk, yk, ln_g, ln_b)


def _row_index_kernel(dest_ref, gsrc0_ref, sdst0_ref, gsrc_ref, sdst_ref, sem, *, n_pairs, T, tme):
    init_g = pltpu.make_async_copy(gsrc0_ref, gsrc_ref, sem.at[0])
    init_s = pltpu.make_async_copy(sdst0_ref, sdst_ref, sem.at[1])
    init_g.start()
    init_s.start()
    init_g.wait()
    init_s.wait()

    assert TOP_K == 2
    group = 8

    def body(g, carry):
        f0 = g * group
        rows = [dest_ref[f0 + k] for k in range(group)]
        for k in range(group):
            tok = g * (group // TOP_K) + k // TOP_K
            gsrc_ref[rows[k]] = tok * ROW_TILE
            sdst_ref[tme + rows[k]] = ((k % TOP_K) * T + tok) * ROW_TILE
        return carry
    lax.fori_loop(0, n_pairs // group, body, 0, unroll=2)


def _row_indices(dest, n_rows, T, tme):
    dump = (TOP_K * T + jnp.arange(n_rows + tme, dtype=I32) % tme) * ROW_TILE
    return pl.pallas_call(
        functools.partial(_row_index_kernel, n_pairs=dest.shape[0], T=T, tme=tme),
        out_shape=[jax.ShapeDtypeStruct((n_rows,), I32),
                   jax.ShapeDtypeStruct((n_rows + tme,), I32)],
        in_specs=[pl.BlockSpec(memory_space=pltpu.SMEM),
                  pl.BlockSpec(memory_space=pl.ANY),
                  pl.BlockSpec(memory_space=pl.ANY)],
        out_specs=[pl.BlockSpec(memory_space=pltpu.SMEM),
                   pl.BlockSpec(memory_space=pltpu.SMEM)],
        scratch_shapes=[pltpu.SemaphoreType.DMA((2,))],
        name="moe_row_indices",
    )(dest, jnp.zeros((n_rows,), I32), dump)


def _moe(attn_args, w_router, b_router, wg, wu, wd, ln_g, ln_b, alpha, tm_route, tme, tmf):
    nb, L, d = attn_args[0].shape
    mod_l = attn_args[4]
    T = nb * L
    x, h, route, counts = _router(attn_args, w_router, b_router, alpha, tm_route)
    route_flat = route.reshape(T, LANES)
    e = route_flat[:, 0:2].astype(I32)
    rank = route_flat[:, 2:4].astype(I32)
    cnt = counts[0, :N_EXPERTS].astype(I32)
    tiles = (cnt + tme - 1) // tme
    tile_end = jnp.cumsum(tiles)
    group_start = (tile_end - tiles) * tme
    dest = (group_start[e] + rank).reshape(-1)
    n_tiles = (TOP_K * T) // tme + N_EXPERTS
    n_rows = n_tiles * tme
    tile_expert = jnp.minimum(
        jnp.sum(jnp.arange(n_tiles, dtype=I32)[:, None] >= tile_end[None, :].astype(I32), axis=1),
        N_EXPERTS - 1).astype(I32)
    n_used = tile_end[-1:].astype(I32)
    gsrc, sdst = _row_indices(dest, n_rows, T, tme)
    yk = _expert_ffn(h.reshape(T * ROW_TILE, LANES), tile_expert, n_used, gsrc, sdst,
                     wg, wu, wd, tme)
    return _moe_finish(x, route, mod_l, yk, ln_g, ln_b, alpha, tmf)


def kernel(x, c, positions, rel_bias, w_ada, b_ada, w_in, w_out, conv_w, conv_b, dt_bias, a_log,
           d_skip, ssd_norm_w, sinks, ln_g, ln_b, ffn_w_gate, ffn_w_up, ffn_w_down, router_w,
           router_b, expert_w_gate, expert_w_up, expert_w_down):
    depth = w_ada.shape[0]
    nb, L, d = x.shape
    alpha = (2 * depth) ** 0.25
    rd = RET_HEADS * HEAD_DIM
    sd = SSD_HEADS * HEAD_DIM
    cd = conv_w.shape[-1]
    qd = SWA_HEADS * HEAD_DIM
    kvd = SWA_KV_HEADS * HEAD_DIM
    sizes = (rd, rd, rd, rd, sd, cd, SSD_HEADS, qd, kvd, kvd)
    offs = np.concatenate([[0], np.cumsum(sizes)])
    tl = min(512, L)

    mod = _ada_mod(c, w_ada, b_ada)
    cos_t, sin_t = _rotary_tables(positions, tl)
    bias_tab = _swa_bias_table(rel_bias)

    q_perm = np.concatenate([np.arange(h * HEAD_DIM, (h + 1) * HEAD_DIM) for h in SWA_HEAD_ORDER])
    widths = (4 * rd, sd + cd, LANES, qd + 2 * kvd)
    dtypes = (BF16, BF16, F32, BF16)

    steps = nb * (L // tl)
    n_exp, _, dff_e = expert_w_gate.shape[1:]
    per = steps // n_exp
    side_ok = (steps % n_exp == 0 and d % per == 0 and dff_e % per == 0
               and (d // per) % 8 == 0 and (dff_e // per) % 8 == 0)

    for layer in range(depth):
        wl = w_in[layer]
        seg = lambda i: wl[:, offs[i]:offs[i + 1]]
        dt_cols = jnp.zeros((d, LANES), F32).at[:, :SSD_HEADS].set(seg(6))
        w_cat = jnp.concatenate(
            [seg(0), seg(1), seg(2), seg(3), seg(4), seg(5), dt_cols,
             seg(7)[:, q_perm], seg(8), seg(9)], axis=1).astype(BF16)
        wo = w_out[layer]
        w_r = wo[0:rd].astype(BF16)
        w_s = wo[rd:rd + sd].astype(BF16)
        w_a = wo[rd + sd:][q_perm].astype(BF16)
        mod_l = mod[layer:layer + 1]

        i = layer // 2
        is_moe = layer % 2 == 1
        mix_cast = (expert_w_up[i], expert_w_down[i]) if is_moe and side_ok else ()
        y_ret, y_ssd, y_swa, *up_down_bf16 = _mixers(
            x, mod_l, w_cat, widths, dtypes, cos_t, sin_t, bias_tab, sinks[layer], conv_w[layer],
            conv_b[layer], dt_bias[layer], a_log[layer], d_skip[layer], ssd_norm_w[layer], tl,
            side_cast=mix_cast)
        attn_args = (x, y_ret, y_ssd, y_swa, mod_l, w_r, w_s, w_a,
                     ln_g[layer, 0][None, :], ln_b[layer, 0][None, :])

        g2 = ln_g[layer, 1][None, :]
        b2 = ln_b[layer, 1][None, :]
        if not is_moe:
            nxt = (layer + 1) // 2
            ride = (expert_w_gate[nxt],) if layer + 1 < depth and side_ok else ()
            x, *gate_bf16 = _dense_ffn(
                attn_args, ffn_w_gate[i].astype(BF16), ffn_w_up[i].astype(BF16),
                ffn_w_down[i].astype(BF16), g2, b2, alpha, tl, side_cast=ride)
        else:
            if side_ok:
                wg_e, = gate_bf16
                wu_e, wd_e = up_down_bf16
            else:
                wg_e, wu_e, wd_e = (expert_w_gate[i].astype(BF16), expert_w_up[i].astype(BF16),
                                    expert_w_down[i].astype(BF16))
            x = _moe(attn_args, router_w[i], router_b[i], wg_e, wu_e, wd_e, g2, b2, alpha,
                     tm_route=min(512, L), tme=min(512, L), tmf=min(512, L))
    return x
```

```python
import functools
import math

import numpy as np
import jax
import jax.numpy as jnp
from jax import lax
from jax.experimental import pallas as pl
from jax.experimental.pallas import tpu as pltpu

F32 = jnp.float32
BF16 = jnp.bfloat16
I32 = jnp.int32

HEAD_DIM = 64
CHUNK = 128
RET_HEADS = 4
SSD_HEADS = 8
SSD_GROUPS = 2
SSD_STATE = 64
SSD_CONV = 4
CONV_TAIL = 16
SWA_HEADS = 4
SWA_KV_HEADS = 2
REL_BUCKETS = 32
N_EXPERTS = 8
LN_EPS = 1e-5
LANES = 128
MASK_VALUE = -1e30

VMEM_LIMIT = 56 * 1024 * 1024


def _cparams(*sem):
    return pltpu.CompilerParams(dimension_semantics=sem, vmem_limit_bytes=VMEM_LIMIT)


def _silu(v):
    return v * (1.0 / (1.0 + jnp.exp(-v)))


def _softplus(v):
    return jnp.maximum(v, 0.0) + jnp.log(1.0 + jnp.exp(-jnp.abs(v)))


def _dot(a, b):
    return jnp.dot(a, b, preferred_element_type=F32)


def _dot_nt(a, b):
    return lax.dot_general(a, b, (((1,), (1,)), ((), ())), preferred_element_type=F32)


def _hi_lo(v):
    bits = lax.bitcast_convert_type(v, jnp.uint32) & jnp.uint32(0xFFFF0000)
    hi = lax.bitcast_convert_type(bits, F32)
    return hi.astype(BF16), (v - hi).astype(BF16)


def _split3(v):
    h1 = v.astype(BF16)
    r1 = v - h1.astype(F32)
    h2 = r1.astype(BF16)
    r2 = r1 - h2.astype(F32)
    return h1, h2, r2.astype(BF16)


def _dot3(v, m3):
    return _dot(jnp.concatenate(_split3(v), axis=1), m3)


def _dot3_left(m3, v):
    return _dot(m3, jnp.concatenate(_split3(v), axis=0))


def _dot2(v, m2):
    h1 = v.astype(BF16)
    h2 = (v - h1.astype(F32)).astype(BF16)
    return _dot(jnp.concatenate([h1, h2], axis=1), m2)


def _layer_norm(r, g, b):
    mu = jnp.mean(r, -1, keepdims=True)
    d = r - mu
    var = jnp.mean(d * d, -1, keepdims=True)
    return d * lax.rsqrt(var + LN_EPS) * g + b


def _ada_kernel(c_ref, w_ref, b_ref, o_ref):
    c_hi, c_lo = _hi_lo(c_ref[...])
    w_hi, w_lo = _hi_lo(w_ref[0])
    cc = jnp.concatenate([c_hi, c_lo], axis=0)
    both = _dot(cc, w_hi) + _dot(cc, w_lo)
    rows = c_hi.shape[0]
    o_ref[0] = both[0:rows] + both[rows:2 * rows] + b_ref[0]


def _ada_mod(c, w_ada, b_ada):
    depth, d, d6 = w_ada.shape
    nb = c.shape[0]
    rows = 8
    c_pad = jnp.zeros((rows, d), F32).at[:nb].set(c)
    out = pl.pallas_call(
        _ada_kernel,
        out_shape=jax.ShapeDtypeStruct((depth, rows, d6), F32),
        grid=(depth, d6 // d),
        in_specs=[pl.BlockSpec((rows, d), lambda l, j: (0, 0)),
                  pl.BlockSpec((1, d, d), lambda l, j: (l, 0, j)),
                  pl.BlockSpec((1, 1, d), lambda l, j: (l, 0, j))],
        out_specs=pl.BlockSpec((1, rows, d), lambda l, j: (l, 0, j)),
        compiler_params=_cparams("arbitrary", "arbitrary"),
        name="ada_mod",
    )(c_pad, w_ada, b_ada.reshape(depth, 1, d6))
    return out[:, :nb].reshape(depth, nb, 6, d)


def _rotary_kernel(pos_ref, cos_ref, sin_ref):
    half = HEAD_DIM // 2
    lane = lax.broadcasted_iota(I32, (1, LANES), 1)
    jj = lane % HEAD_DIM
    idx = (jj % half).astype(F32)
    inv = jnp.exp(-math.log(10000.0) * idx / half)
    ang = pos_ref[0].astype(F32) * inv
    cos_ref[0] = jnp.cos(ang)
    sin_ref[0] = jnp.where(jj < half, -1.0, 1.0) * jnp.sin(ang)


def _rotary_tables(positions, tl):
    nb, L = positions.shape
    pos = positions.reshape(nb, L, 1)
    return pl.pallas_call(
        _rotary_kernel,
        out_shape=[jax.ShapeDtypeStruct((nb, L, LANES), F32)] * 2,
        grid=(nb, L // tl),
        in_specs=[pl.BlockSpec((1, tl, 1), lambda b, i: (b, i, 0))],
        out_specs=[pl.BlockSpec((1, tl, LANES), lambda b, i: (b, i, 0))] * 2,
        compiler_params=_cparams("arbitrary", "arbitrary"),
        name="rotary_tables",
    )(pos)


def _swa_bias_kernel(rb_ref, bucket_ref, band_ref, o_ref):
    bucket = bucket_ref[...]
    band = band_ref[...]
    for h in range(SWA_HEADS):
        acc = jnp.zeros(bucket.shape, F32)
        for b in range(REL_BUCKETS):
            acc = jnp.where(bucket == b, rb_ref[b, h], acc)
        o_ref[h] = jnp.where(band > 0, acc, MASK_VALUE)


def _t5_bucket(dist):
    exact = REL_BUCKETS // 2
    df = jnp.maximum(dist, 1).astype(F32)
    large = exact + (jnp.log(df / exact) / math.log(CHUNK / exact) * (REL_BUCKETS - exact)).astype(I32)
    large = jnp.minimum(large, REL_BUCKETS - 1)
    return jnp.where(dist < exact, dist, large)


def _swa_bias_table(rel_bias):
    W = CHUNK
    qi = jnp.arange(W)[:, None]
    kj = jnp.arange(2 * W)[None, :]
    dist = qi + W - kj
    band = ((dist >= 0) & (dist < W)).astype(I32)
    bucket = _t5_bucket(jnp.clip(dist, 0, W - 1)).astype(I32)
    return pl.pallas_call(
        _swa_bias_kernel,
        out_shape=jax.ShapeDtypeStruct((SWA_HEADS, W, 2 * W), F32),
        in_specs=[pl.BlockSpec(memory_space=pltpu.SMEM),
                  pl.BlockSpec(memory_space=pltpu.VMEM),
                  pl.BlockSpec(memory_space=pltpu.VMEM)],
        out_specs=pl.BlockSpec(memory_space=pltpu.VMEM),
        name="swa_bias_table",
    )(rel_bias, bucket, band)


def _head_lane_mask(width, head):
    lane = lax.broadcasted_iota(I32, (1, width), 1)
    return (lane // HEAD_DIM) == head


def _rotate_half(t):
    width = t.shape[-1]
    lane = lax.broadcasted_iota(I32, (1, width), 1)
    half = HEAD_DIM // 2
    fwd = pltpu.roll(t, width - half, axis=1)
    bwd = pltpu.roll(t, half, axis=1)
    return jnp.where((lane % HEAD_DIM) < half, fwd, bwd)


def _retention_body(u_ref, cos_ref, sin_ref, din_ref, dq_ref, dk_ref, dc_ref,
                    bmask_ref, avg_ref, o_ref, state_ref, *, n_chunks):
    rd = RET_HEADS * HEAD_DIM
    masks = [_head_lane_mask(rd, h) for h in range(RET_HEADS)]

    def stack_heads(t):
        return jnp.concatenate([jnp.where(m, t, 0.0) for m in masks], axis=0).astype(BF16)

    state = state_ref[...]
    for ci in range(n_chunks):
        rows = slice(ci * CHUNK, (ci + 1) * CHUNK)
        cos = cos_ref[0, rows, :]
        sin = sin_ref[0, rows, :]
        cos2 = jnp.concatenate([cos, cos], axis=1)
        sin2 = jnp.concatenate([sin, sin], axis=1)
        q = u_ref[0, rows, 0:rd].astype(F32)
        k = u_ref[0, rows, rd:2 * rd].astype(F32)
        v = u_ref[0, rows, 2 * rd:3 * rd].astype(F32)
        g = u_ref[0, rows, 3 * rd:4 * rd].astype(F32)
        qr = q * cos2 + _rotate_half(q) * sin2
        kr = (k * cos2 + _rotate_half(k) * sin2) * (HEAD_DIM ** -0.5)
        scores = _dot_nt(qr.astype(BF16), stack_heads(kr)) * din_ref[...]
        inner = _dot(scores.astype(BF16), stack_heads(v))
        cross = _dot((qr * dq_ref[...]).astype(BF16), state.astype(BF16))
        o = inner + cross
        kd_t = (kr * dk_ref[...]).T.astype(BF16)
        kv = _dot(kd_t, v.astype(BF16))
        state = dc_ref[...] * state + bmask_ref[...] * kv
        mu = _dot2(o, avg_ref[...])
        dev = o - mu
        var = _dot2(dev * dev, avg_ref[...])
        on = dev * lax.rsqrt(var + LN_EPS)
        o_ref[0, rows, :] = (_silu(g) * on).astype(o_ref.dtype)
        yield
    state_ref[...] = state


def _retention_tables():
    H, d, C = RET_HEADS, HEAD_DIM, CHUNK
    log_gamma = jnp.log(1.0 - 2.0 ** (-5.0 - jnp.arange(H, dtype=F32)))
    idx = jnp.arange(C, dtype=F32)
    diff = idx[:, None] - idx[None, :]
    decay_in = jnp.where(diff >= 0, jnp.exp(log_gamma[:, None, None] * jnp.maximum(diff, 0.0)), 0.0)
    decay_q = jnp.exp(log_gamma[:, None] * (idx + 1.0))
    decay_k = jnp.exp(log_gamma[:, None] * (C - 1.0 - idx))
    decay_chunk = jnp.exp(log_gamma * C)
    din = decay_in.transpose(1, 0, 2).reshape(C, H * C)
    dq = jnp.repeat(decay_q.T, d, axis=1)
    dk = jnp.repeat(decay_k.T, d, axis=1)
    dc = jnp.repeat(decay_chunk, d)[None, :]
    head = jnp.arange(H * d) // d
    bmask = (head[:, None] == head[None, :]).astype(F32)
    avg = jnp.tile((bmask / d).astype(BF16), (2, 1))
    return din, dq, dk, dc, bmask, avg


def _swa_body(u_ref, bias_ref, sink_ref, o_ref, kprev_ref, vprev_ref, *, n_chunks, first_step):
    W = CHUNK
    qd = SWA_HEADS * HEAD_DIM
    kvd = SWA_KV_HEADS * HEAD_DIM
    lane = lax.broadcasted_iota(I32, (1, LANES), 1)
    low = lane < HEAD_DIM
    col = lax.broadcasted_iota(I32, (1, 2 * W), 1)

    kprev = kprev_ref[...]
    vprev = vprev_ref[...]
    sink = sink_ref[...]
    for ci in range(n_chunks):
        rows = slice(ci * W, (ci + 1) * W)
        qa = u_ref[0, rows, 0:LANES].astype(F32)
        qb = u_ref[0, rows, LANES:qd].astype(F32)
        k = u_ref[0, rows, qd:qd + kvd].astype(BF16)
        v = u_ref[0, rows, qd + kvd:qd + 2 * kvd].astype(BF16)
        q4 = jnp.concatenate([jnp.where(low, qa, 0.0), jnp.where(low, 0.0, qa),
                              jnp.where(low, qb, 0.0), jnp.where(low, 0.0, qb)],
                             axis=0).astype(BF16)
        kband = jnp.concatenate([kprev, k], axis=0)
        vband = jnp.concatenate([vprev, v], axis=0)
        logits = _dot_nt(q4, kband) * (HEAD_DIM ** -0.5) + bias_ref[...]
        if ci == 0:
            logits = jnp.where(jnp.logical_and(first_step, col < W), MASK_VALUE, logits)
        m = jnp.maximum(jnp.max(logits, -1, keepdims=True), sink)
        p = jnp.exp(logits - m)
        denom = jnp.sum(p, -1, keepdims=True) + jnp.exp(sink - m)
        res = _dot(p.astype(BF16), vband) / denom
        out_a = jnp.where(low, res[0:W], res[W:2 * W])
        out_b = jnp.where(low, res[2 * W:3 * W], res[3 * W:4 * W])
        o_ref[0, rows, 0:LANES] = out_a.astype(o_ref.dtype)
        o_ref[0, rows, LANES:qd] = out_b.astype(o_ref.dtype)
        kprev, vprev = k, v
        yield
    kprev_ref[...] = kprev
    vprev_ref[...] = vprev


SWA_HEAD_ORDER = (0, 2, 1, 3)


def _ssd_body(u_ref, dt_ref, cw_ref, cb_ref, dtb_c_ref, alog_c_ref, dskip_ref, nw_ref,
              tril_ref, expand_ref, gmask_ref, shift_ref,
              o_ref, state_ref, ext_ref, *, n_chunks):
    C = CHUNK
    sd = SSD_HEADS * HEAD_DIM
    gn = SSD_GROUPS * SSD_STATE
    cd = sd + 2 * gn
    tl = n_chunks * C
    slab = 2 * LANES
    heads_per_group = SSD_HEADS // SSD_GROUPS
    heads_per_slab = slab // HEAD_DIM
    lane = lax.broadcasted_iota(I32, (1, LANES), 1)
    low = lane < SSD_STATE
    slab_masks = [_head_lane_mask(slab, hh) for hh in range(heads_per_slab)]
    row_i = lax.broadcasted_iota(I32, (C, C), 0)
    col_i = lax.broadcasted_iota(I32, (C, C), 1)
    causal = row_i >= col_i
    neg_a_c = -jnp.exp(alog_c_ref[...])

    ext_ref[CONV_TAIL:CONV_TAIL + tl, :] = u_ref[0, :, sd:sd + cd]
    state = state_ref[...]
    for ci in range(n_chunks):
        rows = slice(ci * C, (ci + 1) * C)
        z = u_ref[0, rows, 0:sd].astype(F32)
        dt_raw = dt_ref[0, rows, :]
        window = ext_ref[ci * C:ci * C + CONV_TAIL + C, :]
        shifted = _dot(shift_ref[...], window)
        conv = cb_ref[...] + cw_ref[SSD_CONV - 1:SSD_CONV, :] * window[CONV_TAIL:, :].astype(F32)
        for w in range(SSD_CONV - 1):
            conv = conv + cw_ref[w:w + 1, :] * shifted[w * C:(w + 1) * C, :]
        xbc = _silu(conv)
        xs = xbc[:, 0:sd]
        bm = xbc[:, sd:sd + gn]
        cm = xbc[:, sd + gn:cd]

        dt_c = _softplus(dt_raw + dtb_c_ref[...])
        a_c = neg_a_c * dt_c
        acs_c = _dot3_left(tril_ref[...], a_c)
        acs_t = acs_c.T
        spread = _dot3(acs_c, expand_ref[...])
        acs_x = spread[:, 0:sd]
        dt_x = _dot3(dt_c, expand_ref[:, 0:sd])
        xdt = xs * dt_x
        yield

        bstack = jnp.concatenate([jnp.where(low, bm, 0.0), jnp.where(low, 0.0, bm)],
                                 axis=0).astype(BF16)
        cb = _dot_nt(cm.astype(BF16), bstack)
        y_diag = []
        for s in range(sd // slab):
            ms = []
            for hh in range(heads_per_slab):
                h = s * heads_per_slab + hh
                g = h // heads_per_group
                col_bcast = spread[:, sd + h * LANES:sd + (h + 1) * LANES]
                seg = col_bcast - acs_t[h:h + 1, :]
                lmat = jnp.exp(jnp.where(causal, seg, MASK_VALUE))
                ms.append((cb[:, g * C:(g + 1) * C] * lmat).astype(BF16))
            xslab = xdt[:, s * slab:(s + 1) * slab]
            xstack = jnp.concatenate([jnp.where(m, xslab, 0.0) for m in slab_masks],
                                     axis=0).astype(BF16)
            y_diag.append(_dot(jnp.concatenate(ms, axis=1), xstack))
        y_diag = jnp.concatenate(y_diag, axis=1)
        yield

        y_off = _dot(cm.astype(BF16), state.astype(BF16)) * jnp.exp(acs_x)
        last = acs_x[C - 1:C, :]
        dec = jnp.exp(last - acs_x)
        new = _dot(bm.T.astype(BF16), (xdt * dec).astype(BF16))
        state = jnp.exp(last) * state + gmask_ref[...] * new

        y = y_diag + y_off + xs * dskip_ref[...]
        hgate = y * _silu(z)
        gw = sd // SSD_GROUPS
        for g in range(SSD_GROUPS):
            hg = hgate[:, g * gw:(g + 1) * gw]
            ms_ = jnp.mean(hg * hg, -1, keepdims=True)
            o_ref[0, rows, g * gw:(g + 1) * gw] = (
                hg * lax.rsqrt(ms_ + LN_EPS) * nw_ref[:, g * gw:(g + 1) * gw]).astype(o_ref.dtype)
        yield
    state_ref[...] = state
    ext_ref[0:CONV_TAIL, :] = ext_ref[tl:tl + CONV_TAIL, :]


def _ssd_tables():
    C = CHUNK
    sd = SSD_HEADS * HEAD_DIM
    gn = SSD_GROUPS * SSD_STATE
    t = jnp.arange(C)
    tril = (t[:, None] >= t[None, :]).astype(BF16)
    r = jnp.arange(LANES)[:, None]
    eexp = ((r == (jnp.arange(sd)[None, :] // HEAD_DIM)) & (r < SSD_HEADS)).astype(BF16)
    bsel = ((r == (jnp.arange(SSD_HEADS * LANES)[None, :] // LANES)) & (r < SSD_HEADS)).astype(BF16)
    heads_per_group = SSD_HEADS // SSD_GROUPS
    row_g = jnp.arange(gn)[:, None] // SSD_STATE
    col_g = (jnp.arange(sd)[None, :] // HEAD_DIM) // heads_per_group
    gmask = (row_g == col_g).astype(F32)
    expand = jnp.concatenate([eexp, bsel], axis=1)
    win = jnp.arange(CONV_TAIL + C)[None, :]
    shifts = jnp.concatenate(
        [(win == CONV_TAIL + t[:, None] - (SSD_CONV - 1 - s)).astype(BF16)
         for s in range(SSD_CONV - 1)], axis=0)
    return jnp.tile(tril, (1, 3)), jnp.tile(expand, (3, 1)), gmask, shifts


N_RET_TABLES = 6
N_SSD_CONSTS = 10


def _in_proj_body(x_ref, shift, scale, w_ref, out_refs):
    h = (x_ref[0] * (1.0 + scale) + shift).astype(BF16)
    off = 0
    for ref in out_refs:
        width = ref.shape[-1]
        for a, b in _ff_chunks(width):
            ref[0, :, a:b] = _dot(h, w_ref[:, off + a:off + b]).astype(ref.dtype)
            yield
        off += width


def _mixers_kernel(*refs, n_chunks, n_cast, n_tiles, tiles_per_seq):
    x_ref, mod_ref, w_ref, cos, sin = refs[:5]
    pos = 5
    ret_tables = refs[pos:pos + N_RET_TABLES]
    pos += N_RET_TABLES
    ssd_consts = refs[pos:pos + N_SSD_CONSTS]
    pos += N_SSD_CONSTS
    bias, sink = refs[pos:pos + 2]
    pos += 2
    cast_in = refs[pos:pos + n_cast]
    pos += n_cast
    y_ret, y_ssd, y_swa = refs[pos:pos + 3]
    pos += 3
    cast_out = refs[pos:pos + n_cast]
    pos += n_cast
    ret_state, ssd_state, ssd_ext, kprev, vprev = refs[pos:pos + 5]
    cur = refs[pos + 5:pos + 9]
    nxt = refs[pos + 9:pos + 13]
    _side_cast(cast_in, cast_out)

    s = pl.program_id(0)
    tile = jnp.maximum(s - 1, 0)
    seq_start = tile % tiles_per_seq == 0

    @pl.when(s == 0)
    def _():
        for ref in cur:
            ref[...] = jnp.zeros_like(ref)

    @pl.when(seq_start)
    def _():
        ret_state[...] = jnp.zeros_like(ret_state)
        ssd_state[...] = jnp.zeros_like(ssd_state)
        ssd_ext[0:CONV_TAIL, :] = jnp.zeros((CONV_TAIL, ssd_ext.shape[1]), ssd_ext.dtype)
        kprev[...] = jnp.zeros_like(kprev)
        vprev[...] = jnp.zeros_like(vprev)

    batch = jnp.minimum(s, n_tiles - 1) // tiles_per_seq
    shift = mod_ref[0, batch, 0:1, :]
    scale = mod_ref[0, batch, 1:2, :]
    u_ret, u_ssd, u_dt, u_swa = cur
    bodies = [
        _ssd_body(u_ssd, u_dt, *ssd_consts, y_ssd, ssd_state, ssd_ext, n_chunks=n_chunks),
        _in_proj_body(x_ref, shift, scale, w_ref, nxt),
        _swa_body(u_swa, bias, sink, y_swa, kprev, vprev, n_chunks=n_chunks,
                  first_step=seq_start),
        _retention_body(u_ret, cos, sin, *ret_tables, y_ret, ret_state, n_chunks=n_chunks),
    ]
    alive = list(bodies)
    while alive:
        alive = [b for b in alive if next(b, StopIteration) is not StopIteration]
    for c_ref, n_ref in zip(cur, nxt):
        c_ref[...] = n_ref[...]


def _mixers(x, mod_l, w_cat, widths, dtypes, cos_t, sin_t, bias_tab, sinks_l, conv_w, conv_b,
            dt_bias, a_log, d_skip, norm_w, tl, side_cast=()):
    nb, L, d = x.shape
    W = CHUNK
    rd = RET_HEADS * HEAD_DIM
    sd = SSD_HEADS * HEAD_DIM
    qd = SWA_HEADS * HEAD_DIM
    cd = conv_w.shape[-1]
    ret_tables = _retention_tables()
    pad = lambda v: jnp.zeros((1, LANES), F32).at[0, :SSD_HEADS].set(v)
    rep = lambda v: jnp.repeat(v, HEAD_DIM)[None, :]
    ssd_consts = (conv_w, conv_b[None, :], pad(dt_bias), pad(a_log), rep(d_skip),
                  norm_w[None, :]) + _ssd_tables()
    order = jnp.array(SWA_HEAD_ORDER)
    bias_stacked = bias_tab[order].reshape(SWA_HEADS * W, 2 * W)
    sink_col = jnp.repeat(sinks_l.astype(F32)[order], W)[:, None]
    assert len(ret_tables) == N_RET_TABLES and len(ssd_consts) == N_SSD_CONSTS
    tps = L // tl
    n_tiles = nb * tps

    def tile_of(step):
        return step // tps, step % tps, 0

    proj_map = lambda s: tile_of(jnp.minimum(s, n_tiles - 1))
    mix_map = lambda s: tile_of(jnp.maximum(s - 1, 0))
    const = lambda a: pl.BlockSpec(a.shape, lambda s: (0,) * a.ndim)
    consts = ret_tables + ssd_consts + (bias_stacked, sink_col)
    c_in, c_out, c_shapes = _side_cast_specs(side_cast, n_tiles)
    u_bufs = [pltpu.VMEM((1, tl, w), t) for w, t in zip(widths, dtypes)]
    return pl.pallas_call(
        functools.partial(_mixers_kernel, n_chunks=tl // CHUNK, n_cast=len(side_cast),
                          n_tiles=n_tiles, tiles_per_seq=tps),
        out_shape=[jax.ShapeDtypeStruct((nb, L, w), BF16) for w in (rd, sd, qd)] + c_shapes,
        grid=(n_tiles + 1,),
        in_specs=[pl.BlockSpec((1, tl, d), proj_map), const(mod_l), const(w_cat),
                  pl.BlockSpec((1, tl, LANES), mix_map), pl.BlockSpec((1, tl, LANES), mix_map)]
                 + [const(a) for a in consts] + c_in,
        out_specs=[pl.BlockSpec((1, tl, w), mix_map) for w in (rd, sd, qd)] + c_out,
        scratch_shapes=[pltpu.VMEM((rd, rd), F32),
                        pltpu.VMEM((SSD_GROUPS * SSD_STATE, sd), F32),
                        pltpu.VMEM((CONV_TAIL + tl, cd), BF16),
                        pltpu.VMEM((W, LANES), BF16), pltpu.VMEM((W, LANES), BF16)]
                       + u_bufs + u_bufs,
        compiler_params=_cparams("arbitrary"),
        name="in_proj_mixers",
    )(x, mod_l, w_cat, cos_t, sin_t, *consts, *side_cast)


N_ATTN_OUT = 10


def _attn_out(x_ref, yr_ref, ys_ref, ya_ref, mod_ref, wr_ref, ws_ref, wa_ref, g_ref, b_ref,
              alpha):
    mix = (_dot(yr_ref[0], wr_ref[...]) + _dot(ys_ref[0], ws_ref[...])
           + _dot(ya_ref[0], wa_ref[...]))
    gate = mod_ref[0, 0, 2:3, :]
    r = alpha * x_ref[0] + (1.0 + gate) * mix
    return _layer_norm(r, g_ref[...], b_ref[...])


def _attn_out_operands(x, y_ret, y_ssd, y_swa, mod_l, w_r, w_s, w_a, ln_g, ln_b, tm):
    d = x.shape[-1]
    tok = lambda w: pl.BlockSpec((1, tm, w), lambda b, i: (b, i, 0))
    const = lambda a: pl.BlockSpec(a.shape, lambda b, i: (0,) * a.ndim)
    specs = [tok(d), tok(y_ret.shape[-1]), tok(y_ssd.shape[-1]), tok(y_swa.shape[-1]),
             pl.BlockSpec((1, 1, 6, d), lambda b, i: (0, b, 0, 0)),
             const(w_r), const(w_s), const(w_a), const(ln_g), const(ln_b)]
    return specs, (x, y_ret, y_ssd, y_swa, mod_l, w_r, w_s, w_a, ln_g, ln_b)


def _side_cast_specs(arrays, steps, steps_per_seq=None):
    in_specs, out_specs, out_shapes = [], [], []
    for a in arrays:
        e, r, c = a.shape
        per = steps // e
        if steps_per_seq is None:
            index = lambda s, per=per: (jnp.minimum(s, steps - 1) // per,
                                        jnp.minimum(s, steps - 1) % per, 0)
        else:
            index = lambda b, i, per=per: ((b * steps_per_seq + i) // per,
                                           (b * steps_per_seq + i) % per, 0)
        spec = pl.BlockSpec((1, r // per, c), index)
        in_specs.append(spec)
        out_specs.append(spec)
        out_shapes.append(jax.ShapeDtypeStruct(a.shape, BF16))
    return in_specs, out_specs, out_shapes


def _side_cast(in_refs, out_refs):
    for i_ref, o_ref in zip(in_refs, out_refs):
        o_ref[...] = i_ref[...].astype(o_ref.dtype)


def _dense_ffn_kernel(*refs, alpha):
    attn = refs[:N_ATTN_OUT]
    mod_ref = attn[4]
    wg_ref, wu_ref, wd_ref, g_ref, b_ref = refs[N_ATTN_OUT:N_ATTN_OUT + 5]
    rest = refs[N_ATTN_OUT + 5:]
    n_cast = (len(rest) - 1) // 2
    o_ref = rest[n_cast]
    _side_cast(rest[:n_cast], rest[n_cast + 1:])
    x = _attn_out(*attn, alpha)
    sh = mod_ref[0, 0, 3:4, :]
    sc = mod_ref[0, 0, 4:5, :]
    gate = mod_ref[0, 0, 5:6, :]
    h = (x * (1.0 + sc) + sh).astype(BF16)
    dff = wg_ref.shape[-1]
    acc = None
    for a, b in _ff_chunks(dff):
        gj = _dot(h, wg_ref[:, a:b])
        uj = _dot(h, wu_ref[:, a:b])
        part = _dot((_silu(gj) * uj).astype(BF16), wd_ref[a:b, :])
        acc = part if acc is None else acc + part
    r = alpha * x + (1.0 + gate) * acc
    o_ref[0] = _layer_norm(r, g_ref[...], b_ref[...])


def _dense_ffn(attn_args, wg, wu, wd, ln_g, ln_b, alpha, tm, side_cast=()):
    nb, L, d = attn_args[0].shape
    const = lambda a: pl.BlockSpec(a.shape, lambda b, i: (0,) * a.ndim,
                                   pipeline_mode=pl.Buffered(1))
    a_specs, a_ops = _attn_out_operands(*attn_args, tm)
    c_in, c_out, c_shapes = _side_cast_specs(side_cast, nb * (L // tm), L // tm)
    return pl.pallas_call(
        functools.partial(_dense_ffn_kernel, alpha=alpha),
        out_shape=[jax.ShapeDtypeStruct((nb, L, d), F32)] + c_shapes,
        grid=(nb, L // tm),
        in_specs=a_specs + [const(wg), const(wu), const(wd), const(ln_g), const(ln_b)] + c_in,
        out_specs=[pl.BlockSpec((1, tm, d), lambda b, i: (b, i, 0))] + c_out,
        compiler_params=_cparams("arbitrary", "arbitrary"),
        name="attn_out_dense_ffn_ln",
    )(*a_ops, wg, wu, wd, ln_g, ln_b, *side_cast)


def _store_row_tiles(ref, value):
    rows, d = value.shape
    nt = d // LANES
    for c in range(nt):
        ref[pl.ds(c, rows, stride=nt), :] = value[:, c * LANES:(c + 1) * LANES].astype(ref.dtype)


def _load_row_tiles(ref):
    nt = ROW_TILE
    rows = ref.shape[0] // nt
    return jnp.concatenate([ref[pl.ds(c, rows, stride=nt), :] for c in range(nt)], axis=1)


def _router_kernel(*refs, alpha):
    attn = refs[:N_ATTN_OUT]
    mod_ref = attn[4]
    wr_ref, br_ref, trils_ref, x1_ref, h_ref, route_ref, cnt_ref, base_ref = refs[N_ATTN_OUT:]
    first = jnp.logical_and(pl.program_id(0) == 0, pl.program_id(1) == 0)

    @pl.when(first)
    def _():
        base_ref[...] = jnp.zeros_like(base_ref)

    x1 = _attn_out(*attn, alpha)
    x1_ref[0] = x1
    sh = mod_ref[0, 0, 3:4, :]
    sc = mod_ref[0, 0, 4:5, :]
    h = x1 * (1.0 + sc) + sh
    _store_row_tiles(h_ref.at[0], h)
    h_hi, h_lo = _hi_lo(h)
    wide = _dot(h_hi, wr_ref[...]) + _dot(h_lo, wr_ref[...])
    logits = wide + pltpu.roll(wide, LANES - N_EXPERTS, axis=1) + br_ref[...]
    lane = lax.broadcasted_iota(I32, logits.shape, 1).astype(F32)
    logits = jnp.where(lane < N_EXPERTS, logits, MASK_VALUE)
    v1 = jnp.max(logits, -1, keepdims=True)
    e1 = jnp.min(jnp.where(logits == v1, lane, float(LANES)), -1, keepdims=True)
    rest = jnp.where(lane == e1, MASK_VALUE, logits)
    v2 = jnp.max(rest, -1, keepdims=True)
    e2 = jnp.min(jnp.where(rest == v2, lane, float(LANES)), -1, keepdims=True)
    t = jnp.exp(v2 - v1)
    w1 = 1.0 / (1.0 + t)
    w2 = t / (1.0 + t)
    hot1 = (lane == e1).astype(F32)
    hot2 = (lane == e2).astype(F32)
    both = hot1 + hot2
    base = base_ref[0:1, :]
    before = _dot(trils_ref[...], both.astype(BF16)) + base
    rank1 = jnp.sum(hot1 * before, -1, keepdims=True)
    rank2 = jnp.sum(hot2 * before, -1, keepdims=True)
    total = base + jnp.sum(both, 0, keepdims=True)
    base_ref[0:1, :] = total
    cnt_ref[...] = jnp.broadcast_to(total, cnt_ref.shape)
    route = jnp.where(lane == 0, e1, 0.0)
    route = jnp.where(lane == 1, e2, route)
    route = jnp.where(lane == 2, rank1, route)
    route = jnp.where(lane == 3, rank2, route)
    route = jnp.where(lane == 4, w1, route)
    route = jnp.where(lane == 5, w2, route)
    route_ref[0] = route


def _router(attn_args, w_router, b_router, alpha, tm):
    nb, L, d = attn_args[0].shape
    w_hi, w_lo = _hi_lo(w_router)
    wr = (jnp.zeros((d, LANES), BF16).at[:, :N_EXPERTS].set(w_hi)
          .at[:, N_EXPERTS:2 * N_EXPERTS].set(w_lo))
    br = jnp.zeros((1, LANES), F32).at[0, :N_EXPERTS].set(b_router)
    t = jnp.arange(tm)
    tril_strict = (t[:, None] > t[None, :]).astype(BF16)
    const = lambda a: pl.BlockSpec(a.shape, lambda b, i: (0,) * a.ndim)
    a_specs, a_ops = _attn_out_operands(*attn_args, tm)
    return pl.pallas_call(
        functools.partial(_router_kernel, alpha=alpha),
        out_shape=[jax.ShapeDtypeStruct((nb, L, d), F32),
                   jax.ShapeDtypeStruct((nb, L * (d // LANES), LANES), F32),
                   jax.ShapeDtypeStruct((nb, L, LANES), F32),
                   jax.ShapeDtypeStruct((8, LANES), F32)],
        grid=(nb, L // tm),
        in_specs=a_specs + [const(wr), const(br), const(tril_strict)],
        out_specs=[pl.BlockSpec((1, tm, d), lambda b, i: (b, i, 0)),
                   pl.BlockSpec((1, tm * (d // LANES), LANES), lambda b, i: (b, i, 0)),
                   pl.BlockSpec((1, tm, LANES), lambda b, i: (b, i, 0)),
                   pl.BlockSpec((8, LANES), lambda b, i: (0, 0))],
        scratch_shapes=[pltpu.VMEM((8, LANES), F32)],
        compiler_params=_cparams("arbitrary", "arbitrary"),
        name="attn_out_moe_router",
    )(*a_ops, wr, br, tril_strict)


TOP_K = 2
ROW_TILE = 8
EXPERT_FF_SPLITS = 2


def _ff_chunks(width, step=512):
    return [(a, min(a + step, width)) for a in range(0, width, step)]


def _expert_kernel(te_ref, nu_ref, gsrc_ref, sdst_ref,
                   h_ref, wg_ref, wu_ref, wd_ref, yk_ref,
                   xbuf, ybuf, hbuf, acc, gsem, ssem, *, tme):
    i = pl.program_id(0)
    j = pl.program_id(1)
    n_used = nu_ref[0]
    slot = i % 2
    other = 1 - slot
    used = i < n_used
    chunks = _ff_chunks(wg_ref.shape[-1])
    rows_per_step = tme // EXPERT_FF_SPLITS
    n_front = len(chunks) // 2
    front_w = chunks[n_front - 1][1]
    back_w = wg_ref.shape[-1] - front_w
    gather_ranges, scatter_ranges = [], []
    for c, (a, b) in enumerate(chunks):
        if c < n_front:
            gather_ranges.append((rows_per_step * a // front_w, rows_per_step * b // front_w))
            scatter_ranges.append((0, 0))
        else:
            gather_ranges.append((0, 0))
            scatter_ranges.append((rows_per_step * (a - front_w) // back_w,
                                   rows_per_step * (b - front_w) // back_w))

    nt = ROW_TILE

    def gather(tile, slot_, r):
        src = pl.multiple_of(gsrc_ref[tile * tme + r], nt)
        return pltpu.make_async_copy(h_ref.at[pl.ds(src, nt)],
                                     xbuf.at[slot_, pl.ds(pl.multiple_of(r * nt, nt), nt)],
                                     gsem.at[slot_])

    def wait_gather(slot_):
        pltpu.make_async_copy(h_ref.at[pl.ds(0, tme * nt)], xbuf.at[slot_],
                              gsem.at[slot_]).wait()

    def scatter(block, slot_, r):
        dst = pl.multiple_of(sdst_ref[block * tme + r], nt)
        return pltpu.make_async_copy(ybuf.at[slot_, pl.ds(pl.multiple_of(r * nt, nt), nt)],
                                     yk_ref.at[pl.ds(dst, nt)], ssem.at[slot_])

    def wait_scatter(slot_):
        pltpu.make_async_copy(ybuf.at[slot_], yk_ref.at[pl.ds(0, tme * nt)],
                              ssem.at[slot_]).wait()

    @pl.when(jnp.logical_and(i == 0, j == 0))
    def _():
        ybuf[1] = jnp.zeros(ybuf.shape[1:], F32)

        def body(r, carry):
            gather(0, 0, r).start()
            return carry
        lax.fori_loop(0, tme, body, 0, unroll=8)

    @pl.when(used)
    def _():
        @pl.when(j == 0)
        def _():
            wait_gather(slot)
            hbuf[...] = _load_row_tiles(xbuf.at[slot]).astype(BF16)

            @pl.when(i > 0)
            def _():
                wait_scatter(slot)
            acc[...] = jnp.zeros(acc.shape, F32)

        h = hbuf[...]
        for c, (a, b) in enumerate(chunks):
            for r in range(*gather_ranges[c]):
                gather(i + 1, other, j * rows_per_step + r).start()
            for r in range(*scatter_ranges[c]):
                scatter(i, other, j * rows_per_step + r).start()
            gj = _dot(h, wg_ref[0, :, a:b])
            uj = _dot(h, wu_ref[0, :, a:b])
            part = _dot((_silu(gj) * uj).astype(BF16), wd_ref[0, a:b, :])
            acc[...] = acc[...] + part
            ybuf[slot, 0:ROW_TILE, :] = part[0:ROW_TILE, 0:LANES]

        @pl.when(j == EXPERT_FF_SPLITS - 1)
        def _():
            _store_row_tiles(ybuf.at[slot], acc[...])

        @pl.when(jnp.logical_and(j == EXPERT_FF_SPLITS - 1, i == n_used - 1))
        def _():
            wait_gather(other)
            wait_scatter(other)

            def body(r, carry):
                scatter(i + 1, slot, r).start()
                return carry
            lax.fori_loop(0, tme, body, 0, unroll=8)
            wait_scatter(slot)


def _expert_ffn(h_rows, tile_expert, n_used, gsrc, sdst, wg, wu, wd, tme):
    nt = ROW_TILE
    T = h_rows.shape[0] // nt
    d = nt * LANES
    dff = wg.shape[-1]
    dffh = dff // EXPERT_FF_SPLITS
    n_tiles = gsrc.shape[0] // tme
    yk_rows = TOP_K * T + tme

    def half(i, j):
        return jnp.where(i % 2 == 0, j, EXPERT_FF_SPLITS - 1 - j)

    return pl.pallas_call(
        functools.partial(_expert_kernel, tme=tme),
        out_shape=jax.ShapeDtypeStruct((yk_rows * nt, LANES), F32),
        grid_spec=pltpu.PrefetchScalarGridSpec(
            num_scalar_prefetch=4,
            grid=(n_tiles, EXPERT_FF_SPLITS),
            in_specs=[pl.BlockSpec(memory_space=pl.ANY),
                      pl.BlockSpec((1, d, dffh), lambda i, j, te, *_: (te[i], 0, half(i, j))),
                      pl.BlockSpec((1, d, dffh), lambda i, j, te, *_: (te[i], 0, half(i, j))),
                      pl.BlockSpec((1, dffh, d), lambda i, j, te, *_: (te[i], half(i, j), 0))],
            out_specs=pl.BlockSpec(memory_space=pl.ANY),
            scratch_shapes=[pltpu.VMEM((2, tme * nt, LANES), F32),
                            pltpu.VMEM((2, tme * nt, LANES), F32),
                            pltpu.VMEM((tme, d), BF16), pltpu.VMEM((tme, d), F32),
                            pltpu.SemaphoreType.DMA((2,)), pltpu.SemaphoreType.DMA((2,))]),
        compiler_params=_cparams("arbitrary", "arbitrary"),
        name="moe_experts",
    )(tile_expert, n_used, gsrc, sdst, h_rows, wg, wu, wd)


def _moe_finish_kernel(x_ref, route_ref, mod_ref, y0_ref, y1_ref, g_ref, b_ref, o_ref, *, alpha):
    route = route_ref[0]
    f = route[:, 4:5] * _load_row_tiles(y0_ref) + route[:, 5:6] * _load_row_tiles(y1_ref)
    gate = mod_ref[0, 0, 5:6, :]
    r = alpha * x_ref[0] + (1.0 + gate) * f
    o_ref[0] = _layer_norm(r, g_ref[...], b_ref[...])


def _moe_finish(x, route, mod_l, yk, ln_g, ln_b, alpha, tm):
    nb, L, d = x.shape
    tiles_per_seq = L // tm
    tiles = nb * tiles_per_seq
    return pl.pallas_call(
        functools.partial(_moe_finish_kernel, alpha=alpha),
        out_shape=jax.ShapeDtypeStruct((nb, L, d), F32),
        grid=(nb, tiles_per_seq),
        in_specs=[pl.BlockSpec((1, tm, d), lambda b, i: (b, i, 0)),
                  pl.BlockSpec((1, tm, LANES), lambda b, i: (b, i, 0)),
                  pl.BlockSpec((1, 1, 6, d), lambda b, i: (0, b, 0, 0)),
                  pl.BlockSpec((tm * ROW_TILE, LANES), lambda b, i: (b * tiles_per_seq + i, 0)),
                  pl.BlockSpec((tm * ROW_TILE, LANES),
                               lambda b, i: (tiles + b * tiles_per_seq + i, 0)),
                  pl.BlockSpec(ln_g.shape, lambda b, i: (0, 0)),
                  pl.BlockSpec(ln_b.shape, lambda b, i: (0, 0))],
        out_specs=pl.BlockSpec((1, tm, d), lambda b, i: (b, i, 0)),
        compiler_params=_cparams("arbitrary", "arbitrary"),
        name="moe_finish_ln",
    )(x, route, mod_l, yk, yk, ln_g, ln_b)


def _row_index_kernel(dest_ref, gsrc0_ref, sdst0_ref, gsrc_ref, sdst_ref, sem, *, n_pairs, T, tme):
    init_g = pltpu.make_async_copy(gsrc0_ref, gsrc_ref, sem.at[0])
    init_s = pltpu.make_async_copy(sdst0_ref, sdst_ref, sem.at[1])
    init_g.start()
    init_s.start()
    init_g.wait()
    init_s.wait()

    assert TOP_K == 2
    group = 8

    def body(g, carry):
        f0 = g * group
        rows = [dest_ref[f0 + k] for k in range(group)]
        for k in range(group):
            tok = g * (group // TOP_K) + k // TOP_K
            gsrc_ref[rows[k]] = tok * ROW_TILE
            sdst_ref[tme + rows[k]] = ((k % TOP_K) * T + tok) * ROW_TILE
        return carry
    lax.fori_loop(0, n_pairs // group, body, 0, unroll=2)


def _row_indices(dest, n_rows, T, tme):
    dump = (TOP_K * T + jnp.arange(n_rows + tme, dtype=I32) % tme) * ROW_TILE
    return pl.pallas_call(
        functools.partial(_row_index_kernel, n_pairs=dest.shape[0], T=T, tme=tme),
        out_shape=[jax.ShapeDtypeStruct((n_rows,), I32),
                   jax.ShapeDtypeStruct((n_rows + tme,), I32)],
        in_specs=[pl.BlockSpec(memory_space=pltpu.SMEM),
                  pl.BlockSpec(memory_space=pl.ANY),
                  pl.BlockSpec(memory_space=pl.ANY)],
        out_specs=[pl.BlockSpec(memory_space=pltpu.SMEM),
                   pl.BlockSpec(memory_space=pltpu.SMEM)],
        scratch_shapes=[pltpu.SemaphoreType.DMA((2,))],
        name="moe_row_indices",
    )(dest, jnp.zeros((n_rows,), I32), dump)


def _moe(attn_args, w_router, b_router, wg, wu, wd, ln_g, ln_b, alpha, tm_route, tme, tmf):
    nb, L, d = attn_args[0].shape
    mod_l = attn_args[4]
    T = nb * L
    x, h, route, counts = _router(attn_args, w_router, b_router, alpha, tm_route)
    route_flat = route.reshape(T, LANES)
    e = route_flat[:, 0:2].astype(I32)
    rank = route_flat[:, 2:4].astype(I32)
    cnt = counts[0, :N_EXPERTS].astype(I32)
    tiles = (cnt + tme - 1) // tme
    tile_end = jnp.cumsum(tiles)
    group_start = (tile_end - tiles) * tme
    dest = (group_start[e] + rank).reshape(-1)
    n_tiles = (TOP_K * T) // tme + N_EXPERTS
    n_rows = n_tiles * tme
    tile_expert = jnp.minimum(
        jnp.sum(jnp.arange(n_tiles, dtype=I32)[:, None] >= tile_end[None, :].astype(I32), axis=1),
        N_EXPERTS - 1).astype(I32)
    n_used = tile_end[-1:].astype(I32)
    gsrc, sdst = _row_indices(dest, n_rows, T, tme)
    yk = _expert_ffn(h.reshape(T * ROW_TILE, LANES), tile_expert, n_used, gsrc, sdst,
                     wg, wu, wd, tme)
    return _moe_finish(x, route, mod_l, yk, ln_g, ln_b, alpha, tmf)


def kernel(x, c, positions, rel_bias, w_ada, b_ada, w_in, w_out, conv_w, conv_b, dt_bias, a_log,
           d_skip, ssd_norm_w, sinks, ln_g, ln_b, ffn_w_gate, ffn_w_up, ffn_w_down, router_w,
           router_b, expert_w_gate, expert_w_up, expert_w_down):
    depth = w_ada.shape[0]
    nb, L, d = x.shape
    alpha = (2 * depth) ** 0.25
    rd = RET_HEADS * HEAD_DIM
    sd = SSD_HEADS * HEAD_DIM
    cd = conv_w.shape[-1]
    qd = SWA_HEADS * HEAD_DIM
    kvd = SWA_KV_HEADS * HEAD_DIM
    sizes = (rd, rd, rd, rd, sd, cd, SSD_HEADS, qd, kvd, kvd)
    offs = np.concatenate([[0], np.cumsum(sizes)])
    tl = min(512, L)

    mod = _ada_mod(c, w_ada, b_ada)
    cos_t, sin_t = _rotary_tables(positions, tl)
    bias_tab = _swa_bias_table(rel_bias)

    q_perm = np.concatenate([np.arange(h * HEAD_DIM, (h + 1) * HEAD_DIM) for h in SWA_HEAD_ORDER])
    widths = (4 * rd, sd + cd, LANES, qd + 2 * kvd)
    dtypes = (BF16, BF16, F32, BF16)

    steps = nb * (L // tl)
    n_exp, _, dff_e = expert_w_gate.shape[1:]
    per = steps // n_exp
    side_ok = (steps % n_exp == 0 and d % per == 0 and dff_e % per == 0
               and (d // per) % 8 == 0 and (dff_e // per) % 8 == 0)

    for layer in range(depth):
        wl = w_in[layer]
        seg = lambda i: wl[:, offs[i]:offs[i + 1]]
        dt_cols = jnp.zeros((d, LANES), F32).at[:, :SSD_HEADS].set(seg(6))
        w_cat = jnp.concatenate(
            [seg(0), seg(1), seg(2), seg(3), seg(4), seg(5), dt_cols,
             seg(7)[:, q_perm], seg(8), seg(9)], axis=1).astype(BF16)
        wo = w_out[layer]
        w_r = wo[0:rd].astype(BF16)
        w_s = wo[rd:rd + sd].astype(BF16)
        w_a = wo[rd + sd:][q_perm].astype(BF16)
        mod_l = mod[layer:layer + 1]

        i = layer // 2
        is_moe = layer % 2 == 1
        mix_cast = (expert_w_up[i], expert_w_down[i]) if is_moe and side_ok else ()
        y_ret, y_ssd, y_swa, *up_down_bf16 = _mixers(
            x, mod_l, w_cat, widths, dtypes, cos_t, sin_t, bias_tab, sinks[layer], conv_w[layer],
            conv_b[layer], dt_bias[layer], a_log[layer], d_skip[layer], ssd_norm_w[layer], tl,
            side_cast=mix_cast)
        attn_args = (x, y_ret, y_ssd, y_swa, mod_l, w_r, w_s, w_a,
                     ln_g[layer, 0][None, :], ln_b[layer, 0][None, :])

        g2 = ln_g[layer, 1][None, :]
        b2 = ln_b[layer, 1][None, :]
        if not is_moe:
            nxt = (layer + 1) // 2
            ride = (expert_w_gate[nxt],) if layer + 1 < depth and side_ok else ()
            x, *gate_bf16 = _dense_ffn(
                attn_args, ffn_w_gate[i].astype(BF16), ffn_w_up[i].astype(BF16),
                ffn_w_down[i].astype(BF16), g2, b2, alpha, tl, side_cast=ride)
        else:
            if side_ok:
                wg_e, = gate_bf16
                wu_e, wd_e = up_down_bf16
            else:
                wg_e, wu_e, wd_e = (expert_w_gate[i].astype(BF16), expert_w_up[i].astype(BF16),
                                    expert_w_down[i].astype(BF16))
            x = _moe(attn_args, router_w[i], router_b[i], wg_e, wu_e, wd_e, g2, b2, alpha,
                     tm_route=min(512, L), tme=min(512, L), tmf=min(512, L))
    return x
```

```python
import functools
import math

import numpy as np
import jax
import jax.numpy as jnp
from jax import lax
from jax.experimental import pallas as pl
from jax.experimental.pallas import tpu as pltpu

F32 = jnp.float32
BF16 = jnp.bfloat16
I32 = jnp.int32

HEAD_DIM = 64
CHUNK = 128
RET_HEADS = 4
SSD_HEADS = 8
SSD_GROUPS = 2
SSD_STATE = 64
SSD_CONV = 4
CONV_TAIL = 16
SWA_HEADS = 4
SWA_KV_HEADS = 2
REL_BUCKETS = 32
N_EXPERTS = 8
LN_EPS = 1e-5
LANES = 128
MASK_VALUE = -1e30

V7X_VMEM_BYTES = 64 * 1024 * 1024
VMEM_LIMIT = V7X_VMEM_BYTES - 8 * 1024 * 1024
TOKEN_TILE = 512


def _token_tile(seq_len):
    return min(TOKEN_TILE, seq_len)


def _cparams(*sem):
    return pltpu.CompilerParams(dimension_semantics=sem, vmem_limit_bytes=VMEM_LIMIT)


def _silu(v):
    return v * (1.0 / (1.0 + jnp.exp(-v)))


def _softplus(v):
    return jnp.maximum(v, 0.0) + jnp.log(1.0 + jnp.exp(-jnp.abs(v)))


def _dot(a, b):
    return jnp.dot(a, b, preferred_element_type=F32)


def _dot_nt(a, b):
    return lax.dot_general(a, b, (((1,), (1,)), ((), ())), preferred_element_type=F32)


def _hi_lo(v):
    bits = lax.bitcast_convert_type(v, jnp.uint32) & jnp.uint32(0xFFFF0000)
    hi = lax.bitcast_convert_type(bits, F32)
    return hi.astype(BF16), (v - hi).astype(BF16)


def _split3(v):
    h1 = v.astype(BF16)
    r1 = v - h1.astype(F32)
    h2 = r1.astype(BF16)
    r2 = r1 - h2.astype(F32)
    return h1, h2, r2.astype(BF16)


def _dot3(v, m3):
    return _dot(jnp.concatenate(_split3(v), axis=1), m3)


def _dot3_left(m3, v):
    return _dot(m3, jnp.concatenate(_split3(v), axis=0))


def _dot2(v, m2):
    h1 = v.astype(BF16)
    h2 = (v - h1.astype(F32)).astype(BF16)
    return _dot(jnp.concatenate([h1, h2], axis=1), m2)


def _layer_norm(r, g, b):
    mu = jnp.mean(r, -1, keepdims=True)
    d = r - mu
    var = jnp.mean(d * d, -1, keepdims=True)
    return d * lax.rsqrt(var + LN_EPS) * g + b


def _ada_kernel(c_ref, w_ref, b_ref, o_ref):
    c_hi, c_lo = _hi_lo(c_ref[...])
    w_hi, w_lo = _hi_lo(w_ref[0])
    cc = jnp.concatenate([c_hi, c_lo], axis=0)
    both = _dot(cc, w_hi) + _dot(cc, w_lo)
    rows = c_hi.shape[0]
    o_ref[0] = both[0:rows] + both[rows:2 * rows] + b_ref[0]


def _ada_mod(c, w_ada, b_ada):
    depth, d, d6 = w_ada.shape
    nb = c.shape[0]
    rows = 8
    c_pad = jnp.zeros((rows, d), F32).at[:nb].set(c)
    out = pl.pallas_call(
        _ada_kernel,
        out_shape=jax.ShapeDtypeStruct((depth, rows, d6), F32),
        grid=(depth, d6 // d),
        in_specs=[pl.BlockSpec((rows, d), lambda l, j: (0, 0)),
                  pl.BlockSpec((1, d, d), lambda l, j: (l, 0, j)),
                  pl.BlockSpec((1, 1, d), lambda l, j: (l, 0, j))],
        out_specs=pl.BlockSpec((1, rows, d), lambda l, j: (l, 0, j)),
        compiler_params=_cparams("arbitrary", "arbitrary"),
        name="ada_mod",
    )(c_pad, w_ada, b_ada.reshape(depth, 1, d6))
    return out[:, :nb].reshape(depth, nb, 6, d)


def _rotary_kernel(pos_ref, cos_ref, sin_ref):
    half = HEAD_DIM // 2
    lane = lax.broadcasted_iota(I32, (1, LANES), 1)
    jj = lane % HEAD_DIM
    idx = (jj % half).astype(F32)
    inv = jnp.exp(-math.log(10000.0) * idx / half)
    ang = pos_ref[0].astype(F32) * inv
    cos_ref[0] = jnp.cos(ang)
    sin_ref[0] = jnp.where(jj < half, -1.0, 1.0) * jnp.sin(ang)


def _rotary_tables(positions, tl):
    nb, L = positions.shape
    pos = positions.reshape(nb, L, 1)
    return pl.pallas_call(
        _rotary_kernel,
        out_shape=[jax.ShapeDtypeStruct((nb, L, LANES), F32)] * 2,
        grid=(nb, L // tl),
        in_specs=[pl.BlockSpec((1, tl, 1), lambda b, i: (b, i, 0))],
        out_specs=[pl.BlockSpec((1, tl, LANES), lambda b, i: (b, i, 0))] * 2,
        compiler_params=_cparams("arbitrary", "arbitrary"),
        name="rotary_tables",
    )(pos)


def _swa_bias_kernel(rb_ref, bucket_ref, band_ref, o_ref):
    bucket = bucket_ref[...]
    band = band_ref[...]
    for h in range(SWA_HEADS):
        acc = jnp.zeros(bucket.shape, F32)
        for b in range(REL_BUCKETS):
            acc = jnp.where(bucket == b, rb_ref[b, h], acc)
        o_ref[h] = jnp.where(band > 0, acc, MASK_VALUE)


def _t5_bucket(dist):
    exact = REL_BUCKETS // 2
    df = jnp.maximum(dist, 1).astype(F32)
    large = exact + (jnp.log(df / exact) / math.log(CHUNK / exact) * (REL_BUCKETS - exact)).astype(I32)
    large = jnp.minimum(large, REL_BUCKETS - 1)
    return jnp.where(dist < exact, dist, large)


def _swa_bias_table(rel_bias):
    W = CHUNK
    qi = jnp.arange(W)[:, None]
    kj = jnp.arange(2 * W)[None, :]
    dist = qi + W - kj
    band = ((dist >= 0) & (dist < W)).astype(I32)
    bucket = _t5_bucket(jnp.clip(dist, 0, W - 1)).astype(I32)
    return pl.pallas_call(
        _swa_bias_kernel,
        out_shape=jax.ShapeDtypeStruct((SWA_HEADS, W, 2 * W), F32),
        in_specs=[pl.BlockSpec(memory_space=pltpu.SMEM),
                  pl.BlockSpec(memory_space=pltpu.VMEM),
                  pl.BlockSpec(memory_space=pltpu.VMEM)],
        out_specs=pl.BlockSpec(memory_space=pltpu.VMEM),
        name="swa_bias_table",
    )(rel_bias, bucket, band)


def _head_lane_mask(width, head):
    lane = lax.broadcasted_iota(I32, (1, width), 1)
    return (lane // HEAD_DIM) == head


def _rotate_half(t):
    width = t.shape[-1]
    lane = lax.broadcasted_iota(I32, (1, width), 1)
    half = HEAD_DIM // 2
    fwd = pltpu.roll(t, width - half, axis=1)
    bwd = pltpu.roll(t, half, axis=1)
    return jnp.where((lane % HEAD_DIM) < half, fwd, bwd)


def _retention_body(u_ref, cos_ref, sin_ref, din_ref, dq_ref, dk_ref, dc_ref,
                    bmask_ref, avg_ref, o_ref, state_ref, *, n_chunks):
    rd = RET_HEADS * HEAD_DIM
    masks = [_head_lane_mask(rd, h) for h in range(RET_HEADS)]

    def stack_heads(t):
        return jnp.concatenate([jnp.where(m, t, 0.0) for m in masks], axis=0).astype(BF16)

    state = state_ref[...]
    for ci in range(n_chunks):
        rows = slice(ci * CHUNK, (ci + 1) * CHUNK)
        cos = cos_ref[0, rows, :]
        sin = sin_ref[0, rows, :]
        cos2 = jnp.concatenate([cos, cos], axis=1)
        sin2 = jnp.concatenate([sin, sin], axis=1)
        q = u_ref[0, rows, 0:rd].astype(F32)
        k = u_ref[0, rows, rd:2 * rd].astype(F32)
        v = u_ref[0, rows, 2 * rd:3 * rd].astype(F32)
        g = u_ref[0, rows, 3 * rd:4 * rd].astype(F32)
        qr = q * cos2 + _rotate_half(q) * sin2
        kr = (k * cos2 + _rotate_half(k) * sin2) * (HEAD_DIM ** -0.5)
        scores = _dot_nt(qr.astype(BF16), stack_heads(kr)) * din_ref[...]
        inner = _dot(scores.astype(BF16), stack_heads(v))
        cross = _dot((qr * dq_ref[...]).astype(BF16), state.astype(BF16))
        o = inner + cross
        kd_t = (kr * dk_ref[...]).T.astype(BF16)
        kv = _dot(kd_t, v.astype(BF16))
        state = dc_ref[...] * state + bmask_ref[...] * kv
        mu = _dot2(o, avg_ref[...])
        dev = o - mu
        var = _dot2(dev * dev, avg_ref[...])
        on = dev * lax.rsqrt(var + LN_EPS)
        o_ref[0, rows, :] = (_silu(g) * on).astype(o_ref.dtype)
        yield
    state_ref[...] = state


def _retention_tables():
    H, d, C = RET_HEADS, HEAD_DIM, CHUNK
    log_gamma = jnp.log(1.0 - 2.0 ** (-5.0 - jnp.arange(H, dtype=F32)))
    idx = jnp.arange(C, dtype=F32)
    diff = idx[:, None] - idx[None, :]
    decay_in = jnp.where(diff >= 0, jnp.exp(log_gamma[:, None, None] * jnp.maximum(diff, 0.0)), 0.0)
    decay_q = jnp.exp(log_gamma[:, None] * (idx + 1.0))
    decay_k = jnp.exp(log_gamma[:, None] * (C - 1.0 - idx))
    decay_chunk = jnp.exp(log_gamma * C)
    din = decay_in.transpose(1, 0, 2).reshape(C, H * C)
    dq = jnp.repeat(decay_q.T, d, axis=1)
    dk = jnp.repeat(decay_k.T, d, axis=1)
    dc = jnp.repeat(decay_chunk, d)[None, :]
    head = jnp.arange(H * d) // d
    bmask = (head[:, None] == head[None, :]).astype(F32)
    avg = jnp.tile((bmask / d).astype(BF16), (2, 1))
    return din, dq, dk, dc, bmask, avg


def _swa_body(u_ref, bias_ref, sink_ref, o_ref, kprev_ref, vprev_ref, *, n_chunks, first_step):
    W = CHUNK
    qd = SWA_HEADS * HEAD_DIM
    kvd = SWA_KV_HEADS * HEAD_DIM
    lane = lax.broadcasted_iota(I32, (1, LANES), 1)
    low = lane < HEAD_DIM
    col = lax.broadcasted_iota(I32, (1, 2 * W), 1)

    kprev = kprev_ref[...]
    vprev = vprev_ref[...]
    sink = sink_ref[...]
    for ci in range(n_chunks):
        rows = slice(ci * W, (ci + 1) * W)
        qa = u_ref[0, rows, 0:LANES].astype(F32)
        qb = u_ref[0, rows, LANES:qd].astype(F32)
        k = u_ref[0, rows, qd:qd + kvd].astype(BF16)
        v = u_ref[0, rows, qd + kvd:qd + 2 * kvd].astype(BF16)
        q4 = jnp.concatenate([jnp.where(low, qa, 0.0), jnp.where(low, 0.0, qa),
                              jnp.where(low, qb, 0.0), jnp.where(low, 0.0, qb)],
                             axis=0).astype(BF16)
        kband = jnp.concatenate([kprev, k], axis=0)
        vband = jnp.concatenate([vprev, v], axis=0)
        logits = _dot_nt(q4, kband) * (HEAD_DIM ** -0.5) + bias_ref[...]
        if ci == 0:
            logits = jnp.where(jnp.logical_and(first_step, col < W), MASK_VALUE, logits)
        m = jnp.maximum(jnp.max(logits, -1, keepdims=True), sink)
        p = jnp.exp(logits - m)
        denom = jnp.sum(p, -1, keepdims=True) + jnp.exp(sink - m)
        res = _dot(p.astype(BF16), vband) / denom
        out_a = jnp.where(low, res[0:W], res[W:2 * W])
        out_b = jnp.where(low, res[2 * W:3 * W], res[3 * W:4 * W])
        o_ref[0, rows, 0:LANES] = out_a.astype(o_ref.dtype)
        o_ref[0, rows, LANES:qd] = out_b.astype(o_ref.dtype)
        kprev, vprev = k, v
        yield
    kprev_ref[...] = kprev
    vprev_ref[...] = vprev


SWA_HEAD_ORDER = (0, 2, 1, 3)


def _ssd_body(u_ref, dt_ref, cw_ref, cb_ref, dtb_c_ref, alog_c_ref, dskip_ref, nw_ref,
              tril_ref, expand_ref, gmask_ref, shift_ref,
              o_ref, state_ref, ext_ref, *, n_chunks):
    C = CHUNK
    sd = SSD_HEADS * HEAD_DIM
    gn = SSD_GROUPS * SSD_STATE
    cd = sd + 2 * gn
    tl = n_chunks * C
    slab = 2 * LANES
    heads_per_group = SSD_HEADS // SSD_GROUPS
    heads_per_slab = slab // HEAD_DIM
    lane = lax.broadcasted_iota(I32, (1, LANES), 1)
    low = lane < SSD_STATE
    slab_masks = [_head_lane_mask(slab, hh) for hh in range(heads_per_slab)]
    row_i = lax.broadcasted_iota(I32, (C, C), 0)
    col_i = lax.broadcasted_iota(I32, (C, C), 1)
    causal = row_i >= col_i
    neg_a_c = -jnp.exp(alog_c_ref[...])

    ext_ref[CONV_TAIL:CONV_TAIL + tl, :] = u_ref[0, :, sd:sd + cd]
    state = state_ref[...]
    for ci in range(n_chunks):
        rows = slice(ci * C, (ci + 1) * C)
        z = u_ref[0, rows, 0:sd].astype(F32)
        dt_raw = dt_ref[0, rows, :]
        window = ext_ref[ci * C:ci * C + CONV_TAIL + C, :]
        shifted = _dot(shift_ref[...], window)
        conv = cb_ref[...] + cw_ref[SSD_CONV - 1:SSD_CONV, :] * window[CONV_TAIL:, :].astype(F32)
        for w in range(SSD_CONV - 1):
            conv = conv + cw_ref[w:w + 1, :] * shifted[w * C:(w + 1) * C, :]
        xbc = _silu(conv)
        xs = xbc[:, 0:sd]
        bm = xbc[:, sd:sd + gn]
        cm = xbc[:, sd + gn:cd]

        dt_c = _softplus(dt_raw + dtb_c_ref[...])
        a_c = neg_a_c * dt_c
        acs_c = _dot3_left(tril_ref[...], a_c)
        acs_t = acs_c.T
        spread = _dot3(acs_c, expand_ref[...])
        acs_x = spread[:, 0:sd]
        dt_x = _dot3(dt_c, expand_ref[:, 0:sd])
        xdt = xs * dt_x
        yield

        bstack = jnp.concatenate([jnp.where(low, bm, 0.0), jnp.where(low, 0.0, bm)],
                                 axis=0).astype(BF16)
        cb = _dot_nt(cm.astype(BF16), bstack)
        y_diag = []
        for s in range(sd // slab):
            ms = []
            for hh in range(heads_per_slab):
                h = s * heads_per_slab + hh
                g = h // heads_per_group
                col_bcast = spread[:, sd + h * LANES:sd + (h + 1) * LANES]
                seg = col_bcast - acs_t[h:h + 1, :]
                lmat = jnp.exp(jnp.where(causal, seg, MASK_VALUE))
                ms.append((cb[:, g * C:(g + 1) * C] * lmat).astype(BF16))
            xslab = xdt[:, s * slab:(s + 1) * slab]
            xstack = jnp.concatenate([jnp.where(m, xslab, 0.0) for m in slab_masks],
                                     axis=0).astype(BF16)
            y_diag.append(_dot(jnp.concatenate(ms, axis=1), xstack))
        y_diag = jnp.concatenate(y_diag, axis=1)
        yield

        y_off = _dot(cm.astype(BF16), state.astype(BF16)) * jnp.exp(acs_x)
        last = acs_x[C - 1:C, :]
        dec = jnp.exp(last - acs_x)
        new = _dot(bm.T.astype(BF16), (xdt * dec).astype(BF16))
        state = jnp.exp(last) * state + gmask_ref[...] * new

        y = y_diag + y_off + xs * dskip_ref[...]
        hgate = y * _silu(z)
        gw = sd // SSD_GROUPS
        for g in range(SSD_GROUPS):
            hg = hgate[:, g * gw:(g + 1) * gw]
            ms_ = jnp.mean(hg * hg, -1, keepdims=True)
            o_ref[0, rows, g * gw:(g + 1) * gw] = (
                hg * lax.rsqrt(ms_ + LN_EPS) * nw_ref[:, g * gw:(g + 1) * gw]).astype(o_ref.dtype)
        yield
    state_ref[...] = state
    ext_ref[0:CONV_TAIL, :] = ext_ref[tl:tl + CONV_TAIL, :]


def _ssd_tables():
    C = CHUNK
    sd = SSD_HEADS * HEAD_DIM
    gn = SSD_GROUPS * SSD_STATE
    t = jnp.arange(C)
    tril = (t[:, None] >= t[None, :]).astype(BF16)
    r = jnp.arange(LANES)[:, None]
    eexp = ((r == (jnp.arange(sd)[None, :] // HEAD_DIM)) & (r < SSD_HEADS)).astype(BF16)
    bsel = ((r == (jnp.arange(SSD_HEADS * LANES)[None, :] // LANES)) & (r < SSD_HEADS)).astype(BF16)
    heads_per_group = SSD_HEADS // SSD_GROUPS
    row_g = jnp.arange(gn)[:, None] // SSD_STATE
    col_g = (jnp.arange(sd)[None, :] // HEAD_DIM) // heads_per_group
    gmask = (row_g == col_g).astype(F32)
    expand = jnp.concatenate([eexp, bsel], axis=1)
    win = jnp.arange(CONV_TAIL + C)[None, :]
    shifts = jnp.concatenate(
        [(win == CONV_TAIL + t[:, None] - (SSD_CONV - 1 - s)).astype(BF16)
         for s in range(SSD_CONV - 1)], axis=0)
    return jnp.tile(tril, (1, 3)), jnp.tile(expand, (3, 1)), gmask, shifts


N_RET_TABLES = 6
N_SSD_CONSTS = 10


def _in_proj_body(x_ref, shift, scale, w_ref, out_refs):
    h = (x_ref[0] * (1.0 + scale) + shift).astype(BF16)
    off = 0
    for ref in out_refs:
        width = ref.shape[-1]
        for a, b in _ff_chunks(width):
            ref[0, :, a:b] = _dot(h, w_ref[:, off + a:off + b]).astype(ref.dtype)
            yield
        off += width


def _mixers_kernel(*refs, n_chunks, n_cast, n_tiles, tiles_per_seq):
    x_ref, mod_ref, w_ref, cos, sin = refs[:5]
    pos = 5
    ret_tables = refs[pos:pos + N_RET_TABLES]
    pos += N_RET_TABLES
    ssd_consts = refs[pos:pos + N_SSD_CONSTS]
    pos += N_SSD_CONSTS
    bias, sink = refs[pos:pos + 2]
    pos += 2
    cast_in = refs[pos:pos + n_cast]
    pos += n_cast
    y_ret, y_ssd, y_swa = refs[pos:pos + 3]
    pos += 3
    cast_out = refs[pos:pos + n_cast]
    pos += n_cast
    ret_state, ssd_state, ssd_ext, kprev, vprev = refs[pos:pos + 5]
    cur = refs[pos + 5:pos + 9]
    nxt = refs[pos + 9:pos + 13]
    _side_cast(cast_in, cast_out)

    s = pl.program_id(0)
    tile = jnp.maximum(s - 1, 0)
    seq_start = tile % tiles_per_seq == 0

    @pl.when(s == 0)
    def _():
        for ref in cur:
            ref[...] = jnp.zeros_like(ref)

    @pl.when(seq_start)
    def _():
        ret_state[...] = jnp.zeros_like(ret_state)
        ssd_state[...] = jnp.zeros_like(ssd_state)
        ssd_ext[0:CONV_TAIL, :] = jnp.zeros((CONV_TAIL, ssd_ext.shape[1]), ssd_ext.dtype)
        kprev[...] = jnp.zeros_like(kprev)
        vprev[...] = jnp.zeros_like(vprev)

    batch = jnp.minimum(s, n_tiles - 1) // tiles_per_seq
    shift = mod_ref[0, batch, 0:1, :]
    scale = mod_ref[0, batch, 1:2, :]
    u_ret, u_ssd, u_dt, u_swa = cur
    bodies = [
        _ssd_body(u_ssd, u_dt, *ssd_consts, y_ssd, ssd_state, ssd_ext, n_chunks=n_chunks),
        _in_proj_body(x_ref, shift, scale, w_ref, nxt),
        _swa_body(u_swa, bias, sink, y_swa, kprev, vprev, n_chunks=n_chunks,
                  first_step=seq_start),
        _retention_body(u_ret, cos, sin, *ret_tables, y_ret, ret_state, n_chunks=n_chunks),
    ]
    alive = list(bodies)
    while alive:
        alive = [b for b in alive if next(b, StopIteration) is not StopIteration]
    for c_ref, n_ref in zip(cur, nxt):
        c_ref[...] = n_ref[...]


def _mixers(x, mod_l, w_cat, widths, dtypes, cos_t, sin_t, bias_tab, sinks_l, conv_w, conv_b,
            dt_bias, a_log, d_skip, norm_w, tl, side_cast=()):
    nb, L, d = x.shape
    W = CHUNK
    rd = RET_HEADS * HEAD_DIM
    sd = SSD_HEADS * HEAD_DIM
    qd = SWA_HEADS * HEAD_DIM
    cd = conv_w.shape[-1]
    ret_tables = _retention_tables()
    pad = lambda v: jnp.zeros((1, LANES), F32).at[0, :SSD_HEADS].set(v)
    rep = lambda v: jnp.repeat(v, HEAD_DIM)[None, :]
    ssd_consts = (conv_w, conv_b[None, :], pad(dt_bias), pad(a_log), rep(d_skip),
                  norm_w[None, :]) + _ssd_tables()
    order = jnp.array(SWA_HEAD_ORDER)
    bias_stacked = bias_tab[order].reshape(SWA_HEADS * W, 2 * W)
    sink_col = jnp.repeat(sinks_l.astype(F32)[order], W)[:, None]
    assert len(ret_tables) == N_RET_TABLES and len(ssd_consts) == N_SSD_CONSTS
    tps = L // tl
    n_tiles = nb * tps

    def tile_of(step):
        return step // tps, step % tps, 0

    proj_map = lambda s: tile_of(jnp.minimum(s, n_tiles - 1))
    mix_map = lambda s: tile_of(jnp.maximum(s - 1, 0))
    const = lambda a: pl.BlockSpec(a.shape, lambda s: (0,) * a.ndim)
    consts = ret_tables + ssd_consts + (bias_stacked, sink_col)
    c_in, c_out, c_shapes = _side_cast_specs(side_cast, n_tiles)
    u_bufs = [pltpu.VMEM((1, tl, w), t) for w, t in zip(widths, dtypes)]
    return pl.pallas_call(
        functools.partial(_mixers_kernel, n_chunks=tl // CHUNK, n_cast=len(side_cast),
                          n_tiles=n_tiles, tiles_per_seq=tps),
        out_shape=[jax.ShapeDtypeStruct((nb, L, w), BF16) for w in (rd, sd, qd)] + c_shapes,
        grid=(n_tiles + 1,),
        in_specs=[pl.BlockSpec((1, tl, d), proj_map), const(mod_l), const(w_cat),
                  pl.BlockSpec((1, tl, LANES), mix_map), pl.BlockSpec((1, tl, LANES), mix_map)]
                 + [const(a) for a in consts] + c_in,
        out_specs=[pl.BlockSpec((1, tl, w), mix_map) for w in (rd, sd, qd)] + c_out,
        scratch_shapes=[pltpu.VMEM((rd, rd), F32),
                        pltpu.VMEM((SSD_GROUPS * SSD_STATE, sd), F32),
                        pltpu.VMEM((CONV_TAIL + tl, cd), BF16),
                        pltpu.VMEM((W, LANES), BF16), pltpu.VMEM((W, LANES), BF16)]
                       + u_bufs + u_bufs,
        compiler_params=_cparams("arbitrary"),
        name="in_proj_mixers",
    )(x, mod_l, w_cat, cos_t, sin_t, *consts, *side_cast)


N_ATTN_OUT = 10


def _attn_out(x_ref, yr_ref, ys_ref, ya_ref, mod_ref, wr_ref, ws_ref, wa_ref, g_ref, b_ref,
              alpha):
    mix = (_dot(yr_ref[0], wr_ref[...]) + _dot(ys_ref[0], ws_ref[...])
           + _dot(ya_ref[0], wa_ref[...]))
    gate = mod_ref[0, 0, 2:3, :]
    r = alpha * x_ref[0] + (1.0 + gate) * mix
    return _layer_norm(r, g_ref[...], b_ref[...])


def _attn_out_operands(x, y_ret, y_ssd, y_swa, mod_l, w_r, w_s, w_a, ln_g, ln_b, tm):
    d = x.shape[-1]
    tok = lambda w: pl.BlockSpec((1, tm, w), lambda b, i: (b, i, 0))
    const = lambda a: pl.BlockSpec(a.shape, lambda b, i: (0,) * a.ndim)
    specs = [tok(d), tok(y_ret.shape[-1]), tok(y_ssd.shape[-1]), tok(y_swa.shape[-1]),
             pl.BlockSpec((1, 1, 6, d), lambda b, i: (0, b, 0, 0)),
             const(w_r), const(w_s), const(w_a), const(ln_g), const(ln_b)]
    return specs, (x, y_ret, y_ssd, y_swa, mod_l, w_r, w_s, w_a, ln_g, ln_b)


def _side_cast_specs(arrays, steps, steps_per_seq=None):
    in_specs, out_specs, out_shapes = [], [], []
    for a in arrays:
        e, r, c = a.shape
        per = steps // e
        if steps_per_seq is None:
            index = lambda s, per=per: (jnp.minimum(s, steps - 1) // per,
                                        jnp.minimum(s, steps - 1) % per, 0)
        else:
            index = lambda b, i, per=per: ((b * steps_per_seq + i) // per,
                                           (b * steps_per_seq + i) % per, 0)
        spec = pl.BlockSpec((1, r // per, c), index)
        in_specs.append(spec)
        out_specs.append(spec)
        out_shapes.append(jax.ShapeDtypeStruct(a.shape, BF16))
    return in_specs, out_specs, out_shapes


def _side_cast(in_refs, out_refs):
    for i_ref, o_ref in zip(in_refs, out_refs):
        o_ref[...] = i_ref[...].astype(o_ref.dtype)


def _dense_ffn_kernel(*refs, alpha):
    attn = refs[:N_ATTN_OUT]
    mod_ref = attn[4]
    wg_ref, wu_ref, wd_ref, g_ref, b_ref = refs[N_ATTN_OUT:N_ATTN_OUT + 5]
    rest = refs[N_ATTN_OUT + 5:]
    n_cast = (len(rest) - 1) // 2
    o_ref = rest[n_cast]
    _side_cast(rest[:n_cast], rest[n_cast + 1:])
    x = _attn_out(*attn, alpha)
    sh = mod_ref[0, 0, 3:4, :]
    sc = mod_ref[0, 0, 4:5, :]
    gate = mod_ref[0, 0, 5:6, :]
    h = (x * (1.0 + sc) + sh).astype(BF16)
    dff = wg_ref.shape[-1]
    acc = None
    for a, b in _ff_chunks(dff):
        gj = _dot(h, wg_ref[:, a:b])
        uj = _dot(h, wu_ref[:, a:b])
        part = _dot((_silu(gj) * uj).astype(BF16), wd_ref[a:b, :])
        acc = part if acc is None else acc + part
    r = alpha * x + (1.0 + gate) * acc
    o_ref[0] = _layer_norm(r, g_ref[...], b_ref[...])


def _dense_ffn(attn_args, wg, wu, wd, ln_g, ln_b, alpha, tm, side_cast=()):
    nb, L, d = attn_args[0].shape
    const = lambda a: pl.BlockSpec(a.shape, lambda b, i: (0,) * a.ndim,
                                   pipeline_mode=pl.Buffered(1))
    a_specs, a_ops = _attn_out_operands(*attn_args, tm)
    c_in, c_out, c_shapes = _side_cast_specs(side_cast, nb * (L // tm), L // tm)
    return pl.pallas_call(
        functools.partial(_dense_ffn_kernel, alpha=alpha),
        out_shape=[jax.ShapeDtypeStruct((nb, L, d), F32)] + c_shapes,
        grid=(nb, L // tm),
        in_specs=a_specs + [const(wg), const(wu), const(wd), const(ln_g), const(ln_b)] + c_in,
        out_specs=[pl.BlockSpec((1, tm, d), lambda b, i: (b, i, 0))] + c_out,
        compiler_params=_cparams("arbitrary", "arbitrary"),
        name="attn_out_dense_ffn_ln",
    )(*a_ops, wg, wu, wd, ln_g, ln_b, *side_cast)


def _store_row_tiles(ref, value):
    rows, d = value.shape
    nt = d // LANES
    for c in range(nt):
        ref[pl.ds(c, rows, stride=nt), :] = value[:, c * LANES:(c + 1) * LANES].astype(ref.dtype)


def _load_row_tiles(ref):
    nt = ROW_TILE
    rows = ref.shape[0] // nt
    return jnp.concatenate([ref[pl.ds(c, rows, stride=nt), :] for c in range(nt)], axis=1)


def _router_kernel(*refs, alpha):
    attn = refs[:N_ATTN_OUT]
    mod_ref = attn[4]
    wr_ref, br_ref, trils_ref, x1_ref, h_ref, route_ref, cnt_ref, base_ref = refs[N_ATTN_OUT:]
    first = jnp.logical_and(pl.program_id(0) == 0, pl.program_id(1) == 0)

    @pl.when(first)
    def _():
        base_ref[...] = jnp.zeros_like(base_ref)

    x1 = _attn_out(*attn, alpha)
    x1_ref[0] = x1
    sh = mod_ref[0, 0, 3:4, :]
    sc = mod_ref[0, 0, 4:5, :]
    h = x1 * (1.0 + sc) + sh
    _store_row_tiles(h_ref.at[0], h)
    h_hi, h_lo = _hi_lo(h)
    wide = _dot(h_hi, wr_ref[...]) + _dot(h_lo, wr_ref[...])
    logits = wide + pltpu.roll(wide, LANES - N_EXPERTS, axis=1) + br_ref[...]
    lane = lax.broadcasted_iota(I32, logits.shape, 1).astype(F32)
    logits = jnp.where(lane < N_EXPERTS, logits, MASK_VALUE)
    v1 = jnp.max(logits, -1, keepdims=True)
    e1 = jnp.min(jnp.where(logits == v1, lane, float(LANES)), -1, keepdims=True)
    rest = jnp.where(lane == e1, MASK_VALUE, logits)
    v2 = jnp.max(rest, -1, keepdims=True)
    e2 = jnp.min(jnp.where(rest == v2, lane, float(LANES)), -1, keepdims=True)
    t = jnp.exp(v2 - v1)
    w1 = 1.0 / (1.0 + t)
    w2 = t / (1.0 + t)
    hot1 = (lane == e1).astype(F32)
    hot2 = (lane == e2).astype(F32)
    both = hot1 + hot2
    base = base_ref[0:1, :]
    before = _dot(trils_ref[...], both.astype(BF16)) + base
    rank1 = jnp.sum(hot1 * before, -1, keepdims=True)
    rank2 = jnp.sum(hot2 * before, -1, keepdims=True)
    total = base + jnp.sum(both, 0, keepdims=True)
    base_ref[0:1, :] = total
    cnt_ref[...] = jnp.broadcast_to(total, cnt_ref.shape)
    route = jnp.where(lane == 0, e1, 0.0)
    route = jnp.where(lane == 1, e2, route)
    route = jnp.where(lane == 2, rank1, route)
    route = jnp.where(lane == 3, rank2, route)
    route = jnp.where(lane == 4, w1, route)
    route = jnp.where(lane == 5, w2, route)
    route_ref[0] = route


def _router(attn_args, w_router, b_router, alpha, tm):
    nb, L, d = attn_args[0].shape
    w_hi, w_lo = _hi_lo(w_router)
    wr = (jnp.zeros((d, LANES), BF16).at[:, :N_EXPERTS].set(w_hi)
          .at[:, N_EXPERTS:2 * N_EXPERTS].set(w_lo))
    br = jnp.zeros((1, LANES), F32).at[0, :N_EXPERTS].set(b_router)
    t = jnp.arange(tm)
    tril_strict = (t[:, None] > t[None, :]).astype(BF16)
    const = lambda a: pl.BlockSpec(a.shape, lambda b, i: (0,) * a.ndim)
    a_specs, a_ops = _attn_out_operands(*attn_args, tm)
    return pl.pallas_call(
        functools.partial(_router_kernel, alpha=alpha),
        out_shape=[jax.ShapeDtypeStruct((nb, L, d), F32),
                   jax.ShapeDtypeStruct((nb, L * (d // LANES), LANES), F32),
                   jax.ShapeDtypeStruct((nb, L, LANES), F32),
                   jax.ShapeDtypeStruct((8, LANES), F32)],
        grid=(nb, L // tm),
        in_specs=a_specs + [const(wr), const(br), const(tril_strict)],
        out_specs=[pl.BlockSpec((1, tm, d), lambda b, i: (b, i, 0)),
                   pl.BlockSpec((1, tm * (d // LANES), LANES), lambda b, i: (b, i, 0)),
                   pl.BlockSpec((1, tm, LANES), lambda b, i: (b, i, 0)),
                   pl.BlockSpec((8, LANES), lambda b, i: (0, 0))],
        scratch_shapes=[pltpu.VMEM((8, LANES), F32)],
        compiler_params=_cparams("arbitrary", "arbitrary"),
        name="attn_out_moe_router",
    )(*a_ops, wr, br, tril_strict)


TOP_K = 2
ROW_TILE = 8
EXPERT_FF_SPLITS = 2


def _ff_chunks(width, step=512):
    return [(a, min(a + step, width)) for a in range(0, width, step)]


def _expert_kernel(te_ref, nu_ref, gsrc_ref, sdst_ref,
                   h_ref, wg_ref, wu_ref, wd_ref, yk_ref,
                   xbuf, ybuf, hbuf, acc, gsem, ssem, *, tme):
    i = pl.program_id(0)
    j = pl.program_id(1)
    n_used = nu_ref[0]
    slot = i % 2
    other = 1 - slot
    used = i < n_used
    chunks = _ff_chunks(wg_ref.shape[-1])
    rows_per_step = tme // EXPERT_FF_SPLITS
    n_front = len(chunks) // 2
    front_w = chunks[n_front - 1][1]
    back_w = wg_ref.shape[-1] - front_w
    gather_ranges, scatter_ranges = [], []
    for c, (a, b) in enumerate(chunks):
        if c < n_front:
            gather_ranges.append((rows_per_step * a // front_w, rows_per_step * b // front_w))
            scatter_ranges.append((0, 0))
        else:
            gather_ranges.append((0, 0))
            scatter_ranges.append((rows_per_step * (a - front_w) // back_w,
                                   rows_per_step * (b - front_w) // back_w))

    nt = ROW_TILE

    def gather(tile, slot_, r):
        src = pl.multiple_of(gsrc_ref[tile * tme + r], nt)
        return pltpu.make_async_copy(h_ref.at[pl.ds(src, nt)],
                                     xbuf.at[slot_, pl.ds(pl.multiple_of(r * nt, nt), nt)],
                                     gsem.at[slot_])

    def wait_gather(slot_):
        pltpu.make_async_copy(h_ref.at[pl.ds(0, tme * nt)], xbuf.at[slot_],
                              gsem.at[slot_]).wait()

    def scatter(block, slot_, r):
        dst = pl.multiple_of(sdst_ref[block * tme + r], nt)
        return pltpu.make_async_copy(ybuf.at[slot_, pl.ds(pl.multiple_of(r * nt, nt), nt)],
                                     yk_ref.at[pl.ds(dst, nt)], ssem.at[slot_])

    def wait_scatter(slot_):
        pltpu.make_async_copy(ybuf.at[slot_], yk_ref.at[pl.ds(0, tme * nt)],
                              ssem.at[slot_]).wait()

    @pl.when(jnp.logical_and(i == 0, j == 0))
    def _():
        ybuf[1] = jnp.zeros(ybuf.shape[1:], F32)

        def body(r, carry):
            gather(0, 0, r).start()
            return carry
        lax.fori_loop(0, tme, body, 0, unroll=8)

    @pl.when(used)
    def _():
        @pl.when(j == 0)
        def _():
            wait_gather(slot)
            hbuf[...] = _load_row_tiles(xbuf.at[slot]).astype(BF16)

            @pl.when(i > 0)
            def _():
                wait_scatter(slot)
            acc[...] = jnp.zeros(acc.shape, F32)

        h = hbuf[...]
        for c, (a, b) in enumerate(chunks):
            for r in range(*gather_ranges[c]):
                gather(i + 1, other, j * rows_per_step + r).start()
            for r in range(*scatter_ranges[c]):
                scatter(i, other, j * rows_per_step + r).start()
            gj = _dot(h, wg_ref[0, :, a:b])
            uj = _dot(h, wu_ref[0, :, a:b])
            part = _dot((_silu(gj) * uj).astype(BF16), wd_ref[0, a:b, :])
            acc[...] = acc[...] + part
            ybuf[slot, 0:ROW_TILE, :] = part[0:ROW_TILE, 0:LANES]

        @pl.when(j == EXPERT_FF_SPLITS - 1)
        def _():
            _store_row_tiles(ybuf.at[slot], acc[...])

        @pl.when(jnp.logical_and(j == EXPERT_FF_SPLITS - 1, i == n_used - 1))
        def _():
            wait_gather(other)
            wait_scatter(other)

            def body(r, carry):
                scatter(i + 1, slot, r).start()
                return carry
            lax.fori_loop(0, tme, body, 0, unroll=8)
            wait_scatter(slot)


def _expert_ffn(h_rows, tile_expert, n_used, gsrc, sdst, wg, wu, wd, tme):
    nt = ROW_TILE
    T = h_rows.shape[0] // nt
    d = nt * LANES
    dff = wg.shape[-1]
    dffh = dff // EXPERT_FF_SPLITS
    n_tiles = gsrc.shape[0] // tme
    yk_rows = TOP_K * T + tme

    def half(i, j):
        return jnp.where(i % 2 == 0, j, EXPERT_FF_SPLITS - 1 - j)

    return pl.pallas_call(
        functools.partial(_expert_kernel, tme=tme),
        out_shape=jax.ShapeDtypeStruct((yk_rows * nt, LANES), F32),
        grid_spec=pltpu.PrefetchScalarGridSpec(
            num_scalar_prefetch=4,
            grid=(n_tiles, EXPERT_FF_SPLITS),
            in_specs=[pl.BlockSpec(memory_space=pl.ANY),
                      pl.BlockSpec((1, d, dffh), lambda i, j, te, *_: (te[i], 0, half(i, j))),
                      pl.BlockSpec((1, d, dffh), lambda i, j, te, *_: (te[i], 0, half(i, j))),
                      pl.BlockSpec((1, dffh, d), lambda i, j, te, *_: (te[i], half(i, j), 0))],
            out_specs=pl.BlockSpec(memory_space=pl.ANY),
            scratch_shapes=[pltpu.VMEM((2, tme * nt, LANES), F32),
                            pltpu.VMEM((2, tme * nt, LANES), F32),
                            pltpu.VMEM((tme, d), BF16), pltpu.VMEM((tme, d), F32),
                            pltpu.SemaphoreType.DMA((2,)), pltpu.SemaphoreType.DMA((2,))]),
        compiler_params=_cparams("arbitrary", "arbitrary"),
        name="moe_experts",
    )(tile_expert, n_used, gsrc, sdst, h_rows, wg, wu, wd)


def _moe_finish_kernel(x_ref, route_ref, mod_ref, y0_ref, y1_ref, g_ref, b_ref, o_ref, *, alpha):
    route = route_ref[0]
    f = route[:, 4:5] * _load_row_tiles(y0_ref) + route[:, 5:6] * _load_row_tiles(y1_ref)
    gate = mod_ref[0, 0, 5:6, :]
    r = alpha * x_ref[0] + (1.0 + gate) * f
    o_ref[0] = _layer_norm(r, g_ref[...], b_ref[...])


def _moe_finish(x, route, mod_l, yk, ln_g, ln_b, alpha, tm):
    nb, L, d = x.shape
    tiles_per_seq = L // tm
    tiles = nb * tiles_per_seq
    return pl.pallas_call(
        functools.partial(_moe_finish_kernel, alpha=alpha),
        out_shape=jax.ShapeDtypeStruct((nb, L, d), F32),
        grid=(nb, tiles_per_seq),
        in_specs=[pl.BlockSpec((1, tm, d), lambda b, i: (b, i, 0)),
                  pl.BlockSpec((1, tm, LANES), lambda b, i: (b, i, 0)),
                  pl.BlockSpec((1, 1, 6, d), lambda b, i: (0, b, 0, 0)),
                  pl.BlockSpec((tm * ROW_TILE, LANES), lambda b, i: (b * tiles_per_seq + i, 0)),
                  pl.BlockSpec((tm * ROW_TILE, LANES),
                               lambda b, i: (tiles + b * tiles_per_seq + i, 0)),
                  pl.BlockSpec(ln_g.shape, lambda b, i: (0, 0)),
                  pl.BlockSpec(ln_b.shape, lambda b, i: (0, 0))],
        out_specs=pl.BlockSpec((1, tm, d), lambda b, i: (b, i, 0)),
        compiler_params=_cparams("arbitrary", "arbitrary"),
        name="moe_finish_ln",
    )(x, route, mod_l, yk, yk, ln_g, ln_b)


def _row_index_kernel(dest_ref, gsrc0_ref, sdst0_ref, gsrc_ref, sdst_ref, sem, *, n_pairs, T, tme):
    init_g = pltpu.make_async_copy(gsrc0_ref, gsrc_ref, sem.at[0])
    init_s = pltpu.make_async_copy(sdst0_ref, sdst_ref, sem.at[1])
    init_g.start()
    init_s.start()
    init_g.wait()
    init_s.wait()

    assert TOP_K == 2
    group = 8

    def body(g, carry):
        f0 = g * group
        rows = [dest_ref[f0 + k] for k in range(group)]
        for k in range(group):
            tok = g * (group // TOP_K) + k // TOP_K
            gsrc_ref[rows[k]] = tok * ROW_TILE
            sdst_ref[tme + rows[k]] = ((k % TOP_K) * T + tok) * ROW_TILE
        return carry
    lax.fori_loop(0, n_pairs // group, body, 0, unroll=2)


def _row_indices(dest, n_rows, T, tme):
    dump = (TOP_K * T + jnp.arange(n_rows + tme, dtype=I32) % tme) * ROW_TILE
    return pl.pallas_call(
        functools.partial(_row_index_kernel, n_pairs=dest.shape[0], T=T, tme=tme),
        out_shape=[jax.ShapeDtypeStruct((n_rows,), I32),
                   jax.ShapeDtypeStruct((n_rows + tme,), I32)],
        in_specs=[pl.BlockSpec(memory_space=pltpu.SMEM),
                  pl.BlockSpec(memory_space=pl.ANY),
                  pl.BlockSpec(memory_space=pl.ANY)],
        out_specs=[pl.BlockSpec(memory_space=pltpu.SMEM),
                   pl.BlockSpec(memory_space=pltpu.SMEM)],
        scratch_shapes=[pltpu.SemaphoreType.DMA((2,))],
        name="moe_row_indices",
    )(dest, jnp.zeros((n_rows,), I32), dump)


def _moe(attn_args, w_router, b_router, wg, wu, wd, ln_g, ln_b, alpha, tm_route, tme, tmf):
    nb, L, d = attn_args[0].shape
    mod_l = attn_args[4]
    T = nb * L
    x, h, route, counts = _router(attn_args, w_router, b_router, alpha, tm_route)
    route_flat = route.reshape(T, LANES)
    e = route_flat[:, 0:2].astype(I32)
    rank = route_flat[:, 2:4].astype(I32)
    cnt = counts[0, :N_EXPERTS].astype(I32)
    tiles = (cnt + tme - 1) // tme
    tile_end = jnp.cumsum(tiles)
    group_start = (tile_end - tiles) * tme
    dest = (group_start[e] + rank).reshape(-1)
    n_tiles = (TOP_K * T) // tme + N_EXPERTS
    n_rows = n_tiles * tme
    tile_expert = jnp.minimum(
        jnp.sum(jnp.arange(n_tiles, dtype=I32)[:, None] >= tile_end[None, :].astype(I32), axis=1),
        N_EXPERTS - 1).astype(I32)
    n_used = tile_end[-1:].astype(I32)
    gsrc, sdst = _row_indices(dest, n_rows, T, tme)
    yk = _expert_ffn(h.reshape(T * ROW_TILE, LANES), tile_expert, n_used, gsrc, sdst,
                     wg, wu, wd, tme)
    return _moe_finish(x, route, mod_l, yk, ln_g, ln_b, alpha, tmf)


def kernel(x, c, positions, rel_bias, w_ada, b_ada, w_in, w_out, conv_w, conv_b, dt_bias, a_log,
           d_skip, ssd_norm_w, sinks, ln_g, ln_b, ffn_w_gate, ffn_w_up, ffn_w_down, router_w,
           router_b, expert_w_gate, expert_w_up, expert_w_down):
    depth = w_ada.shape[0]
    nb, L, d = x.shape
    alpha = (2 * depth) ** 0.25
    rd = RET_HEADS * HEAD_DIM
    sd = SSD_HEADS * HEAD_DIM
    cd = conv_w.shape[-1]
    qd = SWA_HEADS * HEAD_DIM
    kvd = SWA_KV_HEADS * HEAD_DIM
    sizes = (rd, rd, rd, rd, sd, cd, SSD_HEADS, qd, kvd, kvd)
    offs = np.concatenate([[0], np.cumsum(sizes)])
    tl = _token_tile(L)

    mod = _ada_mod(c, w_ada, b_ada)
    cos_t, sin_t = _rotary_tables(positions, tl)
    bias_tab = _swa_bias_table(rel_bias)

    q_perm = np.concatenate([np.arange(h * HEAD_DIM, (h + 1) * HEAD_DIM) for h in SWA_HEAD_ORDER])
    widths = (4 * rd, sd + cd, LANES, qd + 2 * kvd)
    dtypes = (BF16, BF16, F32, BF16)

    steps = nb * (L // tl)
    n_exp, _, dff_e = expert_w_gate.shape[1:]
    per = steps // n_exp
    side_ok = (steps % n_exp == 0 and d % per == 0 and dff_e % per == 0
               and (d // per) % 8 == 0 and (dff_e // per) % 8 == 0)

    for layer in range(depth):
        wl = w_in[layer]
        seg = lambda i: wl[:, offs[i]:offs[i + 1]]
        dt_cols = jnp.zeros((d, LANES), F32).at[:, :SSD_HEADS].set(seg(6))
        w_cat = jnp.concatenate(
            [seg(0), seg(1), seg(2), seg(3), seg(4), seg(5), dt_cols,
             seg(7)[:, q_perm], seg(8), seg(9)], axis=1).astype(BF16)
        wo = w_out[layer]
        w_r = wo[0:rd].astype(BF16)
        w_s = wo[rd:rd + sd].astype(BF16)
        w_a = wo[rd + sd:][q_perm].astype(BF16)
        mod_l = mod[layer:layer + 1]

        i = layer // 2
        is_moe = layer % 2 == 1
        mix_cast = (expert_w_up[i], expert_w_down[i]) if is_moe and side_ok else ()
        y_ret, y_ssd, y_swa, *up_down_bf16 = _mixers(
            x, mod_l, w_cat, widths, dtypes, cos_t, sin_t, bias_tab, sinks[layer], conv_w[layer],
            conv_b[layer], dt_bias[layer], a_log[layer], d_skip[layer], ssd_norm_w[layer], tl,
            side_cast=mix_cast)
        attn_args = (x, y_ret, y_ssd, y_swa, mod_l, w_r, w_s, w_a,
                     ln_g[layer, 0][None, :], ln_b[layer, 0][None, :])

        g2 = ln_g[layer, 1][None, :]
        b2 = ln_b[layer, 1][None, :]
        if not is_moe:
            nxt = (layer + 1) // 2
            ride = (expert_w_gate[nxt],) if layer + 1 < depth and side_ok else ()
            x, *gate_bf16 = _dense_ffn(
                attn_args, ffn_w_gate[i].astype(BF16), ffn_w_up[i].astype(BF16),
                ffn_w_down[i].astype(BF16), g2, b2, alpha, tl, side_cast=ride)
        else:
            if side_ok:
                wg_e, = gate_bf16
                wu_e, wd_e = up_down_bf16
            else:
                wg_e, wu_e, wd_e = (expert_w_gate[i].astype(BF16), expert_w_up[i].astype(BF16),
                                    expert_w_down[i].astype(BF16))
            x = _moe(attn_args, router_w[i], router_b[i], wg_e, wu_e, wd_e, g2, b2, alpha,
                     tm_route=tl, tme=tl, tmf=tl)
    return x
```

```python
import functools
import math

import numpy as np
import jax
import jax.numpy as jnp
from jax import lax
from jax.experimental import pallas as pl
from jax.experimental.pallas import tpu as pltpu

F32 = jnp.float32
BF16 = jnp.bfloat16
I32 = jnp.int32

HEAD_DIM = 64
CHUNK = 128
RET_HEADS = 4
SSD_HEADS = 8
SSD_GROUPS = 2
SSD_STATE = 64
SSD_CONV = 4
CONV_TAIL = 16
SWA_HEADS = 4
SWA_KV_HEADS = 2
REL_BUCKETS = 32
N_EXPERTS = 8
LN_EPS = 1e-5
LANES = 128
MASK_VALUE = -1e30

V7X_VMEM_BYTES = 64 * 1024 * 1024
VMEM_LIMIT = V7X_VMEM_BYTES - 8 * 1024 * 1024
TOKEN_TILE = 512


def _token_tile(seq_len):
    return min(TOKEN_TILE, seq_len)


def _cparams(*sem):
    return pltpu.CompilerParams(dimension_semantics=sem, vmem_limit_bytes=VMEM_LIMIT)


def _silu(v):
    return v * (1.0 / (1.0 + jnp.exp(-v)))


def _softplus(v):
    return jnp.maximum(v, 0.0) + jnp.log(1.0 + jnp.exp(-jnp.abs(v)))


def _dot(a, b):
    return jnp.dot(a, b, preferred_element_type=F32)


def _dot_nt(a, b):
    return lax.dot_general(a, b, (((1,), (1,)), ((), ())), preferred_element_type=F32)


def _hi_lo(v):
    bits = lax.bitcast_convert_type(v, jnp.uint32) & jnp.uint32(0xFFFF0000)
    hi = lax.bitcast_convert_type(bits, F32)
    return hi.astype(BF16), (v - hi).astype(BF16)


def _split3(v):
    h1 = v.astype(BF16)
    r1 = v - h1.astype(F32)
    h2 = r1.astype(BF16)
    r2 = r1 - h2.astype(F32)
    return h1, h2, r2.astype(BF16)


def _dot3(v, m3):
    return _dot(jnp.concatenate(_split3(v), axis=1), m3)


def _dot3_left(m3, v):
    return _dot(m3, jnp.concatenate(_split3(v), axis=0))


def _dot2(v, m2):
    h1 = v.astype(BF16)
    h2 = (v - h1.astype(F32)).astype(BF16)
    return _dot(jnp.concatenate([h1, h2], axis=1), m2)


def _layer_norm(r, g, b):
    mu = jnp.mean(r, -1, keepdims=True)
    d = r - mu
    var = jnp.mean(d * d, -1, keepdims=True)
    return d * lax.rsqrt(var + LN_EPS) * g + b


def _ada_kernel(c_ref, w_ref, b_ref, o_ref):
    c_hi, c_lo = _hi_lo(c_ref[...])
    w_hi, w_lo = _hi_lo(w_ref[0])
    cc = jnp.concatenate([c_hi, c_lo], axis=0)
    both = _dot(cc, w_hi) + _dot(cc, w_lo)
    rows = c_hi.shape[0]
    o_ref[0] = both[0:rows] + both[rows:2 * rows] + b_ref[0]


def _ada_mod(c, w_ada, b_ada):
    depth, d, d6 = w_ada.shape
    nb = c.shape[0]
    rows = 8
    c_pad = jnp.zeros((rows, d), F32).at[:nb].set(c)
    out = pl.pallas_call(
        _ada_kernel,
        out_shape=jax.ShapeDtypeStruct((depth, rows, d6), F32),
        grid=(depth, d6 // d),
        in_specs=[pl.BlockSpec((rows, d), lambda l, j: (0, 0)),
                  pl.BlockSpec((1, d, d), lambda l, j: (l, 0, j)),
                  pl.BlockSpec((1, 1, d), lambda l, j: (l, 0, j))],
        out_specs=pl.BlockSpec((1, rows, d), lambda l, j: (l, 0, j)),
        compiler_params=_cparams("arbitrary", "arbitrary"),
        name="ada_mod",
    )(c_pad, w_ada, b_ada.reshape(depth, 1, d6))
    return out[:, :nb].reshape(depth, nb, 6, d)


def _rotary_kernel(pos_ref, cos_ref, sin_ref):
    half = HEAD_DIM // 2
    lane = lax.broadcasted_iota(I32, (1, LANES), 1)
    jj = lane % HEAD_DIM
    idx = (jj % half).astype(F32)
    inv = jnp.exp(-math.log(10000.0) * idx / half)
    ang = pos_ref[0].astype(F32) * inv
    cos_ref[0] = jnp.cos(ang)
    sin_ref[0] = jnp.where(jj < half, -1.0, 1.0) * jnp.sin(ang)


def _rotary_tables(positions, tl):
    nb, L = positions.shape
    pos = positions.reshape(nb, L, 1)
    return pl.pallas_call(
        _rotary_kernel,
        out_shape=[jax.ShapeDtypeStruct((nb, L, LANES), F32)] * 2,
        grid=(nb, L // tl),
        in_specs=[pl.BlockSpec((1, tl, 1), lambda b, i: (b, i, 0))],
        out_specs=[pl.BlockSpec((1, tl, LANES), lambda b, i: (b, i, 0))] * 2,
        compiler_params=_cparams("arbitrary", "arbitrary"),
        name="rotary_tables",
    )(pos)


def _swa_bias_kernel(rb_ref, bucket_ref, band_ref, o_ref):
    bucket = bucket_ref[...]
    band = band_ref[...]
    for h in range(SWA_HEADS):
        acc = jnp.zeros(bucket.shape, F32)
        for b in range(REL_BUCKETS):
            acc = jnp.where(bucket == b, rb_ref[b, h], acc)
        o_ref[h] = jnp.where(band > 0, acc, MASK_VALUE)


def _t5_bucket(dist):
    exact = REL_BUCKETS // 2
    df = jnp.maximum(dist, 1).astype(F32)
    large = exact + (jnp.log(df / exact) / math.log(CHUNK / exact) * (REL_BUCKETS - exact)).astype(I32)
    large = jnp.minimum(large, REL_BUCKETS - 1)
    return jnp.where(dist < exact, dist, large)


def _swa_bias_table(rel_bias):
    W = CHUNK
    qi = jnp.arange(W)[:, None]
    kj = jnp.arange(2 * W)[None, :]
    dist = qi + W - kj
    band = ((dist >= 0) & (dist < W)).astype(I32)
    bucket = _t5_bucket(jnp.clip(dist, 0, W - 1)).astype(I32)
    return pl.pallas_call(
        _swa_bias_kernel,
        out_shape=jax.ShapeDtypeStruct((SWA_HEADS, W, 2 * W), F32),
        in_specs=[pl.BlockSpec(memory_space=pltpu.SMEM),
                  pl.BlockSpec(memory_space=pltpu.VMEM),
                  pl.BlockSpec(memory_space=pltpu.VMEM)],
        out_specs=pl.BlockSpec(memory_space=pltpu.VMEM),
        name="swa_bias_table",
    )(rel_bias, bucket, band)


def _head_lane_mask(width, head):
    lane = lax.broadcasted_iota(I32, (1, width), 1)
    return (lane // HEAD_DIM) == head


def _rotate_half(t):
    width = t.shape[-1]
    lane = lax.broadcasted_iota(I32, (1, width), 1)
    half = HEAD_DIM // 2
    fwd = pltpu.roll(t, width - half, axis=1)
    bwd = pltpu.roll(t, half, axis=1)
    return jnp.where((lane % HEAD_DIM) < half, fwd, bwd)


def _retention_body(u_ref, cos_ref, sin_ref, din_ref, dq_ref, dk_ref, dc_ref,
                    bmask_ref, avg_ref, o_ref, state_ref, *, n_chunks):
    rd = RET_HEADS * HEAD_DIM
    masks = [_head_lane_mask(rd, h) for h in range(RET_HEADS)]

    def stack_heads(t):
        return jnp.concatenate([jnp.where(m, t, 0.0) for m in masks], axis=0).astype(BF16)

    state = state_ref[...]
    for ci in range(n_chunks):
        rows = slice(ci * CHUNK, (ci + 1) * CHUNK)
        cos = cos_ref[0, rows, :]
        sin = sin_ref[0, rows, :]
        cos2 = jnp.concatenate([cos, cos], axis=1)
        sin2 = jnp.concatenate([sin, sin], axis=1)
        q = u_ref[0, rows, 0:rd].astype(F32)
        k = u_ref[0, rows, rd:2 * rd].astype(F32)
        v = u_ref[0, rows, 2 * rd:3 * rd].astype(F32)
        g = u_ref[0, rows, 3 * rd:4 * rd].astype(F32)
        qr = q * cos2 + _rotate_half(q) * sin2
        kr = (k * cos2 + _rotate_half(k) * sin2) * (HEAD_DIM ** -0.5)
        scores = _dot_nt(qr.astype(BF16), stack_heads(kr)) * din_ref[...]
        inner = _dot(scores.astype(BF16), stack_heads(v))
        cross = _dot((qr * dq_ref[...]).astype(BF16), state.astype(BF16))
        o = inner + cross
        kd_t = (kr * dk_ref[...]).T.astype(BF16)
        kv = _dot(kd_t, v.astype(BF16))
        state = dc_ref[...] * state + bmask_ref[...] * kv
        mu = _dot2(o, avg_ref[...])
        dev = o - mu
        var = _dot2(dev * dev, avg_ref[...])
        on = dev * lax.rsqrt(var + LN_EPS)
        o_ref[0, rows, :] = (_silu(g) * on).astype(o_ref.dtype)
        yield
    state_ref[...] = state


def _retention_tables():
    H, d, C = RET_HEADS, HEAD_DIM, CHUNK
    log_gamma = jnp.log(1.0 - 2.0 ** (-5.0 - jnp.arange(H, dtype=F32)))
    idx = jnp.arange(C, dtype=F32)
    diff = idx[:, None] - idx[None, :]
    decay_in = jnp.where(diff >= 0, jnp.exp(log_gamma[:, None, None] * jnp.maximum(diff, 0.0)), 0.0)
    decay_q = jnp.exp(log_gamma[:, None] * (idx + 1.0))
    decay_k = jnp.exp(log_gamma[:, None] * (C - 1.0 - idx))
    decay_chunk = jnp.exp(log_gamma * C)
    din = decay_in.transpose(1, 0, 2).reshape(C, H * C)
    dq = jnp.repeat(decay_q.T, d, axis=1)
    dk = jnp.repeat(decay_k.T, d, axis=1)
    dc = jnp.repeat(decay_chunk, d)[None, :]
    head = jnp.arange(H * d) // d
    bmask = (head[:, None] == head[None, :]).astype(F32)
    avg = jnp.tile((bmask / d).astype(BF16), (2, 1))
    return din, dq, dk, dc, bmask, avg


def _swa_body(u_ref, bias_ref, sink_ref, o_ref, kprev_ref, vprev_ref, *, n_chunks, first_step):
    W = CHUNK
    qd = SWA_HEADS * HEAD_DIM
    kvd = SWA_KV_HEADS * HEAD_DIM
    lane = lax.broadcasted_iota(I32, (1, LANES), 1)
    low = lane < HEAD_DIM
    col = lax.broadcasted_iota(I32, (1, 2 * W), 1)

    kprev = kprev_ref[...]
    vprev = vprev_ref[...]
    sink = sink_ref[...]
    for ci in range(n_chunks):
        rows = slice(ci * W, (ci + 1) * W)
        qa = u_ref[0, rows, 0:LANES].astype(F32)
        qb = u_ref[0, rows, LANES:qd].astype(F32)
        k = u_ref[0, rows, qd:qd + kvd].astype(BF16)
        v = u_ref[0, rows, qd + kvd:qd + 2 * kvd].astype(BF16)
        q4 = jnp.concatenate([jnp.where(low, qa, 0.0), jnp.where(low, 0.0, qa),
                              jnp.where(low, qb, 0.0), jnp.where(low, 0.0, qb)],
                             axis=0).astype(BF16)
        kband = jnp.concatenate([kprev, k], axis=0)
        vband = jnp.concatenate([vprev, v], axis=0)
        logits = _dot_nt(q4, kband) * (HEAD_DIM ** -0.5) + bias_ref[...]
        if ci == 0:
            logits = jnp.where(jnp.logical_and(first_step, col < W), MASK_VALUE, logits)
        m = jnp.maximum(jnp.max(logits, -1, keepdims=True), sink)
        p = jnp.exp(logits - m)
        denom = jnp.sum(p, -1, keepdims=True) + jnp.exp(sink - m)
        res = _dot(p.astype(BF16), vband) / denom
        out_a = jnp.where(low, res[0:W], res[W:2 * W])
        out_b = jnp.where(low, res[2 * W:3 * W], res[3 * W:4 * W])
        o_ref[0, rows, 0:LANES] = out_a.astype(o_ref.dtype)
        o_ref[0, rows, LANES:qd] = out_b.astype(o_ref.dtype)
        kprev, vprev = k, v
        yield
    kprev_ref[...] = kprev
    vprev_ref[...] = vprev


SWA_HEAD_ORDER = (0, 2, 1, 3)


def _ssd_body(u_ref, dt_ref, cw_ref, cb_ref, dtb_c_ref, alog_c_ref, dskip_ref, nw_ref,
              tril_ref, expand_ref, gmask_ref, shift_ref,
              o_ref, state_ref, ext_ref, *, n_chunks):
    C = CHUNK
    sd = SSD_HEADS * HEAD_DIM
    gn = SSD_GROUPS * SSD_STATE
    cd = sd + 2 * gn
    tl = n_chunks * C
    slab = 2 * LANES
    heads_per_group = SSD_HEADS // SSD_GROUPS
    heads_per_slab = slab // HEAD_DIM
    lane = lax.broadcasted_iota(I32, (1, LANES), 1)
    low = lane < SSD_STATE
    slab_masks = [_head_lane_mask(slab, hh) for hh in range(heads_per_slab)]
    row_i = lax.broadcasted_iota(I32, (C, C), 0)
    col_i = lax.broadcasted_iota(I32, (C, C), 1)
    causal = row_i >= col_i
    neg_a_c = -jnp.exp(alog_c_ref[...])

    ext_ref[CONV_TAIL:CONV_TAIL + tl, :] = u_ref[0, :, sd:sd + cd]
    state = state_ref[...]
    for ci in range(n_chunks):
        rows = slice(ci * C, (ci + 1) * C)
        z = u_ref[0, rows, 0:sd].astype(F32)
        dt_raw = dt_ref[0, rows, :]
        window = ext_ref[ci * C:ci * C + CONV_TAIL + C, :]
        shifted = _dot(shift_ref[...], window)
        conv = cb_ref[...] + cw_ref[SSD_CONV - 1:SSD_CONV, :] * window[CONV_TAIL:, :].astype(F32)
        for w in range(SSD_CONV - 1):
            conv = conv + cw_ref[w:w + 1, :] * shifted[w * C:(w + 1) * C, :]
        xbc = _silu(conv)
        xs = xbc[:, 0:sd]
        bm = xbc[:, sd:sd + gn]
        cm = xbc[:, sd + gn:cd]

        dt_c = _softplus(dt_raw + dtb_c_ref[...])
        a_c = neg_a_c * dt_c
        acs_c = _dot3_left(tril_ref[...], a_c)
        acs_t = acs_c.T
        spread = _dot3(acs_c, expand_ref[...])
        acs_x = spread[:, 0:sd]
        dt_x = _dot3(dt_c, expand_ref[:, 0:sd])
        xdt = xs * dt_x
        yield

        bstack = jnp.concatenate([jnp.where(low, bm, 0.0), jnp.where(low, 0.0, bm)],
                                 axis=0).astype(BF16)
        cb = _dot_nt(cm.astype(BF16), bstack)
        y_diag = []
        for s in range(sd // slab):
            ms = []
            for hh in range(heads_per_slab):
                h = s * heads_per_slab + hh
                g = h // heads_per_group
                col_bcast = spread[:, sd + h * LANES:sd + (h + 1) * LANES]
                seg = col_bcast - acs_t[h:h + 1, :]
                lmat = jnp.exp(jnp.where(causal, seg, MASK_VALUE))
                ms.append((cb[:, g * C:(g + 1) * C] * lmat).astype(BF16))
            xslab = xdt[:, s * slab:(s + 1) * slab]
            xstack = jnp.concatenate([jnp.where(m, xslab, 0.0) for m in slab_masks],
                                     axis=0).astype(BF16)
            y_diag.append(_dot(jnp.concatenate(ms, axis=1), xstack))
        y_diag = jnp.concatenate(y_diag, axis=1)
        yield

        y_off = _dot(cm.astype(BF16), state.astype(BF16)) * jnp.exp(acs_x)
        last = acs_x[C - 1:C, :]
        dec = jnp.exp(last - acs_x)
        new = _dot(bm.T.astype(BF16), (xdt * dec).astype(BF16))
        state = jnp.exp(last) * state + gmask_ref[...] * new

        y = y_diag + y_off + xs * dskip_ref[...]
        hgate = y * _silu(z)
        gw = sd // SSD_GROUPS
        for g in range(SSD_GROUPS):
            hg = hgate[:, g * gw:(g + 1) * gw]
            ms_ = jnp.mean(hg * hg, -1, keepdims=True)
            o_ref[0, rows, g * gw:(g + 1) * gw] = (
                hg * lax.rsqrt(ms_ + LN_EPS) * nw_ref[:, g * gw:(g + 1) * gw]).astype(o_ref.dtype)
        yield
    state_ref[...] = state
    ext_ref[0:CONV_TAIL, :] = ext_ref[tl:tl + CONV_TAIL, :]


def _ssd_tables():
    C = CHUNK
    sd = SSD_HEADS * HEAD_DIM
    gn = SSD_GROUPS * SSD_STATE
    t = np.arange(C)
    tril = (t[:, None] >= t[None, :])
    r = np.arange(LANES)[:, None]
    eexp = (r == (np.arange(sd)[None, :] // HEAD_DIM)) & (r < SSD_HEADS)
    bsel = (r == (np.arange(SSD_HEADS * LANES)[None, :] // LANES)) & (r < SSD_HEADS)
    heads_per_group = SSD_HEADS // SSD_GROUPS
    row_g = np.arange(gn)[:, None] // SSD_STATE
    col_g = (np.arange(sd)[None, :] // HEAD_DIM) // heads_per_group
    gmask = (row_g == col_g)
    expand = np.concatenate([eexp, bsel], axis=1)
    win = np.arange(CONV_TAIL + C)[None, :]
    shifts = np.concatenate(
        [(win == CONV_TAIL + t[:, None] - (SSD_CONV - 1 - s))
         for s in range(SSD_CONV - 1)], axis=0)
    return (jnp.asarray(np.tile(tril, (1, 3)), BF16), jnp.asarray(np.tile(expand, (3, 1)), BF16),
            jnp.asarray(gmask, F32), jnp.asarray(shifts, BF16))


N_RET_TABLES = 6
N_SSD_CONSTS = 10


def _in_proj_body(x_ref, shift, scale, w_ref, out_refs):
    h = (x_ref[0] * (1.0 + scale) + shift).astype(BF16)
    off = 0
    for ref in out_refs:
        width = ref.shape[-1]
        for a, b in _ff_chunks(width):
            ref[0, :, a:b] = _dot(h, w_ref[:, off + a:off + b]).astype(ref.dtype)
            yield
        off += width


def _mixers_kernel(*refs, n_chunks, n_cast, n_tiles, tiles_per_seq):
    x_ref, mod_ref, w_ref, cos, sin = refs[:5]
    pos = 5
    ret_tables = refs[pos:pos + N_RET_TABLES]
    pos += N_RET_TABLES
    ssd_consts = refs[pos:pos + N_SSD_CONSTS]
    pos += N_SSD_CONSTS
    bias, sink = refs[pos:pos + 2]
    pos += 2
    cast_in = refs[pos:pos + n_cast]
    pos += n_cast
    y_ret, y_ssd, y_swa = refs[pos:pos + 3]
    pos += 3
    cast_out = refs[pos:pos + n_cast]
    pos += n_cast
    ret_state, ssd_state, ssd_ext, kprev, vprev = refs[pos:pos + 5]
    cur = refs[pos + 5:pos + 9]
    nxt = refs[pos + 9:pos + 13]
    _side_cast(cast_in, cast_out)

    s = pl.program_id(0)
    tile = jnp.maximum(s - 1, 0)
    seq_start = tile % tiles_per_seq == 0

    @pl.when(s == 0)
    def _():
        for ref in cur:
            ref[...] = jnp.zeros_like(ref)

    @pl.when(seq_start)
    def _():
        ret_state[...] = jnp.zeros_like(ret_state)
        ssd_state[...] = jnp.zeros_like(ssd_state)
        ssd_ext[0:CONV_TAIL, :] = jnp.zeros((CONV_TAIL, ssd_ext.shape[1]), ssd_ext.dtype)
        kprev[...] = jnp.zeros_like(kprev)
        vprev[...] = jnp.zeros_like(vprev)

    batch = jnp.minimum(s, n_tiles - 1) // tiles_per_seq
    shift = mod_ref[0, batch, 0:1, :]
    scale = mod_ref[0, batch, 1:2, :]
    u_ret, u_ssd, u_dt, u_swa = cur
    bodies = [
        _ssd_body(u_ssd, u_dt, *ssd_consts, y_ssd, ssd_state, ssd_ext, n_chunks=n_chunks),
        _in_proj_body(x_ref, shift, scale, w_ref, nxt),
        _swa_body(u_swa, bias, sink, y_swa, kprev, vprev, n_chunks=n_chunks,
                  first_step=seq_start),
        _retention_body(u_ret, cos, sin, *ret_tables, y_ret, ret_state, n_chunks=n_chunks),
    ]
    alive = list(bodies)
    while alive:
        alive = [b for b in alive if next(b, StopIteration) is not StopIteration]
    for c_ref, n_ref in zip(cur, nxt):
        c_ref[...] = n_ref[...]


def _mixers(x, mod_l, w_cat, widths, dtypes, cos_t, sin_t, bias_tab, sinks_l, conv_w, conv_b,
            dt_bias, a_log, d_skip, norm_w, tl, side_cast=()):
    nb, L, d = x.shape
    W = CHUNK
    rd = RET_HEADS * HEAD_DIM
    sd = SSD_HEADS * HEAD_DIM
    qd = SWA_HEADS * HEAD_DIM
    cd = conv_w.shape[-1]
    ret_tables = _retention_tables()
    pad = lambda v: jnp.zeros((1, LANES), F32).at[0, :SSD_HEADS].set(v)
    rep = lambda v: jnp.repeat(v, HEAD_DIM)[None, :]
    ssd_consts = (conv_w, conv_b[None, :], pad(dt_bias), pad(a_log), rep(d_skip),
                  norm_w[None, :]) + _ssd_tables()
    order = jnp.array(SWA_HEAD_ORDER)
    bias_stacked = bias_tab[order].reshape(SWA_HEADS * W, 2 * W)
    sink_col = jnp.repeat(sinks_l.astype(F32)[order], W)[:, None]
    assert len(ret_tables) == N_RET_TABLES and len(ssd_consts) == N_SSD_CONSTS
    tps = L // tl
    n_tiles = nb * tps

    def tile_of(step):
        return step // tps, step % tps, 0

    proj_map = lambda s: tile_of(jnp.minimum(s, n_tiles - 1))
    mix_map = lambda s: tile_of(jnp.maximum(s - 1, 0))
    const = lambda a: pl.BlockSpec(a.shape, lambda s: (0,) * a.ndim)
    consts = ret_tables + ssd_consts + (bias_stacked, sink_col)
    c_in, c_out, c_shapes = _side_cast_specs(side_cast, n_tiles)
    u_bufs = [pltpu.VMEM((1, tl, w), t) for w, t in zip(widths, dtypes)]
    return pl.pallas_call(
        functools.partial(_mixers_kernel, n_chunks=tl // CHUNK, n_cast=len(side_cast),
                          n_tiles=n_tiles, tiles_per_seq=tps),
        out_shape=[jax.ShapeDtypeStruct((nb, L, w), BF16) for w in (rd, sd, qd)] + c_shapes,
        grid=(n_tiles + 1,),
        in_specs=[pl.BlockSpec((1, tl, d), proj_map), const(mod_l), const(w_cat),
                  pl.BlockSpec((1, tl, LANES), mix_map), pl.BlockSpec((1, tl, LANES), mix_map)]
                 + [const(a) for a in consts] + c_in,
        out_specs=[pl.BlockSpec((1, tl, w), mix_map) for w in (rd, sd, qd)] + c_out,
        scratch_shapes=[pltpu.VMEM((rd, rd), F32),
                        pltpu.VMEM((SSD_GROUPS * SSD_STATE, sd), F32),
                        pltpu.VMEM((CONV_TAIL + tl, cd), BF16),
                        pltpu.VMEM((W, LANES), BF16), pltpu.VMEM((W, LANES), BF16)]
                       + u_bufs + u_bufs,
        compiler_params=_cparams("arbitrary"),
        name="in_proj_mixers",
    )(x, mod_l, w_cat, cos_t, sin_t, *consts, *side_cast)


N_ATTN_OUT = 10


def _attn_out(x_ref, yr_ref, ys_ref, ya_ref, mod_ref, wr_ref, ws_ref, wa_ref, g_ref, b_ref,
              alpha):
    mix = (_dot(yr_ref[0], wr_ref[...]) + _dot(ys_ref[0], ws_ref[...])
           + _dot(ya_ref[0], wa_ref[...]))
    gate = mod_ref[0, 0, 2:3, :]
    r = alpha * x_ref[0] + (1.0 + gate) * mix
    return _layer_norm(r, g_ref[...], b_ref[...])


def _attn_out_operands(x, y_ret, y_ssd, y_swa, mod_l, w_r, w_s, w_a, ln_g, ln_b, tm):
    d = x.shape[-1]
    tok = lambda w: pl.BlockSpec((1, tm, w), lambda b, i: (b, i, 0))
    const = lambda a: pl.BlockSpec(a.shape, lambda b, i: (0,) * a.ndim)
    specs = [tok(d), tok(y_ret.shape[-1]), tok(y_ssd.shape[-1]), tok(y_swa.shape[-1]),
             pl.BlockSpec((1, 1, 6, d), lambda b, i: (0, b, 0, 0)),
             const(w_r), const(w_s), const(w_a), const(ln_g), const(ln_b)]
    return specs, (x, y_ret, y_ssd, y_swa, mod_l, w_r, w_s, w_a, ln_g, ln_b)


def _side_cast_specs(arrays, steps, steps_per_seq=None):
    in_specs, out_specs, out_shapes = [], [], []
    for a in arrays:
        e, r, c = a.shape
        per = steps // e
        if steps_per_seq is None:
            index = lambda s, per=per: (jnp.minimum(s, steps - 1) // per,
                                        jnp.minimum(s, steps - 1) % per, 0)
        else:
            index = lambda b, i, per=per: ((b * steps_per_seq + i) // per,
                                           (b * steps_per_seq + i) % per, 0)
        spec = pl.BlockSpec((1, r // per, c), index)
        in_specs.append(spec)
        out_specs.append(spec)
        out_shapes.append(jax.ShapeDtypeStruct(a.shape, BF16))
    return in_specs, out_specs, out_shapes


def _side_cast(in_refs, out_refs):
    for i_ref, o_ref in zip(in_refs, out_refs):
        o_ref[...] = i_ref[...].astype(o_ref.dtype)


def _dense_ffn_kernel(*refs, alpha):
    attn = refs[:N_ATTN_OUT]
    mod_ref = attn[4]
    wg_ref, wu_ref, wd_ref, g_ref, b_ref = refs[N_ATTN_OUT:N_ATTN_OUT + 5]
    rest = refs[N_ATTN_OUT + 5:]
    n_cast = (len(rest) - 1) // 2
    o_ref = rest[n_cast]
    _side_cast(rest[:n_cast], rest[n_cast + 1:])
    x = _attn_out(*attn, alpha)
    sh = mod_ref[0, 0, 3:4, :]
    sc = mod_ref[0, 0, 4:5, :]
    gate = mod_ref[0, 0, 5:6, :]
    h = (x * (1.0 + sc) + sh).astype(BF16)
    dff = wg_ref.shape[-1]
    acc = None
    for a, b in _ff_chunks(dff):
        gj = _dot(h, wg_ref[:, a:b])
        uj = _dot(h, wu_ref[:, a:b])
        part = _dot((_silu(gj) * uj).astype(BF16), wd_ref[a:b, :])
        acc = part if acc is None else acc + part
    r = alpha * x + (1.0 + gate) * acc
    o_ref[0] = _layer_norm(r, g_ref[...], b_ref[...])


def _dense_ffn(attn_args, wg, wu, wd, ln_g, ln_b, alpha, tm, side_cast=()):
    nb, L, d = attn_args[0].shape
    const = lambda a: pl.BlockSpec(a.shape, lambda b, i: (0,) * a.ndim,
                                   pipeline_mode=pl.Buffered(1))
    a_specs, a_ops = _attn_out_operands(*attn_args, tm)
    c_in, c_out, c_shapes = _side_cast_specs(side_cast, nb * (L // tm), L // tm)
    return pl.pallas_call(
        functools.partial(_dense_ffn_kernel, alpha=alpha),
        out_shape=[jax.ShapeDtypeStruct((nb, L, d), F32)] + c_shapes,
        grid=(nb, L // tm),
        in_specs=a_specs + [const(wg), const(wu), const(wd), const(ln_g), const(ln_b)] + c_in,
        out_specs=[pl.BlockSpec((1, tm, d), lambda b, i: (b, i, 0))] + c_out,
        compiler_params=_cparams("arbitrary", "arbitrary"),
        name="attn_out_dense_ffn_ln",
    )(*a_ops, wg, wu, wd, ln_g, ln_b, *side_cast)


def _store_row_tiles(ref, value):
    rows, d = value.shape
    nt = d // LANES
    for c in range(nt):
        ref[pl.ds(c, rows, stride=nt), :] = value[:, c * LANES:(c + 1) * LANES].astype(ref.dtype)


def _load_row_tiles(ref):
    nt = ROW_TILE
    rows = ref.shape[0] // nt
    return jnp.concatenate([ref[pl.ds(c, rows, stride=nt), :] for c in range(nt)], axis=1)


def _router_kernel(*refs, alpha):
    attn = refs[:N_ATTN_OUT]
    mod_ref = attn[4]
    wr_ref, br_ref, trils_ref, x1_ref, h_ref, route_ref, cnt_ref, base_ref = refs[N_ATTN_OUT:]
    first = jnp.logical_and(pl.program_id(0) == 0, pl.program_id(1) == 0)

    @pl.when(first)
    def _():
        base_ref[...] = jnp.zeros_like(base_ref)

    x1 = _attn_out(*attn, alpha)
    x1_ref[0] = x1
    sh = mod_ref[0, 0, 3:4, :]
    sc = mod_ref[0, 0, 4:5, :]
    h = x1 * (1.0 + sc) + sh
    _store_row_tiles(h_ref.at[0], h)
    h_hi, h_lo = _hi_lo(h)
    wide = _dot(h_hi, wr_ref[...]) + _dot(h_lo, wr_ref[...])
    logits = wide + pltpu.roll(wide, LANES - N_EXPERTS, axis=1) + br_ref[...]
    lane = lax.broadcasted_iota(I32, logits.shape, 1).astype(F32)
    logits = jnp.where(lane < N_EXPERTS, logits, MASK_VALUE)
    v1 = jnp.max(logits, -1, keepdims=True)
    e1 = jnp.min(jnp.where(logits == v1, lane, float(LANES)), -1, keepdims=True)
    rest = jnp.where(lane == e1, MASK_VALUE, logits)
    v2 = jnp.max(rest, -1, keepdims=True)
    e2 = jnp.min(jnp.where(rest == v2, lane, float(LANES)), -1, keepdims=True)
    t = jnp.exp(v2 - v1)
    w1 = 1.0 / (1.0 + t)
    w2 = t / (1.0 + t)
    hot1 = (lane == e1).astype(F32)
    hot2 = (lane == e2).astype(F32)
    both = hot1 + hot2
    base = base_ref[0:1, :]
    before = _dot(trils_ref[...], both.astype(BF16)) + base
    rank1 = jnp.sum(hot1 * before, -1, keepdims=True)
    rank2 = jnp.sum(hot2 * before, -1, keepdims=True)
    total = base + jnp.sum(both, 0, keepdims=True)
    base_ref[0:1, :] = total
    cnt_ref[...] = jnp.broadcast_to(total, cnt_ref.shape)
    route = jnp.where(lane == 0, e1, 0.0)
    route = jnp.where(lane == 1, e2, route)
    route = jnp.where(lane == 2, rank1, route)
    route = jnp.where(lane == 3, rank2, route)
    route = jnp.where(lane == 4, w1, route)
    route = jnp.where(lane == 5, w2, route)
    route_ref[0] = route


def _router(attn_args, w_router, b_router, alpha, tm):
    nb, L, d = attn_args[0].shape
    w_hi, w_lo = _hi_lo(w_router)
    wr = (jnp.zeros((d, LANES), BF16).at[:, :N_EXPERTS].set(w_hi)
          .at[:, N_EXPERTS:2 * N_EXPERTS].set(w_lo))
    br = jnp.zeros((1, LANES), F32).at[0, :N_EXPERTS].set(b_router)
    t = jnp.arange(tm)
    tril_strict = (t[:, None] > t[None, :]).astype(BF16)
    const = lambda a: pl.BlockSpec(a.shape, lambda b, i: (0,) * a.ndim)
    a_specs, a_ops = _attn_out_operands(*attn_args, tm)
    return pl.pallas_call(
        functools.partial(_router_kernel, alpha=alpha),
        out_shape=[jax.ShapeDtypeStruct((nb, L, d), F32),
                   jax.ShapeDtypeStruct((nb, L * (d // LANES), LANES), F32),
                   jax.ShapeDtypeStruct((nb, L, LANES), F32),
                   jax.ShapeDtypeStruct((8, LANES), F32)],
        grid=(nb, L // tm),
        in_specs=a_specs + [const(wr), const(br), const(tril_strict)],
        out_specs=[pl.BlockSpec((1, tm, d), lambda b, i: (b, i, 0)),
                   pl.BlockSpec((1, tm * (d // LANES), LANES), lambda b, i: (b, i, 0)),
                   pl.BlockSpec((1, tm, LANES), lambda b, i: (b, i, 0)),
                   pl.BlockSpec((8, LANES), lambda b, i: (0, 0))],
        scratch_shapes=[pltpu.VMEM((8, LANES), F32)],
        compiler_params=_cparams("arbitrary", "arbitrary"),
        name="attn_out_moe_router",
    )(*a_ops, wr, br, tril_strict)


TOP_K = 2
ROW_TILE = 8
EXPERT_FF_SPLITS = 2


def _ff_chunks(width, step=512):
    return [(a, min(a + step, width)) for a in range(0, width, step)]


def _expert_kernel(te_ref, nu_ref, gsrc_ref, sdst_ref,
                   h_ref, wg_ref, wu_ref, wd_ref, yk_ref,
                   xbuf, ybuf, hbuf, acc, gsem, ssem, *, tme):
    i = pl.program_id(0)
    j = pl.program_id(1)
    n_used = nu_ref[0]
    slot = i % 2
    other = 1 - slot
    used = i < n_used
    chunks = _ff_chunks(wg_ref.shape[-1])
    rows_per_step = tme // EXPERT_FF_SPLITS
    n_front = len(chunks) // 2
    front_w = chunks[n_front - 1][1]
    back_w = wg_ref.shape[-1] - front_w
    gather_ranges, scatter_ranges = [], []
    for c, (a, b) in enumerate(chunks):
        if c < n_front:
            gather_ranges.append((rows_per_step * a // front_w, rows_per_step * b // front_w))
            scatter_ranges.append((0, 0))
        else:
            gather_ranges.append((0, 0))
            scatter_ranges.append((rows_per_step * (a - front_w) // back_w,
                                   rows_per_step * (b - front_w) // back_w))

    nt = ROW_TILE

    def gather(tile, slot_, r):
        src = pl.multiple_of(gsrc_ref[tile * tme + r], nt)
        return pltpu.make_async_copy(h_ref.at[pl.ds(src, nt)],
                                     xbuf.at[slot_, pl.ds(pl.multiple_of(r * nt, nt), nt)],
                                     gsem.at[slot_])

    def wait_gather(slot_):
        pltpu.make_async_copy(h_ref.at[pl.ds(0, tme * nt)], xbuf.at[slot_],
                              gsem.at[slot_]).wait()

    def scatter(block, slot_, r):
        dst = pl.multiple_of(sdst_ref[block * tme + r], nt)
        return pltpu.make_async_copy(ybuf.at[slot_, pl.ds(pl.multiple_of(r * nt, nt), nt)],
                                     yk_ref.at[pl.ds(dst, nt)], ssem.at[slot_])

    def wait_scatter(slot_):
        pltpu.make_async_copy(ybuf.at[slot_], yk_ref.at[pl.ds(0, tme * nt)],
                              ssem.at[slot_]).wait()

    @pl.when(jnp.logical_and(i == 0, j == 0))
    def _():
        ybuf[1] = jnp.zeros(ybuf.shape[1:], F32)

        def body(r, carry):
            gather(0, 0, r).start()
            return carry
        lax.fori_loop(0, tme, body, 0, unroll=8)

    @pl.when(used)
    def _():
        @pl.when(j == 0)
        def _():
            wait_gather(slot)

            @pl.when(i > 0)
            def _():
                wait_scatter(slot)
            acc[...] = jnp.zeros(acc.shape, F32)

        hbuf[...] = _load_row_tiles(xbuf.at[slot]).astype(BF16)
        h = hbuf[...]
        for c, (a, b) in enumerate(chunks):
            for r in range(*gather_ranges[c]):
                gather(i + 1, other, j * rows_per_step + r).start()
            for r in range(*scatter_ranges[c]):
                scatter(i, other, j * rows_per_step + r).start()
            gj = _dot(h, wg_ref[0, :, a:b])
            uj = _dot(h, wu_ref[0, :, a:b])
            part = _dot((_silu(gj) * uj).astype(BF16), wd_ref[0, a:b, :])
            acc[...] = acc[...] + part
            ybuf[slot, 0:ROW_TILE, :] = part[0:ROW_TILE, 0:LANES]

        @pl.when(j == EXPERT_FF_SPLITS - 1)
        def _():
            _store_row_tiles(ybuf.at[slot], acc[...])

        @pl.when(jnp.logical_and(j == EXPERT_FF_SPLITS - 1, i == n_used - 1))
        def _():
            wait_gather(other)
            wait_scatter(other)

            def body(r, carry):
                scatter(i + 1, slot, r).start()
                return carry
            lax.fori_loop(0, tme, body, 0, unroll=8)
            wait_scatter(slot)


def _expert_ffn(h_rows, tile_expert, n_used, gsrc, sdst, wg, wu, wd, tme):
    nt = ROW_TILE
    T = h_rows.shape[0] // nt
    d = nt * LANES
    dff = wg.shape[-1]
    dffh = dff // EXPERT_FF_SPLITS
    n_tiles = gsrc.shape[0] // tme
    yk_rows = TOP_K * T + tme

    def half(i, j):
        return jnp.where(i % 2 == 0, j, EXPERT_FF_SPLITS - 1 - j)

    return pl.pallas_call(
        functools.partial(_expert_kernel, tme=tme),
        out_shape=jax.ShapeDtypeStruct((yk_rows * nt, LANES), F32),
        grid_spec=pltpu.PrefetchScalarGridSpec(
            num_scalar_prefetch=4,
            grid=(n_tiles, EXPERT_FF_SPLITS),
            in_specs=[pl.BlockSpec(memory_space=pl.ANY),
                      pl.BlockSpec((1, d, dffh), lambda i, j, te, *_: (te[i], 0, half(i, j))),
                      pl.BlockSpec((1, d, dffh), lambda i, j, te, *_: (te[i], 0, half(i, j))),
                      pl.BlockSpec((1, dffh, d), lambda i, j, te, *_: (te[i], half(i, j), 0))],
            out_specs=pl.BlockSpec(memory_space=pl.ANY),
            scratch_shapes=[pltpu.VMEM((2, tme * nt, LANES), F32),
                            pltpu.VMEM((2, tme * nt, LANES), F32),
                            pltpu.VMEM((tme, d), BF16), pltpu.VMEM((tme, d), F32),
                            pltpu.SemaphoreType.DMA((2,)), pltpu.SemaphoreType.DMA((2,))]),
        compiler_params=_cparams("arbitrary", "arbitrary"),
        name="moe_experts",
    )(tile_expert, n_used, gsrc, sdst, h_rows, wg, wu, wd)


def _moe_finish_kernel(x_ref, route_ref, mod_ref, y0_ref, y1_ref, g_ref, b_ref, o_ref, *, alpha):
    route = route_ref[0]
    f = route[:, 4:5] * _load_row_tiles(y0_ref) + route[:, 5:6] * _load_row_tiles(y1_ref)
    gate = mod_ref[0, 0, 5:6, :]
    r = alpha * x_ref[0] + (1.0 + gate) * f
    o_ref[0] = _layer_norm(r, g_ref[...], b_ref[...])


def _moe_finish(x, route, mod_l, yk, ln_g, ln_b, alpha, tm):
    nb, L, d = x.shape
    tiles_per_seq = L // tm
    tiles = nb * tiles_per_seq
    return pl.pallas_call(
        functools.partial(_moe_finish_kernel, alpha=alpha),
        out_shape=jax.ShapeDtypeStruct((nb, L, d), F32),
        grid=(nb, tiles_per_seq),
        in_specs=[pl.BlockSpec((1, tm, d), lambda b, i: (b, i, 0)),
                  pl.BlockSpec((1, tm, LANES), lambda b, i: (b, i, 0)),
                  pl.BlockSpec((1, 1, 6, d), lambda b, i: (0, b, 0, 0)),
                  pl.BlockSpec((tm * ROW_TILE, LANES), lambda b, i: (b * tiles_per_seq + i, 0)),
                  pl.BlockSpec((tm * ROW_TILE, LANES),
                               lambda b, i: (tiles + b * tiles_per_seq + i, 0)),
                  pl.BlockSpec(ln_g.shape, lambda b, i: (0, 0)),
                  pl.BlockSpec(ln_b.shape, lambda b, i: (0, 0))],
        out_specs=pl.BlockSpec((1, tm, d), lambda b, i: (b, i, 0)),
        compiler_params=_cparams("arbitrary", "arbitrary"),
        name="moe_finish_ln",
    )(x, route, mod_l, yk, yk, ln_g, ln_b)


def _row_index_kernel(dest_ref, gsrc0_ref, sdst0_ref, gsrc_ref, sdst_ref, sem, *, n_pairs, T, tme):
    init_g = pltpu.make_async_copy(gsrc0_ref, gsrc_ref, sem.at[0])
    init_s = pltpu.make_async_copy(sdst0_ref, sdst_ref, sem.at[1])
    init_g.start()
    init_s.start()
    init_g.wait()
    init_s.wait()

    assert TOP_K == 2
    group = 8

    def body(g, carry):
        f0 = g * group
        rows = [dest_ref[f0 + k] for k in range(group)]
        for k in range(group):
            tok = g * (group // TOP_K) + k // TOP_K
            gsrc_ref[rows[k]] = tok * ROW_TILE
            sdst_ref[tme + rows[k]] = ((k % TOP_K) * T + tok) * ROW_TILE
        return carry
    lax.fori_loop(0, n_pairs // group, body, 0, unroll=2)


def _row_indices(dest, n_rows, T, tme):
    dump = (TOP_K * T + jnp.arange(n_rows + tme, dtype=I32) % tme) * ROW_TILE
    return pl.pallas_call(
        functools.partial(_row_index_kernel, n_pairs=dest.shape[0], T=T, tme=tme),
        out_shape=[jax.ShapeDtypeStruct((n_rows,), I32),
                   jax.ShapeDtypeStruct((n_rows + tme,), I32)],
        in_specs=[pl.BlockSpec(memory_space=pltpu.SMEM),
                  pl.BlockSpec(memory_space=pl.ANY),
                  pl.BlockSpec(memory_space=pl.ANY)],
        out_specs=[pl.BlockSpec(memory_space=pltpu.SMEM),
                   pl.BlockSpec(memory_space=pltpu.SMEM)],
        scratch_shapes=[pltpu.SemaphoreType.DMA((2,))],
        name="moe_row_indices",
    )(dest, jnp.zeros((n_rows,), I32), dump)


def _moe(attn_args, w_router, b_router, wg, wu, wd, ln_g, ln_b, alpha, tm_route, tme, tmf):
    nb, L, d = attn_args[0].shape
    mod_l = attn_args[4]
    T = nb * L
    x, h, route, counts = _router(attn_args, w_router, b_router, alpha, tm_route)
    route_flat = route.reshape(T, LANES)
    e = route_flat[:, 0:2].astype(I32)
    rank = route_flat[:, 2:4].astype(I32)
    cnt = counts[0, :N_EXPERTS].astype(I32)
    tiles = (cnt + tme - 1) // tme
    tile_end = jnp.cumsum(tiles)
    group_start = (tile_end - tiles) * tme
    dest = (group_start[e] + rank).reshape(-1)
    n_tiles = (TOP_K * T) // tme + N_EXPERTS
    n_rows = n_tiles * tme
    tile_expert = jnp.minimum(
        jnp.sum(jnp.arange(n_tiles, dtype=I32)[:, None] >= tile_end[None, :].astype(I32), axis=1),
        N_EXPERTS - 1).astype(I32)
    n_used = tile_end[-1:].astype(I32)
    gsrc, sdst = _row_indices(dest, n_rows, T, tme)
    yk = _expert_ffn(h.reshape(T * ROW_TILE, LANES), tile_expert, n_used, gsrc, sdst,
                     wg, wu, wd, tme)
    return _moe_finish(x, route, mod_l, yk, ln_g, ln_b, alpha, tmf)


def kernel(x, c, positions, rel_bias, w_ada, b_ada, w_in, w_out, conv_w, conv_b, dt_bias, a_log,
           d_skip, ssd_norm_w, sinks, ln_g, ln_b, ffn_w_gate, ffn_w_up, ffn_w_down, router_w,
           router_b, expert_w_gate, expert_w_up, expert_w_down):
    depth = w_ada.shape[0]
    nb, L, d = x.shape
    alpha = (2 * depth) ** 0.25
    rd = RET_HEADS * HEAD_DIM
    sd = SSD_HEADS * HEAD_DIM
    cd = conv_w.shape[-1]
    qd = SWA_HEADS * HEAD_DIM
    kvd = SWA_KV_HEADS * HEAD_DIM
    sizes = (rd, rd, rd, rd, sd, cd, SSD_HEADS, qd, kvd, kvd)
    offs = np.concatenate([[0], np.cumsum(sizes)])
    tl = _token_tile(L)

    mod = _ada_mod(c, w_ada, b_ada)
    cos_t, sin_t = _rotary_tables(positions, tl)
    bias_tab = _swa_bias_table(rel_bias)

    q_perm = np.concatenate([np.arange(h * HEAD_DIM, (h + 1) * HEAD_DIM) for h in SWA_HEAD_ORDER])
    widths = (4 * rd, sd + cd, LANES, qd + 2 * kvd)
    dtypes = (BF16, BF16, F32, BF16)

    steps = nb * (L // tl)
    n_exp, _, dff_e = expert_w_gate.shape[1:]
    per = steps // n_exp
    side_ok = (steps % n_exp == 0 and d % per == 0 and dff_e % per == 0
               and (d // per) % 8 == 0 and (dff_e // per) % 8 == 0)

    for layer in range(depth):
        wl = w_in[layer]
        seg = lambda i: wl[:, offs[i]:offs[i + 1]]
        dt_cols = jnp.zeros((d, LANES), F32).at[:, :SSD_HEADS].set(seg(6))
        w_cat = jnp.concatenate(
            [seg(0), seg(1), seg(2), seg(3), seg(4), seg(5), dt_cols,
             seg(7)[:, q_perm], seg(8), seg(9)], axis=1).astype(BF16)
        wo = w_out[layer]
        w_r = wo[0:rd].astype(BF16)
        w_s = wo[rd:rd + sd].astype(BF16)
        w_a = wo[rd + sd:][q_perm].astype(BF16)
        mod_l = mod[layer:layer + 1]

        i = layer // 2
        is_moe = layer % 2 == 1
        mix_cast = (expert_w_up[i], expert_w_down[i]) if is_moe and side_ok else ()
        y_ret, y_ssd, y_swa, *up_down_bf16 = _mixers(
            x, mod_l, w_cat, widths, dtypes, cos_t, sin_t, bias_tab, sinks[layer], conv_w[layer],
            conv_b[layer], dt_bias[layer], a_log[layer], d_skip[layer], ssd_norm_w[layer], tl,
            side_cast=mix_cast)
        attn_args = (x, y_ret, y_ssd, y_swa, mod_l, w_r, w_s, w_a,
                     ln_g[layer, 0][None, :], ln_b[layer, 0][None, :])

        g2 = ln_g[layer, 1][None, :]
        b2 = ln_b[layer, 1][None, :]
        if not is_moe:
            nxt = (layer + 1) // 2
            ride = (expert_w_gate[nxt],) if layer + 1 < depth and side_ok else ()
            x, *gate_bf16 = _dense_ffn(
                attn_args, ffn_w_gate[i].astype(BF16), ffn_w_up[i].astype(BF16),
                ffn_w_down[i].astype(BF16), g2, b2, alpha, tl, side_cast=ride)
        else:
            if side_ok:
                wg_e, = gate_bf16
                wu_e, wd_e = up_down_bf16
            else:
                wg_e, wu_e, wd_e = (expert_w_gate[i].astype(BF16), expert_w_up[i].astype(BF16),
                                    expert_w_down[i].astype(BF16))
            x = _moe(attn_args, router_w[i], router_b[i], wg_e, wu_e, wd_e, g2, b2, alpha,
                     tm_route=tl, tme=tl, tmf=tl)
    return x
```

```python
import functools
import math

import numpy as np
import jax
import jax.numpy as jnp
from jax import lax
from jax.experimental import pallas as pl
from jax.experimental.pallas import tpu as pltpu

F32 = jnp.float32
BF16 = jnp.bfloat16
I32 = jnp.int32

HEAD_DIM = 64
CHUNK = 128
RET_HEADS = 4
SSD_HEADS = 8
SSD_GROUPS = 2
SSD_STATE = 64
SSD_CONV = 4
CONV_TAIL = 16
SWA_HEADS = 4
SWA_KV_HEADS = 2
REL_BUCKETS = 32
N_EXPERTS = 8
LN_EPS = 1e-5
LANES = 128
MASK_VALUE = -1e30

V7X_VMEM_BYTES = 64 * 1024 * 1024
VMEM_LIMIT = V7X_VMEM_BYTES - 8 * 1024 * 1024
TOKEN_TILE = 512


def _token_tile(seq_len):
    return min(TOKEN_TILE, seq_len)


def _cparams(*sem):
    return pltpu.CompilerParams(dimension_semantics=sem, vmem_limit_bytes=VMEM_LIMIT)


def _silu(v):
    return v * (1.0 / (1.0 + jnp.exp(-v)))


def _softplus(v):
    return jnp.maximum(v, 0.0) + jnp.log(1.0 + jnp.exp(-jnp.abs(v)))


def _dot(a, b):
    return jnp.dot(a, b, preferred_element_type=F32)


def _dot_nt(a, b):
    return lax.dot_general(a, b, (((1,), (1,)), ((), ())), preferred_element_type=F32)


def _hi_lo(v):
    bits = lax.bitcast_convert_type(v, jnp.uint32) & jnp.uint32(0xFFFF0000)
    hi = lax.bitcast_convert_type(bits, F32)
    return hi.astype(BF16), (v - hi).astype(BF16)


def _split3(v):
    h1 = v.astype(BF16)
    r1 = v - h1.astype(F32)
    h2 = r1.astype(BF16)
    r2 = r1 - h2.astype(F32)
    return h1, h2, r2.astype(BF16)


def _dot3(v, m3):
    return _dot(jnp.concatenate(_split3(v), axis=1), m3)


def _dot3_left(m3, v):
    return _dot(m3, jnp.concatenate(_split3(v), axis=0))


def _dot2(v, m2):
    h1 = v.astype(BF16)
    h2 = (v - h1.astype(F32)).astype(BF16)
    return _dot(jnp.concatenate([h1, h2], axis=1), m2)


def _layer_norm(r, g, b):
    mu = jnp.mean(r, -1, keepdims=True)
    d = r - mu
    var = jnp.mean(d * d, -1, keepdims=True)
    return d * lax.rsqrt(var + LN_EPS) * g + b


def _ada_kernel(c_ref, w_ref, b_ref, o_ref):
    c_hi, c_lo = _hi_lo(c_ref[...])
    w_hi, w_lo = _hi_lo(w_ref[0])
    cc = jnp.concatenate([c_hi, c_lo], axis=0)
    both = _dot(cc, w_hi) + _dot(cc, w_lo)
    rows = c_hi.shape[0]
    o_ref[0] = both[0:rows] + both[rows:2 * rows] + b_ref[0]


def _ada_mod(c, w_ada, b_ada):
    depth, d, d6 = w_ada.shape
    nb = c.shape[0]
    rows = 8
    c_pad = jnp.zeros((rows, d), F32).at[:nb].set(c)
    out = pl.pallas_call(
        _ada_kernel,
        out_shape=jax.ShapeDtypeStruct((depth, rows, d6), F32),
        grid=(depth, d6 // d),
        in_specs=[pl.BlockSpec((rows, d), lambda l, j: (0, 0)),
                  pl.BlockSpec((1, d, d), lambda l, j: (l, 0, j)),
                  pl.BlockSpec((1, 1, d), lambda l, j: (l, 0, j))],
        out_specs=pl.BlockSpec((1, rows, d), lambda l, j: (l, 0, j)),
        compiler_params=_cparams("arbitrary", "arbitrary"),
        name="ada_mod",
    )(c_pad, w_ada, b_ada.reshape(depth, 1, d6))
    return out[:, :nb].reshape(depth, nb, 6, d)


def _rotary_kernel(pos_ref, cos_ref, sin_ref):
    half = HEAD_DIM // 2
    lane = lax.broadcasted_iota(I32, (1, LANES), 1)
    jj = lane % HEAD_DIM
    idx = (jj % half).astype(F32)
    inv = jnp.exp(-math.log(10000.0) * idx / half)
    ang = pos_ref[0].astype(F32) * inv
    cos_ref[0] = jnp.cos(ang)
    sin_ref[0] = jnp.where(jj < half, -1.0, 1.0) * jnp.sin(ang)


def _rotary_tables(positions, tl):
    nb, L = positions.shape
    pos = positions.reshape(nb, L, 1)
    return pl.pallas_call(
        _rotary_kernel,
        out_shape=[jax.ShapeDtypeStruct((nb, L, LANES), F32)] * 2,
        grid=(nb, L // tl),
        in_specs=[pl.BlockSpec((1, tl, 1), lambda b, i: (b, i, 0))],
        out_specs=[pl.BlockSpec((1, tl, LANES), lambda b, i: (b, i, 0))] * 2,
        compiler_params=_cparams("arbitrary", "arbitrary"),
        name="rotary_tables",
    )(pos)


def _swa_bias_kernel(rb_ref, bucket_ref, band_ref, o_ref):
    bucket = bucket_ref[...]
    band = band_ref[...]
    for h in range(SWA_HEADS):
        acc = jnp.zeros(bucket.shape, F32)
        for b in range(REL_BUCKETS):
            acc = jnp.where(bucket == b, rb_ref[b, h], acc)
        o_ref[h] = jnp.where(band > 0, acc, MASK_VALUE)


def _t5_bucket(dist):
    exact = REL_BUCKETS // 2
    df = jnp.maximum(dist, 1).astype(F32)
    large = exact + (jnp.log(df / exact) / math.log(CHUNK / exact) * (REL_BUCKETS - exact)).astype(I32)
    large = jnp.minimum(large, REL_BUCKETS - 1)
    return jnp.where(dist < exact, dist, large)


def _swa_bias_table(rel_bias):
    W = CHUNK
    qi = jnp.arange(W)[:, None]
    kj = jnp.arange(2 * W)[None, :]
    dist = qi + W - kj
    band = ((dist >= 0) & (dist < W)).astype(I32)
    bucket = _t5_bucket(jnp.clip(dist, 0, W - 1)).astype(I32)
    return pl.pallas_call(
        _swa_bias_kernel,
        out_shape=jax.ShapeDtypeStruct((SWA_HEADS, W, 2 * W), F32),
        in_specs=[pl.BlockSpec(memory_space=pltpu.SMEM),
                  pl.BlockSpec(memory_space=pltpu.VMEM),
                  pl.BlockSpec(memory_space=pltpu.VMEM)],
        out_specs=pl.BlockSpec(memory_space=pltpu.VMEM),
        name="swa_bias_table",
    )(rel_bias, bucket, band)


def _head_lane_mask(width, head):
    lane = lax.broadcasted_iota(I32, (1, width), 1)
    return (lane // HEAD_DIM) == head


def _rotate_half(t):
    width = t.shape[-1]
    lane = lax.broadcasted_iota(I32, (1, width), 1)
    half = HEAD_DIM // 2
    fwd = pltpu.roll(t, width - half, axis=1)
    bwd = pltpu.roll(t, half, axis=1)
    return jnp.where((lane % HEAD_DIM) < half, fwd, bwd)


def _retention_body(u_ref, cos_ref, sin_ref, din_ref, dq_ref, dk_ref, dc_ref,
                    bmask_ref, avg_ref, o_ref, state_ref, *, n_chunks):
    rd = RET_HEADS * HEAD_DIM
    masks = [_head_lane_mask(rd, h) for h in range(RET_HEADS)]

    def stack_heads(t):
        return jnp.concatenate([jnp.where(m, t, 0.0) for m in masks], axis=0).astype(BF16)

    state = state_ref[...]
    for ci in range(n_chunks):
        rows = slice(ci * CHUNK, (ci + 1) * CHUNK)
        cos = cos_ref[0, rows, :]
        sin = sin_ref[0, rows, :]
        cos2 = jnp.concatenate([cos, cos], axis=1)
        sin2 = jnp.concatenate([sin, sin], axis=1)
        q = u_ref[0, rows, 0:rd].astype(F32)
        k = u_ref[0, rows, rd:2 * rd].astype(F32)
        v = u_ref[0, rows, 2 * rd:3 * rd].astype(F32)
        g = u_ref[0, rows, 3 * rd:4 * rd].astype(F32)
        qr = q * cos2 + _rotate_half(q) * sin2
        kr = (k * cos2 + _rotate_half(k) * sin2) * (HEAD_DIM ** -0.5)
        scores = _dot_nt(qr.astype(BF16), stack_heads(kr)) * din_ref[...]
        inner = _dot(scores.astype(BF16), stack_heads(v))
        cross = _dot((qr * dq_ref[...]).astype(BF16), state.astype(BF16))
        yield
        o = inner + cross
        kd_t = (kr * dk_ref[...]).T.astype(BF16)
        kv = _dot(kd_t, v.astype(BF16))
        state = dc_ref[...] * state + bmask_ref[...] * kv
        mu = _dot2(o, avg_ref[...])
        dev = o - mu
        var = _dot2(dev * dev, avg_ref[...])
        on = dev * lax.rsqrt(var + LN_EPS)
        o_ref[0, rows, :] = (_silu(g) * on).astype(o_ref.dtype)
        yield
    state_ref[...] = state


def _retention_tables():
    H, d, C = RET_HEADS, HEAD_DIM, CHUNK
    log_gamma = jnp.log(1.0 - 2.0 ** (-5.0 - jnp.arange(H, dtype=F32)))
    idx = jnp.arange(C, dtype=F32)
    diff = idx[:, None] - idx[None, :]
    decay_in = jnp.where(diff >= 0, jnp.exp(log_gamma[:, None, None] * jnp.maximum(diff, 0.0)), 0.0)
    decay_q = jnp.exp(log_gamma[:, None] * (idx + 1.0))
    decay_k = jnp.exp(log_gamma[:, None] * (C - 1.0 - idx))
    decay_chunk = jnp.exp(log_gamma * C)
    din = decay_in.transpose(1, 0, 2).reshape(C, H * C)
    dq = jnp.repeat(decay_q.T, d, axis=1)
    dk = jnp.repeat(decay_k.T, d, axis=1)
    dc = jnp.repeat(decay_chunk, d)[None, :]
    head = jnp.arange(H * d) // d
    bmask = (head[:, None] == head[None, :]).astype(F32)
    avg = jnp.tile((bmask / d).astype(BF16), (2, 1))
    return din, dq, dk, dc, bmask, avg


def _swa_body(u_ref, bias_ref, sink_ref, o_ref, kprev_ref, vprev_ref, *, n_chunks, first_step):
    W = CHUNK
    qd = SWA_HEADS * HEAD_DIM
    kvd = SWA_KV_HEADS * HEAD_DIM
    lane = lax.broadcasted_iota(I32, (1, LANES), 1)
    low = lane < HEAD_DIM
    col = lax.broadcasted_iota(I32, (1, 2 * W), 1)

    kprev = kprev_ref[...]
    vprev = vprev_ref[...]
    sink = sink_ref[...]
    for ci in range(n_chunks):
        rows = slice(ci * W, (ci + 1) * W)
        qa = u_ref[0, rows, 0:LANES].astype(F32)
        qb = u_ref[0, rows, LANES:qd].astype(F32)
        k = u_ref[0, rows, qd:qd + kvd].astype(BF16)
        v = u_ref[0, rows, qd + kvd:qd + 2 * kvd].astype(BF16)
        q4 = jnp.concatenate([jnp.where(low, qa, 0.0), jnp.where(low, 0.0, qa),
                              jnp.where(low, qb, 0.0), jnp.where(low, 0.0, qb)],
                             axis=0).astype(BF16)
        kband = jnp.concatenate([kprev, k], axis=0)
        vband = jnp.concatenate([vprev, v], axis=0)
        logits = _dot_nt(q4, kband) * (HEAD_DIM ** -0.5) + bias_ref[...]
        if ci == 0:
            logits = jnp.where(jnp.logical_and(first_step, col < W), MASK_VALUE, logits)
        m = jnp.maximum(jnp.max(logits, -1, keepdims=True), sink)
        p = jnp.exp(logits - m)
        denom = jnp.sum(p, -1, keepdims=True) + jnp.exp(sink - m)
        yield
        res = _dot(p.astype(BF16), vband) / denom
        out_a = jnp.where(low, res[0:W], res[W:2 * W])
        out_b = jnp.where(low, res[2 * W:3 * W], res[3 * W:4 * W])
        o_ref[0, rows, 0:LANES] = out_a.astype(o_ref.dtype)
        o_ref[0, rows, LANES:qd] = out_b.astype(o_ref.dtype)
        kprev, vprev = k, v
        yield
    kprev_ref[...] = kprev
    vprev_ref[...] = vprev


SWA_HEAD_ORDER = (0, 2, 1, 3)


def _ssd_body(u_ref, dt_ref, cw_ref, cb_ref, dtb_c_ref, alog_c_ref, dskip_ref, nw_ref,
              tril_ref, expand_ref, gmask_ref, shift_ref,
              o_ref, state_ref, ext_ref, *, n_chunks):
    C = CHUNK
    sd = SSD_HEADS * HEAD_DIM
    gn = SSD_GROUPS * SSD_STATE
    cd = sd + 2 * gn
    tl = n_chunks * C
    slab = 2 * LANES
    heads_per_group = SSD_HEADS // SSD_GROUPS
    heads_per_slab = slab // HEAD_DIM
    lane = lax.broadcasted_iota(I32, (1, LANES), 1)
    low = lane < SSD_STATE
    slab_masks = [_head_lane_mask(slab, hh) for hh in range(heads_per_slab)]
    row_i = lax.broadcasted_iota(I32, (C, C), 0)
    col_i = lax.broadcasted_iota(I32, (C, C), 1)
    causal = row_i >= col_i
    neg_a_c = -jnp.exp(alog_c_ref[...])

    ext_ref[CONV_TAIL:CONV_TAIL + tl, :] = u_ref[0, :, sd:sd + cd]
    state = state_ref[...]
    for ci in range(n_chunks):
        rows = slice(ci * C, (ci + 1) * C)
        z = u_ref[0, rows, 0:sd].astype(F32)
        dt_raw = dt_ref[0, rows, :]
        window = ext_ref[ci * C:ci * C + CONV_TAIL + C, :]
        shifted = _dot(shift_ref[...], window)
        conv = cb_ref[...] + cw_ref[SSD_CONV - 1:SSD_CONV, :] * window[CONV_TAIL:, :].astype(F32)
        for w in range(SSD_CONV - 1):
            conv = conv + cw_ref[w:w + 1, :] * shifted[w * C:(w + 1) * C, :]
        xbc = _silu(conv)
        xs = xbc[:, 0:sd]
        bm = xbc[:, sd:sd + gn]
        cm = xbc[:, sd + gn:cd]

        dt_c = _softplus(dt_raw + dtb_c_ref[...])
        a_c = neg_a_c * dt_c
        acs_c = _dot3_left(tril_ref[...], a_c)
        acs_t = acs_c.T
        spread = _dot3(acs_c, expand_ref[...])
        acs_x = spread[:, 0:sd]
        dt_x = _dot3(dt_c, expand_ref[:, 0:sd])
        xdt = xs * dt_x
        yield

        bstack = jnp.concatenate([jnp.where(low, bm, 0.0), jnp.where(low, 0.0, bm)],
                                 axis=0).astype(BF16)
        cb = _dot_nt(cm.astype(BF16), bstack)
        y_diag = []
        for s in range(sd // slab):
            ms = []
            for hh in range(heads_per_slab):
                h = s * heads_per_slab + hh
                g = h // heads_per_group
                col_bcast = spread[:, sd + h * LANES:sd + (h + 1) * LANES]
                seg = col_bcast - acs_t[h:h + 1, :]
                lmat = jnp.exp(jnp.where(causal, seg, MASK_VALUE))
                ms.append((cb[:, g * C:(g + 1) * C] * lmat).astype(BF16))
            xslab = xdt[:, s * slab:(s + 1) * slab]
            xstack = jnp.concatenate([jnp.where(m, xslab, 0.0) for m in slab_masks],
                                     axis=0).astype(BF16)
            y_diag.append(_dot(jnp.concatenate(ms, axis=1), xstack))
        y_diag = jnp.concatenate(y_diag, axis=1)
        yield

        y_off = _dot(cm.astype(BF16), state.astype(BF16)) * jnp.exp(acs_x)
        last = acs_x[C - 1:C, :]
        dec = jnp.exp(last - acs_x)
        new = _dot(bm.T.astype(BF16), (xdt * dec).astype(BF16))
        state = jnp.exp(last) * state + gmask_ref[...] * new

        y = y_diag + y_off + xs * dskip_ref[...]
        hgate = y * _silu(z)
        gw = sd // SSD_GROUPS
        for g in range(SSD_GROUPS):
            hg = hgate[:, g * gw:(g + 1) * gw]
            ms_ = jnp.mean(hg * hg, -1, keepdims=True)
            o_ref[0, rows, g * gw:(g + 1) * gw] = (
                hg * lax.rsqrt(ms_ + LN_EPS) * nw_ref[:, g * gw:(g + 1) * gw]).astype(o_ref.dtype)
        yield
    state_ref[...] = state
    ext_ref[0:CONV_TAIL, :] = ext_ref[tl:tl + CONV_TAIL, :]


def _ssd_tables():
    C = CHUNK
    sd = SSD_HEADS * HEAD_DIM
    gn = SSD_GROUPS * SSD_STATE
    t = np.arange(C)
    tril = (t[:, None] >= t[None, :])
    r = np.arange(LANES)[:, None]
    eexp = (r == (np.arange(sd)[None, :] // HEAD_DIM)) & (r < SSD_HEADS)
    bsel = (r == (np.arange(SSD_HEADS * LANES)[None, :] // LANES)) & (r < SSD_HEADS)
    heads_per_group = SSD_HEADS // SSD_GROUPS
    row_g = np.arange(gn)[:, None] // SSD_STATE
    col_g = (np.arange(sd)[None, :] // HEAD_DIM) // heads_per_group
    gmask = (row_g == col_g)
    expand = np.concatenate([eexp, bsel], axis=1)
    win = np.arange(CONV_TAIL + C)[None, :]
    shifts = np.concatenate(
        [(win == CONV_TAIL + t[:, None] - (SSD_CONV - 1 - s))
         for s in range(SSD_CONV - 1)], axis=0)
    return (jnp.asarray(np.tile(tril, (1, 3)), BF16), jnp.asarray(np.tile(expand, (3, 1)), BF16),
            jnp.asarray(gmask, F32), jnp.asarray(shifts, BF16))


N_RET_TABLES = 6
N_SSD_CONSTS = 10


def _in_proj_body(x_ref, shift, scale, w_ref, out_refs):
    h = (x_ref[0] * (1.0 + scale) + shift).astype(BF16)
    off = 0
    for ref in out_refs:
        width = ref.shape[-1]
        for a, b in _ff_chunks(width):
            ref[0, :, a:b] = _dot(h, w_ref[:, off + a:off + b]).astype(ref.dtype)
            yield
        off += width


def _mixers_kernel(*refs, n_chunks, n_cast, n_tiles, tiles_per_seq):
    x_ref, mod_ref, w_ref, cos, sin = refs[:5]
    pos = 5
    ret_tables = refs[pos:pos + N_RET_TABLES]
    pos += N_RET_TABLES
    ssd_consts = refs[pos:pos + N_SSD_CONSTS]
    pos += N_SSD_CONSTS
    bias, sink = refs[pos:pos + 2]
    pos += 2
    cast_in = refs[pos:pos + n_cast]
    pos += n_cast
    y_ret, y_ssd, y_swa = refs[pos:pos + 3]
    pos += 3
    cast_out = refs[pos:pos + n_cast]
    pos += n_cast
    ret_state, ssd_state, ssd_ext, kprev, vprev = refs[pos:pos + 5]
    cur = refs[pos + 5:pos + 9]
    nxt = refs[pos + 9:pos + 13]
    _side_cast(cast_in, cast_out)

    s = pl.program_id(0)
    tile = jnp.maximum(s - 1, 0)
    seq_start = tile % tiles_per_seq == 0

    @pl.when(s == 0)
    def _():
        for ref in cur:
            ref[...] = jnp.zeros_like(ref)

    @pl.when(seq_start)
    def _():
        ret_state[...] = jnp.zeros_like(ret_state)
        ssd_state[...] = jnp.zeros_like(ssd_state)
        ssd_ext[0:CONV_TAIL, :] = jnp.zeros((CONV_TAIL, ssd_ext.shape[1]), ssd_ext.dtype)
        kprev[...] = jnp.zeros_like(kprev)
        vprev[...] = jnp.zeros_like(vprev)

    batch = jnp.minimum(s, n_tiles - 1) // tiles_per_seq
    shift = mod_ref[0, batch, 0:1, :]
    scale = mod_ref[0, batch, 1:2, :]
    u_ret, u_ssd, u_dt, u_swa = cur
    bodies = [
        _ssd_body(u_ssd, u_dt, *ssd_consts, y_ssd, ssd_state, ssd_ext, n_chunks=n_chunks),
        _in_proj_body(x_ref, shift, scale, w_ref, nxt),
        _swa_body(u_swa, bias, sink, y_swa, kprev, vprev, n_chunks=n_chunks,
                  first_step=seq_start),
        _retention_body(u_ret, cos, sin, *ret_tables, y_ret, ret_state, n_chunks=n_chunks),
    ]
    alive = list(bodies)
    while alive:
        alive = [b for b in alive if next(b, StopIteration) is not StopIteration]
    for c_ref, n_ref in zip(cur, nxt):
        c_ref[...] = n_ref[...]


def _mixers(x, mod_l, w_cat, widths, dtypes, cos_t, sin_t, bias_tab, sinks_l, conv_w, conv_b,
            dt_bias, a_log, d_skip, norm_w, tl, side_cast=()):
    nb, L, d = x.shape
    W = CHUNK
    rd = RET_HEADS * HEAD_DIM
    sd = SSD_HEADS * HEAD_DIM
    qd = SWA_HEADS * HEAD_DIM
    cd = conv_w.shape[-1]
    ret_tables = _retention_tables()
    pad = lambda v: jnp.zeros((1, LANES), F32).at[0, :SSD_HEADS].set(v)
    rep = lambda v: jnp.repeat(v, HEAD_DIM)[None, :]
    ssd_consts = (conv_w, conv_b[None, :], pad(dt_bias), pad(a_log), rep(d_skip),
                  norm_w[None, :]) + _ssd_tables()
    order = jnp.array(SWA_HEAD_ORDER)
    bias_stacked = bias_tab[order].reshape(SWA_HEADS * W, 2 * W)
    sink_col = jnp.repeat(sinks_l.astype(F32)[order], W)[:, None]
    assert len(ret_tables) == N_RET_TABLES and len(ssd_consts) == N_SSD_CONSTS
    tps = L // tl
    n_tiles = nb * tps

    def tile_of(step):
        return step // tps, step % tps, 0

    proj_map = lambda s: tile_of(jnp.minimum(s, n_tiles - 1))
    mix_map = lambda s: tile_of(jnp.maximum(s - 1, 0))
    const = lambda a: pl.BlockSpec(a.shape, lambda s: (0,) * a.ndim)
    consts = ret_tables + ssd_consts + (bias_stacked, sink_col)
    c_in, c_out, c_shapes = _side_cast_specs(side_cast, n_tiles)
    u_bufs = [pltpu.VMEM((1, tl, w), t) for w, t in zip(widths, dtypes)]
    return pl.pallas_call(
        functools.partial(_mixers_kernel, n_chunks=tl // CHUNK, n_cast=len(side_cast),
                          n_tiles=n_tiles, tiles_per_seq=tps),
        out_shape=[jax.ShapeDtypeStruct((nb, L, w), BF16) for w in (rd, sd, qd)] + c_shapes,
        grid=(n_tiles + 1,),
        in_specs=[pl.BlockSpec((1, tl, d), proj_map), const(mod_l), const(w_cat),
                  pl.BlockSpec((1, tl, LANES), mix_map), pl.BlockSpec((1, tl, LANES), mix_map)]
                 + [const(a) for a in consts] + c_in,
        out_specs=[pl.BlockSpec((1, tl, w), mix_map) for w in (rd, sd, qd)] + c_out,
        scratch_shapes=[pltpu.VMEM((rd, rd), F32),
                        pltpu.VMEM((SSD_GROUPS * SSD_STATE, sd), F32),
                        pltpu.VMEM((CONV_TAIL + tl, cd), BF16),
                        pltpu.VMEM((W, LANES), BF16), pltpu.VMEM((W, LANES), BF16)]
                       + u_bufs + u_bufs,
        compiler_params=_cparams("arbitrary"),
        name="in_proj_mixers",
    )(x, mod_l, w_cat, cos_t, sin_t, *consts, *side_cast)


N_ATTN_OUT = 10


def _attn_out(x_ref, yr_ref, ys_ref, ya_ref, mod_ref, wr_ref, ws_ref, wa_ref, g_ref, b_ref,
              alpha):
    mix = (_dot(yr_ref[0], wr_ref[...]) + _dot(ys_ref[0], ws_ref[...])
           + _dot(ya_ref[0], wa_ref[...]))
    gate = mod_ref[0, 0, 2:3, :]
    r = alpha * x_ref[0] + (1.0 + gate) * mix
    return _layer_norm(r, g_ref[...], b_ref[...])


def _attn_out_operands(x, y_ret, y_ssd, y_swa, mod_l, w_r, w_s, w_a, ln_g, ln_b, tm):
    d = x.shape[-1]
    tok = lambda w: pl.BlockSpec((1, tm, w), lambda b, i: (b, i, 0))
    const = lambda a: pl.BlockSpec(a.shape, lambda b, i: (0,) * a.ndim)
    specs = [tok(d), tok(y_ret.shape[-1]), tok(y_ssd.shape[-1]), tok(y_swa.shape[-1]),
             pl.BlockSpec((1, 1, 6, d), lambda b, i: (0, b, 0, 0)),
             const(w_r), const(w_s), const(w_a), const(ln_g), const(ln_b)]
    return specs, (x, y_ret, y_ssd, y_swa, mod_l, w_r, w_s, w_a, ln_g, ln_b)


def _side_cast_specs(arrays, steps, steps_per_seq=None):
    in_specs, out_specs, out_shapes = [], [], []
    for a in arrays:
        e, r, c = a.shape
        per = steps // e
        if steps_per_seq is None:
            index = lambda s, per=per: (jnp.minimum(s, steps - 1) // per,
                                        jnp.minimum(s, steps - 1) % per, 0)
        else:
            index = lambda b, i, per=per: ((b * steps_per_seq + i) // per,
                                           (b * steps_per_seq + i) % per, 0)
        spec = pl.BlockSpec((1, r // per, c), index)
        in_specs.append(spec)
        out_specs.append(spec)
        out_shapes.append(jax.ShapeDtypeStruct(a.shape, BF16))
    return in_specs, out_specs, out_shapes


def _side_cast(in_refs, out_refs):
    for i_ref, o_ref in zip(in_refs, out_refs):
        o_ref[...] = i_ref[...].astype(o_ref.dtype)


def _dense_ffn_kernel(*refs, alpha):
    attn = refs[:N_ATTN_OUT]
    mod_ref = attn[4]
    wg_ref, wu_ref, wd_ref, g_ref, b_ref = refs[N_ATTN_OUT:N_ATTN_OUT + 5]
    rest = refs[N_ATTN_OUT + 5:]
    n_cast = (len(rest) - 1) // 2
    o_ref = rest[n_cast]
    _side_cast(rest[:n_cast], rest[n_cast + 1:])
    x = _attn_out(*attn, alpha)
    sh = mod_ref[0, 0, 3:4, :]
    sc = mod_ref[0, 0, 4:5, :]
    gate = mod_ref[0, 0, 5:6, :]
    h = (x * (1.0 + sc) + sh).astype(BF16)
    dff = wg_ref.shape[-1]
    acc = None
    for a, b in _ff_chunks(dff):
        gj = _dot(h, wg_ref[:, a:b])
        uj = _dot(h, wu_ref[:, a:b])
        part = _dot((_silu(gj) * uj).astype(BF16), wd_ref[a:b, :])
        acc = part if acc is None else acc + part
    r = alpha * x + (1.0 + gate) * acc
    o_ref[0] = _layer_norm(r, g_ref[...], b_ref[...])


def _dense_ffn(attn_args, wg, wu, wd, ln_g, ln_b, alpha, tm, side_cast=()):
    nb, L, d = attn_args[0].shape
    const = lambda a: pl.BlockSpec(a.shape, lambda b, i: (0,) * a.ndim,
                                   pipeline_mode=pl.Buffered(1))
    a_specs, a_ops = _attn_out_operands(*attn_args, tm)
    c_in, c_out, c_shapes = _side_cast_specs(side_cast, nb * (L // tm), L // tm)
    return pl.pallas_call(
        functools.partial(_dense_ffn_kernel, alpha=alpha),
        out_shape=[jax.ShapeDtypeStruct((nb, L, d), F32)] + c_shapes,
        grid=(nb, L // tm),
        in_specs=a_specs + [const(wg), const(wu), const(wd), const(ln_g), const(ln_b)] + c_in,
        out_specs=[pl.BlockSpec((1, tm, d), lambda b, i: (b, i, 0))] + c_out,
        compiler_params=_cparams("arbitrary", "arbitrary"),
        name="attn_out_dense_ffn_ln",
    )(*a_ops, wg, wu, wd, ln_g, ln_b, *side_cast)


def _store_row_tiles(ref, value):
    rows, d = value.shape
    nt = d // LANES
    for c in range(nt):
        ref[pl.ds(c, rows, stride=nt), :] = value[:, c * LANES:(c + 1) * LANES].astype(ref.dtype)


def _load_row_tiles(ref):
    nt = ROW_TILE
    rows = ref.shape[0] // nt
    return jnp.concatenate([ref[pl.ds(c, rows, stride=nt), :] for c in range(nt)], axis=1)


def _router_kernel(*refs, alpha):
    attn = refs[:N_ATTN_OUT]
    mod_ref = attn[4]
    wr_ref, br_ref, trils_ref, x1_ref, h_ref, route_ref, cnt_ref, base_ref = refs[N_ATTN_OUT:]
    first = jnp.logical_and(pl.program_id(0) == 0, pl.program_id(1) == 0)

    @pl.when(first)
    def _():
        base_ref[...] = jnp.zeros_like(base_ref)

    x1 = _attn_out(*attn, alpha)
    x1_ref[0] = x1
    sh = mod_ref[0, 0, 3:4, :]
    sc = mod_ref[0, 0, 4:5, :]
    h = x1 * (1.0 + sc) + sh
    _store_row_tiles(h_ref.at[0], h)
    h_hi, h_lo = _hi_lo(h)
    wide = _dot(h_hi, wr_ref[...]) + _dot(h_lo, wr_ref[...])
    logits = wide + pltpu.roll(wide, LANES - N_EXPERTS, axis=1) + br_ref[...]
    lane = lax.broadcasted_iota(I32, logits.shape, 1).astype(F32)
    logits = jnp.where(lane < N_EXPERTS, logits, MASK_VALUE)
    v1 = jnp.max(logits, -1, keepdims=True)
    e1 = jnp.min(jnp.where(logits == v1, lane, float(LANES)), -1, keepdims=True)
    rest = jnp.where(lane == e1, MASK_VALUE, logits)
    v2 = jnp.max(rest, -1, keepdims=True)
    e2 = jnp.min(jnp.where(rest == v2, lane, float(LANES)), -1, keepdims=True)
    t = jnp.exp(v2 - v1)
    w1 = 1.0 / (1.0 + t)
    w2 = t / (1.0 + t)
    hot1 = (lane == e1).astype(F32)
    hot2 = (lane == e2).astype(F32)
    both = hot1 + hot2
    base = base_ref[0:1, :]
    before = _dot(trils_ref[...], both.astype(BF16)) + base
    rank1 = jnp.sum(hot1 * before, -1, keepdims=True)
    rank2 = jnp.sum(hot2 * before, -1, keepdims=True)
    total = base + jnp.sum(both, 0, keepdims=True)
    base_ref[0:1, :] = total
    cnt_ref[...] = jnp.broadcast_to(total, cnt_ref.shape)
    route = jnp.where(lane == 0, e1, 0.0)
    route = jnp.where(lane == 1, e2, route)
    route = jnp.where(lane == 2, rank1, route)
    route = jnp.where(lane == 3, rank2, route)
    route = jnp.where(lane == 4, w1, route)
    route = jnp.where(lane == 5, w2, route)
    route_ref[0] = route


def _router(attn_args, w_router, b_router, alpha, tm):
    nb, L, d = attn_args[0].shape
    w_hi, w_lo = _hi_lo(w_router)
    wr = (jnp.zeros((d, LANES), BF16).at[:, :N_EXPERTS].set(w_hi)
          .at[:, N_EXPERTS:2 * N_EXPERTS].set(w_lo))
    br = jnp.zeros((1, LANES), F32).at[0, :N_EXPERTS].set(b_router)
    t = jnp.arange(tm)
    tril_strict = (t[:, None] > t[None, :]).astype(BF16)
    const = lambda a: pl.BlockSpec(a.shape, lambda b, i: (0,) * a.ndim)
    a_specs, a_ops = _attn_out_operands(*attn_args, tm)
    return pl.pallas_call(
        functools.partial(_router_kernel, alpha=alpha),
        out_shape=[jax.ShapeDtypeStruct((nb, L, d), F32),
                   jax.ShapeDtypeStruct((nb, L * (d // LANES), LANES), F32),
                   jax.ShapeDtypeStruct((nb, L, LANES), F32),
                   jax.ShapeDtypeStruct((8, LANES), F32)],
        grid=(nb, L // tm),
        in_specs=a_specs + [const(wr), const(br), const(tril_strict)],
        out_specs=[pl.BlockSpec((1, tm, d), lambda b, i: (b, i, 0)),
                   pl.BlockSpec((1, tm * (d // LANES), LANES), lambda b, i: (b, i, 0)),
                   pl.BlockSpec((1, tm, LANES), lambda b, i: (b, i, 0)),
                   pl.BlockSpec((8, LANES), lambda b, i: (0, 0))],
        scratch_shapes=[pltpu.VMEM((8, LANES), F32)],
        compiler_params=_cparams("arbitrary", "arbitrary"),
        name="attn_out_moe_router",
    )(*a_ops, wr, br, tril_strict)


TOP_K = 2
ROW_TILE = 8
EXPERT_FF_SPLITS = 2


def _ff_chunks(width, step=512):
    return [(a, min(a + step, width)) for a in range(0, width, step)]


def _expert_kernel(te_ref, nu_ref, gsrc_ref, sdst_ref,
                   h_ref, wg_ref, wu_ref, wd_ref, yk_ref,
                   xbuf, ybuf, hbuf, acc, gsem, ssem, *, tme):
    i = pl.program_id(0)
    j = pl.program_id(1)
    n_used = nu_ref[0]
    slot = i % 2
    other = 1 - slot
    used = i < n_used
    chunks = _ff_chunks(wg_ref.shape[-1])
    rows_per_step = tme // EXPERT_FF_SPLITS
    n_front = len(chunks) // 2
    front_w = chunks[n_front - 1][1]
    back_w = wg_ref.shape[-1] - front_w
    gather_ranges, scatter_ranges = [], []
    for c, (a, b) in enumerate(chunks):
        if c < n_front:
            gather_ranges.append((rows_per_step * a // front_w, rows_per_step * b // front_w))
            scatter_ranges.append((0, 0))
        else:
            gather_ranges.append((0, 0))
            scatter_ranges.append((rows_per_step * (a - front_w) // back_w,
                                   rows_per_step * (b - front_w) // back_w))

    nt = ROW_TILE

    def gather(tile, slot_, r):
        src = pl.multiple_of(gsrc_ref[tile * tme + r], nt)
        return pltpu.make_async_copy(h_ref.at[pl.ds(src, nt)],
                                     xbuf.at[slot_, pl.ds(pl.multiple_of(r * nt, nt), nt)],
                                     gsem.at[slot_])

    def wait_gather(slot_):
        pltpu.make_async_copy(h_ref.at[pl.ds(0, tme * nt)], xbuf.at[slot_],
                              gsem.at[slot_]).wait()

    def scatter(block, slot_, r):
        dst = pl.multiple_of(sdst_ref[block * tme + r], nt)
        return pltpu.make_async_copy(ybuf.at[slot_, pl.ds(pl.multiple_of(r * nt, nt), nt)],
                                     yk_ref.at[pl.ds(dst, nt)], ssem.at[slot_])

    def wait_scatter(slot_):
        pltpu.make_async_copy(ybuf.at[slot_], yk_ref.at[pl.ds(0, tme * nt)],
                              ssem.at[slot_]).wait()

    @pl.when(jnp.logical_and(i == 0, j == 0))
    def _():
        ybuf[1] = jnp.zeros(ybuf.shape[1:], F32)

        def body(r, carry):
            gather(0, 0, r).start()
            return carry
        lax.fori_loop(0, tme, body, 0, unroll=8)

    @pl.when(used)
    def _():
        @pl.when(j == 0)
        def _():
            wait_gather(slot)

            @pl.when(i > 0)
            def _():
                wait_scatter(slot)
            acc[...] = jnp.zeros(acc.shape, F32)

        hbuf[...] = _load_row_tiles(xbuf.at[slot]).astype(BF16)
        h = hbuf[...]
        for c, (a, b) in enumerate(chunks):
            for r in range(*gather_ranges[c]):
                gather(i + 1, other, j * rows_per_step + r).start()
            for r in range(*scatter_ranges[c]):
                scatter(i, other, j * rows_per_step + r).start()
            gj = _dot(h, wg_ref[0, :, a:b])
            uj = _dot(h, wu_ref[0, :, a:b])
            part = _dot((_silu(gj) * uj).astype(BF16), wd_ref[0, a:b, :])
            acc[...] = acc[...] + part
            ybuf[slot, 0:ROW_TILE, :] = part[0:ROW_TILE, 0:LANES]

        @pl.when(j == EXPERT_FF_SPLITS - 1)
        def _():
            _store_row_tiles(ybuf.at[slot], acc[...])

        @pl.when(jnp.logical_and(j == EXPERT_FF_SPLITS - 1, i == n_used - 1))
        def _():
            wait_gather(other)
            wait_scatter(other)

            def body(r, carry):
                scatter(i + 1, slot, r).start()
                return carry
            lax.fori_loop(0, tme, body, 0, unroll=8)
            wait_scatter(slot)


def _expert_ffn(h_rows, tile_expert, n_used, gsrc, sdst, wg, wu, wd, tme):
    nt = ROW_TILE
    T = h_rows.shape[0] // nt
    d = nt * LANES
    dff = wg.shape[-1]
    dffh = dff // EXPERT_FF_SPLITS
    n_tiles = gsrc.shape[0] // tme
    yk_rows = TOP_K * T + tme

    def half(i, j):
        return jnp.where(i % 2 == 0, j, EXPERT_FF_SPLITS - 1 - j)

    return pl.pallas_call(
        functools.partial(_expert_kernel, tme=tme),
        out_shape=jax.ShapeDtypeStruct((yk_rows * nt, LANES), F32),
        grid_spec=pltpu.PrefetchScalarGridSpec(
            num_scalar_prefetch=4,
            grid=(n_tiles, EXPERT_FF_SPLITS),
            in_specs=[pl.BlockSpec(memory_space=pl.ANY),
                      pl.BlockSpec((1, d, dffh), lambda i, j, te, *_: (te[i], 0, half(i, j))),
                      pl.BlockSpec((1, d, dffh), lambda i, j, te, *_: (te[i], 0, half(i, j))),
                      pl.BlockSpec((1, dffh, d), lambda i, j, te, *_: (te[i], half(i, j), 0))],
            out_specs=pl.BlockSpec(memory_space=pl.ANY),
            scratch_shapes=[pltpu.VMEM((2, tme * nt, LANES), F32),
                            pltpu.VMEM((2, tme * nt, LANES), F32),
                            pltpu.VMEM((tme, d), BF16), pltpu.VMEM((tme, d), F32),
                            pltpu.SemaphoreType.DMA((2,)), pltpu.SemaphoreType.DMA((2,))]),
        compiler_params=_cparams("arbitrary", "arbitrary"),
        name="moe_experts",
    )(tile_expert, n_used, gsrc, sdst, h_rows, wg, wu, wd)


def _moe_finish_kernel(x_ref, route_ref, mod_ref, y0_ref, y1_ref, g_ref, b_ref, o_ref, *, alpha):
    route = route_ref[0]
    f = route[:, 4:5] * _load_row_tiles(y0_ref) + route[:, 5:6] * _load_row_tiles(y1_ref)
    gate = mod_ref[0, 0, 5:6, :]
    r = alpha * x_ref[0] + (1.0 + gate) * f
    o_ref[0] = _layer_norm(r, g_ref[...], b_ref[...])


def _moe_finish(x, route, mod_l, yk, ln_g, ln_b, alpha, tm):
    nb, L, d = x.shape
    tiles_per_seq = L // tm
    tiles = nb * tiles_per_seq
    return pl.pallas_call(
        functools.partial(_moe_finish_kernel, alpha=alpha),
        out_shape=jax.ShapeDtypeStruct((nb, L, d), F32),
        grid=(nb, tiles_per_seq),
        in_specs=[pl.BlockSpec((1, tm, d), lambda b, i: (b, i, 0)),
                  pl.BlockSpec((1, tm, LANES), lambda b, i: (b, i, 0)),
                  pl.BlockSpec((1, 1, 6, d), lambda b, i: (0, b, 0, 0)),
                  pl.BlockSpec((tm * ROW_TILE, LANES), lambda b, i: (b * tiles_per_seq + i, 0)),
                  pl.BlockSpec((tm * ROW_TILE, LANES),
                               lambda b, i: (tiles + b * tiles_per_seq + i, 0)),
                  pl.BlockSpec(ln_g.shape, lambda b, i: (0, 0)),
                  pl.BlockSpec(ln_b.shape, lambda b, i: (0, 0))],
        out_specs=pl.BlockSpec((1, tm, d), lambda b, i: (b, i, 0)),
        compiler_params=_cparams("arbitrary", "arbitrary"),
        name="moe_finish_ln",
    )(x, route, mod_l, yk, yk, ln_g, ln_b)


def _row_index_kernel(dest_ref, gsrc0_ref, sdst0_ref, gsrc_ref, sdst_ref, sem, *, n_pairs, T, tme):
    init_g = pltpu.make_async_copy(gsrc0_ref, gsrc_ref, sem.at[0])
    init_s = pltpu.make_async_copy(sdst0_ref, sdst_ref, sem.at[1])
    init_g.start()
    init_s.start()
    init_g.wait()
    init_s.wait()

    assert TOP_K == 2
    group = 8

    def body(g, carry):
        f0 = g * group
        rows = [dest_ref[f0 + k] for k in range(group)]
        for k in range(group):
            tok = g * (group // TOP_K) + k // TOP_K
            gsrc_ref[rows[k]] = tok * ROW_TILE
            sdst_ref[tme + rows[k]] = ((k % TOP_K) * T + tok) * ROW_TILE
        return carry
    lax.fori_loop(0, n_pairs // group, body, 0, unroll=2)


def _row_indices(dest, n_rows, T, tme):
    dump = (TOP_K * T + jnp.arange(n_rows + tme, dtype=I32) % tme) * ROW_TILE
    return pl.pallas_call(
        functools.partial(_row_index_kernel, n_pairs=dest.shape[0], T=T, tme=tme),
        out_shape=[jax.ShapeDtypeStruct((n_rows,), I32),
                   jax.ShapeDtypeStruct((n_rows + tme,), I32)],
        in_specs=[pl.BlockSpec(memory_space=pltpu.SMEM),
                  pl.BlockSpec(memory_space=pl.ANY),
                  pl.BlockSpec(memory_space=pl.ANY)],
        out_specs=[pl.BlockSpec(memory_space=pltpu.SMEM),
                   pl.BlockSpec(memory_space=pltpu.SMEM)],
        scratch_shapes=[pltpu.SemaphoreType.DMA((2,))],
        name="moe_row_indices",
    )(dest, jnp.zeros((n_rows,), I32), dump)


def _moe(attn_args, w_router, b_router, wg, wu, wd, ln_g, ln_b, alpha, tm_route, tme, tmf):
    nb, L, d = attn_args[0].shape
    mod_l = attn_args[4]
    T = nb * L
    x, h, route, counts = _router(attn_args, w_router, b_router, alpha, tm_route)
    route_flat = route.reshape(T, LANES)
    e = route_flat[:, 0:2].astype(I32)
    rank = route_flat[:, 2:4].astype(I32)
    cnt = counts[0, :N_EXPERTS].astype(I32)
    tiles = (cnt + tme - 1) // tme
    tile_end = jnp.cumsum(tiles)
    group_start = (tile_end - tiles) * tme
    dest = (group_start[e] + rank).reshape(-1)
    n_tiles = (TOP_K * T) // tme + N_EXPERTS
    n_rows = n_tiles * tme
    tile_expert = jnp.minimum(
        jnp.sum(jnp.arange(n_tiles, dtype=I32)[:, None] >= tile_end[None, :].astype(I32), axis=1),
        N_EXPERTS - 1).astype(I32)
    n_used = tile_end[-1:].astype(I32)
    gsrc, sdst = _row_indices(dest, n_rows, T, tme)
    yk = _expert_ffn(h.reshape(T * ROW_TILE, LANES), tile_expert, n_used, gsrc, sdst,
                     wg, wu, wd, tme)
    return _moe_finish(x, route, mod_l, yk, ln_g, ln_b, alpha, tmf)


def kernel(x, c, positions, rel_bias, w_ada, b_ada, w_in, w_out, conv_w, conv_b, dt_bias, a_log,
           d_skip, ssd_norm_w, sinks, ln_g, ln_b, ffn_w_gate, ffn_w_up, ffn_w_down, router_w,
           router_b, expert_w_gate, expert_w_up, expert_w_down):
    depth = w_ada.shape[0]
    nb, L, d = x.shape
    alpha = (2 * depth) ** 0.25
    rd = RET_HEADS * HEAD_DIM
    sd = SSD_HEADS * HEAD_DIM
    cd = conv_w.shape[-1]
    qd = SWA_HEADS * HEAD_DIM
    kvd = SWA_KV_HEADS * HEAD_DIM
    sizes = (rd, rd, rd, rd, sd, cd, SSD_HEADS, qd, kvd, kvd)
    offs = np.concatenate([[0], np.cumsum(sizes)])
    tl = _token_tile(L)

    mod = _ada_mod(c, w_ada, b_ada)
    cos_t, sin_t = _rotary_tables(positions, tl)
    bias_tab = _swa_bias_table(rel_bias)

    q_perm = np.concatenate([np.arange(h * HEAD_DIM, (h + 1) * HEAD_DIM) for h in SWA_HEAD_ORDER])
    widths = (4 * rd, sd + cd, LANES, qd + 2 * kvd)
    dtypes = (BF16, BF16, F32, BF16)

    steps = nb * (L // tl)
    n_exp, _, dff_e = expert_w_gate.shape[1:]
    per = steps // n_exp
    side_ok = (steps % n_exp == 0 and d % per == 0 and dff_e % per == 0
               and (d // per) % 8 == 0 and (dff_e // per) % 8 == 0)

    for layer in range(depth):
        wl = w_in[layer]
        seg = lambda i: wl[:, offs[i]:offs[i + 1]]
        dt_cols = jnp.zeros((d, LANES), F32).at[:, :SSD_HEADS].set(seg(6))
        w_cat = jnp.concatenate(
            [seg(0), seg(1), seg(2), seg(3), seg(4), seg(5), dt_cols,
             seg(7)[:, q_perm], seg(8), seg(9)], axis=1).astype(BF16)
        wo = w_out[layer]
        w_r = wo[0:rd].astype(BF16)
        w_s = wo[rd:rd + sd].astype(BF16)
        w_a = wo[rd + sd:][q_perm].astype(BF16)
        mod_l = mod[layer:layer + 1]

        i = layer // 2
        is_moe = layer % 2 == 1
        mix_cast = (expert_w_up[i], expert_w_down[i]) if is_moe and side_ok else ()
        y_ret, y_ssd, y_swa, *up_down_bf16 = _mixers(
            x, mod_l, w_cat, widths, dtypes, cos_t, sin_t, bias_tab, sinks[layer], conv_w[layer],
            conv_b[layer], dt_bias[layer], a_log[layer], d_skip[layer], ssd_norm_w[layer], tl,
            side_cast=mix_cast)
        attn_args = (x, y_ret, y_ssd, y_swa, mod_l, w_r, w_s, w_a,
                     ln_g[layer, 0][None, :], ln_b[layer, 0][None, :])

        g2 = ln_g[layer, 1][None, :]
        b2 = ln_b[layer, 1][None, :]
        if not is_moe:
            nxt = (layer + 1) // 2
            ride = (expert_w_gate[nxt],) if layer + 1 < depth and side_ok else ()
            x, *gate_bf16 = _dense_ffn(
                attn_args, ffn_w_gate[i].astype(BF16), ffn_w_up[i].astype(BF16),
                ffn_w_down[i].astype(BF16), g2, b2, alpha, tl, side_cast=ride)
        else:
            if side_ok:
                wg_e, = gate_bf16
                wu_e, wd_e = up_down_bf16
            else:
                wg_e, wu_e, wd_e = (expert_w_gate[i].astype(BF16), expert_w_up[i].astype(BF16),
                                    expert_w_down[i].astype(BF16))
            x = _moe(attn_args, router_w[i], router_b[i], wg_e, wu_e, wd_e, g2, b2, alpha,
                     tm_route=tl, tme=tl, tmf=tl)
    return x
```

```python
import functools
import math

import numpy as np
import jax
import jax.numpy as jnp
from jax import lax
from jax.experimental import pallas as pl
from jax.experimental.pallas import tpu as pltpu

F32 = jnp.float32
BF16 = jnp.bfloat16
I32 = jnp.int32

HEAD_DIM = 64
CHUNK = 128
RET_HEADS = 4
SSD_HEADS = 8
SSD_GROUPS = 2
SSD_STATE = 64
SSD_CONV = 4
CONV_TAIL = 16
SWA_HEADS = 4
SWA_KV_HEADS = 2
REL_BUCKETS = 32
N_EXPERTS = 8
LN_EPS = 1e-5
LANES = 128
MASK_VALUE = -1e30

V7X_VMEM_BYTES = 64 * 1024 * 1024
VMEM_LIMIT = V7X_VMEM_BYTES - 8 * 1024 * 1024
TOKEN_TILE = 512


def _token_tile(seq_len):
    return min(TOKEN_TILE, seq_len)


def _cparams(*sem):
    return pltpu.CompilerParams(dimension_semantics=sem, vmem_limit_bytes=VMEM_LIMIT)


def _silu(v):
    return v * (1.0 / (1.0 + jnp.exp(-v)))


def _softplus(v):
    return jnp.maximum(v, 0.0) + jnp.log(1.0 + jnp.exp(-jnp.abs(v)))


def _dot(a, b):
    return jnp.dot(a, b, preferred_element_type=F32)


def _dot_nt(a, b):
    return lax.dot_general(a, b, (((1,), (1,)), ((), ())), preferred_element_type=F32)


def _hi_lo(v):
    bits = lax.bitcast_convert_type(v, jnp.uint32) & jnp.uint32(0xFFFF0000)
    hi = lax.bitcast_convert_type(bits, F32)
    return hi.astype(BF16), (v - hi).astype(BF16)


def _split3(v):
    h1 = v.astype(BF16)
    r1 = v - h1.astype(F32)
    h2 = r1.astype(BF16)
    r2 = r1 - h2.astype(F32)
    return h1, h2, r2.astype(BF16)


def _dot3(v, m3):
    return _dot(jnp.concatenate(_split3(v), axis=1), m3)


def _dot3_left(m3, v):
    return _dot(m3, jnp.concatenate(_split3(v), axis=0))


def _dot2(v, m2):
    h1 = v.astype(BF16)
    h2 = (v - h1.astype(F32)).astype(BF16)
    return _dot(jnp.concatenate([h1, h2], axis=1), m2)


def _layer_norm(r, g, b):
    mu = jnp.mean(r, -1, keepdims=True)
    d = r - mu
    var = jnp.mean(d * d, -1, keepdims=True)
    return d * lax.rsqrt(var + LN_EPS) * g + b


def _ada_kernel(c_ref, w_ref, b_ref, o_ref):
    c_hi, c_lo = _hi_lo(c_ref[...])
    w_hi, w_lo = _hi_lo(w_ref[0])
    cc = jnp.concatenate([c_hi, c_lo], axis=0)
    both = _dot(cc, w_hi) + _dot(cc, w_lo)
    rows = c_hi.shape[0]
    o_ref[0] = both[0:rows] + both[rows:2 * rows] + b_ref[0]


def _ada_mod(c, w_ada, b_ada):
    depth, d, d6 = w_ada.shape
    nb = c.shape[0]
    rows = 8
    c_pad = jnp.zeros((rows, d), F32).at[:nb].set(c)
    out = pl.pallas_call(
        _ada_kernel,
        out_shape=jax.ShapeDtypeStruct((depth, rows, d6), F32),
        grid=(depth, d6 // d),
        in_specs=[pl.BlockSpec((rows, d), lambda l, j: (0, 0)),
                  pl.BlockSpec((1, d, d), lambda l, j: (l, 0, j)),
                  pl.BlockSpec((1, 1, d), lambda l, j: (l, 0, j))],
        out_specs=pl.BlockSpec((1, rows, d), lambda l, j: (l, 0, j)),
        compiler_params=_cparams("arbitrary", "arbitrary"),
        name="ada_mod",
    )(c_pad, w_ada, b_ada.reshape(depth, 1, d6))
    return out[:, :nb].reshape(depth, nb, 6, d)


def _rotary_kernel(pos_ref, cos_ref, sin_ref):
    half = HEAD_DIM // 2
    lane = lax.broadcasted_iota(I32, (1, LANES), 1)
    jj = lane % HEAD_DIM
    idx = (jj % half).astype(F32)
    inv = jnp.exp(-math.log(10000.0) * idx / half)
    ang = pos_ref[0].astype(F32) * inv
    cos_ref[0] = jnp.cos(ang)
    sin_ref[0] = jnp.where(jj < half, -1.0, 1.0) * jnp.sin(ang)


def _rotary_tables(positions, tl):
    nb, L = positions.shape
    pos = positions.reshape(nb, L, 1)
    return pl.pallas_call(
        _rotary_kernel,
        out_shape=[jax.ShapeDtypeStruct((nb, L, LANES), F32)] * 2,
        grid=(nb, L // tl),
        in_specs=[pl.BlockSpec((1, tl, 1), lambda b, i: (b, i, 0))],
        out_specs=[pl.BlockSpec((1, tl, LANES), lambda b, i: (b, i, 0))] * 2,
        compiler_params=_cparams("arbitrary", "arbitrary"),
        name="rotary_tables",
    )(pos)


def _swa_bias_kernel(rb_ref, bucket_ref, band_ref, o_ref):
    bucket = bucket_ref[...]
    band = band_ref[...]
    for h in range(SWA_HEADS):
        acc = jnp.zeros(bucket.shape, F32)
        for b in range(REL_BUCKETS):
            acc = jnp.where(bucket == b, rb_ref[b, h], acc)
        o_ref[h] = jnp.where(band > 0, acc, MASK_VALUE)


def _t5_bucket(dist):
    exact = REL_BUCKETS // 2
    df = jnp.maximum(dist, 1).astype(F32)
    large = exact + (jnp.log(df / exact) / math.log(CHUNK / exact) * (REL_BUCKETS - exact)).astype(I32)
    large = jnp.minimum(large, REL_BUCKETS - 1)
    return jnp.where(dist < exact, dist, large)


def _swa_bias_table(rel_bias):
    W = CHUNK
    qi = jnp.arange(W)[:, None]
    kj = jnp.arange(2 * W)[None, :]
    dist = qi + W - kj
    band = ((dist >= 0) & (dist < W)).astype(I32)
    bucket = _t5_bucket(jnp.clip(dist, 0, W - 1)).astype(I32)
    return pl.pallas_call(
        _swa_bias_kernel,
        out_shape=jax.ShapeDtypeStruct((SWA_HEADS, W, 2 * W), F32),
        in_specs=[pl.BlockSpec(memory_space=pltpu.SMEM),
                  pl.BlockSpec(memory_space=pltpu.VMEM),
                  pl.BlockSpec(memory_space=pltpu.VMEM)],
        out_specs=pl.BlockSpec(memory_space=pltpu.VMEM),
        name="swa_bias_table",
    )(rel_bias, bucket, band)


def _head_lane_mask(width, head):
    lane = lax.broadcasted_iota(I32, (1, width), 1)
    return (lane // HEAD_DIM) == head


def _rotate_half(t):
    width = t.shape[-1]
    lane = lax.broadcasted_iota(I32, (1, width), 1)
    half = HEAD_DIM // 2
    fwd = pltpu.roll(t, width - half, axis=1)
    bwd = pltpu.roll(t, half, axis=1)
    return jnp.where((lane % HEAD_DIM) < half, fwd, bwd)


def _retention_body(u_ref, cos_ref, sin_ref, din_ref, dq_ref, dk_ref, dc_ref,
                    bmask_ref, avg_ref, o_ref, state_ref, *, n_chunks):
    rd = RET_HEADS * HEAD_DIM
    masks = [_head_lane_mask(rd, h) for h in range(RET_HEADS)]

    def stack_heads(t):
        return jnp.concatenate([jnp.where(m, t, 0.0) for m in masks], axis=0).astype(BF16)

    state = state_ref[...]
    for ci in range(n_chunks):
        rows = slice(ci * CHUNK, (ci + 1) * CHUNK)
        cos = cos_ref[0, rows, :]
        sin = sin_ref[0, rows, :]
        cos2 = jnp.concatenate([cos, cos], axis=1)
        sin2 = jnp.concatenate([sin, sin], axis=1)
        q = u_ref[0, rows, 0:rd].astype(F32)
        k = u_ref[0, rows, rd:2 * rd].astype(F32)
        v = u_ref[0, rows, 2 * rd:3 * rd].astype(F32)
        g = u_ref[0, rows, 3 * rd:4 * rd].astype(F32)
        qr = q * cos2 + _rotate_half(q) * sin2
        kr = (k * cos2 + _rotate_half(k) * sin2) * (HEAD_DIM ** -0.5)
        scores = _dot_nt(qr.astype(BF16), stack_heads(kr)) * din_ref[...]
        inner = _dot(scores.astype(BF16), stack_heads(v))
        cross = _dot((qr * dq_ref[...]).astype(BF16), state.astype(BF16))
        yield
        o = inner + cross
        kd_t = (kr * dk_ref[...]).T.astype(BF16)
        kv = _dot(kd_t, v.astype(BF16))
        state = dc_ref[...] * state + bmask_ref[...] * kv
        mu = _dot2(o, avg_ref[...])
        dev = o - mu
        var = _dot2(dev * dev, avg_ref[...])
        on = dev * lax.rsqrt(var + LN_EPS)
        o_ref[0, rows, :] = (_silu(g) * on).astype(o_ref.dtype)
        yield
    state_ref[...] = state


def _retention_tables():
    H, d, C = RET_HEADS, HEAD_DIM, CHUNK
    log_gamma = jnp.log(1.0 - 2.0 ** (-5.0 - jnp.arange(H, dtype=F32)))
    idx = jnp.arange(C, dtype=F32)
    diff = idx[:, None] - idx[None, :]
    decay_in = jnp.where(diff >= 0, jnp.exp(log_gamma[:, None, None] * jnp.maximum(diff, 0.0)), 0.0)
    decay_q = jnp.exp(log_gamma[:, None] * (idx + 1.0))
    decay_k = jnp.exp(log_gamma[:, None] * (C - 1.0 - idx))
    decay_chunk = jnp.exp(log_gamma * C)
    din = decay_in.transpose(1, 0, 2).reshape(C, H * C)
    dq = jnp.repeat(decay_q.T, d, axis=1)
    dk = jnp.repeat(decay_k.T, d, axis=1)
    dc = jnp.repeat(decay_chunk, d)[None, :]
    head = jnp.arange(H * d) // d
    bmask = (head[:, None] == head[None, :]).astype(F32)
    avg = jnp.tile((bmask / d).astype(BF16), (2, 1))
    return din, dq, dk, dc, bmask, avg


def _swa_body(u_ref, bias_ref, sink_ref, o_ref, kprev_ref, vprev_ref, *, n_chunks, first_step):
    W = CHUNK
    qd = SWA_HEADS * HEAD_DIM
    kvd = SWA_KV_HEADS * HEAD_DIM
    lane = lax.broadcasted_iota(I32, (1, LANES), 1)
    low = lane < HEAD_DIM
    col = lax.broadcasted_iota(I32, (1, 2 * W), 1)

    kprev = kprev_ref[...]
    vprev = vprev_ref[...]
    sink = sink_ref[...]
    for ci in range(n_chunks):
        rows = slice(ci * W, (ci + 1) * W)
        qa = u_ref[0, rows, 0:LANES].astype(F32)
        qb = u_ref[0, rows, LANES:qd].astype(F32)
        k = u_ref[0, rows, qd:qd + kvd].astype(BF16)
        v = u_ref[0, rows, qd + kvd:qd + 2 * kvd].astype(BF16)
        q4 = jnp.concatenate([jnp.where(low, qa, 0.0), jnp.where(low, 0.0, qa),
                              jnp.where(low, qb, 0.0), jnp.where(low, 0.0, qb)],
                             axis=0).astype(BF16)
        kband = jnp.concatenate([kprev, k], axis=0)
        vband = jnp.concatenate([vprev, v], axis=0)
        logits = _dot_nt(q4, kband) * (HEAD_DIM ** -0.5) + bias_ref[...]
        if ci == 0:
            logits = jnp.where(jnp.logical_and(first_step, col < W), MASK_VALUE, logits)
        m = jnp.maximum(jnp.max(logits, -1, keepdims=True), sink)
        p = jnp.exp(logits - m)
        denom = jnp.sum(p, -1, keepdims=True) + jnp.exp(sink - m)
        yield
        res = _dot(p.astype(BF16), vband) / denom
        out_a = jnp.where(low, res[0:W], res[W:2 * W])
        out_b = jnp.where(low, res[2 * W:3 * W], res[3 * W:4 * W])
        o_ref[0, rows, 0:LANES] = out_a.astype(o_ref.dtype)
        o_ref[0, rows, LANES:qd] = out_b.astype(o_ref.dtype)
        kprev, vprev = k, v
        yield
    kprev_ref[...] = kprev
    vprev_ref[...] = vprev


SWA_HEAD_ORDER = (0, 2, 1, 3)


def _ssd_body(u_ref, dt_ref, cw_ref, cb_ref, dtb_c_ref, alog_c_ref, dskip_ref, nw_ref,
              tril_ref, expand_ref, gmask_ref, shift_ref,
              o_ref, state_ref, ext_ref, *, n_chunks):
    C = CHUNK
    sd = SSD_HEADS * HEAD_DIM
    gn = SSD_GROUPS * SSD_STATE
    cd = sd + 2 * gn
    tl = n_chunks * C
    slab = 2 * LANES
    heads_per_group = SSD_HEADS // SSD_GROUPS
    heads_per_slab = slab // HEAD_DIM
    lane = lax.broadcasted_iota(I32, (1, LANES), 1)
    low = lane < SSD_STATE
    slab_masks = [_head_lane_mask(slab, hh) for hh in range(heads_per_slab)]
    row_i = lax.broadcasted_iota(I32, (C, C), 0)
    col_i = lax.broadcasted_iota(I32, (C, C), 1)
    causal = row_i >= col_i
    neg_a_c = -jnp.exp(alog_c_ref[...])

    ext_ref[CONV_TAIL:CONV_TAIL + tl, :] = u_ref[0, :, sd:sd + cd]
    state = state_ref[...]
    for ci in range(n_chunks):
        rows = slice(ci * C, (ci + 1) * C)
        z = u_ref[0, rows, 0:sd].astype(F32)
        dt_raw = dt_ref[0, rows, :]
        window = ext_ref[ci * C:ci * C + CONV_TAIL + C, :]
        shifted = _dot(shift_ref[...], window)
        conv = cb_ref[...] + cw_ref[SSD_CONV - 1:SSD_CONV, :] * window[CONV_TAIL:, :].astype(F32)
        for w in range(SSD_CONV - 1):
            conv = conv + cw_ref[w:w + 1, :] * shifted[w * C:(w + 1) * C, :]
        xbc = _silu(conv)
        xs = xbc[:, 0:sd]
        bm = xbc[:, sd:sd + gn]
        cm = xbc[:, sd + gn:cd]

        dt_c = _softplus(dt_raw + dtb_c_ref[...])
        a_c = neg_a_c * dt_c
        acs_c = _dot3_left(tril_ref[...], a_c)
        acs_t = acs_c.T
        spread = _dot3(acs_c, expand_ref[...])
        acs_x = spread[:, 0:sd]
        dt_x = _dot3(dt_c, expand_ref[:, 0:sd])
        xdt = xs * dt_x
        yield

        bstack = jnp.concatenate([jnp.where(low, bm, 0.0), jnp.where(low, 0.0, bm)],
                                 axis=0).astype(BF16)
        cb = _dot_nt(cm.astype(BF16), bstack)
        y_diag = []
        for s in range(sd // slab):
            ms = []
            for hh in range(heads_per_slab):
                h = s * heads_per_slab + hh
                g = h // heads_per_group
                col_bcast = spread[:, sd + h * LANES:sd + (h + 1) * LANES]
                seg = col_bcast - acs_t[h:h + 1, :]
                lmat = jnp.exp(jnp.where(causal, seg, MASK_VALUE))
                ms.append((cb[:, g * C:(g + 1) * C] * lmat).astype(BF16))
            xslab = xdt[:, s * slab:(s + 1) * slab]
            xstack = jnp.concatenate([jnp.where(m, xslab, 0.0) for m in slab_masks],
                                     axis=0).astype(BF16)
            y_diag.append(_dot(jnp.concatenate(ms, axis=1), xstack))
        y_diag = jnp.concatenate(y_diag, axis=1)
        yield

        y_off = _dot(cm.astype(BF16), state.astype(BF16)) * jnp.exp(acs_x)
        last = acs_x[C - 1:C, :]
        dec = jnp.exp(last - acs_x)
        new = _dot(bm.T.astype(BF16), (xdt * dec).astype(BF16))
        state = jnp.exp(last) * state + gmask_ref[...] * new

        y = y_diag + y_off + xs * dskip_ref[...]
        hgate = y * _silu(z)
        gw = sd // SSD_GROUPS
        for g in range(SSD_GROUPS):
            hg = hgate[:, g * gw:(g + 1) * gw]
            ms_ = jnp.mean(hg * hg, -1, keepdims=True)
            o_ref[0, rows, g * gw:(g + 1) * gw] = (
                hg * lax.rsqrt(ms_ + LN_EPS) * nw_ref[:, g * gw:(g + 1) * gw]).astype(o_ref.dtype)
        yield
    state_ref[...] = state
    ext_ref[0:CONV_TAIL, :] = ext_ref[tl:tl + CONV_TAIL, :]


def _ssd_tables():
    C = CHUNK
    sd = SSD_HEADS * HEAD_DIM
    gn = SSD_GROUPS * SSD_STATE
    t = np.arange(C)
    tril = (t[:, None] >= t[None, :])
    r = np.arange(LANES)[:, None]
    eexp = (r == (np.arange(sd)[None, :] // HEAD_DIM)) & (r < SSD_HEADS)
    bsel = (r == (np.arange(SSD_HEADS * LANES)[None, :] // LANES)) & (r < SSD_HEADS)
    heads_per_group = SSD_HEADS // SSD_GROUPS
    row_g = np.arange(gn)[:, None] // SSD_STATE
    col_g = (np.arange(sd)[None, :] // HEAD_DIM) // heads_per_group
    gmask = (row_g == col_g)
    expand = np.concatenate([eexp, bsel], axis=1)
    win = np.arange(CONV_TAIL + C)[None, :]
    shifts = np.concatenate(
        [(win == CONV_TAIL + t[:, None] - (SSD_CONV - 1 - s))
         for s in range(SSD_CONV - 1)], axis=0)
    return (jnp.asarray(np.tile(tril, (1, 3)), BF16), jnp.asarray(np.tile(expand, (3, 1)), BF16),
            jnp.asarray(gmask, F32), jnp.asarray(shifts, BF16))


N_RET_TABLES = 6
N_SSD_CONSTS = 10


def _in_proj_body(x_ref, shift, scale, w_ref, out_refs):
    h = (x_ref[0] * (1.0 + scale) + shift).astype(BF16)
    off = 0
    for ref in out_refs:
        width = ref.shape[-1]
        for a, b in _ff_chunks(width):
            ref[0, :, a:b] = _dot(h, w_ref[:, off + a:off + b]).astype(ref.dtype)
            yield
        off += width


def _mixers_kernel(*refs, n_chunks, n_cast, n_tiles, tiles_per_seq):
    x_ref, mod_ref, w_ref, cos, sin = refs[:5]
    pos = 5
    ret_tables = refs[pos:pos + N_RET_TABLES]
    pos += N_RET_TABLES
    ssd_consts = refs[pos:pos + N_SSD_CONSTS]
    pos += N_SSD_CONSTS
    bias, sink = refs[pos:pos + 2]
    pos += 2
    cast_in = refs[pos:pos + n_cast]
    pos += n_cast
    y_ret, y_ssd, y_swa = refs[pos:pos + 3]
    pos += 3
    cast_out = refs[pos:pos + n_cast]
    pos += n_cast
    ret_state, ssd_state, ssd_ext, kprev, vprev = refs[pos:pos + 5]
    cur = refs[pos + 5:pos + 9]
    nxt = refs[pos + 9:pos + 13]
    _side_cast(cast_in, cast_out)

    s = pl.program_id(0)
    tile = jnp.maximum(s - 1, 0)
    seq_start = tile % tiles_per_seq == 0

    @pl.when(s == 0)
    def _():
        for ref in cur:
            ref[...] = jnp.zeros_like(ref)

    @pl.when(seq_start)
    def _():
        ret_state[...] = jnp.zeros_like(ret_state)
        ssd_state[...] = jnp.zeros_like(ssd_state)
        ssd_ext[0:CONV_TAIL, :] = jnp.zeros((CONV_TAIL, ssd_ext.shape[1]), ssd_ext.dtype)
        kprev[...] = jnp.zeros_like(kprev)
        vprev[...] = jnp.zeros_like(vprev)

    batch = jnp.minimum(s, n_tiles - 1) // tiles_per_seq
    shift = mod_ref[0, batch, 0:1, :]
    scale = mod_ref[0, batch, 1:2, :]
    u_ret, u_ssd, u_dt, u_swa = cur
    bodies = [
        _ssd_body(u_ssd, u_dt, *ssd_consts, y_ssd, ssd_state, ssd_ext, n_chunks=n_chunks),
        _in_proj_body(x_ref, shift, scale, w_ref, nxt),
        _swa_body(u_swa, bias, sink, y_swa, kprev, vprev, n_chunks=n_chunks,
                  first_step=seq_start),
        _retention_body(u_ret, cos, sin, *ret_tables, y_ret, ret_state, n_chunks=n_chunks),
    ]
    alive = list(bodies)
    while alive:
        alive = [b for b in alive if next(b, StopIteration) is not StopIteration]
    for c_ref, n_ref in zip(cur, nxt):
        c_ref[...] = n_ref[...]


def _mixers(x, mod_l, w_cat, widths, dtypes, cos_t, sin_t, bias_tab, sinks_l, conv_w, conv_b,
            dt_bias, a_log, d_skip, norm_w, tl, side_cast=()):
    nb, L, d = x.shape
    W = CHUNK
    rd = RET_HEADS * HEAD_DIM
    sd = SSD_HEADS * HEAD_DIM
    qd = SWA_HEADS * HEAD_DIM
    cd = conv_w.shape[-1]
    ret_tables = _retention_tables()
    pad = lambda v: jnp.zeros((1, LANES), F32).at[0, :SSD_HEADS].set(v)
    rep = lambda v: jnp.repeat(v, HEAD_DIM)[None, :]
    ssd_consts = (conv_w, conv_b[None, :], pad(dt_bias), pad(a_log), rep(d_skip),
                  norm_w[None, :]) + _ssd_tables()
    order = jnp.array(SWA_HEAD_ORDER)
    bias_stacked = bias_tab[order].reshape(SWA_HEADS * W, 2 * W)
    sink_col = jnp.repeat(sinks_l.astype(F32)[order], W)[:, None]
    assert len(ret_tables) == N_RET_TABLES and len(ssd_consts) == N_SSD_CONSTS
    tps = L // tl
    n_tiles = nb * tps

    def tile_of(step):
        return step // tps, step % tps, 0

    proj_map = lambda s: tile_of(jnp.minimum(s, n_tiles - 1))
    mix_map = lambda s: tile_of(jnp.maximum(s - 1, 0))
    const = lambda a: pl.BlockSpec(a.shape, lambda s: (0,) * a.ndim)
    consts = ret_tables + ssd_consts + (bias_stacked, sink_col)
    c_in, c_out, c_shapes = _side_cast_specs(side_cast, n_tiles)
    u_bufs = [pltpu.VMEM((1, tl, w), t) for w, t in zip(widths, dtypes)]
    return pl.pallas_call(
        functools.partial(_mixers_kernel, n_chunks=tl // CHUNK, n_cast=len(side_cast),
                          n_tiles=n_tiles, tiles_per_seq=tps),
        out_shape=[jax.ShapeDtypeStruct((nb, L, w), BF16) for w in (rd, sd, qd)] + c_shapes,
        grid=(n_tiles + 1,),
        in_specs=[pl.BlockSpec((1, tl, d), proj_map), const(mod_l), const(w_cat),
                  pl.BlockSpec((1, tl, LANES), mix_map), pl.BlockSpec((1, tl, LANES), mix_map)]
                 + [const(a) for a in consts] + c_in,
        out_specs=[pl.BlockSpec((1, tl, w), mix_map) for w in (rd, sd, qd)] + c_out,
        scratch_shapes=[pltpu.VMEM((rd, rd), F32),
                        pltpu.VMEM((SSD_GROUPS * SSD_STATE, sd), F32),
                        pltpu.VMEM((CONV_TAIL + tl, cd), BF16),
                        pltpu.VMEM((W, LANES), BF16), pltpu.VMEM((W, LANES), BF16)]
                       + u_bufs + u_bufs,
        compiler_params=_cparams("arbitrary"),
        name="in_proj_mixers",
    )(x, mod_l, w_cat, cos_t, sin_t, *consts, *side_cast)


N_ATTN_OUT = 10


def _attn_out(x_ref, yr_ref, ys_ref, ya_ref, mod_ref, wr_ref, ws_ref, wa_ref, g_ref, b_ref,
              alpha):
    mix = (_dot(yr_ref[0], wr_ref[...]) + _dot(ys_ref[0], ws_ref[...])
           + _dot(ya_ref[0], wa_ref[...]))
    gate = mod_ref[0, 0, 2:3, :]
    r = alpha * x_ref[0] + (1.0 + gate) * mix
    return _layer_norm(r, g_ref[...], b_ref[...])


def _attn_out_operands(x, y_ret, y_ssd, y_swa, mod_l, w_r, w_s, w_a, ln_g, ln_b, tm):
    d = x.shape[-1]
    tok = lambda w: pl.BlockSpec((1, tm, w), lambda b, i: (b, i, 0))
    const = lambda a: pl.BlockSpec(a.shape, lambda b, i: (0,) * a.ndim)
    specs = [tok(d), tok(y_ret.shape[-1]), tok(y_ssd.shape[-1]), tok(y_swa.shape[-1]),
             pl.BlockSpec((1, 1, 6, d), lambda b, i: (0, b, 0, 0)),
             const(w_r), const(w_s), const(w_a), const(ln_g), const(ln_b)]
    return specs, (x, y_ret, y_ssd, y_swa, mod_l, w_r, w_s, w_a, ln_g, ln_b)


def _side_cast_specs(arrays, steps, steps_per_seq=None):
    in_specs, out_specs, out_shapes = [], [], []
    for a in arrays:
        e, r, c = a.shape
        per = steps // e
        if steps_per_seq is None:
            index = lambda s, per=per: (jnp.minimum(s, steps - 1) // per,
                                        jnp.minimum(s, steps - 1) % per, 0)
        else:
            index = lambda b, i, per=per: ((b * steps_per_seq + i) // per,
                                           (b * steps_per_seq + i) % per, 0)
        spec = pl.BlockSpec((1, r // per, c), index)
        in_specs.append(spec)
        out_specs.append(spec)
        out_shapes.append(jax.ShapeDtypeStruct(a.shape, BF16))
    return in_specs, out_specs, out_shapes


def _side_cast(in_refs, out_refs):
    for i_ref, o_ref in zip(in_refs, out_refs):
        o_ref[...] = i_ref[...].astype(o_ref.dtype)


def _dense_ffn_kernel(*refs, alpha):
    attn = refs[:N_ATTN_OUT]
    mod_ref = attn[4]
    wg_ref, wu_ref, wd_ref, g_ref, b_ref = refs[N_ATTN_OUT:N_ATTN_OUT + 5]
    rest = refs[N_ATTN_OUT + 5:]
    n_cast = (len(rest) - 1) // 2
    o_ref = rest[n_cast]
    _side_cast(rest[:n_cast], rest[n_cast + 1:])
    x = _attn_out(*attn, alpha)
    sh = mod_ref[0, 0, 3:4, :]
    sc = mod_ref[0, 0, 4:5, :]
    gate = mod_ref[0, 0, 5:6, :]
    h = (x * (1.0 + sc) + sh).astype(BF16)
    dff = wg_ref.shape[-1]
    acc = None
    for a, b in _ff_chunks(dff):
        gj = _dot(h, wg_ref[:, a:b])
        uj = _dot(h, wu_ref[:, a:b])
        part = _dot((_silu(gj) * uj).astype(BF16), wd_ref[a:b, :])
        acc = part if acc is None else acc + part
    r = alpha * x + (1.0 + gate) * acc
    o_ref[0] = _layer_norm(r, g_ref[...], b_ref[...])


def _dense_ffn(attn_args, wg, wu, wd, ln_g, ln_b, alpha, tm, side_cast=()):
    nb, L, d = attn_args[0].shape
    const = lambda a: pl.BlockSpec(a.shape, lambda b, i: (0,) * a.ndim,
                                   pipeline_mode=pl.Buffered(1))
    a_specs, a_ops = _attn_out_operands(*attn_args, tm)
    c_in, c_out, c_shapes = _side_cast_specs(side_cast, nb * (L // tm), L // tm)
    return pl.pallas_call(
        functools.partial(_dense_ffn_kernel, alpha=alpha),
        out_shape=[jax.ShapeDtypeStruct((nb, L, d), F32)] + c_shapes,
        grid=(nb, L // tm),
        in_specs=a_specs + [const(wg), const(wu), const(wd), const(ln_g), const(ln_b)] + c_in,
        out_specs=[pl.BlockSpec((1, tm, d), lambda b, i: (b, i, 0))] + c_out,
        compiler_params=_cparams("arbitrary", "arbitrary"),
        name="attn_out_dense_ffn_ln",
    )(*a_ops, wg, wu, wd, ln_g, ln_b, *side_cast)


def _store_row_tiles(ref, value):
    rows, d = value.shape
    nt = d // LANES
    for c in range(nt):
        ref[pl.ds(c, rows, stride=nt), :] = value[:, c * LANES:(c + 1) * LANES].astype(ref.dtype)


def _load_row_tiles(ref):
    nt = ROW_TILE
    rows = ref.shape[0] // nt
    return jnp.concatenate([ref[pl.ds(c, rows, stride=nt), :] for c in range(nt)], axis=1)


def _router_kernel(*refs, alpha):
    attn = refs[:N_ATTN_OUT]
    mod_ref = attn[4]
    wr_ref, br_ref, trils_ref, x1_ref, h_ref, route_ref, cnt_ref, base_ref = refs[N_ATTN_OUT:]
    first = jnp.logical_and(pl.program_id(0) == 0, pl.program_id(1) == 0)

    @pl.when(first)
    def _():
        base_ref[...] = jnp.zeros_like(base_ref)

    x1 = _attn_out(*attn, alpha)
    x1_ref[0] = x1
    sh = mod_ref[0, 0, 3:4, :]
    sc = mod_ref[0, 0, 4:5, :]
    h = x1 * (1.0 + sc) + sh
    _store_row_tiles(h_ref.at[0], h)
    h_hi, h_lo = _hi_lo(h)
    wide = _dot(h_hi, wr_ref[...]) + _dot(h_lo, wr_ref[...])
    logits = wide + pltpu.roll(wide, LANES - N_EXPERTS, axis=1) + br_ref[...]
    lane = lax.broadcasted_iota(I32, logits.shape, 1).astype(F32)
    logits = jnp.where(lane < N_EXPERTS, logits, MASK_VALUE)
    v1 = jnp.max(logits, -1, keepdims=True)
    e1 = jnp.min(jnp.where(logits == v1, lane, float(LANES)), -1, keepdims=True)
    rest = jnp.where(lane == e1, MASK_VALUE, logits)
    v2 = jnp.max(rest, -1, keepdims=True)
    e2 = jnp.min(jnp.where(rest == v2, lane, float(LANES)), -1, keepdims=True)
    t = jnp.exp(v2 - v1)
    w1 = 1.0 / (1.0 + t)
    w2 = t / (1.0 + t)
    hot1 = (lane == e1).astype(F32)
    hot2 = (lane == e2).astype(F32)
    both = hot1 + hot2
    base = base_ref[0:1, :]
    before = _dot(trils_ref[...], both.astype(BF16)) + base
    rank1 = jnp.sum(hot1 * before, -1, keepdims=True)
    rank2 = jnp.sum(hot2 * before, -1, keepdims=True)
    total = base + jnp.sum(both, 0, keepdims=True)
    base_ref[0:1, :] = total
    cnt_ref[...] = jnp.broadcast_to(total, cnt_ref.shape)
    route = jnp.where(lane == 0, e1, 0.0)
    route = jnp.where(lane == 1, e2, route)
    route = jnp.where(lane == 2, rank1, route)
    route = jnp.where(lane == 3, rank2, route)
    route = jnp.where(lane == 4, w1, route)
    route = jnp.where(lane == 5, w2, route)
    route_ref[0] = route


def _router(attn_args, w_router, b_router, alpha, tm):
    nb, L, d = attn_args[0].shape
    w_hi, w_lo = _hi_lo(w_router)
    wr = (jnp.zeros((d, LANES), BF16).at[:, :N_EXPERTS].set(w_hi)
          .at[:, N_EXPERTS:2 * N_EXPERTS].set(w_lo))
    br = jnp.zeros((1, LANES), F32).at[0, :N_EXPERTS].set(b_router)
    t = jnp.arange(tm)
    tril_strict = (t[:, None] > t[None, :]).astype(BF16)
    const = lambda a: pl.BlockSpec(a.shape, lambda b, i: (0,) * a.ndim)
    a_specs, a_ops = _attn_out_operands(*attn_args, tm)
    return pl.pallas_call(
        functools.partial(_router_kernel, alpha=alpha),
        out_shape=[jax.ShapeDtypeStruct((nb, L, d), F32),
                   jax.ShapeDtypeStruct((nb, L * (d // LANES), LANES), F32),
                   jax.ShapeDtypeStruct((nb, L, LANES), F32),
                   jax.ShapeDtypeStruct((8, LANES), F32)],
        grid=(nb, L // tm),
        in_specs=a_specs + [const(wr), const(br), const(tril_strict)],
        out_specs=[pl.BlockSpec((1, tm, d), lambda b, i: (b, i, 0)),
                   pl.BlockSpec((1, tm * (d // LANES), LANES), lambda b, i: (b, i, 0)),
                   pl.BlockSpec((1, tm, LANES), lambda b, i: (b, i, 0)),
                   pl.BlockSpec((8, LANES), lambda b, i: (0, 0))],
        scratch_shapes=[pltpu.VMEM((8, LANES), F32)],
        compiler_params=_cparams("arbitrary", "arbitrary"),
        name="attn_out_moe_router",
    )(*a_ops, wr, br, tril_strict)


TOP_K = 2
ROW_TILE = 8
EXPERT_FF_SPLITS = 2


def _ff_chunks(width, step=512):
    return [(a, min(a + step, width)) for a in range(0, width, step)]


def _expert_kernel(te_ref, nu_ref, gsrc_ref, sdst_ref,
                   h_ref, wg_ref, wu_ref, wd_ref, yk_ref,
                   xbuf, ybuf, hbuf, acc, gsem, ssem, *, tme):
    i = pl.program_id(0)
    j = pl.program_id(1)
    n_used = nu_ref[0]
    slot = i % 2
    other = 1 - slot
    used = i < n_used
    chunks = _ff_chunks(wg_ref.shape[-1])
    rows_per_step = tme // EXPERT_FF_SPLITS
    n_front = len(chunks) // 2
    front_w = chunks[n_front - 1][1]
    back_w = wg_ref.shape[-1] - front_w
    gather_ranges, scatter_ranges = [], []
    for c, (a, b) in enumerate(chunks):
        if c < n_front:
            gather_ranges.append((rows_per_step * a // front_w, rows_per_step * b // front_w))
            scatter_ranges.append((0, 0))
        else:
            gather_ranges.append((0, 0))
            scatter_ranges.append((rows_per_step * (a - front_w) // back_w,
                                   rows_per_step * (b - front_w) // back_w))

    nt = ROW_TILE

    def gather(tile, slot_, r):
        src = pl.multiple_of(gsrc_ref[tile * tme + r], nt)
        return pltpu.make_async_copy(h_ref.at[pl.ds(src, nt)],
                                     xbuf.at[slot_, pl.ds(pl.multiple_of(r * nt, nt), nt)],
                                     gsem.at[slot_])

    def wait_gather(slot_):
        pltpu.make_async_copy(h_ref.at[pl.ds(0, tme * nt)], xbuf.at[slot_],
                              gsem.at[slot_]).wait()

    def scatter(block, slot_, r):
        dst = pl.multiple_of(sdst_ref[block * tme + r], nt)
        return pltpu.make_async_copy(ybuf.at[slot_, pl.ds(pl.multiple_of(r * nt, nt), nt)],
                                     yk_ref.at[pl.ds(dst, nt)], ssem.at[slot_])

    def wait_scatter(slot_):
        pltpu.make_async_copy(ybuf.at[slot_], yk_ref.at[pl.ds(0, tme * nt)],
                              ssem.at[slot_]).wait()

    @pl.when(jnp.logical_and(i == 0, j == 0))
    def _():
        ybuf[1] = jnp.zeros(ybuf.shape[1:], F32)

        def body(r, carry):
            gather(0, 0, r).start()
            return carry
        lax.fori_loop(0, tme, body, 0, unroll=8)

    @pl.when(used)
    def _():
        @pl.when(j == 0)
        def _():
            wait_gather(slot)

            @pl.when(i > 0)
            def _():
                wait_scatter(slot)
            acc[...] = jnp.zeros(acc.shape, F32)

        hbuf[...] = _load_row_tiles(xbuf.at[slot]).astype(BF16)
        h = hbuf[...]
        for c, (a, b) in enumerate(chunks):
            for r in range(*gather_ranges[c]):
                gather(i + 1, other, j * rows_per_step + r).start()
            for r in range(*scatter_ranges[c]):
                scatter(i, other, j * rows_per_step + r).start(priority=r % 2)
            gj = _dot(h, wg_ref[0, :, a:b])
            uj = _dot(h, wu_ref[0, :, a:b])
            part = _dot((_silu(gj) * uj).astype(BF16), wd_ref[0, a:b, :])
            acc[...] = acc[...] + part
            ybuf[slot, 0:ROW_TILE, :] = part[0:ROW_TILE, 0:LANES]

        @pl.when(j == EXPERT_FF_SPLITS - 1)
        def _():
            _store_row_tiles(ybuf.at[slot], acc[...])

        @pl.when(jnp.logical_and(j == EXPERT_FF_SPLITS - 1, i == n_used - 1))
        def _():
            wait_gather(other)
            wait_scatter(other)

            def body(r, carry):
                scatter(i + 1, slot, r).start()
                return carry
            lax.fori_loop(0, tme, body, 0, unroll=8)
            wait_scatter(slot)


def _expert_ffn(h_rows, tile_expert, n_used, gsrc, sdst, wg, wu, wd, tme):
    nt = ROW_TILE
    T = h_rows.shape[0] // nt
    d = nt * LANES
    dff = wg.shape[-1]
    dffh = dff // EXPERT_FF_SPLITS
    n_tiles = gsrc.shape[0] // tme
    yk_rows = TOP_K * T + tme

    def half(i, j):
        return jnp.where(i % 2 == 0, j, EXPERT_FF_SPLITS - 1 - j)

    return pl.pallas_call(
        functools.partial(_expert_kernel, tme=tme),
        out_shape=jax.ShapeDtypeStruct((yk_rows * nt, LANES), F32),
        grid_spec=pltpu.PrefetchScalarGridSpec(
            num_scalar_prefetch=4,
            grid=(n_tiles, EXPERT_FF_SPLITS),
            in_specs=[pl.BlockSpec(memory_space=pl.ANY),
                      pl.BlockSpec((1, d, dffh), lambda i, j, te, *_: (te[i], 0, half(i, j))),
                      pl.BlockSpec((1, d, dffh), lambda i, j, te, *_: (te[i], 0, half(i, j))),
                      pl.BlockSpec((1, dffh, d), lambda i, j, te, *_: (te[i], half(i, j), 0))],
            out_specs=pl.BlockSpec(memory_space=pl.ANY),
            scratch_shapes=[pltpu.VMEM((2, tme * nt, LANES), F32),
                            pltpu.VMEM((2, tme * nt, LANES), F32),
                            pltpu.VMEM((tme, d), BF16), pltpu.VMEM((tme, d), F32),
                            pltpu.SemaphoreType.DMA((2,)), pltpu.SemaphoreType.DMA((2,))]),
        compiler_params=_cparams("arbitrary", "arbitrary"),
        name="moe_experts",
    )(tile_expert, n_used, gsrc, sdst, h_rows, wg, wu, wd)


def _moe_finish_kernel(x_ref, route_ref, mod_ref, y0_ref, y1_ref, g_ref, b_ref, o_ref, *, alpha):
    route = route_ref[0]
    f = route[:, 4:5] * _load_row_tiles(y0_ref) + route[:, 5:6] * _load_row_tiles(y1_ref)
    gate = mod_ref[0, 0, 5:6, :]
    r = alpha * x_ref[0] + (1.0 + gate) * f
    o_ref[0] = _layer_norm(r, g_ref[...], b_ref[...])


def _moe_finish(x, route, mod_l, yk, ln_g, ln_b, alpha, tm):
    nb, L, d = x.shape
    tiles_per_seq = L // tm
    tiles = nb * tiles_per_seq
    return pl.pallas_call(
        functools.partial(_moe_finish_kernel, alpha=alpha),
        out_shape=jax.ShapeDtypeStruct((nb, L, d), F32),
        grid=(nb, tiles_per_seq),
        in_specs=[pl.BlockSpec((1, tm, d), lambda b, i: (b, i, 0)),
                  pl.BlockSpec((1, tm, LANES), lambda b, i: (b, i, 0)),
                  pl.BlockSpec((1, 1, 6, d), lambda b, i: (0, b, 0, 0)),
                  pl.BlockSpec((tm * ROW_TILE, LANES), lambda b, i: (b * tiles_per_seq + i, 0)),
                  pl.BlockSpec((tm * ROW_TILE, LANES),
                               lambda b, i: (tiles + b * tiles_per_seq + i, 0)),
                  pl.BlockSpec(ln_g.shape, lambda b, i: (0, 0)),
                  pl.BlockSpec(ln_b.shape, lambda b, i: (0, 0))],
        out_specs=pl.BlockSpec((1, tm, d), lambda b, i: (b, i, 0)),
        compiler_params=_cparams("arbitrary", "arbitrary"),
        name="moe_finish_ln",
    )(x, route, mod_l, yk, yk, ln_g, ln_b)


def _row_index_kernel(dest_ref, gsrc0_ref, sdst0_ref, gsrc_ref, sdst_ref, sem, *, n_pairs, T, tme):
    init_g = pltpu.make_async_copy(gsrc0_ref, gsrc_ref, sem.at[0])
    init_s = pltpu.make_async_copy(sdst0_ref, sdst_ref, sem.at[1])
    init_g.start()
    init_s.start()
    init_g.wait()
    init_s.wait()

    assert TOP_K == 2
    group = 8

    def body(g, carry):
        f0 = g * group
        rows = [dest_ref[f0 + k] for k in range(group)]
        for k in range(group):
            tok = g * (group // TOP_K) + k // TOP_K
            gsrc_ref[rows[k]] = tok * ROW_TILE
            sdst_ref[tme + rows[k]] = ((k % TOP_K) * T + tok) * ROW_TILE
        return carry
    lax.fori_loop(0, n_pairs // group, body, 0, unroll=2)


def _row_indices(dest, n_rows, T, tme):
    dump = (TOP_K * T + jnp.arange(n_rows + tme, dtype=I32) % tme) * ROW_TILE
    return pl.pallas_call(
        functools.partial(_row_index_kernel, n_pairs=dest.shape[0], T=T, tme=tme),
        out_shape=[jax.ShapeDtypeStruct((n_rows,), I32),
                   jax.ShapeDtypeStruct((n_rows + tme,), I32)],
        in_specs=[pl.BlockSpec(memory_space=pltpu.SMEM),
                  pl.BlockSpec(memory_space=pl.ANY),
                  pl.BlockSpec(memory_space=pl.ANY)],
        out_specs=[pl.BlockSpec(memory_space=pltpu.SMEM),
                   pl.BlockSpec(memory_space=pltpu.SMEM)],
        scratch_shapes=[pltpu.SemaphoreType.DMA((2,))],
        name="moe_row_indices",
    )(dest, jnp.zeros((n_rows,), I32), dump)


def _moe(attn_args, w_router, b_router, wg, wu, wd, ln_g, ln_b, alpha, tm_route, tme, tmf):
    nb, L, d = attn_args[0].shape
    mod_l = attn_args[4]
    T = nb * L
    x, h, route, counts = _router(attn_args, w_router, b_router, alpha, tm_route)
    route_flat = route.reshape(T, LANES)
    e = route_flat[:, 0:2].astype(I32)
    rank = route_flat[:, 2:4].astype(I32)
    cnt = counts[0, :N_EXPERTS].astype(I32)
    tiles = (cnt + tme - 1) // tme
    tile_end = jnp.cumsum(tiles)
    group_start = (tile_end - tiles) * tme
    dest = (group_start[e] + rank).reshape(-1)
    n_tiles = (TOP_K * T) // tme + N_EXPERTS
    n_rows = n_tiles * tme
    tile_expert = jnp.minimum(
        jnp.sum(jnp.arange(n_tiles, dtype=I32)[:, None] >= tile_end[None, :].astype(I32), axis=1),
        N_EXPERTS - 1).astype(I32)
    n_used = tile_end[-1:].astype(I32)
    gsrc, sdst = _row_indices(dest, n_rows, T, tme)
    yk = _expert_ffn(h.reshape(T * ROW_TILE, LANES), tile_expert, n_used, gsrc, sdst,
                     wg, wu, wd, tme)
    return _moe_finish(x, route, mod_l, yk, ln_g, ln_b, alpha, tmf)


def kernel(x, c, positions, rel_bias, w_ada, b_ada, w_in, w_out, conv_w, conv_b, dt_bias, a_log,
           d_skip, ssd_norm_w, sinks, ln_g, ln_b, ffn_w_gate, ffn_w_up, ffn_w_down, router_w,
           router_b, expert_w_gate, expert_w_up, expert_w_down):
    depth = w_ada.shape[0]
    nb, L, d = x.shape
    alpha = (2 * depth) ** 0.25
    rd = RET_HEADS * HEAD_DIM
    sd = SSD_HEADS * HEAD_DIM
    cd = conv_w.shape[-1]
    qd = SWA_HEADS * HEAD_DIM
    kvd = SWA_KV_HEADS * HEAD_DIM
    sizes = (rd, rd, rd, rd, sd, cd, SSD_HEADS, qd, kvd, kvd)
    offs = np.concatenate([[0], np.cumsum(sizes)])
    tl = _token_tile(L)

    mod = _ada_mod(c, w_ada, b_ada)
    cos_t, sin_t = _rotary_tables(positions, tl)
    bias_tab = _swa_bias_table(rel_bias)

    q_perm = np.concatenate([np.arange(h * HEAD_DIM, (h + 1) * HEAD_DIM) for h in SWA_HEAD_ORDER])
    widths = (4 * rd, sd + cd, LANES, qd + 2 * kvd)
    dtypes = (BF16, BF16, F32, BF16)

    steps = nb * (L // tl)
    n_exp, _, dff_e = expert_w_gate.shape[1:]
    per = steps // n_exp
    side_ok = (steps % n_exp == 0 and d % per == 0 and dff_e % per == 0
               and (d // per) % 8 == 0 and (dff_e // per) % 8 == 0)

    for layer in range(depth):
        wl = w_in[layer]
        seg = lambda i: wl[:, offs[i]:offs[i + 1]]
        dt_cols = jnp.zeros((d, LANES), F32).at[:, :SSD_HEADS].set(seg(6))
        w_cat = jnp.concatenate(
            [seg(0), seg(1), seg(2), seg(3), seg(4), seg(5), dt_cols,
             seg(7)[:, q_perm], seg(8), seg(9)], axis=1).astype(BF16)
        wo = w_out[layer]
        w_r = wo[0:rd].astype(BF16)
        w_s = wo[rd:rd + sd].astype(BF16)
        w_a = wo[rd + sd:][q_perm].astype(BF16)
        mod_l = mod[layer:layer + 1]

        i = layer // 2
        is_moe = layer % 2 == 1
        mix_cast = (expert_w_up[i], expert_w_down[i]) if is_moe and side_ok else ()
        y_ret, y_ssd, y_swa, *up_down_bf16 = _mixers(
            x, mod_l, w_cat, widths, dtypes, cos_t, sin_t, bias_tab, sinks[layer], conv_w[layer],
            conv_b[layer], dt_bias[layer], a_log[layer], d_skip[layer], ssd_norm_w[layer], tl,
            side_cast=mix_cast)
        attn_args = (x, y_ret, y_ssd, y_swa, mod_l, w_r, w_s, w_a,
                     ln_g[layer, 0][None, :], ln_b[layer, 0][None, :])

        g2 = ln_g[layer, 1][None, :]
        b2 = ln_b[layer, 1][None, :]
        if not is_moe:
            nxt = (layer + 1) // 2
            ride = (expert_w_gate[nxt],) if layer + 1 < depth and side_ok else ()
            x, *gate_bf16 = _dense_ffn(
                attn_args, ffn_w_gate[i].astype(BF16), ffn_w_up[i].astype(BF16),
                ffn_w_down[i].astype(BF16), g2, b2, alpha, tl, side_cast=ride)
        else:
            if side_ok:
                wg_e, = gate_bf16
                wu_e, wd_e = up_down_bf16
            else:
                wg_e, wu_e, wd_e = (expert_w_gate[i].astype(BF16), expert_w_up[i].astype(BF16),
                                    expert_w_down[i].astype(BF16))
            x = _moe(attn_args, router_w[i], router_b[i], wg_e, wu_e, wd_e, g2, b2, alpha,
                     tm_route=tl, tme=tl, tmf=tl)
    return x
```
